```python
import math
import jax, jax.numpy as jnp
from jax import lax
import numpy as np

D_MODEL = 1024
BATCH = 4
SEQ = 4096
DEPTH = 1
DEC_BATCH = 16
DEC_SEQ = 16
PAST_LEN = 4096

CHUNK = 64
Q_BLOCK = 128
GDN_HEADS = 4
GDN_DK = 128
GDN_DV = 128
CONV_W = 4
DIFF_HEADS = 4
DIFF_DK = 64
DIFF_DV = 128
ROT_DIM = DIFF_DK // 4
ROPE_THETA = 500000.0
D_FF = 2816
EPS = 1e-6

GDN_QK = GDN_HEADS * GDN_DK
GDN_V = GDN_HEADS * GDN_DV
CONV_CH = 2 * GDN_QK + GDN_V
DIFF_QK = DIFF_HEADS * 2 * DIFF_DK
DIFF_V = DIFF_HEADS * DIFF_DV
MIX_WIDTH = GDN_V + DIFF_V
IN_SPLITS = [CONV_CH,
             CONV_CH + GDN_HEADS,
             CONV_CH + 2 * GDN_HEADS,
             CONV_CH + 2 * GDN_HEADS + GDN_V,
             CONV_CH + 2 * GDN_HEADS + GDN_V + DIFF_QK,
             CONV_CH + 2 * GDN_HEADS + GDN_V + 2 * DIFF_QK]
IN_COLS = IN_SPLITS[-1] + DIFF_V

kernel_name = 'hymba_gdn_diffattn_macaron_stream_step'


def rms_norm(x, g):
    xf = x.astype(jnp.float32)
    y = xf * lax.rsqrt(jnp.mean(xf * xf, axis=-1, keepdims=True) + EPS)
    return (y * g.astype(jnp.float32)).astype(x.dtype)


def l2_norm(x):
    xf = x.astype(jnp.float32)
    return xf * lax.rsqrt(jnp.sum(xf * xf, axis=-1, keepdims=True) + EPS)


def swiglu(x, w_gu, w_down):
    gate, up = jnp.split(x @ w_gu, 2, axis=-1)
    return (jax.nn.silu(gate) * up) @ w_down


def causal_conv(x, hist, w):
    L = x.shape[1]
    xp = jnp.concatenate([hist.astype(x.dtype), x], axis=1)
    y = xp[:, 0:L] * w[0]
    for i in range(1, CONV_W):
        y = y + xp[:, i:i + L] * w[i]
    return jax.nn.silu(y), xp[:, -(CONV_W - 1):]


def rope_partial(x, pos):
    half = ROT_DIM // 2
    inv = jnp.float32(ROPE_THETA) ** (-jnp.arange(half, dtype=jnp.float32) * 2.0 / ROT_DIM)
    ang = pos.astype(jnp.float32)[:, None] * inv[None, :]
    cos = jnp.cos(ang)[None, :, None, None, :]
    sin = jnp.sin(ang)[None, :, None, None, :]
    xr = x[..., :ROT_DIM].astype(jnp.float32)
    x1, x2 = xr[..., :half], xr[..., half:]
    rot = jnp.concatenate([x1 * cos - x2 * sin, x2 * cos + x1 * sin], axis=-1).astype(x.dtype)
    return jnp.concatenate([rot, x[..., ROT_DIM:]], axis=-1)


def gated_delta_rule(q, k, v, g, beta, s0, chunk):
    B, L, H, DK = q.shape
    DV = v.shape[-1]
    n = L // chunk

    def blk(t):
        return jnp.moveaxis(t.reshape((B, n, chunk) + t.shape[2:]), 3, 2)

    qc, kc, vc, gc, bc = blk(q), blk(k), blk(v), blk(g), blk(beta)
    G = jnp.cumsum(gc, axis=-1)
    idx = jnp.arange(chunk)
    incl = idx[:, None] >= idx[None, :]
    strict = idx[:, None] > idx[None, :]
    dec = jnp.exp(jnp.where(incl, G[..., :, None] - G[..., None, :], -jnp.inf))
    kk = jnp.einsum('bnhck,bnhsk->bnhcs', kc, kc)
    A = jnp.where(strict, bc[..., None] * kk * dec, 0.0) + jnp.eye(chunk, dtype=jnp.float32)
    rhs = jnp.concatenate([vc * bc[..., None], kc * (bc * jnp.exp(G))[..., None]], axis=-1)
    sol = lax.linalg.triangular_solve(A, rhs, left_side=True, lower=True, unit_diagonal=True)
    u0, w = sol[..., :DV], sol[..., DV:]
    qk = jnp.einsum('bnhck,bnhsk->bnhcs', qc, kc) * dec
    q_dec = qc * jnp.exp(G)[..., None]
    k_dec = kc * jnp.exp(G[..., -1:] - G)[..., None]
    g_last = jnp.exp(G[..., -1])

    def step(S, inp):
        u0_i, w_i, qk_i, qd_i, kd_i, gl_i = inp
        u = u0_i - jnp.einsum('bhck,bhkv->bhcv', w_i, S)
        o = jnp.einsum('bhck,bhkv->bhcv', qd_i, S) + jnp.einsum('bhcs,bhsv->bhcv', qk_i, u)
        S = S * gl_i[..., None, None] + jnp.einsum('bhck,bhcv->bhkv', kd_i, u)
        return S, o

    xs = (jnp.moveaxis(u0, 1, 0), jnp.moveaxis(w, 1, 0), jnp.moveaxis(qk, 1, 0),
          jnp.moveaxis(q_dec, 1, 0), jnp.moveaxis(k_dec, 1, 0), jnp.moveaxis(g_last, 1, 0))
    S, o = lax.scan(step, s0, xs)
    o = jnp.moveaxis(jnp.moveaxis(o, 0, 1), 3, 2).reshape(B, L, H, DV)
    return o, S


def gated_deltanet(conv_in, b_raw, a_raw, z, conv_hist, s0, conv_w, a_log, dt_bias, out_norm):
    B, L, _ = conv_in.shape
    c, new_hist = causal_conv(conv_in, conv_hist, conv_w)
    q, k, v = jnp.split(c, [GDN_QK, 2 * GDN_QK], axis=-1)
    q = l2_norm(q.reshape(B, L, GDN_HEADS, GDN_DK)) * (GDN_DK ** -0.5)
    k = l2_norm(k.reshape(B, L, GDN_HEADS, GDN_DK))
    v = v.reshape(B, L, GDN_HEADS, GDN_DV).astype(jnp.float32)
    beta = jax.nn.sigmoid(b_raw.astype(jnp.float32))
    g = -jnp.exp(a_log.astype(jnp.float32)) * jax.nn.softplus(a_raw.astype(jnp.float32) + dt_bias.astype(jnp.float32))
    o, S = gated_delta_rule(q, k, v, g, beta, s0.astype(jnp.float32), min(CHUNK, L))
    o = rms_norm(o, out_norm) * jax.nn.silu(z.reshape(B, L, GDN_HEADS, GDN_DV).astype(jnp.float32))
    return o.reshape(B, L, GDN_V).astype(conv_in.dtype), S.astype(s0.dtype), new_hist


def diff_attention(q, k, v, mask, lam):
    s = jnp.einsum('bqhmd,bshmd->bhmqs', q, k, preferred_element_type=jnp.float32) * (DIFF_DK ** -0.5)
    if mask is not None:
        s = jnp.where(mask, s, -jnp.inf)
    p = jax.nn.softmax(s, axis=-1)
    p = p[:, :, 0] - lam * p[:, :, 1]
    return jnp.einsum('bhqs,bshv->bqhv', p.astype(v.dtype), v)


def diff_attention_prompt(q, k, v, lam):
    B, L = q.shape[:2]
    nb = L // Q_BLOCK
    qb = jnp.moveaxis(q.reshape(B, nb, Q_BLOCK, DIFF_HEADS, 2, DIFF_DK), 1, 0)
    k_chunk = jnp.arange(L) // CHUNK

    def one(args):
        qi, b = args
        q_chunk = (b * Q_BLOCK + jnp.arange(Q_BLOCK)) // CHUNK
        mask = k_chunk[None, :] <= q_chunk[:, None]
        return diff_attention(qi, k, v, mask, lam)

    o = lax.map(one, (qb, jnp.arange(nb)))
    return jnp.moveaxis(o, 0, 1).reshape(B, L, DIFF_HEADS, DIFF_DV)


def encoder_layer(x, pos, k_hist, v_hist, conv_hist, s0, p, lam_init):
    B, L, _ = x.shape
    x = x + 0.5 * swiglu(rms_norm(x, p['ffn1_norm']), p['ffn1_w_gu'], p['ffn1_w_down'])
    h = rms_norm(x, p['mix_norm'])
    proj = h @ p['w_in']
    conv_in, b_raw, a_raw, z, dq, dk, dv = jnp.split(proj, IN_SPLITS, axis=-1)
    if conv_hist is None:
        conv_hist = jnp.zeros((B, CONV_W - 1, CONV_CH), proj.dtype)
        s0 = jnp.zeros((B, GDN_HEADS, GDN_DK, GDN_DV), x.dtype)
    o_gdn, s_new, conv_new = gated_deltanet(conv_in, b_raw, a_raw, z, conv_hist, s0, p['conv_w'],
                                            p['a_log'], p['dt_bias'], p['gdn_out_norm'])
    q = rope_partial(rms_norm(dq.reshape(B, L, DIFF_HEADS, 2, DIFF_DK), p['q_norm']), pos)
    k = rope_partial(rms_norm(dk.reshape(B, L, DIFF_HEADS, 2, DIFF_DK), p['k_norm']), pos)
    v = dv.reshape(B, L, DIFF_HEADS, DIFF_DV)
    lam_e = jnp.exp(jnp.sum(p['lambda_q'].astype(jnp.float32) * p['lambda_k'].astype(jnp.float32), axis=-1))
    lam = lam_e[0] - lam_e[1] + lam_init
    if k_hist is None:
        o_diff = diff_attention_prompt(q, k, v, lam)
    else:
        k_all = jnp.concatenate([k_hist.astype(k.dtype), k], axis=1)
        v_all = jnp.concatenate([v_hist.astype(v.dtype), v], axis=1)
        o_diff = diff_attention(q, k_all, v_all, None, lam)
    o_diff = rms_norm(o_diff, p['diff_out_norm']) * (1.0 - lam_init)
    mixed = jnp.concatenate([o_gdn, o_diff.reshape(B, L, DIFF_V).astype(o_gdn.dtype)], axis=-1)
    x = x + mixed @ p['w_out']
    x = x + 0.5 * swiglu(rms_norm(x, p['ffn2_norm']), p['ffn2_w_gu'], p['ffn2_w_down'])
    return x, k, v, s_new, conv_new


def setup_inputs(seed: int = 0) -> dict:
    key = jax.random.key(seed)
    ks = jax.random.split(key, 32)
    nrm = lambda i, shape, s: jax.random.normal(ks[i], shape, jnp.float32) * s
    dt = jnp.exp(jax.random.uniform(ks[9], (DEPTH, GDN_HEADS), jnp.float32, math.log(1e-3), math.log(1e-1)))
    return {
        'x_prompt': nrm(0, (BATCH, SEQ, D_MODEL), 1.0),
        'x_sample': nrm(1, (DEC_BATCH, DEC_SEQ, D_MODEL), 1.0),
        'cache_k': nrm(2, (DEPTH, DEC_BATCH, PAST_LEN, DIFF_HEADS, 2, DIFF_DK), 1.0),
        'cache_v': nrm(3, (DEPTH, DEC_BATCH, PAST_LEN, DIFF_HEADS, DIFF_DV), 1.0),
        'state_gdn': nrm(4, (DEPTH, DEC_BATCH, GDN_HEADS, GDN_DK, GDN_DV), 1.0),
        'state_conv': nrm(5, (DEPTH, DEC_BATCH, CONV_W - 1, CONV_CH), 1.0),
        'ffn1_norm': 1.0 + nrm(6, (DEPTH, D_MODEL), 0.02),
        'ffn1_w_gu': nrm(7, (DEPTH, D_MODEL, 2 * D_FF), D_MODEL ** -0.5),
        'ffn1_w_down': nrm(8, (DEPTH, D_FF, D_MODEL), D_FF ** -0.5),
        'mix_norm': 1.0 + nrm(10, (DEPTH, D_MODEL), 0.02),
        'w_in': nrm(11, (DEPTH, D_MODEL, IN_COLS), D_MODEL ** -0.5),
        'conv_w': nrm(12, (DEPTH, CONV_W, CONV_CH), CONV_W ** -0.5),
        'a_log': jnp.log(jax.random.uniform(ks[13], (DEPTH, GDN_HEADS), jnp.float32, 1.0, 16.0)),
        'dt_bias': dt + jnp.log(-jnp.expm1(-dt)),
        'gdn_out_norm': 1.0 + nrm(14, (DEPTH, GDN_DV), 0.02),
        'q_norm': 1.0 + nrm(15, (DEPTH, DIFF_DK), 0.02),
        'k_norm': 1.0 + nrm(16, (DEPTH, DIFF_DK), 0.02),
        'lambda_q': nrm(17, (DEPTH, 2, DIFF_DK), 0.1),
        'lambda_k': nrm(18, (DEPTH, 2, DIFF_DK), 0.1),
        'diff_out_norm': 1.0 + nrm(19, (DEPTH, DIFF_DV), 0.02),
        'w_out': nrm(20, (DEPTH, MIX_WIDTH, D_MODEL), MIX_WIDTH ** -0.5),
        'ffn2_norm': 1.0 + nrm(21, (DEPTH, D_MODEL), 0.02),
        'ffn2_w_gu': nrm(22, (DEPTH, D_MODEL, 2 * D_FF), D_MODEL ** -0.5),
        'ffn2_w_down': nrm(23, (DEPTH, D_FF, D_MODEL), D_FF ** -0.5),
    }


def reference(x_prompt, x_sample, cache_k, cache_v, state_gdn, state_conv,
              ffn1_norm, ffn1_w_gu, ffn1_w_down, mix_norm, w_in, conv_w, a_log, dt_bias,
              gdn_out_norm, q_norm, k_norm, lambda_q, lambda_k, diff_out_norm, w_out,
              ffn2_norm, ffn2_w_gu, ffn2_w_down):
    pos_p = jnp.arange(x_prompt.shape[1])
    pos_s = cache_k.shape[2] + jnp.arange(x_sample.shape[1])
    hp, hs = x_prompt, x_sample
    kp_l, vp_l, sp_l, cp_l, ks_l, vs_l, ss_l, cs_l = [], [], [], [], [], [], [], []
    for l in range(DEPTH):
        lam_init = 0.8 - 0.6 * math.exp(-0.3 * l)
        p = dict(ffn1_norm=ffn1_norm[l], ffn1_w_gu=ffn1_w_gu[l], ffn1_w_down=ffn1_w_down[l],
                 mix_norm=mix_norm[l], w_in=w_in[l], conv_w=conv_w[l], a_log=a_log[l],
                 dt_bias=dt_bias[l], gdn_out_norm=gdn_out_norm[l], q_norm=q_norm[l], k_norm=k_norm[l],
                 lambda_q=lambda_q[l], lambda_k=lambda_k[l], diff_out_norm=diff_out_norm[l],
                 w_out=w_out[l], ffn2_norm=ffn2_norm[l], ffn2_w_gu=ffn2_w_gu[l], ffn2_w_down=ffn2_w_down[l])
        hp, kp, vp, sp, cp = encoder_layer(hp, pos_p, None, None, None, None, p, lam_init)
        hs, ks_, vs_, ss_, cs_ = encoder_layer(hs, pos_s, cache_k[l], cache_v[l], state_conv[l], state_gdn[l], p, lam_init)
        kp_l.append(kp); vp_l.append(vp); sp_l.append(sp); cp_l.append(cp)
        ks_l.append(ks_); vs_l.append(vs_); ss_l.append(ss_); cs_l.append(cs_)
    return (hp, hs,
            jnp.stack(kp_l), jnp.stack(vp_l), jnp.stack(sp_l), jnp.stack(cp_l),
            jnp.stack(ks_l), jnp.stack(vs_l), jnp.stack(ss_l), jnp.stack(cs_l))
```

```python
import functools
import math

import jax
import jax.numpy as jnp
from jax import lax
from jax.experimental import pallas as pl
from jax.experimental.pallas import tpu as pltpu

F32 = jnp.float32
BF16 = jnp.bfloat16

EPS = 1e-6
CHUNK = 64
GDN_HEADS = 4
GDN_DK = 128
GDN_DV = 128
CONV_W = 4
DIFF_HEADS = 4
DIFF_DK = 64
DIFF_DV = 128
ROT_DIM = DIFF_DK // 4
ROPE_THETA = 500000.0
GDN_QK = GDN_HEADS * GDN_DK
GDN_V = GDN_HEADS * GDN_DV
CONV_CH = 2 * GDN_QK + GDN_V
DIFF_QK = DIFF_HEADS * 2 * DIFF_DK
DIFF_V = DIFF_HEADS * DIFF_DV

LANES = 128
INV_BLOCK = 16
VMEM_LIMIT = 56 * 1024 * 1024


def _dot(a, b):
    return jnp.dot(a, b, preferred_element_type=F32)


def _dot_nt(a, b):
    return lax.dot_general(a, b, (((1,), (1,)), ((), ())), preferred_element_type=F32)


def _split(x):
    hi = x.astype(BF16)
    lo = (x - hi.astype(F32)).astype(BF16)
    return hi, lo


def _dot3(x, y):
    xh, xl = _split(x)
    yh, yl = _split(y)
    return _dot(xh, yh) + (_dot(xl, yh) + _dot(xh, yl))


def _rms(x, g):
    return x * lax.rsqrt(jnp.mean(x * x, axis=-1, keepdims=True) + EPS) * g


def _silu(x):
    return x * jax.nn.sigmoid(x)


def _swiglu(xn, wg_ref, wu_ref, wd_ref):
    g = _dot(xn, wg_ref[...])
    u = _dot(xn, wu_ref[...])
    act = (_silu(g) * u).astype(BF16)
    return _dot(act, wd_ref[...])


def _const_spec(shape):
    nd = len(shape)
    return pl.BlockSpec(shape, lambda *_: (0,) * nd, pipeline_mode=pl.Buffered(1))


def _pre_kernel(x_ref, cos_ref, sa_ref, sb_ref, n1_ref, wg_ref, wu_ref, wd_ref, nm_ref,
                wc_ref, wz_ref, wba_ref, wq_ref, wk_ref, wv_ref, qn_ref, kn_ref, gm_ref,
                x1_ref, conv_ref, z_ref, ba_ref, qb_ref, kf_ref, vf_ref, kb_ref, vb_ref):
    x = x_ref[...]
    xn = _rms(x, n1_ref[...]).astype(BF16)
    x1 = x + 0.5 * _swiglu(xn, wg_ref, wu_ref, wd_ref)
    x1_ref[...] = x1
    h = _rms(x1, nm_ref[...]).astype(BF16)
    conv_ref[...] = _dot(h, wc_ref[...])
    z_ref[...] = _dot(h, wz_ref[...])
    ba_ref[...] = _dot(h, wba_ref[...])
    v = _dot(h, wv_ref[...])
    vf_ref[...] = v
    vb_ref[...] = v.astype(BF16)

    cos = cos_ref[...]
    sa = sa_ref[...]
    sb = sb_ref[...]
    gm = gm_ref[...]

    def norm_rope(t, gw):
        ss = _dot((t * t).astype(BF16), gm)
        t = t * lax.rsqrt(ss * (1.0 / DIFF_DK) + EPS) * gw
        outs = []
        for hh in range(DIFF_HEADS):
            th = t[:, hh * LANES:(hh + 1) * LANES]
            up = pltpu.roll(th, LANES - ROT_DIM // 2, 1)
            dn = pltpu.roll(th, ROT_DIM // 2, 1)
            outs.append(th * cos + up * sa + dn * sb)
        return jnp.concatenate(outs, axis=1)

    q = norm_rope(_dot(h, wq_ref[...]), qn_ref[...])
    qb_ref[...] = (q * (DIFF_DK ** -0.5)).astype(BF16)
    k = norm_rope(_dot(h, wk_ref[...]), kn_ref[...])
    kf_ref[...] = k
    kb_ref[...] = k.astype(BF16)


def _pre_call(x2d, tabs, p, tm, n_pos_tiles):
    n, d = x2d.shape
    grid = (n // tm,)
    row = lambda w: pl.BlockSpec((tm, w), lambda i: (i, 0))
    tab = pl.BlockSpec((tm, LANES), lambda i: (i % n_pos_tiles, 0))
    weights = [p['n1'], p['wg1'], p['wu1'], p['wd1'], p['nm'], p['w_conv'], p['w_z'], p['w_ba'],
               p['w_q'], p['w_k'], p['w_v'], p['qn'], p['kn'], p['gmat']]
    in_specs = [row(d), tab, tab, tab] + [_const_spec(w.shape) for w in weights]
    outs = [(d, F32), (CONV_CH, F32), (GDN_V, F32), (2 * LANES, F32), (DIFF_QK, BF16),
            (DIFF_QK, F32), (DIFF_V, F32), (DIFF_QK, BF16), (DIFF_V, BF16)]
    return pl.pallas_call(
        _pre_kernel,
        grid=grid,
        in_specs=in_specs,
        out_specs=[row(w) for w, _ in outs],
        out_shape=[jax.ShapeDtypeStruct((n, w), dt) for w, dt in outs],
        compiler_params=pltpu.CompilerParams(dimension_semantics=("arbitrary",),
                                             vmem_limit_bytes=VMEM_LIMIT),
        name="pre",
    )(x2d, *tabs, *weights)


def _post_kernel(x1_ref, og_ref, od_ref, wo_ref, n2_ref, wg_ref, wu_ref, wd_ref, y_ref):
    mixed = jnp.concatenate([og_ref[...], od_ref[...]], axis=1)
    x2 = x1_ref[...] + _dot(mixed, wo_ref[...])
    xn = _rms(x2, n2_ref[...]).astype(BF16)
    y_ref[...] = x2 + 0.5 * _swiglu(xn, wg_ref, wu_ref, wd_ref)


def _post_call(x1, og, od, p, tm):
    n, d = x1.shape
    row = lambda w: pl.BlockSpec((tm, w), lambda i: (i, 0))
    weights = [p['wo'], p['n2'], p['wg2'], p['wu2'], p['wd2']]
    return pl.pallas_call(
        _post_kernel,
        grid=(n // tm,),
        in_specs=[row(d), row(GDN_V), row(DIFF_V)] + [_const_spec(w.shape) for w in weights],
        out_specs=row(d),
        out_shape=jax.ShapeDtypeStruct((n, d), F32),
        compiler_params=pltpu.CompilerParams(dimension_semantics=("arbitrary",),
                                             vmem_limit_bytes=VMEM_LIMIT),
        name="post",
    )(x1, og, od, *weights)


def _unit_lower_inverse(a, c):
    ri = lax.broadcasted_iota(jnp.int32, (c, c), 0)
    ci = lax.broadcasted_iota(jnp.int32, (c, c), 1)
    eye = (ri == ci).astype(F32)
    if c == INV_BLOCK:
        d = a
    else:
        same = (ri // INV_BLOCK) == (ci // INV_BLOCK)
        d = jnp.where(same, a, 0.0)
    assert INV_BLOCK == 16
    d2 = _dot3(d, d)
    d3 = _dot3(d, d2)
    d4 = _dot3(d2, d2)
    n1 = eye - d + d2 - d3
    n2 = n1 + _dot3(n1, d4)
    d8 = _dot3(d4, d4)
    td = n2 + _dot3(n2, d8)
    if c == INV_BLOCK:
        return td
    assert c == 4 * INV_BLOCK
    m = _dot3(td, a - d)
    m2 = _dot3(m, m)
    m3 = _dot3(m, m2)
    return _dot3(eye - m + m2 - m3, td)


def _gdn_kernel(conv_ref, z_ref, ba_ref, hist_ref, s0_ref, cw_ref, alog_ref, dtb_ref, on_ref,
                o_ref, s_out_ref,
                xp_ref, s_ref, q_scr, k_scr, v_scr, g_scr, b_scr, *, tc, c):
    t = pl.program_id(1)
    nt = pl.num_programs(1)
    pad = 8

    @pl.when(t == 0)
    def _():
        xp_ref[0:pad, :] = jnp.zeros((pad, CONV_CH), F32)
        xp_ref[pad - (CONV_W - 1):pad, :] = hist_ref[...]
        s_ref[...] = s0_ref[...]

    x = conv_ref[...]
    xp_ref[pad:pad + tc, :] = x
    cw = cw_ref[...]
    y = xp_ref[pad - 3:pad - 3 + tc, :] * cw[0:1, :]
    for i in range(1, CONV_W):
        y = y + xp_ref[pad - 3 + i:pad - 3 + i + tc, :] * cw[i:i + 1, :]
    xp_ref[pad - (CONV_W - 1):pad, :] = x[tc - (CONV_W - 1):tc, :]
    y = _silu(y)

    for h in range(GDN_HEADS):
        qh = y[:, h * GDN_DK:(h + 1) * GDN_DK]
        kh = y[:, GDN_QK + h * GDN_DK:GDN_QK + (h + 1) * GDN_DK]
        q_scr[:, h * GDN_DK:(h + 1) * GDN_DK] = qh * (
            lax.rsqrt(jnp.sum(qh * qh, axis=-1, keepdims=True) + EPS) * (GDN_DK ** -0.5))
        k_scr[:, h * GDN_DK:(h + 1) * GDN_DK] = kh * lax.rsqrt(
            jnp.sum(kh * kh, axis=-1, keepdims=True) + EPS)
    v_scr[...] = y[:, 2 * GDN_QK:]

    ba = ba_ref[...]
    b_scr[...] = jax.nn.sigmoid(ba[:, :LANES])
    g = -jnp.exp(alog_ref[...]) * jax.nn.softplus(ba[:, LANES:] + dtb_ref[...])
    ri = lax.broadcasted_iota(jnp.int32, (tc, tc), 0)
    ci = lax.broadcasted_iota(jnp.int32, (tc, tc), 1)
    tri = jnp.where((ci <= ri) & ((ri // c) == (ci // c)), 1.0, 0.0).astype(BF16)
    g1 = g.astype(BF16)
    r1 = g - g1.astype(F32)
    g2 = r1.astype(BF16)
    g3 = (r1 - g2.astype(F32)).astype(BF16)
    g_scr[...] = _dot(tri, g1) + (_dot(tri, g2) + _dot(tri, g3))

    rc = lax.broadcasted_iota(jnp.int32, (c, c), 0)
    cc = lax.broadcasted_iota(jnp.int32, (c, c), 1)
    incl = rc >= cc
    strict = rc > cc
    lane0 = lax.broadcasted_iota(jnp.int32, (c, LANES), 1) == 0
    ones_l0 = jnp.where(lane0, 1.0, 0.0).astype(BF16)
    on = on_ref[...]

    def chunk(ic, carry):
        r0 = pl.multiple_of(ic * c, c)
        rows = pl.ds(r0, c)
        g_all = g_scr[rows, :]
        b_all = b_scr[rows, :]
        for h in range(GDN_HEADS):
            cols = slice(h * GDN_DK, (h + 1) * GDN_DK)
            qh = q_scr[rows, cols]
            kh = k_scr[rows, cols]
            vh = v_scr[rows, cols]
            gc = g_all[:, h:h + 1]
            bc = b_all[:, h:h + 1]
            gx = jnp.where(lane0, gc, 0.0)
            x1 = gx.astype(BF16)
            xr = gx - x1.astype(F32)
            x2 = xr.astype(BF16)
            x3 = (xr - x2.astype(F32)).astype(BF16)
            gr = _dot_nt(ones_l0, x1) + (_dot_nt(ones_l0, x2) + _dot_nt(ones_l0, x3))
            dec = jnp.exp(jnp.where(incl, gc - gr, -jnp.inf))
            khb = kh.astype(BF16)
            kk = _dot_nt(khb, khb)
            a = jnp.where(strict, bc * kk * dec, 0.0)
            tinv = _unit_lower_inverse(a, c)
            eg = jnp.exp(gc)
            rhs = jnp.concatenate([vh * bc, kh * (bc * eg)], axis=1)
            sol = _dot3(tinv, rhs)
            u0 = sol[:, :GDN_DV]
            w = sol[:, GDN_DV:]
            qk = _dot_nt(qh.astype(BF16), khb) * dec
            qd = qh * eg
            g_last = g_all[c - 1:c, h:h + 1]
            kd = kh * jnp.exp(g_last - gc)
            s = s_ref[h]
            sb = s.astype(BF16)
            ws = _dot(jnp.concatenate([w, qd], axis=0).astype(BF16), sb)
            u = u0 - ws[:c]
            ub = u.astype(BF16)
            o = ws[c:] + _dot(qk.astype(BF16), ub)
            s_ref[h] = s * jnp.exp(g_last) + lax.dot_general(
                kd.astype(BF16), ub, (((0,), (0,)), ((), ())), preferred_element_type=F32)
            zh = z_ref[rows, cols]
            o_ref[rows, cols] = (_rms(o, on) * _silu(zh)).astype(o_ref.dtype)
        return carry

    lax.fori_loop(0, tc // c, chunk, 0)

    @pl.when(t == nt - 1)
    def _():
        s_out_ref[...] = s_ref[...]


def _gdn_call(conv, z, ba, hist, s0, p, tc, c):
    b, l, _ = conv.shape
    row = lambda w: pl.BlockSpec((None, tc, w), lambda i, j: (i, j, 0))
    per_b = lambda shape: pl.BlockSpec((None,) + shape, lambda i, j: (i,) + (0,) * len(shape))
    small = [p['conv_w'], p['a_log'], p['dt_bias'], p['gdn_on']]
    return pl.pallas_call(
        functools.partial(_gdn_kernel, tc=tc, c=c),
        grid=(b, l // tc),
        in_specs=[row(CONV_CH), row(GDN_V), row(2 * LANES), per_b((CONV_W - 1, CONV_CH)),
                  per_b((GDN_HEADS, GDN_DK, GDN_DV))] + [
                      pl.BlockSpec(w.shape, lambda i, j: (0, 0)) for w in small],
        out_specs=[row(GDN_V), per_b((GDN_HEADS, GDN_DK, GDN_DV))],
        out_shape=[jax.ShapeDtypeStruct((b, l, GDN_V), BF16),
                   jax.ShapeDtypeStruct((b, GDN_HEADS, GDN_DK, GDN_DV), F32)],
        scratch_shapes=[pltpu.VMEM((tc + 8, CONV_CH), F32),
                        pltpu.VMEM((GDN_HEADS, GDN_DK, GDN_DV), F32),
                        pltpu.VMEM((tc, GDN_QK), F32), pltpu.VMEM((tc, GDN_QK), F32),
                        pltpu.VMEM((tc, GDN_V), F32),
                        pltpu.VMEM((tc, LANES), F32), pltpu.VMEM((tc, LANES), F32)],
        compiler_params=pltpu.CompilerParams(dimension_semantics=("arbitrary", "arbitrary"),
                                             vmem_limit_bytes=VMEM_LIMIT),
        name="gdn",
    )(conv, z, ba, hist, s0, *small)


def _stack_maps(q):
    lane = lax.broadcasted_iota(jnp.int32, q.shape, 1)
    zero = jnp.zeros_like(q)
    return jnp.concatenate([jnp.where(lane < DIFF_DK, q, zero), jnp.where(lane >= DIFF_DK, q, zero)],
                           axis=0)


def _diff_finish(acc, l, t, lq_ref, lk_ref, dn_ref, lam_init):
    lam_e = jnp.exp(jnp.sum(lq_ref[...] * lk_ref[...], axis=-1, keepdims=True))
    lam = lam_e[0:1] - lam_e[1:2] + lam_init
    o = acc[:t] / l[:t] - lam * (acc[t:] / l[t:])
    return _rms(o, dn_ref[...]) * (1.0 - lam_init)


def _attn_prompt_kernel(q_ref, k_ref, v_ref, lq_ref, lk_ref, dn_ref, o_ref,
                        m_scr, l_scr, acc_scr, *, tq, lam_init):
    i = pl.program_id(2)
    qq = _stack_maps(q_ref[...])

    def scores(j):
        rows = pl.ds(pl.multiple_of(j * tq, tq), tq)
        return _dot_nt(qq, k_ref[rows, :]), v_ref[rows, :]

    s, vb = scores(i)
    r = lax.broadcasted_iota(jnp.int32, (2 * tq, tq), 0)
    r = jnp.where(r >= tq, r - tq, r)
    cidx = lax.broadcasted_iota(jnp.int32, (2 * tq, tq), 1)
    s = jnp.where((cidx // CHUNK) <= (r // CHUNK), s, -jnp.inf)
    m = jnp.max(s, axis=-1, keepdims=True)
    pexp = jnp.exp(s - m)
    m_scr[...] = jnp.broadcast_to(m, m_scr.shape)
    l_scr[...] = jnp.broadcast_to(jnp.sum(pexp, axis=-1, keepdims=True), l_scr.shape)
    acc_scr[...] = _dot(pexp.astype(BF16), vb)

    def body(j, carry):
        s, vb = scores(j)
        m_old = m_scr[...]
        m_new = jnp.maximum(m_old, jnp.max(s, axis=-1, keepdims=True))
        alpha = jnp.exp(m_old - m_new)
        pexp = jnp.exp(s - m_new[:, 0:1])
        l_scr[...] = alpha * l_scr[...] + jnp.sum(pexp, axis=-1, keepdims=True)
        acc_scr[...] = alpha * acc_scr[...] + _dot(pexp.astype(BF16), vb)
        m_scr[...] = m_new
        return carry

    lax.fori_loop(0, i, body, 0)
    o_ref[...] = _diff_finish(acc_scr[...], l_scr[...], tq, lq_ref, lk_ref, dn_ref,
                              lam_init).astype(o_ref.dtype)


def _attn_prompt_call(qb, kb, vb, p, tq, lam_init):
    b, l, _ = qb.shape
    small = [p['lambda_q'], p['lambda_k'], p['diff_on']]
    return pl.pallas_call(
        functools.partial(_attn_prompt_kernel, tq=tq, lam_init=lam_init),
        grid=(b, DIFF_HEADS, l // tq),
        in_specs=[pl.BlockSpec((None, tq, LANES), lambda bi, h, i: (bi, i, h)),
                  pl.BlockSpec((None, l, LANES), lambda bi, h, i: (bi, 0, h)),
                  pl.BlockSpec((None, l, LANES), lambda bi, h, i: (bi, 0, h))] + [
                      pl.BlockSpec(w.shape, lambda bi, h, i: (0, 0)) for w in small],
        out_specs=pl.BlockSpec((None, tq, LANES), lambda bi, h, i: (bi, i, h)),
        out_shape=jax.ShapeDtypeStruct((b, l, DIFF_V), BF16),
        scratch_shapes=[pltpu.VMEM((2 * tq, LANES), F32), pltpu.VMEM((2 * tq, LANES), F32),
                        pltpu.VMEM((2 * tq, LANES), F32)],
        compiler_params=pltpu.CompilerParams(
            dimension_semantics=("arbitrary", "arbitrary", "arbitrary"),
            vmem_limit_bytes=VMEM_LIMIT),
        name="attn_prompt",
    )(qb, kb, vb, *small)


def _attn_sample_kernel(q_ref, kn_ref, vn_ref, ck_ref, cv_ref, lq_ref, lk_ref, dn_ref, o_ref,
                        *, t, lam_init):
    qq = _stack_maps(q_ref[...])
    s_c = _dot_nt(qq, ck_ref[...].astype(BF16))
    s_n = _dot_nt(qq, kn_ref[...])
    m = jnp.maximum(jnp.max(s_c, axis=-1, keepdims=True), jnp.max(s_n, axis=-1, keepdims=True))
    p_c = jnp.exp(s_c - m)
    p_n = jnp.exp(s_n - m)
    l = jnp.sum(p_c, axis=-1, keepdims=True) + jnp.sum(p_n, axis=-1, keepdims=True)
    acc = _dot(p_c.astype(BF16), cv_ref[...].astype(BF16)) + _dot(p_n.astype(BF16), vn_ref[...])
    o_ref[...] = _diff_finish(acc, l, t, lq_ref, lk_ref, dn_ref, lam_init).astype(o_ref.dtype)


def _attn_sample_call(qb, kb, vb, ck, cv, p, lam_init):
    b, t, _ = qb.shape
    past = ck.shape[1]
    small = [p['lambda_q'], p['lambda_k'], p['diff_on']]
    new = pl.BlockSpec((None, t, LANES), lambda bi, h: (bi, 0, h))
    old = pl.BlockSpec((None, past, LANES), lambda bi, h: (bi, 0, h))
    return pl.pallas_call(
        functools.partial(_attn_sample_kernel, t=t, lam_init=lam_init),
        grid=(b, DIFF_HEADS),
        in_specs=[new, new, new, old, old] + [pl.BlockSpec(w.shape, lambda bi, h: (0, 0)) for w in small],
        out_specs=new,
        out_shape=jax.ShapeDtypeStruct((b, t, DIFF_V), BF16),
        compiler_params=pltpu.CompilerParams(dimension_semantics=("arbitrary", "arbitrary"),
                                             vmem_limit_bytes=VMEM_LIMIT),
        name="attn_sample",
    )(qb, kb, vb, ck, cv, *small)


def _rope_tables(pos):
    half = ROT_DIM // 2
    inv = jnp.float32(ROPE_THETA) ** (-jnp.arange(half, dtype=F32) * 2.0 / ROT_DIM)
    ang = pos.astype(F32)[:, None] * inv[None, :]
    cos, sin = jnp.cos(ang), jnp.sin(ang)
    n = pos.shape[0]
    rest = DIFF_DK - ROT_DIM
    one = jnp.ones((n, rest), F32)
    zero = jnp.zeros((n, rest), F32)
    zh = jnp.zeros((n, half), F32)
    c64 = jnp.concatenate([cos, cos, one], axis=1)
    a64 = jnp.concatenate([-sin, zh, zero], axis=1)
    b64 = jnp.concatenate([zh, sin, zero], axis=1)
    rep = LANES // DIFF_DK
    return tuple(jnp.tile(t, (1, rep)) for t in (c64, a64, b64))


def _layer_params(w, l):
    d = w['w_in'].shape[1]
    d_ff = w['ffn1_w_down'].shape[1]
    win = w['w_in'][l]
    o = 0
    w_conv = win[:, o:o + CONV_CH]; o += CONV_CH
    w_b = win[:, o:o + GDN_HEADS]; o += GDN_HEADS
    w_a = win[:, o:o + GDN_HEADS]; o += GDN_HEADS
    w_z = win[:, o:o + GDN_V]; o += GDN_V
    w_q = win[:, o:o + DIFF_QK]; o += DIFF_QK
    w_k = win[:, o:o + DIFF_QK]; o += DIFF_QK
    w_v = win[:, o:o + DIFF_V]
    zpad = jnp.zeros((d, LANES - GDN_HEADS), win.dtype)
    w_ba = jnp.concatenate([w_b, zpad, w_a, zpad], axis=1)
    hpad = lambda v: jnp.concatenate([v.astype(F32), jnp.zeros((LANES - GDN_HEADS,), F32)])[None, :]
    grp = jnp.arange(DIFF_QK) // DIFF_DK
    bf = lambda t: t.astype(BF16)
    r2 = lambda v: v.astype(F32)[None, :]
    return dict(
        n1=r2(w['ffn1_norm'][l]), wg1=bf(w['ffn1_w_gu'][l][:, :d_ff]), wu1=bf(w['ffn1_w_gu'][l][:, d_ff:]),
        wd1=bf(w['ffn1_w_down'][l]),
        nm=r2(w['mix_norm'][l]), w_conv=bf(w_conv), w_z=bf(w_z), w_ba=bf(w_ba), w_q=bf(w_q), w_k=bf(w_k),
        w_v=bf(w_v),
        qn=r2(jnp.tile(w['q_norm'][l], DIFF_QK // DIFF_DK)), kn=r2(jnp.tile(w['k_norm'][l], DIFF_QK // DIFF_DK)),
        gmat=(grp[:, None] == grp[None, :]).astype(BF16),
        conv_w=w['conv_w'][l].astype(F32), a_log=hpad(w['a_log'][l]), dt_bias=hpad(w['dt_bias'][l]),
        gdn_on=r2(w['gdn_out_norm'][l]),
        lambda_q=w['lambda_q'][l].astype(F32), lambda_k=w['lambda_k'][l].astype(F32),
        diff_on=r2(w['diff_out_norm'][l]),
        wo=bf(w['w_out'][l]), n2=r2(w['ffn2_norm'][l]), wg2=bf(w['ffn2_w_gu'][l][:, :d_ff]),
        wu2=bf(w['ffn2_w_gu'][l][:, d_ff:]), wd2=bf(w['ffn2_w_down'][l]),
    )


def _pick_tile(n, pref):
    t = min(n, pref)
    assert n % t == 0
    return t


def _layer(x, pos, k_hist, v_hist, conv_hist, s0, p, lam_init):
    b, l, d = x.shape
    n = b * l
    tm = _pick_tile(n, 256)
    tabs = _rope_tables(pos)
    if l >= tm:
        assert l % tm == 0
        n_pos_tiles = l // tm
    else:
        assert tm % l == 0
        tabs = tuple(jnp.tile(t, (tm // l, 1)) for t in tabs)
        n_pos_tiles = 1
    x1, conv, z, ba, qb, kf, vf, kb, vb = _pre_call(x.reshape(n, d), tabs, p, tm, n_pos_tiles)
    c = min(CHUNK, l)
    assert c in (INV_BLOCK, 4 * INV_BLOCK) and l % c == 0
    tc = _pick_tile(l, 4 * c)
    if conv_hist is None:
        conv_hist = jnp.zeros((b, CONV_W - 1, CONV_CH), F32)
        s0 = jnp.zeros((b, GDN_HEADS, GDN_DK, GDN_DV), F32)
    conv3 = conv.reshape(b, l, CONV_CH)
    og, s_new = _gdn_call(conv3, z.reshape(b, l, GDN_V), ba.reshape(b, l, 2 * LANES),
                          conv_hist.astype(F32), s0.astype(F32), p, tc, c)
    conv_new = jnp.concatenate([conv_hist.astype(F32), conv3], axis=1)[:, -(CONV_W - 1):]
    q3, k3, v3 = (t.reshape(b, l, -1) for t in (qb, kb, vb))
    if k_hist is None:
        od = _attn_prompt_call(q3, k3, v3, p, _pick_tile(l, 256), lam_init)
    else:
        past = k_hist.shape[1]
        od = _attn_sample_call(q3, k3, v3, k_hist.reshape(b, past, DIFF_QK),
                               v_hist.reshape(b, past, DIFF_V), p, lam_init)
    y = _post_call(x1, og.reshape(n, GDN_V), od.reshape(n, DIFF_V), p, tm)
    return (y.reshape(b, l, d), kf.reshape(b, l, DIFF_HEADS, 2, DIFF_DK),
            vf.reshape(b, l, DIFF_HEADS, DIFF_DV), s_new, conv_new)


def kernel(x_prompt, x_sample, cache_k, cache_v, state_gdn, state_conv, ffn1_norm, ffn1_w_gu, ffn1_w_down,
           mix_norm, w_in, conv_w, a_log, dt_bias, gdn_out_norm, q_norm, k_norm, lambda_q, lambda_k,
           diff_out_norm, w_out, ffn2_norm, ffn2_w_gu, ffn2_w_down):
    w = dict(ffn1_norm=ffn1_norm, ffn1_w_gu=ffn1_w_gu, ffn1_w_down=ffn1_w_down, mix_norm=mix_norm, w_in=w_in,
             conv_w=conv_w, a_log=a_log, dt_bias=dt_bias, gdn_out_norm=gdn_out_norm, q_norm=q_norm,
             k_norm=k_norm, lambda_q=lambda_q, lambda_k=lambda_k, diff_out_norm=diff_out_norm, w_out=w_out,
             ffn2_norm=ffn2_norm, ffn2_w_gu=ffn2_w_gu, ffn2_w_down=ffn2_w_down)
    depth = w_in.shape[0]
    pos_p = jnp.arange(x_prompt.shape[1])
    pos_s = cache_k.shape[2] + jnp.arange(x_sample.shape[1])
    hp, hs = x_prompt, x_sample
    outs = [[] for _ in range(8)]
    for l in range(depth):
        lam_init = 0.8 - 0.6 * math.exp(-0.3 * l)
        p = _layer_params(w, l)
        hp, kp, vp, sp, cp = _layer(hp, pos_p, None, None, None, None, p, lam_init)
        hs, ks, vs, ss, cs = _layer(hs, pos_s, cache_k[l], cache_v[l], state_conv[l], state_gdn[l], p, lam_init)
        for acc, val in zip(outs, (kp, vp, sp, cp, ks, vs, ss, cs)):
            acc.append(val)
    return (hp, hs) + tuple(jnp.stack(o) for o in outs)
```

```python
import functools
import math

import jax
import jax.numpy as jnp
from jax import lax
from jax.experimental import pallas as pl
from jax.experimental.pallas import tpu as pltpu

F32 = jnp.float32
BF16 = jnp.bfloat16

EPS = 1e-6
CHUNK = 64
GDN_HEADS = 4
GDN_DK = 128
GDN_DV = 128
CONV_W = 4
DIFF_HEADS = 4
DIFF_DK = 64
DIFF_DV = 128
ROT_DIM = DIFF_DK // 4
ROPE_THETA = 500000.0
GDN_QK = GDN_HEADS * GDN_DK
GDN_V = GDN_HEADS * GDN_DV
CONV_CH = 2 * GDN_QK + GDN_V
DIFF_QK = DIFF_HEADS * 2 * DIFF_DK
DIFF_V = DIFF_HEADS * DIFF_DV

LANES = 128
INV_BLOCK = 16
VMEM_LIMIT = 56 * 1024 * 1024
Q_SCALE = (DIFF_DK ** -0.5) * math.log2(math.e)


def _dot(a, b):
    return jnp.dot(a, b, preferred_element_type=F32)


def _dot_nt(a, b):
    return lax.dot_general(a, b, (((1,), (1,)), ((), ())), preferred_element_type=F32)


def _split(x):
    hi = x.astype(BF16)
    lo = (x - hi.astype(F32)).astype(BF16)
    return hi, lo


def _dot3(x, y):
    xh, xl = _split(x)
    yh, yl = _split(y)
    return _dot(xh, yh) + (_dot(xl, yh) + _dot(xh, yl))


def _rms(x, g):
    return x * lax.rsqrt(jnp.mean(x * x, axis=-1, keepdims=True) + EPS) * g


def _silu(x):
    return x * jax.nn.sigmoid(x)


def _swiglu(xn, wg_ref, wu_ref, wd_ref):
    g = _dot(xn, wg_ref[...])
    u = _dot(xn, wu_ref[...])
    act = (_silu(g) * u).astype(BF16)
    return _dot(act, wd_ref[...])


def _const_spec(shape):
    nd = len(shape)
    return pl.BlockSpec(shape, lambda *_: (0,) * nd, pipeline_mode=pl.Buffered(1))


def _pre_kernel(x_ref, cos_ref, sa_ref, sb_ref, n1_ref, wg_ref, wu_ref, wd_ref, nm_ref,
                wc_ref, wz_ref, wba_ref, wq_ref, wk_ref, wv_ref, qn_ref, kn_ref, gm_ref,
                x1_ref, conv_ref, z_ref, ba_ref, qb_ref, vb_ref, kf_ref, vf_ref, *maybe_kb_ref,
                k_pos_minor):
    x = x_ref[...]
    xn = _rms(x, n1_ref[...]).astype(BF16)
    x1 = x + 0.5 * _swiglu(xn, wg_ref, wu_ref, wd_ref)
    x1_ref[...] = x1
    h = _rms(x1, nm_ref[...]).astype(BF16)
    conv_ref[...] = _dot(h, wc_ref[...])
    z_ref[...] = _dot(h, wz_ref[...])
    ba_ref[...] = _dot(h, wba_ref[...])
    v = _dot(h, wv_ref[...])
    vb_ref[...] = v.astype(BF16)
    tm = v.shape[0]
    if k_pos_minor:
        for hh in range(DIFF_HEADS):
            vf_ref[pl.ds(hh, tm, stride=DIFF_HEADS), :] = v[:, hh * DIFF_DV:(hh + 1) * DIFF_DV]
    else:
        vf_ref[...] = v

    cos = cos_ref[...]
    sa = sa_ref[...]
    sb = sb_ref[...]
    gm = gm_ref[...]

    def norm_rope(t, gw):
        ss = _dot((t * t).astype(BF16), gm)
        t = t * lax.rsqrt(ss * (1.0 / DIFF_DK) + EPS) * gw
        outs = []
        for hh in range(DIFF_HEADS):
            th = t[:, hh * LANES:(hh + 1) * LANES]
            up = pltpu.roll(th, LANES - ROT_DIM // 2, 1)
            dn = pltpu.roll(th, ROT_DIM // 2, 1)
            outs.append(th * cos + up * sa + dn * sb)
        return jnp.concatenate(outs, axis=1)

    q = norm_rope(_dot(h, wq_ref[...]), qn_ref[...])
    qb_ref[...] = (q * Q_SCALE).astype(BF16)
    k = norm_rope(_dot(h, wk_ref[...]), kn_ref[...])
    if k_pos_minor:
        kf_ref[...] = k.T.reshape(DIFF_HEADS, LANES, tm)
    else:
        kf_ref[...] = k
        maybe_kb_ref[0][...] = k.astype(BF16)


def _pre_call(x2d, tabs, p, tm, n_pos_tiles, batch, k_pos_minor):
    n, d = x2d.shape
    grid = (n // tm,)
    row = lambda w: pl.BlockSpec((tm, w), lambda i: (i, 0))
    tab = pl.BlockSpec((tm, LANES), lambda i: (i % n_pos_tiles, 0))
    weights = [p['n1'], p['wg1'], p['wu1'], p['wd1'], p['nm'], p['w_conv'], p['w_z'], p['w_ba'],
               p['w_q'], p['w_k'], p['w_v'], p['qn'], p['kn'], p['gmat']]
    in_specs = [row(d), tab, tab, tab] + [_const_spec(w.shape) for w in weights]
    outs = [(d, F32), (CONV_CH, F32), (GDN_V, F32), (2 * LANES, F32), (DIFF_QK, BF16), (DIFF_V, BF16)]
    out_specs = [row(w) for w, _ in outs]
    out_shape = [jax.ShapeDtypeStruct((n, w), dt) for w, dt in outs]
    if k_pos_minor:
        l = n // batch
        tiles = l // tm
        out_specs += [pl.BlockSpec((None, DIFF_HEADS, LANES, tm), lambda i: (i // tiles, 0, 0, i % tiles)),
                      pl.BlockSpec((tm * DIFF_HEADS, DIFF_DV), lambda i: (i, 0))]
        out_shape += [jax.ShapeDtypeStruct((batch, DIFF_HEADS, LANES, l), F32),
                      jax.ShapeDtypeStruct((n * DIFF_HEADS, DIFF_DV), F32)]
    else:
        out_specs += [row(DIFF_QK), row(DIFF_V), row(DIFF_QK)]
        out_shape += [jax.ShapeDtypeStruct((n, DIFF_QK), F32), jax.ShapeDtypeStruct((n, DIFF_V), F32),
                      jax.ShapeDtypeStruct((n, DIFF_QK), BF16)]
    return pl.pallas_call(
        functools.partial(_pre_kernel, k_pos_minor=k_pos_minor),
        grid=grid,
        in_specs=in_specs,
        out_specs=out_specs,
        out_shape=out_shape,
        compiler_params=pltpu.CompilerParams(dimension_semantics=("arbitrary",),
                                             vmem_limit_bytes=VMEM_LIMIT),
        name="pre",
    )(x2d, *tabs, *weights)


def _post_kernel(x1_ref, og_ref, od_ref, wo_ref, n2_ref, wg_ref, wu_ref, wd_ref, y_ref):
    mixed = jnp.concatenate([og_ref[...], od_ref[...]], axis=1)
    x2 = x1_ref[...] + _dot(mixed, wo_ref[...])
    xn = _rms(x2, n2_ref[...]).astype(BF16)
    y_ref[...] = x2 + 0.5 * _swiglu(xn, wg_ref, wu_ref, wd_ref)


def _post_call(x1, og, od, p, tm):
    n, d = x1.shape
    row = lambda w: pl.BlockSpec((tm, w), lambda i: (i, 0))
    weights = [p['wo'], p['n2'], p['wg2'], p['wu2'], p['wd2']]
    return pl.pallas_call(
        _post_kernel,
        grid=(n // tm,),
        in_specs=[row(d), row(GDN_V), row(DIFF_V)] + [_const_spec(w.shape) for w in weights],
        out_specs=row(d),
        out_shape=jax.ShapeDtypeStruct((n, d), F32),
        compiler_params=pltpu.CompilerParams(dimension_semantics=("arbitrary",),
                                             vmem_limit_bytes=VMEM_LIMIT),
        name="post",
    )(x1, og, od, *weights)


def _unit_lower_inverse(a, c):
    ri = lax.broadcasted_iota(jnp.int32, (c, c), 0)
    ci = lax.broadcasted_iota(jnp.int32, (c, c), 1)
    eye = (ri == ci).astype(F32)
    if c == INV_BLOCK:
        d = a
    else:
        same = (ri // INV_BLOCK) == (ci // INV_BLOCK)
        d = jnp.where(same, a, 0.0)
    assert INV_BLOCK == 16
    d2 = _dot3(d, d)
    d3 = _dot3(d, d2)
    d4 = _dot3(d2, d2)
    n1 = eye - d + d2 - d3
    n2 = n1 + _dot3(n1, d4)
    d8 = _dot3(d4, d4)
    td = n2 + _dot3(n2, d8)
    if c == INV_BLOCK:
        return td
    assert c == 4 * INV_BLOCK
    m = _dot3(td, a - d)
    m2 = _dot3(m, m)
    m3 = _dot3(m, m2)
    return _dot3(eye - m + m2 - m3, td)


def _gdn_kernel(conv_ref, z_ref, ba_ref, hist_ref, s0_ref, cw_ref, alog_ref, dtb_ref, on_ref,
                o_ref, s_out_ref,
                xp_ref, s_ref, q_scr, k_scr, v_scr, g_scr, b_scr, *, tc, c):
    t = pl.program_id(1)
    nt = pl.num_programs(1)
    pad = 8

    @pl.when(t == 0)
    def _():
        xp_ref[0:pad, :] = jnp.zeros((pad, CONV_CH), F32)
        xp_ref[pad - (CONV_W - 1):pad, :] = hist_ref[...]
        s_ref[...] = s0_ref[...]

    x = conv_ref[...]
    xp_ref[pad:pad + tc, :] = x
    cw = cw_ref[...]
    y = xp_ref[pad - 3:pad - 3 + tc, :] * cw[0:1, :]
    for i in range(1, CONV_W):
        y = y + xp_ref[pad - 3 + i:pad - 3 + i + tc, :] * cw[i:i + 1, :]
    xp_ref[pad - (CONV_W - 1):pad, :] = x[tc - (CONV_W - 1):tc, :]
    y = _silu(y)

    for h in range(GDN_HEADS):
        qh = y[:, h * GDN_DK:(h + 1) * GDN_DK]
        kh = y[:, GDN_QK + h * GDN_DK:GDN_QK + (h + 1) * GDN_DK]
        q_scr[:, h * GDN_DK:(h + 1) * GDN_DK] = qh * (
            lax.rsqrt(jnp.sum(qh * qh, axis=-1, keepdims=True) + EPS) * (GDN_DK ** -0.5))
        k_scr[:, h * GDN_DK:(h + 1) * GDN_DK] = kh * lax.rsqrt(
            jnp.sum(kh * kh, axis=-1, keepdims=True) + EPS)
    v_scr[...] = y[:, 2 * GDN_QK:]

    ba = ba_ref[...]
    b_scr[...] = jax.nn.sigmoid(ba[:, :LANES])
    g = -jnp.exp(alog_ref[...]) * jax.nn.softplus(ba[:, LANES:] + dtb_ref[...])
    ri = lax.broadcasted_iota(jnp.int32, (tc, tc), 0)
    ci = lax.broadcasted_iota(jnp.int32, (tc, tc), 1)
    tri = jnp.where((ci <= ri) & ((ri // c) == (ci // c)), 1.0, 0.0).astype(BF16)
    g1 = g.astype(BF16)
    r1 = g - g1.astype(F32)
    g2 = r1.astype(BF16)
    g3 = (r1 - g2.astype(F32)).astype(BF16)
    g_scr[...] = _dot(tri, g1) + (_dot(tri, g2) + _dot(tri, g3))

    rc = lax.broadcasted_iota(jnp.int32, (c, c), 0)
    cc = lax.broadcasted_iota(jnp.int32, (c, c), 1)
    incl = rc >= cc
    strict = rc > cc
    lane0 = lax.broadcasted_iota(jnp.int32, (c, LANES), 1) == 0
    ones_l0 = jnp.where(lane0, 1.0, 0.0).astype(BF16)
    on = on_ref[...]

    def chunk(ic, carry):
        r0 = pl.multiple_of(ic * c, c)
        rows = pl.ds(r0, c)
        g_all = g_scr[rows, :]
        b_all = b_scr[rows, :]
        for h in range(GDN_HEADS):
            cols = slice(h * GDN_DK, (h + 1) * GDN_DK)
            qh = q_scr[rows, cols]
            kh = k_scr[rows, cols]
            vh = v_scr[rows, cols]
            gc = g_all[:, h:h + 1]
            bc = b_all[:, h:h + 1]
            gx = jnp.where(lane0, gc, 0.0)
            x1 = gx.astype(BF16)
            xr = gx - x1.astype(F32)
            x2 = xr.astype(BF16)
            x3 = (xr - x2.astype(F32)).astype(BF16)
            gr = _dot_nt(ones_l0, x1) + (_dot_nt(ones_l0, x2) + _dot_nt(ones_l0, x3))
            dec = jnp.exp(jnp.where(incl, gc - gr, -jnp.inf))
            khb = kh.astype(BF16)
            kk = _dot_nt(khb, khb)
            a = jnp.where(strict, bc * kk * dec, 0.0)
            tinv = _unit_lower_inverse(a, c)
            eg = jnp.exp(gc)
            rhs = jnp.concatenate([vh * bc, kh * (bc * eg)], axis=1)
            sol = _dot3(tinv, rhs)
            u0 = sol[:, :GDN_DV]
            w = sol[:, GDN_DV:]
            qk = _dot_nt(qh.astype(BF16), khb) * dec
            qd = qh * eg
            g_last = g_all[c - 1:c, h:h + 1]
            kd = kh * jnp.exp(g_last - gc)
            s = s_ref[h]
            sb = s.astype(BF16)
            ws = _dot(jnp.concatenate([w, qd], axis=0).astype(BF16), sb)
            u = u0 - ws[:c]
            ub = u.astype(BF16)
            o = ws[c:] + _dot(qk.astype(BF16), ub)
            s_ref[h] = s * jnp.exp(g_last) + lax.dot_general(
                kd.astype(BF16), ub, (((0,), (0,)), ((), ())), preferred_element_type=F32)
            zh = z_ref[rows, cols]
            o_ref[rows, cols] = (_rms(o, on) * _silu(zh)).astype(o_ref.dtype)
        return carry

    lax.fori_loop(0, tc // c, chunk, 0)

    @pl.when(t == nt - 1)
    def _():
        s_out_ref[...] = s_ref[...]


def _gdn_call(conv, z, ba, hist, s0, p, tc, c):
    b, l, _ = conv.shape
    row = lambda w: pl.BlockSpec((None, tc, w), lambda i, j: (i, j, 0))
    per_b = lambda shape: pl.BlockSpec((None,) + shape, lambda i, j: (i,) + (0,) * len(shape))
    small = [p['conv_w'], p['a_log'], p['dt_bias'], p['gdn_on']]
    return pl.pallas_call(
        functools.partial(_gdn_kernel, tc=tc, c=c),
        grid=(b, l // tc),
        in_specs=[row(CONV_CH), row(GDN_V), row(2 * LANES), per_b((CONV_W - 1, CONV_CH)),
                  per_b((GDN_HEADS, GDN_DK, GDN_DV))] + [
                      pl.BlockSpec(w.shape, lambda i, j: (0, 0)) for w in small],
        out_specs=[row(GDN_V), per_b((GDN_HEADS, GDN_DK, GDN_DV))],
        out_shape=[jax.ShapeDtypeStruct((b, l, GDN_V), BF16),
                   jax.ShapeDtypeStruct((b, GDN_HEADS, GDN_DK, GDN_DV), F32)],
        scratch_shapes=[pltpu.VMEM((tc + 8, CONV_CH), F32),
                        pltpu.VMEM((GDN_HEADS, GDN_DK, GDN_DV), F32),
                        pltpu.VMEM((tc, GDN_QK), F32), pltpu.VMEM((tc, GDN_QK), F32),
                        pltpu.VMEM((tc, GDN_V), F32),
                        pltpu.VMEM((tc, LANES), F32), pltpu.VMEM((tc, LANES), F32)],
        compiler_params=pltpu.CompilerParams(dimension_semantics=("arbitrary", "arbitrary"),
                                             vmem_limit_bytes=VMEM_LIMIT),
        name="gdn",
    )(conv, z, ba, hist, s0, *small)


def _stack_maps(q):
    lane = lax.broadcasted_iota(jnp.int32, q.shape, 1)
    zero = jnp.zeros_like(q)
    return jnp.concatenate([jnp.where(lane < DIFF_DK, q, zero), jnp.where(lane >= DIFF_DK, q, zero)],
                           axis=0)


def _diff_finish(acc, l, t, lq_ref, lk_ref, dn_ref, lam_init):
    lam_e = jnp.exp(jnp.sum(lq_ref[...] * lk_ref[...], axis=-1, keepdims=True))
    lam = lam_e[0:1] - lam_e[1:2] + lam_init
    o = acc[:t] / l[:t] - lam * (acc[t:] / l[t:])
    return _rms(o, dn_ref[...]) * (1.0 - lam_init)


def _halves_max(s):
    return jnp.maximum(s[:, :LANES], s[:, LANES:])


def _attn_prompt_kernel(q_ref, kt_ref, v_ref, lq_ref, lk_ref, dn_ref, o_ref,
                        kt_scr, m_scr, l_scr, acc_scr, *, tq, tk, lam_init):
    i = pl.program_id(2)
    assert tk == 2 * LANES and tq % tk == 0
    nsub = tq // tk

    @pl.when(i == 0)
    def _():
        for j in range(kt_scr.shape[0]):
            kt_scr[j] = kt_ref[:, j * tk:(j + 1) * tk].astype(BF16)

    qq = _stack_maps(q_ref[...])

    def scores(j):
        return _dot(qq, kt_scr[j])

    def values(j):
        return v_ref[pl.ds(pl.multiple_of(j * tk, tk), tk), :]

    def probs(s, mm):
        return jnp.concatenate([jnp.exp2(s[:, :LANES] - mm), jnp.exp2(s[:, LANES:] - mm)], axis=1)

    r = lax.broadcasted_iota(jnp.int32, (2 * tq, tk), 0)
    r = jnp.where(r >= tq, r - tq, r)
    cidx = lax.broadcasted_iota(jnp.int32, (2 * tq, tk), 1)
    s_diag = [jnp.where((cidx + d * tk) // CHUNK <= r // CHUNK, scores(i * nsub + d), -jnp.inf)
              for d in range(nsub)]
    m_scr[...] = functools.reduce(jnp.maximum, [_halves_max(s) for s in s_diag])

    def pass1(jj, carry):
        hm = [_halves_max(scores(jj * nsub + u)) for u in range(nsub)]
        m_scr[...] = jnp.maximum(m_scr[...], functools.reduce(jnp.maximum, hm))
        return carry

    lax.fori_loop(0, i, pass1, 0)
    m = jnp.broadcast_to(jnp.max(m_scr[...], axis=-1, keepdims=True), m_scr.shape)
    m_scr[...] = m

    def accumulate(ps, vs, first):
        lsum = functools.reduce(jnp.add, [p[:, :LANES] + p[:, LANES:] for p in ps])
        pv = functools.reduce(jnp.add, [_dot(p.astype(BF16), v) for p, v in zip(ps, vs)])
        if first:
            l_scr[...] = lsum
            acc_scr[...] = pv
        else:
            l_scr[...] += lsum
            acc_scr[...] += pv

    accumulate([probs(s, m) for s in s_diag], [values(i * nsub + d) for d in range(nsub)], True)

    def pass2(jj, carry):
        mm = m_scr[...]
        accumulate([probs(scores(jj * nsub + u), mm) for u in range(nsub)],
                   [values(jj * nsub + u) for u in range(nsub)], False)
        return carry

    lax.fori_loop(0, i, pass2, 0)
    l = jnp.sum(l_scr[...], axis=-1, keepdims=True)
    o_ref[...] = _diff_finish(acc_scr[...], l, tq, lq_ref, lk_ref, dn_ref, lam_init).astype(o_ref.dtype)


def _attn_prompt_call(qb, kt, vb, p, tq, tk, lam_init):
    b, l, _ = qb.shape
    small = [p['lambda_q'], p['lambda_k'], p['diff_on']]
    return pl.pallas_call(
        functools.partial(_attn_prompt_kernel, tq=tq, tk=tk, lam_init=lam_init),
        grid=(b, DIFF_HEADS, l // tq),
        in_specs=[pl.BlockSpec((None, tq, LANES), lambda bi, h, i: (bi, i, h)),
                  pl.BlockSpec((None, None, LANES, l), lambda bi, h, i: (bi, h, 0, 0)),
                  pl.BlockSpec((None, l, LANES), lambda bi, h, i: (bi, 0, h))] + [
                      pl.BlockSpec(w.shape, lambda bi, h, i: (0, 0)) for w in small],
        out_specs=pl.BlockSpec((None, tq, LANES), lambda bi, h, i: (bi, i, h)),
        out_shape=jax.ShapeDtypeStruct((b, l, DIFF_V), BF16),
        scratch_shapes=[pltpu.VMEM((l // tk, LANES, tk), BF16),
                        pltpu.VMEM((2 * tq, LANES), F32), pltpu.VMEM((2 * tq, LANES), F32),
                        pltpu.VMEM((2 * tq, LANES), F32)],
        compiler_params=pltpu.CompilerParams(
            dimension_semantics=("arbitrary", "arbitrary", "arbitrary"),
            vmem_limit_bytes=VMEM_LIMIT),
        name="attn_prompt",
    )(qb, kt, vb, *small)


def _attn_sample_kernel(q_ref, kn_ref, vn_ref, ckt_ref, cv_ref, lq_ref, lk_ref, dn_ref, o_ref,
                        *, t, lam_init):
    past = ckt_ref.shape[-1]
    for h in range(DIFF_HEADS):
        cols = slice(h * LANES, (h + 1) * LANES)
        qq = _stack_maps(q_ref[:, cols])
        s_c = _dot(qq, ckt_ref[h].astype(BF16))
        s_n = _dot_nt(qq, kn_ref[:, cols])
        m = jnp.maximum(jnp.max(s_c, axis=-1, keepdims=True), jnp.max(s_n, axis=-1, keepdims=True))
        p_c = jnp.exp2(s_c - m)
        p_n = jnp.exp2(s_n - m)
        l = jnp.sum(p_c, axis=-1, keepdims=True) + jnp.sum(p_n, axis=-1, keepdims=True)
        cv = cv_ref[pl.ds(h, past, stride=DIFF_HEADS), :].astype(BF16)
        acc = _dot(p_c.astype(BF16), cv) + _dot(p_n.astype(BF16), vn_ref[:, cols])
        o_ref[:, cols] = _diff_finish(acc, l, t, lq_ref, lk_ref, dn_ref, lam_init).astype(o_ref.dtype)


def _attn_sample_call(qb, kb, vb, ckt, cv, p, lam_init):
    b, t, _ = qb.shape
    past = ckt.shape[-1]
    small = [p['lambda_q'], p['lambda_k'], p['diff_on']]
    new = pl.BlockSpec((None, t, DIFF_V), lambda bi: (bi, 0, 0))
    return pl.pallas_call(
        functools.partial(_attn_sample_kernel, t=t, lam_init=lam_init),
        grid=(b,),
        in_specs=[new, new, new,
                  pl.BlockSpec((None, DIFF_HEADS, LANES, past), lambda bi: (bi, 0, 0, 0)),
                  pl.BlockSpec((None, past * DIFF_HEADS, DIFF_DV), lambda bi: (bi, 0, 0))] + [
                      pl.BlockSpec(w.shape, lambda bi: (0, 0)) for w in small],
        out_specs=new,
        out_shape=jax.ShapeDtypeStruct((b, t, DIFF_V), BF16),
        compiler_params=pltpu.CompilerParams(dimension_semantics=("arbitrary",),
                                             vmem_limit_bytes=VMEM_LIMIT),
        name="attn_sample",
    )(qb, kb, vb, ckt, cv, *small)


def _rope_tables(pos):
    half = ROT_DIM // 2
    inv = jnp.float32(ROPE_THETA) ** (-jnp.arange(half, dtype=F32) * 2.0 / ROT_DIM)
    ang = pos.astype(F32)[:, None] * inv[None, :]
    cos, sin = jnp.cos(ang), jnp.sin(ang)
    n = pos.shape[0]
    rest = DIFF_DK - ROT_DIM
    one = jnp.ones((n, rest), F32)
    zero = jnp.zeros((n, rest), F32)
    zh = jnp.zeros((n, half), F32)
    c64 = jnp.concatenate([cos, cos, one], axis=1)
    a64 = jnp.concatenate([-sin, zh, zero], axis=1)
    b64 = jnp.concatenate([zh, sin, zero], axis=1)
    rep = LANES // DIFF_DK
    return tuple(jnp.tile(t, (1, rep)) for t in (c64, a64, b64))


def _layer_params(w, l):
    d = w['w_in'].shape[1]
    d_ff = w['ffn1_w_down'].shape[1]
    win = w['w_in'][l]
    o = 0
    w_conv = win[:, o:o + CONV_CH]; o += CONV_CH
    w_b = win[:, o:o + GDN_HEADS]; o += GDN_HEADS
    w_a = win[:, o:o + GDN_HEADS]; o += GDN_HEADS
    w_z = win[:, o:o + GDN_V]; o += GDN_V
    w_q = win[:, o:o + DIFF_QK]; o += DIFF_QK
    w_k = win[:, o:o + DIFF_QK]; o += DIFF_QK
    w_v = win[:, o:o + DIFF_V]
    zpad = jnp.zeros((d, LANES - GDN_HEADS), win.dtype)
    w_ba = jnp.concatenate([w_b, zpad, w_a, zpad], axis=1)
    hpad = lambda v: jnp.concatenate([v.astype(F32), jnp.zeros((LANES - GDN_HEADS,), F32)])[None, :]
    grp = jnp.arange(DIFF_QK) // DIFF_DK
    bf = lambda t: t.astype(BF16)
    r2 = lambda v: v.astype(F32)[None, :]
    return dict(
        n1=r2(w['ffn1_norm'][l]), wg1=bf(w['ffn1_w_gu'][l][:, :d_ff]), wu1=bf(w['ffn1_w_gu'][l][:, d_ff:]),
        wd1=bf(w['ffn1_w_down'][l]),
        nm=r2(w['mix_norm'][l]), w_conv=bf(w_conv), w_z=bf(w_z), w_ba=bf(w_ba), w_q=bf(w_q), w_k=bf(w_k),
        w_v=bf(w_v),
        qn=r2(jnp.tile(w['q_norm'][l], DIFF_QK // DIFF_DK)), kn=r2(jnp.tile(w['k_norm'][l], DIFF_QK // DIFF_DK)),
        gmat=(grp[:, None] == grp[None, :]).astype(BF16),
        conv_w=w['conv_w'][l].astype(F32), a_log=hpad(w['a_log'][l]), dt_bias=hpad(w['dt_bias'][l]),
        gdn_on=r2(w['gdn_out_norm'][l]),
        lambda_q=w['lambda_q'][l].astype(F32), lambda_k=w['lambda_k'][l].astype(F32),
        diff_on=r2(w['diff_out_norm'][l]),
        wo=bf(w['w_out'][l]), n2=r2(w['ffn2_norm'][l]), wg2=bf(w['ffn2_w_gu'][l][:, :d_ff]),
        wu2=bf(w['ffn2_w_gu'][l][:, d_ff:]), wd2=bf(w['ffn2_w_down'][l]),
    )


def _pick_tile(n, pref):
    t = min(n, pref)
    assert n % t == 0
    return t


def _layer(x, pos, k_hist, v_hist, conv_hist, s0, p, lam_init):
    b, l, d = x.shape
    n = b * l
    tm = _pick_tile(n, 256)
    tabs = _rope_tables(pos)
    if l >= tm:
        assert l % tm == 0
        n_pos_tiles = l // tm
    else:
        assert tm % l == 0
        tabs = tuple(jnp.tile(t, (tm // l, 1)) for t in tabs)
        n_pos_tiles = 1
    prompt = k_hist is None
    pre = _pre_call(x.reshape(n, d), tabs, p, tm, n_pos_tiles, b, prompt)
    x1, conv, z, ba, qb, vb, kf, vf = pre[:8]
    c = min(CHUNK, l)
    assert c in (INV_BLOCK, 4 * INV_BLOCK) and l % c == 0
    tc = _pick_tile(l, 4 * c)
    if conv_hist is None:
        conv_hist = jnp.zeros((b, CONV_W - 1, CONV_CH), F32)
        s0 = jnp.zeros((b, GDN_HEADS, GDN_DK, GDN_DV), F32)
    conv3 = conv.reshape(b, l, CONV_CH)
    og, s_new = _gdn_call(conv3, z.reshape(b, l, GDN_V), ba.reshape(b, l, 2 * LANES),
                          conv_hist.astype(F32), s0.astype(F32), p, tc, c)
    conv_new = jnp.concatenate([conv_hist.astype(F32), conv3], axis=1)[:, -(CONV_W - 1):]
    q3 = qb.reshape(b, l, DIFF_QK)
    v3 = vb.reshape(b, l, DIFF_V)
    if prompt:
        tk = 2 * LANES
        od = _attn_prompt_call(q3, kf, v3, p, _pick_tile(l, 2 * tk), tk, lam_init)
        k_out = jnp.transpose(kf.reshape(b, DIFF_HEADS, 2, DIFF_DK, l), (0, 4, 1, 2, 3))
    else:
        past = k_hist.shape[1]
        ckt = jnp.transpose(k_hist, (0, 2, 3, 4, 1)).reshape(b, DIFF_HEADS, LANES, past)
        od = _attn_sample_call(q3, pre[8].reshape(b, l, DIFF_QK), v3, ckt,
                               v_hist.reshape(b, past * DIFF_HEADS, DIFF_DV), p, lam_init)
        k_out = kf.reshape(b, l, DIFF_HEADS, 2, DIFF_DK)
    y = _post_call(x1, og.reshape(n, GDN_V), od.reshape(n, DIFF_V), p, tm)
    return (y.reshape(b, l, d), k_out, vf.reshape(b, l, DIFF_HEADS, DIFF_DV), s_new, conv_new)


def kernel(x_prompt, x_sample, cache_k, cache_v, state_gdn, state_conv, ffn1_norm, ffn1_w_gu, ffn1_w_down,
           mix_norm, w_in, conv_w, a_log, dt_bias, gdn_out_norm, q_norm, k_norm, lambda_q, lambda_k,
           diff_out_norm, w_out, ffn2_norm, ffn2_w_gu, ffn2_w_down):
    w = dict(ffn1_norm=ffn1_norm, ffn1_w_gu=ffn1_w_gu, ffn1_w_down=ffn1_w_down, mix_norm=mix_norm, w_in=w_in,
             conv_w=conv_w, a_log=a_log, dt_bias=dt_bias, gdn_out_norm=gdn_out_norm, q_norm=q_norm,
             k_norm=k_norm, lambda_q=lambda_q, lambda_k=lambda_k, diff_out_norm=diff_out_norm, w_out=w_out,
             ffn2_norm=ffn2_norm, ffn2_w_gu=ffn2_w_gu, ffn2_w_down=ffn2_w_down)
    depth = w_in.shape[0]
    pos_p = jnp.arange(x_prompt.shape[1])
    pos_s = cache_k.shape[2] + jnp.arange(x_sample.shape[1])
    hp, hs = x_prompt, x_sample
    outs = [[] for _ in range(8)]
    for l in range(depth):
        lam_init = 0.8 - 0.6 * math.exp(-0.3 * l)
        p = _layer_params(w, l)
        hp, kp, vp, sp, cp = _layer(hp, pos_p, None, None, None, None, p, lam_init)
        hs, ks, vs, ss, cs = _layer(hs, pos_s, cache_k[l], cache_v[l], state_conv[l], state_gdn[l], p, lam_init)
        for acc, val in zip(outs, (kp, vp, sp, cp, ks, vs, ss, cs)):
            acc.append(val)
    return (hp, hs) + tuple(jnp.stack(o) for o in outs)
```

```python
import functools
import math

import jax
import jax.numpy as jnp
from jax import lax
from jax.experimental import pallas as pl
from jax.experimental.pallas import tpu as pltpu

F32 = jnp.float32
BF16 = jnp.bfloat16

EPS = 1e-6
CHUNK = 64
GDN_HEADS = 4
GDN_DK = 128
GDN_DV = 128
CONV_W = 4
DIFF_HEADS = 4
DIFF_DK = 64
DIFF_DV = 128
ROT_DIM = DIFF_DK // 4
ROPE_THETA = 500000.0
GDN_QK = GDN_HEADS * GDN_DK
GDN_V = GDN_HEADS * GDN_DV
CONV_CH = 2 * GDN_QK + GDN_V
DIFF_QK = DIFF_HEADS * 2 * DIFF_DK
DIFF_V = DIFF_HEADS * DIFF_DV

LANES = 128
INV_BLOCK = 16
VMEM_LIMIT = 56 * 1024 * 1024
Q_SCALE = (DIFF_DK ** -0.5) * math.log2(math.e)


def _dot(a, b):
    return jnp.dot(a, b, preferred_element_type=F32)


def _dot_nt(a, b):
    return lax.dot_general(a, b, (((1,), (1,)), ((), ())), preferred_element_type=F32)


def _split(x):
    hi = x.astype(BF16)
    lo = (x - hi.astype(F32)).astype(BF16)
    return hi, lo


def _dot3(x, y):
    xh, xl = _split(x)
    yh, yl = _split(y)
    return _dot(xh, yh) + (_dot(xl, yh) + _dot(xh, yl))


def _rms(x, g):
    return x * lax.rsqrt(jnp.mean(x * x, axis=-1, keepdims=True) + EPS) * g


def _silu(x):
    return x * jax.nn.sigmoid(x)


def _swiglu(xn, wg_ref, wu_ref, wd_ref):
    g = _dot(xn, wg_ref[...])
    u = _dot(xn, wu_ref[...])
    act = (_silu(g) * u).astype(BF16)
    return _dot(act, wd_ref[...])


def _const_spec(shape):
    nd = len(shape)
    return pl.BlockSpec(shape, lambda *_: (0,) * nd, pipeline_mode=pl.Buffered(1))


def _pre_kernel(x_ref, cos_ref, sa_ref, sb_ref, n1_ref, wg_ref, wu_ref, wd_ref, nm_ref,
                wc_ref, wz_ref, wba_ref, wq_ref, wk_ref, wv_ref, qn_ref, kn_ref, gm_ref,
                x1_ref, conv_ref, z_ref, ba_ref, qb_ref, vb_ref, kf_ref, vf_ref, *maybe_kb_ref,
                k_pos_minor):
    x = x_ref[...]
    xn = _rms(x, n1_ref[...]).astype(BF16)
    x1 = x + 0.5 * _swiglu(xn, wg_ref, wu_ref, wd_ref)
    x1_ref[...] = x1
    h = _rms(x1, nm_ref[...]).astype(BF16)
    conv_ref[...] = _dot(h, wc_ref[...])
    z_ref[...] = _dot(h, wz_ref[...])
    ba_ref[...] = _dot(h, wba_ref[...])
    v = _dot(h, wv_ref[...])
    vb_ref[...] = v.astype(BF16)
    tm = v.shape[0]
    if k_pos_minor:
        for hh in range(DIFF_HEADS):
            vf_ref[pl.ds(hh, tm, stride=DIFF_HEADS), :] = v[:, hh * DIFF_DV:(hh + 1) * DIFF_DV]
    else:
        vf_ref[...] = v

    cos = cos_ref[...]
    sa = sa_ref[...]
    sb = sb_ref[...]
    gm = gm_ref[...]

    def norm_rope(t, gw):
        ss = _dot((t * t).astype(BF16), gm)
        t = t * lax.rsqrt(ss * (1.0 / DIFF_DK) + EPS) * gw
        outs = []
        for hh in range(DIFF_HEADS):
            th = t[:, hh * LANES:(hh + 1) * LANES]
            up = pltpu.roll(th, LANES - ROT_DIM // 2, 1)
            dn = pltpu.roll(th, ROT_DIM // 2, 1)
            outs.append(th * cos + up * sa + dn * sb)
        return jnp.concatenate(outs, axis=1)

    q = norm_rope(_dot(h, wq_ref[...]), qn_ref[...])
    qb_ref[...] = (q * Q_SCALE).astype(BF16)
    k = norm_rope(_dot(h, wk_ref[...]), kn_ref[...])
    if k_pos_minor:
        kf_ref[...] = k.T.reshape(DIFF_HEADS, LANES, tm)
    else:
        kf_ref[...] = k
        maybe_kb_ref[0][...] = k.astype(BF16)


def _pre_call(x2d, tabs, p, tm, n_pos_tiles, batch, k_pos_minor):
    n, d = x2d.shape
    grid = (n // tm,)
    row = lambda w: pl.BlockSpec((tm, w), lambda i: (i, 0))
    tab = pl.BlockSpec((tm, LANES), lambda i: (i % n_pos_tiles, 0))
    weights = [p['n1'], p['wg1'], p['wu1'], p['wd1'], p['nm'], p['w_conv'], p['w_z'], p['w_ba'],
               p['w_q'], p['w_k'], p['w_v'], p['qn'], p['kn'], p['gmat']]
    in_specs = [row(d), tab, tab, tab] + [_const_spec(w.shape) for w in weights]
    outs = [(d, F32), (CONV_CH, F32), (GDN_V, F32), (2 * LANES, F32), (DIFF_QK, BF16), (DIFF_V, BF16)]
    out_specs = [row(w) for w, _ in outs]
    out_shape = [jax.ShapeDtypeStruct((n, w), dt) for w, dt in outs]
    if k_pos_minor:
        l = n // batch
        tiles = l // tm
        out_specs += [pl.BlockSpec((None, DIFF_HEADS, LANES, tm), lambda i: (i // tiles, 0, 0, i % tiles)),
                      pl.BlockSpec((tm * DIFF_HEADS, DIFF_DV), lambda i: (i, 0))]
        out_shape += [jax.ShapeDtypeStruct((batch, DIFF_HEADS, LANES, l), F32),
                      jax.ShapeDtypeStruct((n * DIFF_HEADS, DIFF_DV), F32)]
    else:
        out_specs += [row(DIFF_QK), row(DIFF_V), row(DIFF_QK)]
        out_shape += [jax.ShapeDtypeStruct((n, DIFF_QK), F32), jax.ShapeDtypeStruct((n, DIFF_V), F32),
                      jax.ShapeDtypeStruct((n, DIFF_QK), BF16)]
    return pl.pallas_call(
        functools.partial(_pre_kernel, k_pos_minor=k_pos_minor),
        grid=grid,
        in_specs=in_specs,
        out_specs=out_specs,
        out_shape=out_shape,
        compiler_params=pltpu.CompilerParams(dimension_semantics=("arbitrary",),
                                             vmem_limit_bytes=VMEM_LIMIT),
        name="pre",
    )(x2d, *tabs, *weights)


def _post_kernel(x1_ref, og_ref, od_ref, wo_ref, n2_ref, wg_ref, wu_ref, wd_ref, y_ref):
    mixed = jnp.concatenate([og_ref[...], od_ref[...]], axis=1)
    x2 = x1_ref[...] + _dot(mixed, wo_ref[...])
    xn = _rms(x2, n2_ref[...]).astype(BF16)
    y_ref[...] = x2 + 0.5 * _swiglu(xn, wg_ref, wu_ref, wd_ref)


def _post_call(x1, og, od, p, tm):
    n, d = x1.shape
    row = lambda w: pl.BlockSpec((tm, w), lambda i: (i, 0))
    weights = [p['wo'], p['n2'], p['wg2'], p['wu2'], p['wd2']]
    return pl.pallas_call(
        _post_kernel,
        grid=(n // tm,),
        in_specs=[row(d), row(GDN_V), row(DIFF_V)] + [_const_spec(w.shape) for w in weights],
        out_specs=row(d),
        out_shape=jax.ShapeDtypeStruct((n, d), F32),
        compiler_params=pltpu.CompilerParams(dimension_semantics=("arbitrary",),
                                             vmem_limit_bytes=VMEM_LIMIT),
        name="post",
    )(x1, og, od, *weights)


def _unit_lower_inverse(a, c):
    ri = lax.broadcasted_iota(jnp.int32, (c, c), 0)
    ci = lax.broadcasted_iota(jnp.int32, (c, c), 1)
    eye = (ri == ci).astype(F32)
    if c == INV_BLOCK:
        d = a
    else:
        same = (ri // INV_BLOCK) == (ci // INV_BLOCK)
        d = jnp.where(same, a, 0.0)
    assert INV_BLOCK == 16
    d2 = _dot3(d, d)
    d3 = _dot3(d, d2)
    d4 = _dot3(d2, d2)
    n1 = eye - d + d2 - d3
    n2 = n1 + _dot3(n1, d4)
    d8 = _dot3(d4, d4)
    td = n2 + _dot3(n2, d8)
    if c == INV_BLOCK:
        return td
    assert c == 4 * INV_BLOCK
    m = _dot3(td, a - d)
    m2 = _dot3(m, m)
    m3 = _dot3(m, m2)
    return _dot3(eye - m + m2 - m3, td)


def _gdn_kernel(conv_ref, z_ref, ba_ref, hist_ref, s0_ref, cw_ref, alog_ref, dtb_ref, on_ref,
                o_ref, s_out_ref,
                xp_ref, s_ref, q_scr, k_scr, v_scr, g_scr, b_scr, *, tc, c):
    t = pl.program_id(1)
    nt = pl.num_programs(1)
    pad = 8

    @pl.when(t == 0)
    def _():
        xp_ref[0:pad, :] = jnp.zeros((pad, CONV_CH), F32)
        xp_ref[pad - (CONV_W - 1):pad, :] = hist_ref[...]
        s_ref[...] = s0_ref[...]

    x = conv_ref[...]
    xp_ref[pad:pad + tc, :] = x
    cw = cw_ref[...]
    y = xp_ref[pad - 3:pad - 3 + tc, :] * cw[0:1, :]
    for i in range(1, CONV_W):
        y = y + xp_ref[pad - 3 + i:pad - 3 + i + tc, :] * cw[i:i + 1, :]
    xp_ref[pad - (CONV_W - 1):pad, :] = x[tc - (CONV_W - 1):tc, :]
    y = _silu(y)

    for h in range(GDN_HEADS):
        qh = y[:, h * GDN_DK:(h + 1) * GDN_DK]
        kh = y[:, GDN_QK + h * GDN_DK:GDN_QK + (h + 1) * GDN_DK]
        q_scr[:, h * GDN_DK:(h + 1) * GDN_DK] = qh * (
            lax.rsqrt(jnp.sum(qh * qh, axis=-1, keepdims=True) + EPS) * (GDN_DK ** -0.5))
        k_scr[:, h * GDN_DK:(h + 1) * GDN_DK] = kh * lax.rsqrt(
            jnp.sum(kh * kh, axis=-1, keepdims=True) + EPS)
    v_scr[...] = y[:, 2 * GDN_QK:]

    ba = ba_ref[...]
    b_scr[...] = jax.nn.sigmoid(ba[:, :LANES])
    g = -jnp.exp(alog_ref[...]) * jax.nn.softplus(ba[:, LANES:] + dtb_ref[...])
    ri = lax.broadcasted_iota(jnp.int32, (tc, tc), 0)
    ci = lax.broadcasted_iota(jnp.int32, (tc, tc), 1)
    tri = jnp.where((ci <= ri) & ((ri // c) == (ci // c)), 1.0, 0.0).astype(BF16)
    g1 = g.astype(BF16)
    r1 = g - g1.astype(F32)
    g2 = r1.astype(BF16)
    g3 = (r1 - g2.astype(F32)).astype(BF16)
    g_scr[...] = _dot(tri, g1) + (_dot(tri, g2) + _dot(tri, g3))

    rc = lax.broadcasted_iota(jnp.int32, (c, c), 0)
    cc = lax.broadcasted_iota(jnp.int32, (c, c), 1)
    incl = rc >= cc
    strict = rc > cc
    lane0 = lax.broadcasted_iota(jnp.int32, (c, LANES), 1) == 0
    ones_l0 = jnp.where(lane0, 1.0, 0.0).astype(BF16)
    on = on_ref[...]

    def chunk(ic, carry):
        r0 = pl.multiple_of(ic * c, c)
        rows = pl.ds(r0, c)
        g_all = g_scr[rows, :]
        b_all = b_scr[rows, :]
        for h in range(GDN_HEADS):
            cols = slice(h * GDN_DK, (h + 1) * GDN_DK)
            qh = q_scr[rows, cols]
            kh = k_scr[rows, cols]
            vh = v_scr[rows, cols]
            gc = g_all[:, h:h + 1]
            bc = b_all[:, h:h + 1]
            gx = jnp.where(lane0, gc, 0.0)
            x1 = gx.astype(BF16)
            xr = gx - x1.astype(F32)
            x2 = xr.astype(BF16)
            x3 = (xr - x2.astype(F32)).astype(BF16)
            gr = _dot_nt(ones_l0, x1) + (_dot_nt(ones_l0, x2) + _dot_nt(ones_l0, x3))
            dec = jnp.exp(jnp.where(incl, gc - gr, -jnp.inf))
            khb = kh.astype(BF16)
            kk = _dot_nt(khb, khb)
            a = jnp.where(strict, bc * kk * dec, 0.0)
            tinv = _unit_lower_inverse(a, c)
            eg = jnp.exp(gc)
            rhs = jnp.concatenate([vh * bc, kh * (bc * eg)], axis=1)
            sol = _dot3(tinv, rhs)
            u0 = sol[:, :GDN_DV]
            w = sol[:, GDN_DV:]
            qk = _dot_nt(qh.astype(BF16), khb) * dec
            qd = qh * eg
            g_last = g_all[c - 1:c, h:h + 1]
            kd = kh * jnp.exp(g_last - gc)
            s = s_ref[h]
            sb = s.astype(BF16)
            ws = _dot(jnp.concatenate([w, qd], axis=0).astype(BF16), sb)
            u = u0 - ws[:c]
            ub = u.astype(BF16)
            o = ws[c:] + _dot(qk.astype(BF16), ub)
            s_ref[h] = s * jnp.exp(g_last) + lax.dot_general(
                kd.astype(BF16), ub, (((0,), (0,)), ((), ())), preferred_element_type=F32)
            zh = z_ref[rows, cols]
            o_ref[rows, cols] = (_rms(o, on) * _silu(zh)).astype(o_ref.dtype)
        return carry

    lax.fori_loop(0, tc // c, chunk, 0)

    @pl.when(t == nt - 1)
    def _():
        s_out_ref[...] = s_ref[...]


class _Split:
    def __init__(self, x):
        self.hi = x.astype(BF16)
        self.lo = (x - self.hi.astype(F32)).astype(BF16)


def _dot3s(a, b):
    return _dot(a.hi, b.hi) + (_dot(a.lo, b.hi) + _dot(a.hi, b.lo))


def _exact3(m01, x):
    x1 = x.astype(BF16)
    r1 = x - x1.astype(F32)
    x2 = r1.astype(BF16)
    x3 = (r1 - x2.astype(F32)).astype(BF16)
    return _dot(m01, x1) + (_dot(m01, x2) + _dot(m01, x3))


def _gdn_tile_kernel(conv_ref, z_ref, ba_ref, hist_ref, s0_ref, cw_ref, alog_ref, dtb_ref, on_ref,
                     o_ref, s_out_ref, xp_ref, s_ref, *, tc, c):
    t = pl.program_id(1)
    nt = pl.num_programs(1)
    pad = 8
    assert c == 4 * INV_BLOCK and tc % c == 0 and tc % LANES == 0

    @pl.when(t == 0)
    def _():
        xp_ref[0:pad, :] = jnp.zeros((pad, CONV_CH), F32)
        xp_ref[pad - (CONV_W - 1):pad, :] = hist_ref[...]
        s_ref[...] = s0_ref[...]

    x = conv_ref[...]
    xp_ref[pad:pad + tc, :] = x
    cw = cw_ref[...]
    y = xp_ref[pad - 3:pad - 3 + tc, :] * cw[0:1, :]
    for i in range(1, CONV_W):
        y = y + xp_ref[pad - 3 + i:pad - 3 + i + tc, :] * cw[i:i + 1, :]
    xp_ref[pad - (CONV_W - 1):pad, :] = x[tc - (CONV_W - 1):tc, :]
    y = _silu(y)

    ba = ba_ref[...]
    beta = jax.nn.sigmoid(ba[:, :LANES])
    g = -jnp.exp(alog_ref[...]) * jax.nn.softplus(ba[:, LANES:] + dtb_ref[...])
    ri = lax.broadcasted_iota(jnp.int32, (tc, tc), 0)
    ci = lax.broadcasted_iota(jnp.int32, (tc, tc), 1)
    same_chunk = (ri // c) == (ci // c)
    incl = same_chunk & (ci <= ri)
    same_blk = (ri // INV_BLOCK) == (ci // INV_BLOCK)
    diag = ri == ci
    eye = jnp.where(diag, 1.0, 0.0)
    gcum = _exact3(jnp.where(incl, 1.0, 0.0).astype(BF16), g)
    gend = _exact3(jnp.where(same_chunk, 1.0, 0.0).astype(BF16), g)
    gcum_t = gcum.T
    on = on_ref[...]

    heads = range(GDN_HEADS)

    def per_head(f, *lists):
        return [f(*args) for args in zip(*lists)]

    qs = [y[:, h * GDN_DK:(h + 1) * GDN_DK] for h in heads]
    ks = [y[:, GDN_QK + h * GDN_DK:GDN_QK + (h + 1) * GDN_DK] for h in heads]
    vs = [y[:, 2 * GDN_QK + h * GDN_DV:2 * GDN_QK + (h + 1) * GDN_DV] for h in heads]
    qs = per_head(lambda q: q * (lax.rsqrt(jnp.sum(q * q, axis=-1, keepdims=True) + EPS) * (GDN_DK ** -0.5)), qs)
    ks = per_head(lambda k: k * lax.rsqrt(jnp.sum(k * k, axis=-1, keepdims=True) + EPS), ks)
    gcs = [gcum[:, h:h + 1] for h in heads]
    bcs = [beta[:, h:h + 1] for h in heads]
    ges = [gend[:, h:h + 1] for h in heads]
    decs = [jnp.exp(jnp.where(incl, gcs[h] - gcum_t[h:h + 1, :], -jnp.inf)) for h in heads]
    kbs = per_head(lambda k: k.astype(BF16), ks)
    a_s = per_head(lambda bc, kb, dec: jnp.where(diag, 0.0, bc * _dot_nt(kb, kb) * dec), bcs, kbs, decs)
    qks = per_head(lambda q, kb, dec: (_dot_nt(q.astype(BF16), kb) * dec).astype(BF16), qs, kbs, decs)

    split = lambda xs: per_head(_Split, xs)
    mm = lambda xs, ys: per_head(_dot3s, xs, ys)
    ds = per_head(lambda a: jnp.where(same_blk, a, 0.0), a_s)
    sd = split(ds)
    d2 = mm(sd, sd)
    s2 = split(d2)
    d3 = mm(sd, s2)
    d4 = mm(s2, s2)
    s4 = split(d4)
    n1 = per_head(lambda d, x2, x3: eye - d + x2 - x3, ds, d2, d3)
    n2 = per_head(jnp.add, n1, mm(split(n1), s4))
    d8 = mm(s4, s4)
    td = per_head(jnp.add, n2, mm(split(n2), split(d8)))
    std = split(td)
    ms = mm(std, split(per_head(jnp.subtract, a_s, ds)))
    sm = split(ms)
    m2 = mm(sm, sm)
    m3 = mm(sm, split(m2))
    tinv = mm(split(per_head(lambda m, x2, x3: eye - m + x2 - x3, ms, m2, m3)), std)

    egs = per_head(jnp.exp, gcs)
    rhs = per_head(lambda v, k, bc, eg: jnp.concatenate([v * bc, k * (bc * eg)], axis=1), vs, ks, bcs, egs)
    sols = mm(split(tinv), split(rhs))
    u0s = [sol[:, :GDN_DV] for sol in sols]
    wbs = [sol[:, GDN_DV:].astype(BF16) for sol in sols]
    qds = per_head(lambda q, eg: (q * eg).astype(BF16), qs, egs)
    kds = per_head(lambda k, ge, gc: (k * jnp.exp(ge - gc)).astype(BF16), ks, ges, gcs)

    nchunk = tc // c
    states = [s_ref[h] for h in heads]
    us = [[] for _ in heads]
    outs = [[] for _ in heads]
    for ic in range(nchunk):
        rs = slice(ic * c, (ic + 1) * c)
        wss = [_dot(jnp.concatenate([wbs[h][rs], qds[h][rs]], axis=0), states[h].astype(BF16))
               for h in heads]
        for h in heads:
            us[h].append((u0s[h][rs] - wss[h][:c]).astype(BF16))
        zpad = [jnp.zeros((tc - (ic + 1) * c, GDN_DV), BF16)] if ic + 1 < nchunk else []
        for h in heads:
            outs[h].append(wss[h][c:] + _dot(qks[h][rs], jnp.concatenate(us[h] + zpad, axis=0)))
        states = [states[h] * jnp.exp(ges[h][ic * c:ic * c + 1, :]) + lax.dot_general(
            kds[h][rs], us[h][ic], (((0,), (0,)), ((), ())), preferred_element_type=F32) for h in heads]
    for h in heads:
        s_ref[h] = states[h]
        cols = slice(h * GDN_DV, (h + 1) * GDN_DV)
        o = jnp.concatenate(outs[h], axis=0)
        o_ref[:, cols] = (_rms(o, on) * _silu(z_ref[:, cols])).astype(o_ref.dtype)

    @pl.when(t == nt - 1)
    def _():
        s_out_ref[...] = s_ref[...]


def _gdn_call(conv, z, ba, hist, s0, p, tc, c):
    b, l, _ = conv.shape
    row = lambda w: pl.BlockSpec((None, tc, w), lambda i, j: (i, j, 0))
    per_b = lambda shape: pl.BlockSpec((None,) + shape, lambda i, j: (i,) + (0,) * len(shape))
    small = [p['conv_w'], p['a_log'], p['dt_bias'], p['gdn_on']]
    if tc % LANES == 0 and c == 4 * INV_BLOCK:
        return pl.pallas_call(
            functools.partial(_gdn_tile_kernel, tc=tc, c=c),
            grid=(b, l // tc),
            in_specs=[row(CONV_CH), row(GDN_V), row(2 * LANES), per_b((CONV_W - 1, CONV_CH)),
                      per_b((GDN_HEADS, GDN_DK, GDN_DV))] + [
                          pl.BlockSpec(w.shape, lambda i, j: (0, 0)) for w in small],
            out_specs=[row(GDN_V), per_b((GDN_HEADS, GDN_DK, GDN_DV))],
            out_shape=[jax.ShapeDtypeStruct((b, l, GDN_V), BF16),
                       jax.ShapeDtypeStruct((b, GDN_HEADS, GDN_DK, GDN_DV), F32)],
            scratch_shapes=[pltpu.VMEM((tc + 8, CONV_CH), F32),
                            pltpu.VMEM((GDN_HEADS, GDN_DK, GDN_DV), F32)],
            compiler_params=pltpu.CompilerParams(dimension_semantics=("arbitrary", "arbitrary"),
                                                 vmem_limit_bytes=VMEM_LIMIT),
            name="gdn_tile",
        )(conv, z, ba, hist, s0, *small)
    return pl.pallas_call(
        functools.partial(_gdn_kernel, tc=tc, c=c),
        grid=(b, l // tc),
        in_specs=[row(CONV_CH), row(GDN_V), row(2 * LANES), per_b((CONV_W - 1, CONV_CH)),
                  per_b((GDN_HEADS, GDN_DK, GDN_DV))] + [
                      pl.BlockSpec(w.shape, lambda i, j: (0, 0)) for w in small],
        out_specs=[row(GDN_V), per_b((GDN_HEADS, GDN_DK, GDN_DV))],
        out_shape=[jax.ShapeDtypeStruct((b, l, GDN_V), BF16),
                   jax.ShapeDtypeStruct((b, GDN_HEADS, GDN_DK, GDN_DV), F32)],
        scratch_shapes=[pltpu.VMEM((tc + 8, CONV_CH), F32),
                        pltpu.VMEM((GDN_HEADS, GDN_DK, GDN_DV), F32),
                        pltpu.VMEM((tc, GDN_QK), F32), pltpu.VMEM((tc, GDN_QK), F32),
                        pltpu.VMEM((tc, GDN_V), F32),
                        pltpu.VMEM((tc, LANES), F32), pltpu.VMEM((tc, LANES), F32)],
        compiler_params=pltpu.CompilerParams(dimension_semantics=("arbitrary", "arbitrary"),
                                             vmem_limit_bytes=VMEM_LIMIT),
        name="gdn",
    )(conv, z, ba, hist, s0, *small)


def _stack_maps(q):
    lane = lax.broadcasted_iota(jnp.int32, q.shape, 1)
    zero = jnp.zeros_like(q)
    return jnp.concatenate([jnp.where(lane < DIFF_DK, q, zero), jnp.where(lane >= DIFF_DK, q, zero)],
                           axis=0)


def _diff_finish(acc, l, t, lq_ref, lk_ref, dn_ref, lam_init):
    lam_e = jnp.exp(jnp.sum(lq_ref[...] * lk_ref[...], axis=-1, keepdims=True))
    lam = lam_e[0:1] - lam_e[1:2] + lam_init
    o = acc[:t] / l[:t] - lam * (acc[t:] / l[t:])
    return _rms(o, dn_ref[...]) * (1.0 - lam_init)


def _halves_max(s):
    return jnp.maximum(s[:, :LANES], s[:, LANES:])


def _attn_prompt_kernel(q_ref, kt_ref, v_ref, lq_ref, lk_ref, dn_ref, o_ref,
                        kt_scr, m_scr, l_scr, acc_scr, *, tq, tk, lam_init):
    i = pl.program_id(2)
    assert tk == 2 * LANES and tq % tk == 0
    nsub = tq // tk

    @pl.when(i == 0)
    def _():
        for j in range(kt_scr.shape[0]):
            kt_scr[j] = kt_ref[:, j * tk:(j + 1) * tk].astype(BF16)

    qq = _stack_maps(q_ref[...])

    def scores(j):
        return _dot(qq, kt_scr[j])

    def values(j):
        return v_ref[pl.ds(pl.multiple_of(j * tk, tk), tk), :]

    def probs(s, mm):
        return jnp.concatenate([jnp.exp2(s[:, :LANES] - mm), jnp.exp2(s[:, LANES:] - mm)], axis=1)

    r = lax.broadcasted_iota(jnp.int32, (2 * tq, tk), 0)
    r = jnp.where(r >= tq, r - tq, r)
    cidx = lax.broadcasted_iota(jnp.int32, (2 * tq, tk), 1)
    s_diag = [jnp.where((cidx + d * tk) // CHUNK <= r // CHUNK, scores(i * nsub + d), -jnp.inf)
              for d in range(nsub)]
    m_scr[...] = functools.reduce(jnp.maximum, [_halves_max(s) for s in s_diag])

    def pass1(jj, carry):
        hm = [_halves_max(scores(jj * nsub + u)) for u in range(nsub)]
        m_scr[...] = jnp.maximum(m_scr[...], functools.reduce(jnp.maximum, hm))
        return carry

    lax.fori_loop(0, i, pass1, 0)
    m = jnp.broadcast_to(jnp.max(m_scr[...], axis=-1, keepdims=True), m_scr.shape)
    m_scr[...] = m

    def accumulate(ps, vs, first):
        lsum = functools.reduce(jnp.add, [p[:, :LANES] + p[:, LANES:] for p in ps])
        pv = functools.reduce(jnp.add, [_dot(p.astype(BF16), v) for p, v in zip(ps, vs)])
        if first:
            l_scr[...] = lsum
            acc_scr[...] = pv
        else:
            l_scr[...] += lsum
            acc_scr[...] += pv

    accumulate([probs(s, m) for s in s_diag], [values(i * nsub + d) for d in range(nsub)], True)

    def pass2(jj, carry):
        mm = m_scr[...]
        accumulate([probs(scores(jj * nsub + u), mm) for u in range(nsub)],
                   [values(jj * nsub + u) for u in range(nsub)], False)
        return carry

    lax.fori_loop(0, i, pass2, 0)
    l = jnp.sum(l_scr[...], axis=-1, keepdims=True)
    o_ref[...] = _diff_finish(acc_scr[...], l, tq, lq_ref, lk_ref, dn_ref, lam_init).astype(o_ref.dtype)


def _attn_prompt_call(qb, kt, vb, p, tq, tk, lam_init):
    b, l, _ = qb.shape
    small = [p['lambda_q'], p['lambda_k'], p['diff_on']]
    return pl.pallas_call(
        functools.partial(_attn_prompt_kernel, tq=tq, tk=tk, lam_init=lam_init),
        grid=(b, DIFF_HEADS, l // tq),
        in_specs=[pl.BlockSpec((None, tq, LANES), lambda bi, h, i: (bi, i, h)),
                  pl.BlockSpec((None, None, LANES, l), lambda bi, h, i: (bi, h, 0, 0)),
                  pl.BlockSpec((None, l, LANES), lambda bi, h, i: (bi, 0, h))] + [
                      pl.BlockSpec(w.shape, lambda bi, h, i: (0, 0)) for w in small],
        out_specs=pl.BlockSpec((None, tq, LANES), lambda bi, h, i: (bi, i, h)),
        out_shape=jax.ShapeDtypeStruct((b, l, DIFF_V), BF16),
        scratch_shapes=[pltpu.VMEM((l // tk, LANES, tk), BF16),
                        pltpu.VMEM((2 * tq, LANES), F32), pltpu.VMEM((2 * tq, LANES), F32),
                        pltpu.VMEM((2 * tq, LANES), F32)],
        compiler_params=pltpu.CompilerParams(
            dimension_semantics=("arbitrary", "arbitrary", "arbitrary"),
            vmem_limit_bytes=VMEM_LIMIT),
        name="attn_prompt",
    )(qb, kt, vb, *small)


def _attn_sample_kernel(q_ref, kn_ref, vn_ref, ckt_ref, cv_ref, lq_ref, lk_ref, dn_ref, o_ref,
                        *, t, lam_init):
    past = ckt_ref.shape[-1]
    for h in range(DIFF_HEADS):
        cols = slice(h * LANES, (h + 1) * LANES)
        qq = _stack_maps(q_ref[:, cols])
        s_c = _dot(qq, ckt_ref[h].astype(BF16))
        s_n = _dot_nt(qq, kn_ref[:, cols])
        m = jnp.maximum(jnp.max(s_c, axis=-1, keepdims=True), jnp.max(s_n, axis=-1, keepdims=True))
        p_c = jnp.exp2(s_c - m)
        p_n = jnp.exp2(s_n - m)
        l = jnp.sum(p_c, axis=-1, keepdims=True) + jnp.sum(p_n, axis=-1, keepdims=True)
        cv = cv_ref[pl.ds(h, past, stride=DIFF_HEADS), :].astype(BF16)
        acc = _dot(p_c.astype(BF16), cv) + _dot(p_n.astype(BF16), vn_ref[:, cols])
        o_ref[:, cols] = _diff_finish(acc, l, t, lq_ref, lk_ref, dn_ref, lam_init).astype(o_ref.dtype)


def _attn_sample_call(qb, kb, vb, ckt, cv, p, lam_init):
    b, t, _ = qb.shape
    past = ckt.shape[-1]
    small = [p['lambda_q'], p['lambda_k'], p['diff_on']]
    new = pl.BlockSpec((None, t, DIFF_V), lambda bi: (bi, 0, 0))
    return pl.pallas_call(
        functools.partial(_attn_sample_kernel, t=t, lam_init=lam_init),
        grid=(b,),
        in_specs=[new, new, new,
                  pl.BlockSpec((None, DIFF_HEADS, LANES, past), lambda bi: (bi, 0, 0, 0)),
                  pl.BlockSpec((None, past * DIFF_HEADS, DIFF_DV), lambda bi: (bi, 0, 0))] + [
                      pl.BlockSpec(w.shape, lambda bi: (0, 0)) for w in small],
        out_specs=new,
        out_shape=jax.ShapeDtypeStruct((b, t, DIFF_V), BF16),
        compiler_params=pltpu.CompilerParams(dimension_semantics=("arbitrary",),
                                             vmem_limit_bytes=VMEM_LIMIT),
        name="attn_sample",
    )(qb, kb, vb, ckt, cv, *small)


def _rope_tables(pos):
    half = ROT_DIM // 2
    inv = jnp.float32(ROPE_THETA) ** (-jnp.arange(half, dtype=F32) * 2.0 / ROT_DIM)
    ang = pos.astype(F32)[:, None] * inv[None, :]
    cos, sin = jnp.cos(ang), jnp.sin(ang)
    n = pos.shape[0]
    rest = DIFF_DK - ROT_DIM
    one = jnp.ones((n, rest), F32)
    zero = jnp.zeros((n, rest), F32)
    zh = jnp.zeros((n, half), F32)
    c64 = jnp.concatenate([cos, cos, one], axis=1)
    a64 = jnp.concatenate([-sin, zh, zero], axis=1)
    b64 = jnp.concatenate([zh, sin, zero], axis=1)
    rep = LANES // DIFF_DK
    return tuple(jnp.tile(t, (1, rep)) for t in (c64, a64, b64))


def _layer_params(w, l):
    d = w['w_in'].shape[1]
    d_ff = w['ffn1_w_down'].shape[1]
    win = w['w_in'][l]
    o = 0
    w_conv = win[:, o:o + CONV_CH]; o += CONV_CH
    w_b = win[:, o:o + GDN_HEADS]; o += GDN_HEADS
    w_a = win[:, o:o + GDN_HEADS]; o += GDN_HEADS
    w_z = win[:, o:o + GDN_V]; o += GDN_V
    w_q = win[:, o:o + DIFF_QK]; o += DIFF_QK
    w_k = win[:, o:o + DIFF_QK]; o += DIFF_QK
    w_v = win[:, o:o + DIFF_V]
    zpad = jnp.zeros((d, LANES - GDN_HEADS), win.dtype)
    w_ba = jnp.concatenate([w_b, zpad, w_a, zpad], axis=1)
    hpad = lambda v: jnp.concatenate([v.astype(F32), jnp.zeros((LANES - GDN_HEADS,), F32)])[None, :]
    grp = jnp.arange(DIFF_QK) // DIFF_DK
    bf = lambda t: t.astype(BF16)
    r2 = lambda v: v.astype(F32)[None, :]
    return dict(
        n1=r2(w['ffn1_norm'][l]), wg1=bf(w['ffn1_w_gu'][l][:, :d_ff]), wu1=bf(w['ffn1_w_gu'][l][:, d_ff:]),
        wd1=bf(w['ffn1_w_down'][l]),
        nm=r2(w['mix_norm'][l]), w_conv=bf(w_conv), w_z=bf(w_z), w_ba=bf(w_ba), w_q=bf(w_q), w_k=bf(w_k),
        w_v=bf(w_v),
        qn=r2(jnp.tile(w['q_norm'][l], DIFF_QK // DIFF_DK)), kn=r2(jnp.tile(w['k_norm'][l], DIFF_QK // DIFF_DK)),
        gmat=(grp[:, None] == grp[None, :]).astype(BF16),
        conv_w=w['conv_w'][l].astype(F32), a_log=hpad(w['a_log'][l]), dt_bias=hpad(w['dt_bias'][l]),
        gdn_on=r2(w['gdn_out_norm'][l]),
        lambda_q=w['lambda_q'][l].astype(F32), lambda_k=w['lambda_k'][l].astype(F32),
        diff_on=r2(w['diff_out_norm'][l]),
        wo=bf(w['w_out'][l]), n2=r2(w['ffn2_norm'][l]), wg2=bf(w['ffn2_w_gu'][l][:, :d_ff]),
        wu2=bf(w['ffn2_w_gu'][l][:, d_ff:]), wd2=bf(w['ffn2_w_down'][l]),
    )


def _pick_tile(n, pref):
    t = min(n, pref)
    assert n % t == 0
    return t


def _layer(x, pos, k_hist, v_hist, conv_hist, s0, p, lam_init):
    b, l, d = x.shape
    n = b * l
    tm = _pick_tile(n, 256)
    tabs = _rope_tables(pos)
    if l >= tm:
        assert l % tm == 0
        n_pos_tiles = l // tm
    else:
        assert tm % l == 0
        tabs = tuple(jnp.tile(t, (tm // l, 1)) for t in tabs)
        n_pos_tiles = 1
    prompt = k_hist is None
    pre = _pre_call(x.reshape(n, d), tabs, p, tm, n_pos_tiles, b, prompt)
    x1, conv, z, ba, qb, vb, kf, vf = pre[:8]
    c = min(CHUNK, l)
    assert c in (INV_BLOCK, 4 * INV_BLOCK) and l % c == 0
    tc = _pick_tile(l, 4 * c)
    if conv_hist is None:
        conv_hist = jnp.zeros((b, CONV_W - 1, CONV_CH), F32)
        s0 = jnp.zeros((b, GDN_HEADS, GDN_DK, GDN_DV), F32)
    conv3 = conv.reshape(b, l, CONV_CH)
    og, s_new = _gdn_call(conv3, z.reshape(b, l, GDN_V), ba.reshape(b, l, 2 * LANES),
                          conv_hist.astype(F32), s0.astype(F32), p, tc, c)
    conv_new = jnp.concatenate([conv_hist.astype(F32), conv3], axis=1)[:, -(CONV_W - 1):]
    q3 = qb.reshape(b, l, DIFF_QK)
    v3 = vb.reshape(b, l, DIFF_V)
    if prompt:
        tk = 2 * LANES
        od = _attn_prompt_call(q3, kf, v3, p, _pick_tile(l, 2 * tk), tk, lam_init)
        k_out = jnp.transpose(kf.reshape(b, DIFF_HEADS, 2, DIFF_DK, l), (0, 4, 1, 2, 3))
    else:
        past = k_hist.shape[1]
        ckt = jnp.transpose(k_hist, (0, 2, 3, 4, 1)).reshape(b, DIFF_HEADS, LANES, past)
        od = _attn_sample_call(q3, pre[8].reshape(b, l, DIFF_QK), v3, ckt,
                               v_hist.reshape(b, past * DIFF_HEADS, DIFF_DV), p, lam_init)
        k_out = kf.reshape(b, l, DIFF_HEADS, 2, DIFF_DK)
    y = _post_call(x1, og.reshape(n, GDN_V), od.reshape(n, DIFF_V), p, tm)
    return (y.reshape(b, l, d), k_out, vf.reshape(b, l, DIFF_HEADS, DIFF_DV), s_new, conv_new)


def kernel(x_prompt, x_sample, cache_k, cache_v, state_gdn, state_conv, ffn1_norm, ffn1_w_gu, ffn1_w_down,
           mix_norm, w_in, conv_w, a_log, dt_bias, gdn_out_norm, q_norm, k_norm, lambda_q, lambda_k,
           diff_out_norm, w_out, ffn2_norm, ffn2_w_gu, ffn2_w_down):
    w = dict(ffn1_norm=ffn1_norm, ffn1_w_gu=ffn1_w_gu, ffn1_w_down=ffn1_w_down, mix_norm=mix_norm, w_in=w_in,
             conv_w=conv_w, a_log=a_log, dt_bias=dt_bias, gdn_out_norm=gdn_out_norm, q_norm=q_norm,
             k_norm=k_norm, lambda_q=lambda_q, lambda_k=lambda_k, diff_out_norm=diff_out_norm, w_out=w_out,
             ffn2_norm=ffn2_norm, ffn2_w_gu=ffn2_w_gu, ffn2_w_down=ffn2_w_down)
    depth = w_in.shape[0]
    pos_p = jnp.arange(x_prompt.shape[1])
    pos_s = cache_k.shape[2] + jnp.arange(x_sample.shape[1])
    hp, hs = x_prompt, x_sample
    outs = [[] for _ in range(8)]
    for l in range(depth):
        lam_init = 0.8 - 0.6 * math.exp(-0.3 * l)
        p = _layer_params(w, l)
        hp, kp, vp, sp, cp = _layer(hp, pos_p, None, None, None, None, p, lam_init)
        hs, ks, vs, ss, cs = _layer(hs, pos_s, cache_k[l], cache_v[l], state_conv[l], state_gdn[l], p, lam_init)
        for acc, val in zip(outs, (kp, vp, sp, cp, ks, vs, ss, cs)):
            acc.append(val)
    return (hp, hs) + tuple(jnp.stack(o) for o in outs)
```

```python
import functools
import math

import jax
import jax.numpy as jnp
from jax import lax
from jax.experimental import pallas as pl
from jax.experimental.pallas import tpu as pltpu

F32 = jnp.float32
BF16 = jnp.bfloat16

EPS = 1e-6
CHUNK = 64
GDN_HEADS = 4
GDN_DK = 128
GDN_DV = 128
CONV_W = 4
DIFF_HEADS = 4
DIFF_DK = 64
DIFF_DV = 128
ROT_DIM = DIFF_DK // 4
ROPE_THETA = 500000.0
GDN_QK = GDN_HEADS * GDN_DK
GDN_V = GDN_HEADS * GDN_DV
CONV_CH = 2 * GDN_QK + GDN_V
DIFF_QK = DIFF_HEADS * 2 * DIFF_DK
DIFF_V = DIFF_HEADS * DIFF_DV

LANES = 128
INV_BLOCK = 16
VMEM_LIMIT = 56 * 1024 * 1024
Q_SCALE = (DIFF_DK ** -0.5) * math.log2(math.e)


def _dot(a, b):
    return jnp.dot(a, b, preferred_element_type=F32)


def _dot_nt(a, b):
    return lax.dot_general(a, b, (((1,), (1,)), ((), ())), preferred_element_type=F32)


def _hi_lo(x):
    hi = x.astype(BF16)
    lo = (x - hi.astype(F32)).astype(BF16)
    return hi, lo


def _dot3(x, y):
    xh, xl = _hi_lo(x)
    yh, yl = _hi_lo(y)
    return _dot(xh, yh) + (_dot(xl, yh) + _dot(xh, yl))


def _rms(x, g):
    return x * lax.rsqrt(jnp.mean(x * x, axis=-1, keepdims=True) + EPS) * g


def _silu(x):
    return x * jax.nn.sigmoid(x)


def _swiglu(xn, wg_ref, wu_ref, wd_ref):
    g = _dot(xn, wg_ref[...])
    u = _dot(xn, wu_ref[...])
    act = (_silu(g) * u).astype(BF16)
    return _dot(act, wd_ref[...])


def _const_spec(shape):
    nd = len(shape)
    return pl.BlockSpec(shape, lambda *_: (0,) * nd, pipeline_mode=pl.Buffered(1))


def _pre_kernel(x_ref, cos_ref, sa_ref, sb_ref, n1_ref, wg_ref, wu_ref, wd_ref, nm_ref,
                wc_ref, wz_ref, wba_ref, wq_ref, wk_ref, wv_ref, qn_ref, kn_ref, gm_ref,
                x1_ref, conv_ref, z_ref, ba_ref, qb_ref, vb_ref, kf_ref, vf_ref, *maybe_kb_ref,
                k_pos_minor):
    x = x_ref[...]
    xn = _rms(x, n1_ref[...]).astype(BF16)
    x1 = x + 0.5 * _swiglu(xn, wg_ref, wu_ref, wd_ref)
    x1_ref[...] = x1
    h = _rms(x1, nm_ref[...]).astype(BF16)
    conv_ref[...] = _dot(h, wc_ref[...])
    z_ref[...] = _dot(h, wz_ref[...])
    ba_ref[...] = _dot(h, wba_ref[...])
    v = _dot(h, wv_ref[...])
    vb_ref[...] = v.astype(BF16)
    tm = v.shape[0]
    if k_pos_minor:
        for hh in range(DIFF_HEADS):
            vf_ref[pl.ds(hh, tm, stride=DIFF_HEADS), :] = v[:, hh * DIFF_DV:(hh + 1) * DIFF_DV]
    else:
        vf_ref[...] = v

    cos = cos_ref[...]
    sa = sa_ref[...]
    sb = sb_ref[...]
    gm = gm_ref[...]

    def norm_rope(t, gw):
        ss = _dot((t * t).astype(BF16), gm)
        t = t * lax.rsqrt(ss * (1.0 / DIFF_DK) + EPS) * gw
        outs = []
        for hh in range(DIFF_HEADS):
            th = t[:, hh * LANES:(hh + 1) * LANES]
            up = pltpu.roll(th, LANES - ROT_DIM // 2, 1)
            dn = pltpu.roll(th, ROT_DIM // 2, 1)
            outs.append(th * cos + up * sa + dn * sb)
        return jnp.concatenate(outs, axis=1)

    q = norm_rope(_dot(h, wq_ref[...]), qn_ref[...])
    qb_ref[...] = (q * Q_SCALE).astype(BF16)
    k = norm_rope(_dot(h, wk_ref[...]), kn_ref[...])
    if k_pos_minor:
        kf_ref[...] = k.T.reshape(DIFF_HEADS, LANES, tm)
    else:
        kf_ref[...] = k
        maybe_kb_ref[0][...] = k.astype(BF16)


def _pre_call(x2d, tabs, p, tm, n_pos_tiles, batch, k_pos_minor):
    n, d = x2d.shape
    grid = (n // tm,)
    row = lambda w: pl.BlockSpec((tm, w), lambda i: (i, 0))
    tab = pl.BlockSpec((tm, LANES), lambda i: (i % n_pos_tiles, 0))
    weights = [p['n1'], p['wg1'], p['wu1'], p['wd1'], p['nm'], p['w_conv'], p['w_z'], p['w_ba'],
               p['w_q'], p['w_k'], p['w_v'], p['qn'], p['kn'], p['gmat']]
    in_specs = [row(d), tab, tab, tab] + [_const_spec(w.shape) for w in weights]
    outs = [(d, F32), (CONV_CH, F32), (GDN_V, F32), (2 * LANES, F32), (DIFF_QK, BF16), (DIFF_V, BF16)]
    out_specs = [row(w) for w, _ in outs]
    out_shape = [jax.ShapeDtypeStruct((n, w), dt) for w, dt in outs]
    if k_pos_minor:
        l = n // batch
        tiles = l // tm
        out_specs += [pl.BlockSpec((None, DIFF_HEADS, LANES, tm), lambda i: (i // tiles, 0, 0, i % tiles)),
                      pl.BlockSpec((tm * DIFF_HEADS, DIFF_DV), lambda i: (i, 0))]
        out_shape += [jax.ShapeDtypeStruct((batch, DIFF_HEADS, LANES, l), F32),
                      jax.ShapeDtypeStruct((n * DIFF_HEADS, DIFF_DV), F32)]
    else:
        out_specs += [row(DIFF_QK), row(DIFF_V), row(DIFF_QK)]
        out_shape += [jax.ShapeDtypeStruct((n, DIFF_QK), F32), jax.ShapeDtypeStruct((n, DIFF_V), F32),
                      jax.ShapeDtypeStruct((n, DIFF_QK), BF16)]
    return pl.pallas_call(
        functools.partial(_pre_kernel, k_pos_minor=k_pos_minor),
        grid=grid,
        in_specs=in_specs,
        out_specs=out_specs,
        out_shape=out_shape,
        compiler_params=pltpu.CompilerParams(dimension_semantics=("arbitrary",),
                                             vmem_limit_bytes=VMEM_LIMIT),
        name="pre",
    )(x2d, *tabs, *weights)


def _post_kernel(x1_ref, og_ref, od_ref, wo_ref, n2_ref, wg_ref, wu_ref, wd_ref, y_ref):
    mixed = jnp.concatenate([og_ref[...], od_ref[...]], axis=1)
    x2 = x1_ref[...] + _dot(mixed, wo_ref[...])
    xn = _rms(x2, n2_ref[...]).astype(BF16)
    y_ref[...] = x2 + 0.5 * _swiglu(xn, wg_ref, wu_ref, wd_ref)


def _post_call(x1, og, od, p, tm):
    n, d = x1.shape
    row = lambda w: pl.BlockSpec((tm, w), lambda i: (i, 0))
    weights = [p['wo'], p['n2'], p['wg2'], p['wu2'], p['wd2']]
    return pl.pallas_call(
        _post_kernel,
        grid=(n // tm,),
        in_specs=[row(d), row(GDN_V), row(DIFF_V)] + [_const_spec(w.shape) for w in weights],
        out_specs=row(d),
        out_shape=jax.ShapeDtypeStruct((n, d), F32),
        compiler_params=pltpu.CompilerParams(dimension_semantics=("arbitrary",),
                                             vmem_limit_bytes=VMEM_LIMIT),
        name="post",
    )(x1, og, od, *weights)


def _unit_lower_inverse(a, c):
    ri = lax.broadcasted_iota(jnp.int32, (c, c), 0)
    ci = lax.broadcasted_iota(jnp.int32, (c, c), 1)
    eye = (ri == ci).astype(F32)
    if c == INV_BLOCK:
        d = a
    else:
        same = (ri // INV_BLOCK) == (ci // INV_BLOCK)
        d = jnp.where(same, a, 0.0)
    assert INV_BLOCK == 16
    d2 = _dot3(d, d)
    d3 = _dot3(d, d2)
    d4 = _dot3(d2, d2)
    n1 = eye - d + d2 - d3
    n2 = n1 + _dot3(n1, d4)
    d8 = _dot3(d4, d4)
    td = n2 + _dot3(n2, d8)
    if c == INV_BLOCK:
        return td
    assert c == 4 * INV_BLOCK
    m = _dot3(td, a - d)
    m2 = _dot3(m, m)
    m3 = _dot3(m, m2)
    return _dot3(eye - m + m2 - m3, td)


def _gdn_kernel(conv_ref, z_ref, ba_ref, hist_ref, s0_ref, cw_ref, alog_ref, dtb_ref, on_ref,
                o_ref, s_out_ref,
                xp_ref, s_ref, q_scr, k_scr, v_scr, g_scr, b_scr, *, tc, c):
    t = pl.program_id(1)
    nt = pl.num_programs(1)
    pad = 8

    @pl.when(t == 0)
    def _():
        xp_ref[0:pad, :] = jnp.zeros((pad, CONV_CH), F32)
        xp_ref[pad - (CONV_W - 1):pad, :] = hist_ref[...]
        s_ref[...] = s0_ref[...]

    x = conv_ref[...]
    xp_ref[pad:pad + tc, :] = x
    cw = cw_ref[...]
    y = xp_ref[pad - 3:pad - 3 + tc, :] * cw[0:1, :]
    for i in range(1, CONV_W):
        y = y + xp_ref[pad - 3 + i:pad - 3 + i + tc, :] * cw[i:i + 1, :]
    xp_ref[pad - (CONV_W - 1):pad, :] = x[tc - (CONV_W - 1):tc, :]
    y = _silu(y)

    for h in range(GDN_HEADS):
        qh = y[:, h * GDN_DK:(h + 1) * GDN_DK]
        kh = y[:, GDN_QK + h * GDN_DK:GDN_QK + (h + 1) * GDN_DK]
        q_scr[:, h * GDN_DK:(h + 1) * GDN_DK] = qh * (
            lax.rsqrt(jnp.sum(qh * qh, axis=-1, keepdims=True) + EPS) * (GDN_DK ** -0.5))
        k_scr[:, h * GDN_DK:(h + 1) * GDN_DK] = kh * lax.rsqrt(
            jnp.sum(kh * kh, axis=-1, keepdims=True) + EPS)
    v_scr[...] = y[:, 2 * GDN_QK:]

    ba = ba_ref[...]
    b_scr[...] = jax.nn.sigmoid(ba[:, :LANES])
    g = -jnp.exp(alog_ref[...]) * jax.nn.softplus(ba[:, LANES:] + dtb_ref[...])
    ri = lax.broadcasted_iota(jnp.int32, (tc, tc), 0)
    ci = lax.broadcasted_iota(jnp.int32, (tc, tc), 1)
    tri = jnp.where((ci <= ri) & ((ri // c) == (ci // c)), 1.0, 0.0).astype(BF16)
    g1 = g.astype(BF16)
    r1 = g - g1.astype(F32)
    g2 = r1.astype(BF16)
    g3 = (r1 - g2.astype(F32)).astype(BF16)
    g_scr[...] = _dot(tri, g1) + (_dot(tri, g2) + _dot(tri, g3))

    rc = lax.broadcasted_iota(jnp.int32, (c, c), 0)
    cc = lax.broadcasted_iota(jnp.int32, (c, c), 1)
    incl = rc >= cc
    strict = rc > cc
    lane0 = lax.broadcasted_iota(jnp.int32, (c, LANES), 1) == 0
    ones_l0 = jnp.where(lane0, 1.0, 0.0).astype(BF16)
    on = on_ref[...]

    def chunk(ic, carry):
        r0 = pl.multiple_of(ic * c, c)
        rows = pl.ds(r0, c)
        g_all = g_scr[rows, :]
        b_all = b_scr[rows, :]
        for h in range(GDN_HEADS):
            cols = slice(h * GDN_DK, (h + 1) * GDN_DK)
            qh = q_scr[rows, cols]
            kh = k_scr[rows, cols]
            vh = v_scr[rows, cols]
            gc = g_all[:, h:h + 1]
            bc = b_all[:, h:h + 1]
            gx = jnp.where(lane0, gc, 0.0)
            x1 = gx.astype(BF16)
            xr = gx - x1.astype(F32)
            x2 = xr.astype(BF16)
            x3 = (xr - x2.astype(F32)).astype(BF16)
            gr = _dot_nt(ones_l0, x1) + (_dot_nt(ones_l0, x2) + _dot_nt(ones_l0, x3))
            dec = jnp.exp(jnp.where(incl, gc - gr, -jnp.inf))
            khb = kh.astype(BF16)
            kk = _dot_nt(khb, khb)
            a = jnp.where(strict, bc * kk * dec, 0.0)
            tinv = _unit_lower_inverse(a, c)
            eg = jnp.exp(gc)
            rhs = jnp.concatenate([vh * bc, kh * (bc * eg)], axis=1)
            sol = _dot3(tinv, rhs)
            u0 = sol[:, :GDN_DV]
            w = sol[:, GDN_DV:]
            qk = _dot_nt(qh.astype(BF16), khb) * dec
            qd = qh * eg
            g_last = g_all[c - 1:c, h:h + 1]
            kd = kh * jnp.exp(g_last - gc)
            s = s_ref[h]
            sb = s.astype(BF16)
            ws = _dot(jnp.concatenate([w, qd], axis=0).astype(BF16), sb)
            u = u0 - ws[:c]
            ub = u.astype(BF16)
            o = ws[c:] + _dot(qk.astype(BF16), ub)
            s_ref[h] = s * jnp.exp(g_last) + lax.dot_general(
                kd.astype(BF16), ub, (((0,), (0,)), ((), ())), preferred_element_type=F32)
            zh = z_ref[rows, cols]
            o_ref[rows, cols] = (_rms(o, on) * _silu(zh)).astype(o_ref.dtype)
        return carry

    lax.fori_loop(0, tc // c, chunk, 0)

    @pl.when(t == nt - 1)
    def _():
        s_out_ref[...] = s_ref[...]


def _exact3(m01, x):
    x1 = x.astype(BF16)
    r1 = x - x1.astype(F32)
    x2 = r1.astype(BF16)
    x3 = (r1 - x2.astype(F32)).astype(BF16)
    return _dot(m01, x1) + (_dot(m01, x2) + _dot(m01, x3))


def _gdn_tile_kernel(conv_ref, z_ref, ba_ref, hist_ref, s0_ref, cw_ref, alog_ref, dtb_ref, on_ref, hm_ref,
                     o_ref, s_out_ref, xp_ref, s_ref, *, tc, c):
    t = pl.program_id(1)
    nt = pl.num_programs(1)
    pad = 8
    assert c == 4 * INV_BLOCK and tc % c == 0 and tc % LANES == 0

    @pl.when(t == 0)
    def _():
        xp_ref[0:pad, :] = jnp.zeros((pad, CONV_CH), F32)
        xp_ref[pad - (CONV_W - 1):pad, :] = hist_ref[...]
        s_ref[...] = s0_ref[...]

    x = conv_ref[...]
    xp_ref[pad:pad + tc, :] = x
    cw = cw_ref[...]
    xp = xp_ref[...]
    y = x * cw[CONV_W - 1:CONV_W, :]
    for k in range(1, CONV_W):
        y = y + pltpu.roll(xp, k, 0)[pad:pad + tc, :] * cw[CONV_W - 1 - k:CONV_W - k, :]
    xp_ref[pad - (CONV_W - 1):pad, :] = x[tc - (CONV_W - 1):tc, :]
    y = _silu(y)
    hm = hm_ref[...]
    yq = y[:, :GDN_QK]
    yk = y[:, GDN_QK:2 * GDN_QK]
    yq = yq * (lax.rsqrt(_dot((yq * yq).astype(BF16), hm) + EPS) * (GDN_DK ** -0.5))
    yk = yk * lax.rsqrt(_dot((yk * yk).astype(BF16), hm) + EPS)

    ba = ba_ref[...]
    beta = jax.nn.sigmoid(ba[:, :LANES])
    g = -jnp.exp(alog_ref[...]) * jax.nn.softplus(ba[:, LANES:] + dtb_ref[...])
    ri = lax.broadcasted_iota(jnp.int32, (tc, tc), 0)
    ci = lax.broadcasted_iota(jnp.int32, (tc, tc), 1)
    same_chunk = (ri // c) == (ci // c)
    incl = same_chunk & (ci <= ri)
    same_blk = (ri // INV_BLOCK) == (ci // INV_BLOCK)
    diag = ri == ci
    eye = jnp.where(diag, 1.0, 0.0)
    gcum = _exact3(jnp.where(incl, 1.0, 0.0).astype(BF16), g)
    gcum_t = gcum.T
    on = on_ref[...]

    heads = range(GDN_HEADS)

    def per_head(f, *lists):
        return [f(*args) for args in zip(*lists)]

    qs = [yq[:, h * GDN_DK:(h + 1) * GDN_DK] for h in heads]
    ks = [yk[:, h * GDN_DK:(h + 1) * GDN_DK] for h in heads]
    vs = [y[:, 2 * GDN_QK + h * GDN_DV:2 * GDN_QK + (h + 1) * GDN_DV] for h in heads]
    gcs = [gcum[:, h:h + 1] for h in heads]
    bcs = [beta[:, h:h + 1] for h in heads]
    decs = [jnp.exp(jnp.where(incl, gcs[h] - gcum_t[h:h + 1, :], -jnp.inf)) for h in heads]
    kbs = per_head(lambda k: k.astype(BF16), ks)
    a_s = per_head(lambda bc, kb, dec: jnp.where(diag, 0.0, bc * _dot_nt(kb, kb) * dec), bcs, kbs, decs)
    qks = per_head(lambda q, kb, dec: (_dot_nt(q.astype(BF16), kb) * dec).astype(BF16), qs, kbs, decs)

    bf = lambda xs: per_head(lambda x: x.astype(BF16), xs)
    mm = lambda xs, ys: per_head(_dot, xs, ys)
    ds = per_head(lambda a: jnp.where(same_blk, a, 0.0), a_s)
    sd = bf(ds)
    d2 = mm(sd, sd)
    s2 = bf(d2)
    d3 = mm(sd, s2)
    d4 = mm(s2, s2)
    s4 = bf(d4)
    n1 = per_head(lambda d, x2, x3: eye - d + x2 - x3, ds, d2, d3)
    n2 = per_head(jnp.add, n1, mm(bf(n1), s4))
    d8 = mm(s4, s4)
    td = per_head(jnp.add, n2, mm(bf(n2), bf(d8)))
    std = bf(td)
    ms = mm(std, bf(per_head(jnp.subtract, a_s, ds)))
    sm = bf(ms)
    m2 = mm(sm, sm)
    m3 = mm(sm, bf(m2))
    tinv = mm(bf(per_head(lambda m, x2, x3: eye - m + x2 - x3, ms, m2, m3)), std)

    egs = per_head(jnp.exp, gcs)
    rhs = per_head(lambda v, k, bc, eg: jnp.concatenate([v * bc, k * (bc * eg)], axis=1), vs, ks, bcs, egs)
    sols = mm(bf(tinv), bf(rhs))
    u0s = [sol[:, :GDN_DV] for sol in sols]
    wbs = [sol[:, GDN_DV:].astype(BF16) for sol in sols]
    qds = per_head(lambda q, eg: (q * eg).astype(BF16), qs, egs)

    nchunk = tc // c
    states = [s_ref[h] for h in heads]
    us = [[] for _ in heads]
    outs = [[] for _ in heads]
    for ic in range(nchunk):
        rs = slice(ic * c, (ic + 1) * c)
        gends = [gcs[h][(ic + 1) * c - 1:(ic + 1) * c, :] for h in heads]
        kds = [(ks[h][rs] * jnp.exp(gends[h] - gcs[h][rs])).astype(BF16) for h in heads]
        wss = [_dot(jnp.concatenate([wbs[h][rs], qds[h][rs]], axis=0), states[h].astype(BF16))
               for h in heads]
        for h in heads:
            us[h].append((u0s[h][rs] - wss[h][:c]).astype(BF16))
        zpad = [jnp.zeros((tc - (ic + 1) * c, GDN_DV), BF16)] if ic + 1 < nchunk else []
        for h in heads:
            outs[h].append(wss[h][c:] + _dot(qks[h][rs], jnp.concatenate(us[h] + zpad, axis=0)))
        states = [states[h] * jnp.exp(gends[h]) + lax.dot_general(
            kds[h], us[h][ic], (((0,), (0,)), ((), ())), preferred_element_type=F32) for h in heads]
    for h in heads:
        s_ref[h] = states[h]
        cols = slice(h * GDN_DV, (h + 1) * GDN_DV)
        o = jnp.concatenate(outs[h], axis=0)
        o_ref[:, cols] = (_rms(o, on) * _silu(z_ref[:, cols])).astype(o_ref.dtype)

    @pl.when(t == nt - 1)
    def _():
        s_out_ref[...] = s_ref[...]


def _gdn_call(conv, z, ba, hist, s0, p, tc, c):
    b, l, _ = conv.shape
    row = lambda w: pl.BlockSpec((None, tc, w), lambda i, j: (i, j, 0))
    per_b = lambda shape: pl.BlockSpec((None,) + shape, lambda i, j: (i,) + (0,) * len(shape))
    small = [p['conv_w'], p['a_log'], p['dt_bias'], p['gdn_on']]
    if tc % LANES == 0 and c == 4 * INV_BLOCK:
        small = small + [p['head_mat']]
        return pl.pallas_call(
            functools.partial(_gdn_tile_kernel, tc=tc, c=c),
            grid=(b, l // tc),
            in_specs=[row(CONV_CH), row(GDN_V), row(2 * LANES), per_b((CONV_W - 1, CONV_CH)),
                      per_b((GDN_HEADS, GDN_DK, GDN_DV))] + [
                          pl.BlockSpec(w.shape, lambda i, j: (0, 0)) for w in small],
            out_specs=[row(GDN_V), per_b((GDN_HEADS, GDN_DK, GDN_DV))],
            out_shape=[jax.ShapeDtypeStruct((b, l, GDN_V), BF16),
                       jax.ShapeDtypeStruct((b, GDN_HEADS, GDN_DK, GDN_DV), F32)],
            scratch_shapes=[pltpu.VMEM((tc + 8, CONV_CH), F32),
                            pltpu.VMEM((GDN_HEADS, GDN_DK, GDN_DV), F32)],
            compiler_params=pltpu.CompilerParams(dimension_semantics=("arbitrary", "arbitrary"),
                                                 vmem_limit_bytes=VMEM_LIMIT),
            name="gdn_tile",
        )(conv, z, ba, hist, s0, *small)
    return pl.pallas_call(
        functools.partial(_gdn_kernel, tc=tc, c=c),
        grid=(b, l // tc),
        in_specs=[row(CONV_CH), row(GDN_V), row(2 * LANES), per_b((CONV_W - 1, CONV_CH)),
                  per_b((GDN_HEADS, GDN_DK, GDN_DV))] + [
                      pl.BlockSpec(w.shape, lambda i, j: (0, 0)) for w in small],
        out_specs=[row(GDN_V), per_b((GDN_HEADS, GDN_DK, GDN_DV))],
        out_shape=[jax.ShapeDtypeStruct((b, l, GDN_V), BF16),
                   jax.ShapeDtypeStruct((b, GDN_HEADS, GDN_DK, GDN_DV), F32)],
        scratch_shapes=[pltpu.VMEM((tc + 8, CONV_CH), F32),
                        pltpu.VMEM((GDN_HEADS, GDN_DK, GDN_DV), F32),
                        pltpu.VMEM((tc, GDN_QK), F32), pltpu.VMEM((tc, GDN_QK), F32),
                        pltpu.VMEM((tc, GDN_V), F32),
                        pltpu.VMEM((tc, LANES), F32), pltpu.VMEM((tc, LANES), F32)],
        compiler_params=pltpu.CompilerParams(dimension_semantics=("arbitrary", "arbitrary"),
                                             vmem_limit_bytes=VMEM_LIMIT),
        name="gdn",
    )(conv, z, ba, hist, s0, *small)


def _stack_maps(q):
    lane = lax.broadcasted_iota(jnp.int32, q.shape, 1)
    zero = jnp.zeros_like(q)
    return jnp.concatenate([jnp.where(lane < DIFF_DK, q, zero), jnp.where(lane >= DIFF_DK, q, zero)],
                           axis=0)


def _diff_finish(acc, l, t, lq_ref, lk_ref, dn_ref, lam_init):
    lam_e = jnp.exp(jnp.sum(lq_ref[...] * lk_ref[...], axis=-1, keepdims=True))
    lam = lam_e[0:1] - lam_e[1:2] + lam_init
    o = acc[:t] / l[:t] - lam * (acc[t:] / l[t:])
    return _rms(o, dn_ref[...]) * (1.0 - lam_init)


def _halves_max(s):
    return jnp.maximum(s[:, :LANES], s[:, LANES:])


def _attn_prompt_kernel(q_ref, kt_ref, v_ref, lq_ref, lk_ref, dn_ref, o_ref,
                        kt_scr, s_scr, m_scr, l_scr, acc_scr, *, tq, tk, lam_init):
    i = pl.program_id(2)
    assert tk == 2 * LANES and tq % tk == 0
    nsub = tq // tk

    @pl.when(i == 0)
    def _():
        for j in range(kt_scr.shape[0]):
            kt_scr[j] = kt_ref[:, j * tk:(j + 1) * tk].astype(BF16)

    qq = _stack_maps(q_ref[...])

    def scores(j):
        return _dot(qq, kt_scr[j])

    def values(j):
        return v_ref[pl.ds(pl.multiple_of(j * tk, tk), tk), :]

    def probs(s, mm):
        return jnp.concatenate([jnp.exp2(s[:, :LANES] - mm), jnp.exp2(s[:, LANES:] - mm)], axis=1)

    r = lax.broadcasted_iota(jnp.int32, (2 * tq, tk), 0)
    r = jnp.where(r >= tq, r - tq, r)
    cidx = lax.broadcasted_iota(jnp.int32, (2 * tq, tk), 1)
    s_diag = [jnp.where((cidx + d * tk) // CHUNK <= r // CHUNK, scores(i * nsub + d), -jnp.inf)
              for d in range(nsub)]
    m_scr[...] = functools.reduce(jnp.maximum, [_halves_max(s) for s in s_diag])

    def pass1(jj, carry):
        ss = [scores(jj * nsub + u) for u in range(nsub)]
        for u in range(nsub):
            s_scr[jj * nsub + u] = ss[u]
        m_scr[...] = jnp.maximum(m_scr[...], functools.reduce(jnp.maximum, [_halves_max(s) for s in ss]))
        return carry

    lax.fori_loop(0, i, pass1, 0)
    m = jnp.broadcast_to(jnp.max(m_scr[...], axis=-1, keepdims=True), m_scr.shape)
    m_scr[...] = m

    def accumulate(ps, vs, first):
        lsum = functools.reduce(jnp.add, [p[:, :LANES] + p[:, LANES:] for p in ps])
        pv = functools.reduce(jnp.add, [_dot(p.astype(BF16), v) for p, v in zip(ps, vs)])
        if first:
            l_scr[...] = lsum
            acc_scr[...] = pv
        else:
            l_scr[...] += lsum
            acc_scr[...] += pv

    accumulate([probs(s, m) for s in s_diag], [values(i * nsub + d) for d in range(nsub)], True)

    def pass2(jj, carry):
        mm = m_scr[...]
        accumulate([probs(s_scr[jj * nsub + u], mm) for u in range(nsub)],
                   [values(jj * nsub + u) for u in range(nsub)], False)
        return carry

    lax.fori_loop(0, i, pass2, 0)
    l = jnp.sum(l_scr[...], axis=-1, keepdims=True)
    o_ref[...] = _diff_finish(acc_scr[...], l, tq, lq_ref, lk_ref, dn_ref, lam_init).astype(o_ref.dtype)


def _attn_prompt_call(qb, kt, vb, p, tq, tk, lam_init):
    b, l, _ = qb.shape
    small = [p['lambda_q'], p['lambda_k'], p['diff_on']]
    return pl.pallas_call(
        functools.partial(_attn_prompt_kernel, tq=tq, tk=tk, lam_init=lam_init),
        grid=(b, DIFF_HEADS, l // tq),
        in_specs=[pl.BlockSpec((None, tq, LANES), lambda bi, h, i: (bi, i, h)),
                  pl.BlockSpec((None, None, LANES, l), lambda bi, h, i: (bi, h, 0, 0)),
                  pl.BlockSpec((None, l, LANES), lambda bi, h, i: (bi, 0, h))] + [
                      pl.BlockSpec(w.shape, lambda bi, h, i: (0, 0)) for w in small],
        out_specs=pl.BlockSpec((None, tq, LANES), lambda bi, h, i: (bi, i, h)),
        out_shape=jax.ShapeDtypeStruct((b, l, DIFF_V), BF16),
        scratch_shapes=[pltpu.VMEM((l // tk, LANES, tk), BF16),
                        pltpu.VMEM((max((l - tq) // tk, 1), 2 * tq, tk), F32),
                        pltpu.VMEM((2 * tq, LANES), F32), pltpu.VMEM((2 * tq, LANES), F32),
                        pltpu.VMEM((2 * tq, LANES), F32)],
        compiler_params=pltpu.CompilerParams(
            dimension_semantics=("arbitrary", "arbitrary", "arbitrary"),
            vmem_limit_bytes=VMEM_LIMIT),
        name="attn_prompt",
    )(qb, kt, vb, *small)


def _attn_sample_kernel(q_ref, kn_ref, vn_ref, ckt_ref, cv_ref, lq_ref, lk_ref, dn_ref, o_ref,
                        *, t, lam_init):
    past = ckt_ref.shape[-1]
    for h in range(DIFF_HEADS):
        cols = slice(h * LANES, (h + 1) * LANES)
        qq = _stack_maps(q_ref[:, cols])
        s_c = _dot(qq, ckt_ref[h].astype(BF16))
        s_n = _dot_nt(qq, kn_ref[:, cols])
        m = jnp.maximum(jnp.max(s_c, axis=-1, keepdims=True), jnp.max(s_n, axis=-1, keepdims=True))
        p_c = jnp.exp2(s_c - m)
        p_n = jnp.exp2(s_n - m)
        l = jnp.sum(p_c, axis=-1, keepdims=True) + jnp.sum(p_n, axis=-1, keepdims=True)
        cv = cv_ref[pl.ds(h, past, stride=DIFF_HEADS), :].astype(BF16)
        acc = _dot(p_c.astype(BF16), cv) + _dot(p_n.astype(BF16), vn_ref[:, cols])
        o_ref[:, cols] = _diff_finish(acc, l, t, lq_ref, lk_ref, dn_ref, lam_init).astype(o_ref.dtype)


def _attn_sample_call(qb, kb, vb, ckt, cv, p, lam_init):
    b, t, _ = qb.shape
    past = ckt.shape[-1]
    small = [p['lambda_q'], p['lambda_k'], p['diff_on']]
    new = pl.BlockSpec((None, t, DIFF_V), lambda bi: (bi, 0, 0))
    return pl.pallas_call(
        functools.partial(_attn_sample_kernel, t=t, lam_init=lam_init),
        grid=(b,),
        in_specs=[new, new, new,
                  pl.BlockSpec((None, DIFF_HEADS, LANES, past), lambda bi: (bi, 0, 0, 0)),
                  pl.BlockSpec((None, past * DIFF_HEADS, DIFF_DV), lambda bi: (bi, 0, 0))] + [
                      pl.BlockSpec(w.shape, lambda bi: (0, 0)) for w in small],
        out_specs=new,
        out_shape=jax.ShapeDtypeStruct((b, t, DIFF_V), BF16),
        compiler_params=pltpu.CompilerParams(dimension_semantics=("arbitrary",),
                                             vmem_limit_bytes=VMEM_LIMIT),
        name="attn_sample",
    )(qb, kb, vb, ckt, cv, *small)


def _rope_tables(pos):
    half = ROT_DIM // 2
    inv = jnp.float32(ROPE_THETA) ** (-jnp.arange(half, dtype=F32) * 2.0 / ROT_DIM)
    ang = pos.astype(F32)[:, None] * inv[None, :]
    cos, sin = jnp.cos(ang), jnp.sin(ang)
    n = pos.shape[0]
    rest = DIFF_DK - ROT_DIM
    one = jnp.ones((n, rest), F32)
    zero = jnp.zeros((n, rest), F32)
    zh = jnp.zeros((n, half), F32)
    c64 = jnp.concatenate([cos, cos, one], axis=1)
    a64 = jnp.concatenate([-sin, zh, zero], axis=1)
    b64 = jnp.concatenate([zh, sin, zero], axis=1)
    rep = LANES // DIFF_DK
    return tuple(jnp.tile(t, (1, rep)) for t in (c64, a64, b64))


def _layer_params(w, l):
    d = w['w_in'].shape[1]
    d_ff = w['ffn1_w_down'].shape[1]
    win = w['w_in'][l]
    o = 0
    w_conv = win[:, o:o + CONV_CH]; o += CONV_CH
    w_b = win[:, o:o + GDN_HEADS]; o += GDN_HEADS
    w_a = win[:, o:o + GDN_HEADS]; o += GDN_HEADS
    w_z = win[:, o:o + GDN_V]; o += GDN_V
    w_q = win[:, o:o + DIFF_QK]; o += DIFF_QK
    w_k = win[:, o:o + DIFF_QK]; o += DIFF_QK
    w_v = win[:, o:o + DIFF_V]
    zpad = jnp.zeros((d, LANES - GDN_HEADS), win.dtype)
    w_ba = jnp.concatenate([w_b, zpad, w_a, zpad], axis=1)
    hpad = lambda v: jnp.concatenate([v.astype(F32), jnp.zeros((LANES - GDN_HEADS,), F32)])[None, :]
    grp = jnp.arange(DIFF_QK) // DIFF_DK
    bf = lambda t: t.astype(BF16)
    r2 = lambda v: v.astype(F32)[None, :]
    return dict(
        n1=r2(w['ffn1_norm'][l]), wg1=bf(w['ffn1_w_gu'][l][:, :d_ff]), wu1=bf(w['ffn1_w_gu'][l][:, d_ff:]),
        wd1=bf(w['ffn1_w_down'][l]),
        nm=r2(w['mix_norm'][l]), w_conv=bf(w_conv), w_z=bf(w_z), w_ba=bf(w_ba), w_q=bf(w_q), w_k=bf(w_k),
        w_v=bf(w_v),
        qn=r2(jnp.tile(w['q_norm'][l], DIFF_QK // DIFF_DK)), kn=r2(jnp.tile(w['k_norm'][l], DIFF_QK // DIFF_DK)),
        gmat=(grp[:, None] == grp[None, :]).astype(BF16),
        head_mat=(grp[:, None] // 2 == grp[None, :] // 2).astype(BF16),
        conv_w=w['conv_w'][l].astype(F32), a_log=hpad(w['a_log'][l]), dt_bias=hpad(w['dt_bias'][l]),
        gdn_on=r2(w['gdn_out_norm'][l]),
        lambda_q=w['lambda_q'][l].astype(F32), lambda_k=w['lambda_k'][l].astype(F32),
        diff_on=r2(w['diff_out_norm'][l]),
        wo=bf(w['w_out'][l]), n2=r2(w['ffn2_norm'][l]), wg2=bf(w['ffn2_w_gu'][l][:, :d_ff]),
        wu2=bf(w['ffn2_w_gu'][l][:, d_ff:]), wd2=bf(w['ffn2_w_down'][l]),
    )


def _pick_tile(n, pref):
    t = min(n, pref)
    assert n % t == 0
    return t


def _layer(x, pos, k_hist, v_hist, conv_hist, s0, p, lam_init):
    b, l, d = x.shape
    n = b * l
    tm = _pick_tile(n, 256)
    tabs = _rope_tables(pos)
    if l >= tm:
        assert l % tm == 0
        n_pos_tiles = l // tm
    else:
        assert tm % l == 0
        tabs = tuple(jnp.tile(t, (tm // l, 1)) for t in tabs)
        n_pos_tiles = 1
    prompt = k_hist is None
    pre = _pre_call(x.reshape(n, d), tabs, p, tm, n_pos_tiles, b, prompt)
    x1, conv, z, ba, qb, vb, kf, vf = pre[:8]
    c = min(CHUNK, l)
    assert c in (INV_BLOCK, 4 * INV_BLOCK) and l % c == 0
    tc = _pick_tile(l, 4 * c)
    if conv_hist is None:
        conv_hist = jnp.zeros((b, CONV_W - 1, CONV_CH), F32)
        s0 = jnp.zeros((b, GDN_HEADS, GDN_DK, GDN_DV), F32)
    conv3 = conv.reshape(b, l, CONV_CH)
    og, s_new = _gdn_call(conv3, z.reshape(b, l, GDN_V), ba.reshape(b, l, 2 * LANES),
                          conv_hist.astype(F32), s0.astype(F32), p, tc, c)
    conv_new = jnp.concatenate([conv_hist.astype(F32), conv3], axis=1)[:, -(CONV_W - 1):]
    q3 = qb.reshape(b, l, DIFF_QK)
    v3 = vb.reshape(b, l, DIFF_V)
    if prompt:
        tk = 2 * LANES
        od = _attn_prompt_call(q3, kf, v3, p, _pick_tile(l, 2 * tk), tk, lam_init)
        k_out = jnp.transpose(kf.reshape(b, DIFF_HEADS, 2, DIFF_DK, l), (0, 4, 1, 2, 3))
    else:
        past = k_hist.shape[1]
        ckt = jnp.transpose(k_hist, (0, 2, 3, 4, 1)).reshape(b, DIFF_HEADS, LANES, past)
        od = _attn_sample_call(q3, pre[8].reshape(b, l, DIFF_QK), v3, ckt,
                               v_hist.reshape(b, past * DIFF_HEADS, DIFF_DV), p, lam_init)
        k_out = kf.reshape(b, l, DIFF_HEADS, 2, DIFF_DK)
    y = _post_call(x1, og.reshape(n, GDN_V), od.reshape(n, DIFF_V), p, tm)
    return (y.reshape(b, l, d), k_out, vf.reshape(b, l, DIFF_HEADS, DIFF_DV), s_new, conv_new)


def kernel(x_prompt, x_sample, cache_k, cache_v, state_gdn, state_conv, ffn1_norm, ffn1_w_gu, ffn1_w_down,
           mix_norm, w_in, conv_w, a_log, dt_bias, gdn_out_norm, q_norm, k_norm, lambda_q, lambda_k,
           diff_out_norm, w_out, ffn2_norm, ffn2_w_gu, ffn2_w_down):
    w = dict(ffn1_norm=ffn1_norm, ffn1_w_gu=ffn1_w_gu, ffn1_w_down=ffn1_w_down, mix_norm=mix_norm, w_in=w_in,
             conv_w=conv_w, a_log=a_log, dt_bias=dt_bias, gdn_out_norm=gdn_out_norm, q_norm=q_norm,
             k_norm=k_norm, lambda_q=lambda_q, lambda_k=lambda_k, diff_out_norm=diff_out_norm, w_out=w_out,
             ffn2_norm=ffn2_norm, ffn2_w_gu=ffn2_w_gu, ffn2_w_down=ffn2_w_down)
    depth = w_in.shape[0]
    pos_p = jnp.arange(x_prompt.shape[1])
    pos_s = cache_k.shape[2] + jnp.arange(x_sample.shape[1])
    hp, hs = x_prompt, x_sample
    outs = [[] for _ in range(8)]
    for l in range(depth):
        lam_init = 0.8 - 0.6 * math.exp(-0.3 * l)
        p = _layer_params(w, l)
        hp, kp, vp, sp, cp = _layer(hp, pos_p, None, None, None, None, p, lam_init)
        hs, ks, vs, ss, cs = _layer(hs, pos_s, cache_k[l], cache_v[l], state_conv[l], state_gdn[l], p, lam_init)
        for acc, val in zip(outs, (kp, vp, sp, cp, ks, vs, ss, cs)):
            acc.append(val)
    return (hp, hs) + tuple(jnp.stack(o) for o in outs)
```

```python
import functools
import math

import jax
import jax.numpy as jnp
from jax import lax
from jax.experimental import pallas as pl
from jax.experimental.pallas import tpu as pltpu

F32 = jnp.float32
BF16 = jnp.bfloat16

EPS = 1e-6
CHUNK = 64
GDN_HEADS = 4
GDN_DK = 128
GDN_DV = 128
CONV_W = 4
DIFF_HEADS = 4
DIFF_DK = 64
DIFF_DV = 128
ROT_DIM = DIFF_DK // 4
ROPE_THETA = 500000.0
GDN_QK = GDN_HEADS * GDN_DK
GDN_V = GDN_HEADS * GDN_DV
CONV_CH = 2 * GDN_QK + GDN_V
DIFF_QK = DIFF_HEADS * 2 * DIFF_DK
DIFF_V = DIFF_HEADS * DIFF_DV

LANES = 128
INV_BLOCK = 16
GDN_TILE = 256
VMEM_LIMIT = 56 * 1024 * 1024
Q_SCALE = (DIFF_DK ** -0.5) * math.log2(math.e)


def _dot(a, b):
    return jnp.dot(a, b, preferred_element_type=F32)


def _dot_nt(a, b):
    return lax.dot_general(a, b, (((1,), (1,)), ((), ())), preferred_element_type=F32)


def _rms(x, g):
    return x * lax.rsqrt(jnp.mean(x * x, axis=-1, keepdims=True) + EPS) * g


def _silu(x):
    return x * jax.nn.sigmoid(x)


def _swiglu(xn, wg_ref, wu_ref, wd_ref):
    g = _dot(xn, wg_ref[...])
    u = _dot(xn, wu_ref[...])
    act = (_silu(g) * u).astype(BF16)
    return _dot(act, wd_ref[...])


def _const_spec(shape):
    nd = len(shape)
    return pl.BlockSpec(shape, lambda *_: (0,) * nd, pipeline_mode=pl.Buffered(1))


def _pre_kernel(x_ref, cos_ref, sa_ref, sb_ref, n1_ref, wg_ref, wu_ref, wd_ref, nm_ref,
                wc_ref, wz_ref, wba_ref, wq_ref, wk_ref, wv_ref, qn_ref, kn_ref, gm_ref,
                x1_ref, conv_ref, z_ref, ba_ref, qb_ref, vb_ref, kf_ref, vf_ref, *maybe_kb_ref,
                k_pos_minor):
    x = x_ref[...]
    xn = _rms(x, n1_ref[...]).astype(BF16)
    x1 = x + 0.5 * _swiglu(xn, wg_ref, wu_ref, wd_ref)
    x1_ref[...] = x1
    h = _rms(x1, nm_ref[...]).astype(BF16)
    conv_ref[...] = _dot(h, wc_ref[...])
    z_ref[...] = _dot(h, wz_ref[...])
    ba_ref[...] = _dot(h, wba_ref[...])
    v = _dot(h, wv_ref[...])
    vb_ref[...] = v.astype(BF16)
    tm = v.shape[0]
    if k_pos_minor:
        for hh in range(DIFF_HEADS):
            vf_ref[pl.ds(hh, tm, stride=DIFF_HEADS), :] = v[:, hh * DIFF_DV:(hh + 1) * DIFF_DV]
    else:
        vf_ref[...] = v

    cos = cos_ref[...]
    sa = sa_ref[...]
    sb = sb_ref[...]
    gm = gm_ref[...]

    def norm_rope(t, gw):
        ss = _dot((t * t).astype(BF16), gm)
        t = t * lax.rsqrt(ss * (1.0 / DIFF_DK) + EPS) * gw
        outs = []
        for hh in range(DIFF_HEADS):
            th = t[:, hh * LANES:(hh + 1) * LANES]
            up = pltpu.roll(th, LANES - ROT_DIM // 2, 1)
            dn = pltpu.roll(th, ROT_DIM // 2, 1)
            outs.append(th * cos + up * sa + dn * sb)
        return jnp.concatenate(outs, axis=1)

    q = norm_rope(_dot(h, wq_ref[...]), qn_ref[...])
    qb_ref[...] = (q * Q_SCALE).astype(BF16)
    k = norm_rope(_dot(h, wk_ref[...]), kn_ref[...])
    if k_pos_minor:
        kf_ref[...] = k.T.reshape(DIFF_HEADS, LANES, tm)
    else:
        kf_ref[...] = k
        maybe_kb_ref[0][...] = k.astype(BF16)


def _pre_call(x2d, tabs, p, tm, n_pos_tiles, batch, k_pos_minor):
    n, d = x2d.shape
    grid = (n // tm,)
    row = lambda w: pl.BlockSpec((tm, w), lambda i: (i, 0))
    tab = pl.BlockSpec((tm, LANES), lambda i: (i % n_pos_tiles, 0))
    weights = [p['n1'], p['wg1'], p['wu1'], p['wd1'], p['nm'], p['w_conv'], p['w_z'], p['w_ba'],
               p['w_q'], p['w_k'], p['w_v'], p['qn'], p['kn'], p['gmat']]
    in_specs = [row(d), tab, tab, tab] + [_const_spec(w.shape) for w in weights]
    outs = [(d, F32), (CONV_CH, F32), (GDN_V, F32), (2 * LANES, F32), (DIFF_QK, BF16), (DIFF_V, BF16)]
    out_specs = [row(w) for w, _ in outs]
    out_shape = [jax.ShapeDtypeStruct((n, w), dt) for w, dt in outs]
    if k_pos_minor:
        l = n // batch
        tiles = l // tm
        out_specs += [pl.BlockSpec((None, DIFF_HEADS, LANES, tm), lambda i: (i // tiles, 0, 0, i % tiles)),
                      pl.BlockSpec((tm * DIFF_HEADS, DIFF_DV), lambda i: (i, 0))]
        out_shape += [jax.ShapeDtypeStruct((batch, DIFF_HEADS, LANES, l), F32),
                      jax.ShapeDtypeStruct((n * DIFF_HEADS, DIFF_DV), F32)]
    else:
        out_specs += [row(DIFF_QK), row(DIFF_V), row(DIFF_QK)]
        out_shape += [jax.ShapeDtypeStruct((n, DIFF_QK), F32), jax.ShapeDtypeStruct((n, DIFF_V), F32),
                      jax.ShapeDtypeStruct((n, DIFF_QK), BF16)]
    return pl.pallas_call(
        functools.partial(_pre_kernel, k_pos_minor=k_pos_minor),
        grid=grid,
        in_specs=in_specs,
        out_specs=out_specs,
        out_shape=out_shape,
        compiler_params=pltpu.CompilerParams(dimension_semantics=("arbitrary",),
                                             vmem_limit_bytes=VMEM_LIMIT),
        name="pre",
    )(x2d, *tabs, *weights)


def _post_kernel(x1_ref, og_ref, od_ref, wo_ref, n2_ref, wg_ref, wu_ref, wd_ref, y_ref):
    mixed = jnp.concatenate([og_ref[...], od_ref[...]], axis=1)
    x2 = x1_ref[...] + _dot(mixed, wo_ref[...])
    xn = _rms(x2, n2_ref[...]).astype(BF16)
    y_ref[...] = x2 + 0.5 * _swiglu(xn, wg_ref, wu_ref, wd_ref)


def _post_call(x1, og, od, p, tm):
    n, d = x1.shape
    row = lambda w: pl.BlockSpec((tm, w), lambda i: (i, 0))
    weights = [p['wo'], p['n2'], p['wg2'], p['wu2'], p['wd2']]
    return pl.pallas_call(
        _post_kernel,
        grid=(n // tm,),
        in_specs=[row(d), row(GDN_V), row(DIFF_V)] + [_const_spec(w.shape) for w in weights],
        out_specs=row(d),
        out_shape=jax.ShapeDtypeStruct((n, d), F32),
        compiler_params=pltpu.CompilerParams(dimension_semantics=("arbitrary",),
                                             vmem_limit_bytes=VMEM_LIMIT),
        name="post",
    )(x1, og, od, *weights)


def _exact3(m01, x):
    x1 = x.astype(BF16)
    r1 = x - x1.astype(F32)
    x2 = r1.astype(BF16)
    x3 = (r1 - x2.astype(F32)).astype(BF16)
    return _dot(m01, x1) + (_dot(m01, x2) + _dot(m01, x3))


CONV_PAD = 8


def _gdn_kernel(x_ref, z_ref, ba_ref, hist_ref, s0_ref, cw_ref, alog_ref, dtb_ref, on_ref, hm_ref,
                o_ref, s_out_ref, xp_ref, *maybe_s_ref, tc, c, seqs):
    pad = CONV_PAD
    nchunk = tc // c
    assert c % INV_BLOCK == 0 and tc % LANES == 0 and (seqs == 1 or seqs == nchunk)
    cw = cw_ref[...]
    x = x_ref[...]

    def conv(xp, rows):
        y = xp[rows] * cw[CONV_W - 1:CONV_W, :]
        for k in range(1, CONV_W):
            y = y + pltpu.roll(xp, k, 0)[rows] * cw[CONV_W - 1 - k:CONV_W - k, :]
        return y

    if seqs == 1:
        s_ref, = maybe_s_ref
        t = pl.program_id(1)

        @pl.when(t == 0)
        def _():
            xp_ref[0:pad, :] = jnp.zeros((pad, CONV_CH), F32)
            xp_ref[pad - (CONV_W - 1):pad, :] = hist_ref[...]
            s_ref[...] = s0_ref[...]

        xp_ref[pad:pad + tc, :] = x
        y = conv(xp_ref[...], slice(pad, pad + tc))
        xp_ref[pad - (CONV_W - 1):pad, :] = x[tc - (CONV_W - 1):tc, :]
    else:
        stride = pad + c
        xp_ref[...] = jnp.zeros(xp_ref.shape, F32)
        for b in range(seqs):
            xp_ref[b * stride + pad - (CONV_W - 1):b * stride + pad, :] = hist_ref[b]
            xp_ref[b * stride + pad:(b + 1) * stride, :] = x[b * c:(b + 1) * c, :]
        yp = conv(xp_ref[...], slice(None))
        y = jnp.concatenate([yp[b * stride + pad:(b + 1) * stride, :] for b in range(seqs)], axis=0)
    y = _silu(y)
    hm = hm_ref[...]
    yq = y[:, :GDN_QK]
    yk = y[:, GDN_QK:2 * GDN_QK]
    yq = yq * (lax.rsqrt(_dot((yq * yq).astype(BF16), hm) + EPS) * (GDN_DK ** -0.5))
    yk = yk * lax.rsqrt(_dot((yk * yk).astype(BF16), hm) + EPS)

    ba = ba_ref[...]
    beta = jax.nn.sigmoid(ba[:, :LANES])
    g = -jnp.exp(alog_ref[...]) * jax.nn.softplus(ba[:, LANES:] + dtb_ref[...])
    ri = lax.broadcasted_iota(jnp.int32, (tc, tc), 0)
    ci = lax.broadcasted_iota(jnp.int32, (tc, tc), 1)
    incl = ((ri // c) == (ci // c)) & (ci <= ri)
    same_blk = (ri // INV_BLOCK) == (ci // INV_BLOCK)
    diag = ri == ci
    eye = jnp.where(diag, 1.0, 0.0)
    gcum = _exact3(jnp.where(incl, 1.0, 0.0).astype(BF16), g)
    gcum_t = gcum.T
    on = on_ref[...]

    heads = range(GDN_HEADS)

    def per_head(f, *lists):
        return [f(*args) for args in zip(*lists)]

    qs = [yq[:, h * GDN_DK:(h + 1) * GDN_DK] for h in heads]
    ks = [yk[:, h * GDN_DK:(h + 1) * GDN_DK] for h in heads]
    vs = [y[:, 2 * GDN_QK + h * GDN_DV:2 * GDN_QK + (h + 1) * GDN_DV] for h in heads]
    gcs = [gcum[:, h:h + 1] for h in heads]
    bcs = [beta[:, h:h + 1] for h in heads]
    decs = [jnp.exp(jnp.where(incl, gcs[h] - gcum_t[h:h + 1, :], -jnp.inf)) for h in heads]
    kbs = per_head(lambda k: k.astype(BF16), ks)
    a_s = per_head(lambda bc, kb, dec: jnp.where(diag, 0.0, bc * _dot_nt(kb, kb) * dec), bcs, kbs, decs)
    qks = per_head(lambda q, kb, dec: (_dot_nt(q.astype(BF16), kb) * dec).astype(BF16), qs, kbs, decs)

    assert INV_BLOCK == 16 and c in (INV_BLOCK, 4 * INV_BLOCK)
    bf = lambda xs: per_head(lambda x: x.astype(BF16), xs)
    mm = lambda xs, ys: per_head(_dot, xs, ys)
    ds = per_head(lambda a: jnp.where(same_blk, a, 0.0), a_s) if c > INV_BLOCK else a_s
    sd = bf(ds)
    d2 = mm(sd, sd)
    s2 = bf(d2)
    d3 = mm(sd, s2)
    d4 = mm(s2, s2)
    s4 = bf(d4)
    n1 = per_head(lambda d, x2, x3: eye - d + x2 - x3, ds, d2, d3)
    n2 = per_head(jnp.add, n1, mm(bf(n1), s4))
    d8 = mm(s4, s4)
    tinv = per_head(jnp.add, n2, mm(bf(n2), bf(d8)))
    if c > INV_BLOCK:
        std = bf(tinv)
        ms = mm(std, bf(per_head(jnp.subtract, a_s, ds)))
        sm = bf(ms)
        m2 = mm(sm, sm)
        m3 = mm(sm, bf(m2))
        tinv = mm(bf(per_head(lambda m, x2, x3: eye - m + x2 - x3, ms, m2, m3)), std)

    egs = per_head(jnp.exp, gcs)
    rhs = per_head(lambda v, k, bc, eg: jnp.concatenate([v * bc, k * (bc * eg)], axis=1), vs, ks, bcs, egs)
    sols = mm(bf(tinv), bf(rhs))
    u0s = [sol[:, :GDN_DV] for sol in sols]
    wbs = [sol[:, GDN_DV:].astype(BF16) for sol in sols]
    qds = per_head(lambda q, eg: (q * eg).astype(BF16), qs, egs)

    def chunk_terms(ic):
        rs = slice(ic * c, (ic + 1) * c)
        gends = [gcs[h][(ic + 1) * c - 1:(ic + 1) * c, :] for h in heads]
        kds = [(ks[h][rs] * jnp.exp(gends[h] - gcs[h][rs])).astype(BF16) for h in heads]
        return rs, gends, kds

    def state_terms(rs, states):
        return [_dot(jnp.concatenate([wbs[h][rs], qds[h][rs]], axis=0), states[h].astype(BF16)) for h in heads]

    def next_state(states, gends, kds, us_c):
        return [states[h] * jnp.exp(gends[h]) + lax.dot_general(
            kds[h], us_c[h], (((0,), (0,)), ((), ())), preferred_element_type=F32) for h in heads]

    us = [[] for _ in heads]
    outs = [[] for _ in heads]
    if seqs == 1:
        states = [s_ref[h] for h in heads]
        for ic in range(nchunk):
            rs, gends, kds = chunk_terms(ic)
            wss = state_terms(rs, states)
            for h in heads:
                us[h].append((u0s[h][rs] - wss[h][:c]).astype(BF16))
            zpad = [jnp.zeros((tc - (ic + 1) * c, GDN_DV), BF16)] if ic + 1 < nchunk else []
            for h in heads:
                outs[h].append(wss[h][c:] + _dot(qks[h][rs], jnp.concatenate(us[h] + zpad, axis=0)))
            states = next_state(states, gends, kds, [us[h][ic] for h in heads])
        for h in heads:
            s_ref[h] = states[h]
        o = [jnp.concatenate(outs[h], axis=0) for h in heads]

        @pl.when(t == pl.num_programs(1) - 1)
        def _():
            s_out_ref[...] = s_ref[...]
    else:
        for ic in range(nchunk):
            rs, gends, kds = chunk_terms(ic)
            states = [s0_ref[ic, h] for h in heads]
            wss = state_terms(rs, states)
            us_c = [(u0s[h][rs] - wss[h][:c]).astype(BF16) for h in heads]
            new = next_state(states, gends, kds, us_c)
            for h in heads:
                us[h].append(us_c[h])
                outs[h].append(wss[h][c:])
                s_out_ref[ic, h] = new[h]
        o = [jnp.concatenate(outs[h], axis=0) + _dot(qks[h], jnp.concatenate(us[h], axis=0)) for h in heads]
    for h in heads:
        cols = slice(h * GDN_DV, (h + 1) * GDN_DV)
        o_ref[:, cols] = (_rms(o[h], on) * _silu(z_ref[:, cols])).astype(o_ref.dtype)


def _gdn_call(conv, z, ba, hist, s0, p, tc, c):
    b, l, _ = conv.shape
    small = [p['conv_w'], p['a_log'], p['dt_bias'], p['gdn_on'], p['head_mat']]
    state = (GDN_HEADS, GDN_DK, GDN_DV)
    if l >= tc:
        seqs = 1
        grid = (b, l // tc)
        row = lambda w: pl.BlockSpec((None, tc, w), lambda i, j: (i, j, 0))
        per_b = lambda shape: pl.BlockSpec((None,) + shape, lambda i, j: (i,) + (0,) * len(shape))
        scratch = [pltpu.VMEM((tc + CONV_PAD, CONV_CH), F32), pltpu.VMEM(state, F32)]
    else:
        assert l == c and b * l == tc
        seqs = b
        grid = (1, 1)
        conv, z, ba = (t.reshape(1, tc, t.shape[-1]) for t in (conv, z, ba))
        row = lambda w: pl.BlockSpec((None, tc, w), lambda i, j: (0, 0, 0))
        per_b = lambda shape: pl.BlockSpec((b,) + shape, lambda i, j: (0,) * (len(shape) + 1))
        scratch = [pltpu.VMEM((seqs * (c + CONV_PAD), CONV_CH), F32)]
    o, s_new = pl.pallas_call(
        functools.partial(_gdn_kernel, tc=tc, c=c, seqs=seqs),
        grid=grid,
        in_specs=[row(CONV_CH), row(GDN_V), row(2 * LANES), per_b((CONV_W - 1, CONV_CH)), per_b(state)] + [
            pl.BlockSpec(w.shape, lambda i, j: (0, 0)) for w in small],
        out_specs=[row(GDN_V), per_b(state)],
        out_shape=[jax.ShapeDtypeStruct(conv.shape[:2] + (GDN_V,), BF16),
                   jax.ShapeDtypeStruct((b,) + state, F32)],
        scratch_shapes=scratch,
        compiler_params=pltpu.CompilerParams(dimension_semantics=("arbitrary", "arbitrary"),
                                             vmem_limit_bytes=VMEM_LIMIT),
        name="gdn",
    )(conv, z, ba, hist, s0, *small)
    return o.reshape(b, l, GDN_V), s_new


def _stack_maps(q):
    lane = lax.broadcasted_iota(jnp.int32, q.shape, 1)
    zero = jnp.zeros_like(q)
    return jnp.concatenate([jnp.where(lane < DIFF_DK, q, zero), jnp.where(lane >= DIFF_DK, q, zero)],
                           axis=0)


def _diff_finish(acc, l, t, lq_ref, lk_ref, dn_ref, lam_init):
    lam_e = jnp.exp(jnp.sum(lq_ref[...] * lk_ref[...], axis=-1, keepdims=True))
    lam = lam_e[0:1] - lam_e[1:2] + lam_init
    o = acc[:t] / l[:t] - lam * (acc[t:] / l[t:])
    return _rms(o, dn_ref[...]) * (1.0 - lam_init)


def _halves_max(s):
    return jnp.maximum(s[:, :LANES], s[:, LANES:])


def _attn_prompt_kernel(q_ref, kt_ref, v_ref, bias_ref, lq_ref, lk_ref, dn_ref, o_ref,
                        kt_scr, vx_scr, s_scr, m_scr, acc_scr, *, tq, tk, lam_init):
    i = pl.program_id(2)
    assert tk == 2 * LANES and tq % tk == 0
    nsub = tq // tk

    @pl.when(i == 0)
    def _():
        for j in range(kt_scr.shape[0]):
            kt_scr[j] = kt_ref[:, j * tk:(j + 1) * tk].astype(BF16)
        vx_scr[:, :DIFF_DV] = v_ref[...]
        vx_scr[:, DIFF_DV:] = jnp.ones((vx_scr.shape[0], LANES), BF16)

    qq = _stack_maps(q_ref[...])

    def scores(j):
        return _dot(qq, kt_scr[j])

    def weighted_values(ss, blocks, mm):
        p = [jnp.exp2(s[:, half * LANES:(half + 1) * LANES] - mm) for s in ss for half in range(tk // LANES)]
        vx = [vx_scr[pl.ds(pl.multiple_of(j * tk, tk), tk), :] for j in blocks]
        return _dot(jnp.concatenate(p, axis=1).astype(BF16), jnp.concatenate(vx, axis=0))

    def pass1(blocks):
        ss = [scores(j) for j in blocks]
        for j, s in zip(blocks, ss):
            s_scr[j] = s
        m_scr[...] = jnp.maximum(m_scr[...], functools.reduce(jnp.maximum, [_halves_max(s) for s in ss]))

    def pass2(blocks):
        acc_scr[...] += weighted_values([s_scr[j] for j in blocks], blocks, m_scr[...])

    def over_off_diagonal(body):
        def trip(t, carry):
            body([t * 2 * nsub + u for u in range(2 * nsub)])
            return carry

        lax.fori_loop(0, i // 2, trip, 0)

        @pl.when(i % 2 == 1)
        def _():
            body([(i - 1) * nsub + u for u in range(nsub)])

    diag_blocks = [i * nsub + d for d in range(nsub)]
    s_diag = [scores(j) + bias_ref[d] for d, j in enumerate(diag_blocks)]
    m_scr[...] = functools.reduce(jnp.maximum, [_halves_max(s) for s in s_diag])
    over_off_diagonal(pass1)
    m = jnp.broadcast_to(jnp.max(m_scr[...], axis=-1, keepdims=True), m_scr.shape)
    m_scr[...] = m
    acc_scr[...] = weighted_values(s_diag, diag_blocks, m)
    over_off_diagonal(pass2)
    acc = acc_scr[...]
    o_ref[...] = _diff_finish(acc[:, :DIFF_DV], acc[:, DIFF_DV:], tq, lq_ref, lk_ref, dn_ref,
                              lam_init).astype(o_ref.dtype)


def _attn_prompt_call(qb, kt, vb, p, tq, tk, lam_init):
    b, l, _ = qb.shape
    small = [p['lambda_q'], p['lambda_k'], p['diff_on']]
    r = (jnp.arange(2 * tq) % tq)[None, :, None] // CHUNK
    cidx = (jnp.arange(tk)[None, None, :] + tk * jnp.arange(tq // tk)[:, None, None]) // CHUNK
    bias = jnp.where(cidx <= r, 0.0, -jnp.inf).astype(F32)
    return pl.pallas_call(
        functools.partial(_attn_prompt_kernel, tq=tq, tk=tk, lam_init=lam_init),
        grid=(b, DIFF_HEADS, l // tq),
        in_specs=[pl.BlockSpec((None, tq, LANES), lambda bi, h, i: (bi, i, h)),
                  pl.BlockSpec((None, None, LANES, l), lambda bi, h, i: (bi, h, 0, 0)),
                  pl.BlockSpec((None, l, LANES), lambda bi, h, i: (bi, 0, h)),
                  _const_spec(bias.shape)] + [
                      pl.BlockSpec(w.shape, lambda bi, h, i: (0, 0)) for w in small],
        out_specs=pl.BlockSpec((None, tq, LANES), lambda bi, h, i: (bi, i, h)),
        out_shape=jax.ShapeDtypeStruct((b, l, DIFF_V), BF16),
        scratch_shapes=[pltpu.VMEM((l // tk, LANES, tk), BF16),
                        pltpu.VMEM((l, DIFF_DV + LANES), BF16),
                        pltpu.VMEM((max((l - tq) // tk, 1), 2 * tq, tk), F32),
                        pltpu.VMEM((2 * tq, LANES), F32),
                        pltpu.VMEM((2 * tq, DIFF_DV + LANES), F32)],
        compiler_params=pltpu.CompilerParams(
            dimension_semantics=("arbitrary", "arbitrary", "arbitrary"),
            vmem_limit_bytes=VMEM_LIMIT),
        name="attn_prompt",
    )(qb, kt, vb, bias, *small)


def _attn_sample_kernel(q_ref, kn_ref, vn_ref, ckt_ref, cv_ref, lq_ref, lk_ref, dn_ref, o_ref,
                        *, t, lam_init):
    past = ckt_ref.shape[-1]
    for h in range(DIFF_HEADS):
        cols = slice(h * LANES, (h + 1) * LANES)
        qq = _stack_maps(q_ref[:, cols])
        s_c = _dot(qq, ckt_ref[h].astype(BF16))
        s_n = _dot_nt(qq, kn_ref[:, cols])
        m = jnp.maximum(jnp.max(s_c, axis=-1, keepdims=True), jnp.max(s_n, axis=-1, keepdims=True))
        p_c = jnp.exp2(s_c - m)
        p_n = jnp.exp2(s_n - m)
        l = jnp.sum(p_c, axis=-1, keepdims=True) + jnp.sum(p_n, axis=-1, keepdims=True)
        cv = cv_ref[pl.ds(h, past, stride=DIFF_HEADS), :].astype(BF16)
        acc = _dot(p_c.astype(BF16), cv) + _dot(p_n.astype(BF16), vn_ref[:, cols])
        o_ref[:, cols] = _diff_finish(acc, l, t, lq_ref, lk_ref, dn_ref, lam_init).astype(o_ref.dtype)


def _attn_sample_call(qb, kb, vb, ckt, cv, p, lam_init):
    b, t, _ = qb.shape
    past = ckt.shape[-1]
    small = [p['lambda_q'], p['lambda_k'], p['diff_on']]
    new = pl.BlockSpec((None, t, DIFF_V), lambda bi: (bi, 0, 0))
    return pl.pallas_call(
        functools.partial(_attn_sample_kernel, t=t, lam_init=lam_init),
        grid=(b,),
        in_specs=[new, new, new,
                  pl.BlockSpec((None, DIFF_HEADS, LANES, past), lambda bi: (bi, 0, 0, 0)),
                  pl.BlockSpec((None, past * DIFF_HEADS, DIFF_DV), lambda bi: (bi, 0, 0))] + [
                      pl.BlockSpec(w.shape, lambda bi: (0, 0)) for w in small],
        out_specs=new,
        out_shape=jax.ShapeDtypeStruct((b, t, DIFF_V), BF16),
        compiler_params=pltpu.CompilerParams(dimension_semantics=("arbitrary",),
                                             vmem_limit_bytes=VMEM_LIMIT),
        name="attn_sample",
    )(qb, kb, vb, ckt, cv, *small)


def _rope_tables(pos):
    half = ROT_DIM // 2
    inv = jnp.float32(ROPE_THETA) ** (-jnp.arange(half, dtype=F32) * 2.0 / ROT_DIM)
    ang = pos.astype(F32)[:, None] * inv[None, :]
    cos, sin = jnp.cos(ang), jnp.sin(ang)
    n = pos.shape[0]
    rest = DIFF_DK - ROT_DIM
    one = jnp.ones((n, rest), F32)
    zero = jnp.zeros((n, rest), F32)
    zh = jnp.zeros((n, half), F32)
    c64 = jnp.concatenate([cos, cos, one], axis=1)
    a64 = jnp.concatenate([-sin, zh, zero], axis=1)
    b64 = jnp.concatenate([zh, sin, zero], axis=1)
    rep = LANES // DIFF_DK
    return tuple(jnp.tile(t, (1, rep)) for t in (c64, a64, b64))


def _layer_params(w, l):
    d = w['w_in'].shape[1]
    d_ff = w['ffn1_w_down'].shape[1]
    win = w['w_in'][l]
    o = 0
    w_conv = win[:, o:o + CONV_CH]; o += CONV_CH
    w_b = win[:, o:o + GDN_HEADS]; o += GDN_HEADS
    w_a = win[:, o:o + GDN_HEADS]; o += GDN_HEADS
    w_z = win[:, o:o + GDN_V]; o += GDN_V
    w_q = win[:, o:o + DIFF_QK]; o += DIFF_QK
    w_k = win[:, o:o + DIFF_QK]; o += DIFF_QK
    w_v = win[:, o:o + DIFF_V]
    zpad = jnp.zeros((d, LANES - GDN_HEADS), win.dtype)
    w_ba = jnp.concatenate([w_b, zpad, w_a, zpad], axis=1)
    hpad = lambda v: jnp.concatenate([v.astype(F32), jnp.zeros((LANES - GDN_HEADS,), F32)])[None, :]
    grp = jnp.arange(DIFF_QK) // DIFF_DK
    bf = lambda t: t.astype(BF16)
    r2 = lambda v: v.astype(F32)[None, :]
    return dict(
        n1=r2(w['ffn1_norm'][l]), wg1=bf(w['ffn1_w_gu'][l][:, :d_ff]), wu1=bf(w['ffn1_w_gu'][l][:, d_ff:]),
        wd1=bf(w['ffn1_w_down'][l]),
        nm=r2(w['mix_norm'][l]), w_conv=bf(w_conv), w_z=bf(w_z), w_ba=bf(w_ba), w_q=bf(w_q), w_k=bf(w_k),
        w_v=bf(w_v),
        qn=r2(jnp.tile(w['q_norm'][l], DIFF_QK // DIFF_DK)), kn=r2(jnp.tile(w['k_norm'][l], DIFF_QK // DIFF_DK)),
        gmat=(grp[:, None] == grp[None, :]).astype(BF16),
        head_mat=(grp[:, None] // 2 == grp[None, :] // 2).astype(BF16),
        conv_w=w['conv_w'][l].astype(F32), a_log=hpad(w['a_log'][l]), dt_bias=hpad(w['dt_bias'][l]),
        gdn_on=r2(w['gdn_out_norm'][l]),
        lambda_q=w['lambda_q'][l].astype(F32), lambda_k=w['lambda_k'][l].astype(F32),
        diff_on=r2(w['diff_out_norm'][l]),
        wo=bf(w['w_out'][l]), n2=r2(w['ffn2_norm'][l]), wg2=bf(w['ffn2_w_gu'][l][:, :d_ff]),
        wu2=bf(w['ffn2_w_gu'][l][:, d_ff:]), wd2=bf(w['ffn2_w_down'][l]),
    )


def _pick_tile(n, pref):
    t = min(n, pref)
    assert n % t == 0
    return t


def _layer(x, pos, k_hist, v_hist, conv_hist, s0, p, lam_init):
    b, l, d = x.shape
    n = b * l
    tm = _pick_tile(n, 256)
    tabs = _rope_tables(pos)
    if l >= tm:
        assert l % tm == 0
        n_pos_tiles = l // tm
    else:
        assert tm % l == 0
        tabs = tuple(jnp.tile(t, (tm // l, 1)) for t in tabs)
        n_pos_tiles = 1
    prompt = k_hist is None
    pre = _pre_call(x.reshape(n, d), tabs, p, tm, n_pos_tiles, b, prompt)
    x1, conv, z, ba, qb, vb, kf, vf = pre[:8]
    c = min(CHUNK, l)
    assert l % c == 0
    tc = GDN_TILE
    assert l % tc == 0 or n == tc
    if conv_hist is None:
        conv_hist = jnp.zeros((b, CONV_W - 1, CONV_CH), F32)
        s0 = jnp.zeros((b, GDN_HEADS, GDN_DK, GDN_DV), F32)
    conv3 = conv.reshape(b, l, CONV_CH)
    og, s_new = _gdn_call(conv3, z.reshape(b, l, GDN_V), ba.reshape(b, l, 2 * LANES),
                          conv_hist.astype(F32), s0.astype(F32), p, tc, c)
    conv_new = jnp.concatenate([conv_hist.astype(F32), conv3], axis=1)[:, -(CONV_W - 1):]
    q3 = qb.reshape(b, l, DIFF_QK)
    v3 = vb.reshape(b, l, DIFF_V)
    if prompt:
        tk = 2 * LANES
        od = _attn_prompt_call(q3, kf, v3, p, _pick_tile(l, 2 * tk), tk, lam_init)
        k_out = jnp.transpose(kf.reshape(b, DIFF_HEADS, 2, DIFF_DK, l), (0, 4, 1, 2, 3))
    else:
        past = k_hist.shape[1]
        ckt = jnp.transpose(k_hist, (0, 2, 3, 4, 1)).reshape(b, DIFF_HEADS, LANES, past)
        od = _attn_sample_call(q3, pre[8].reshape(b, l, DIFF_QK), v3, ckt,
                               v_hist.reshape(b, past * DIFF_HEADS, DIFF_DV), p, lam_init)
        k_out = kf.reshape(b, l, DIFF_HEADS, 2, DIFF_DK)
    y = _post_call(x1, og.reshape(n, GDN_V), od.reshape(n, DIFF_V), p, tm)
    return (y.reshape(b, l, d), k_out, vf.reshape(b, l, DIFF_HEADS, DIFF_DV), s_new, conv_new)


def kernel(x_prompt, x_sample, cache_k, cache_v, state_gdn, state_conv, ffn1_norm, ffn1_w_gu, ffn1_w_down,
           mix_norm, w_in, conv_w, a_log, dt_bias, gdn_out_norm, q_norm, k_norm, lambda_q, lambda_k,
           diff_out_norm, w_out, ffn2_norm, ffn2_w_gu, ffn2_w_down):
    w = dict(ffn1_norm=ffn1_norm, ffn1_w_gu=ffn1_w_gu, ffn1_w_down=ffn1_w_down, mix_norm=mix_norm, w_in=w_in,
             conv_w=conv_w, a_log=a_log, dt_bias=dt_bias, gdn_out_norm=gdn_out_norm, q_norm=q_norm,
             k_norm=k_norm, lambda_q=lambda_q, lambda_k=lambda_k, diff_out_norm=diff_out_norm, w_out=w_out,
             ffn2_norm=ffn2_norm, ffn2_w_gu=ffn2_w_gu, ffn2_w_down=ffn2_w_down)
    depth = w_in.shape[0]
    pos_p = jnp.arange(x_prompt.shape[1])
    pos_s = cache_k.shape[2] + jnp.arange(x_sample.shape[1])
    hp, hs = x_prompt, x_sample
    outs = [[] for _ in range(8)]
    for l in range(depth):
        lam_init = 0.8 - 0.6 * math.exp(-0.3 * l)
        p = _layer_params(w, l)
        hp, kp, vp, sp, cp = _layer(hp, pos_p, None, None, None, None, p, lam_init)
        hs, ks, vs, ss, cs = _layer(hs, pos_s, cache_k[l], cache_v[l], state_conv[l], state_gdn[l], p, lam_init)
        for acc, val in zip(outs, (kp, vp, sp, cp, ks, vs, ss, cs)):
            acc.append(val)
    return (hp, hs) + tuple(jnp.stack(o) for o in outs)
```

```python
import functools
import math
import types

import jax
import jax.numpy as jnp
from jax import lax
from jax.experimental import pallas as pl
from jax.experimental.pallas import tpu as pltpu

F32 = jnp.float32
BF16 = jnp.bfloat16

EPS = 1e-6
CHUNK = 64
GDN_HEADS = 4
GDN_DK = 128
GDN_DV = 128
CONV_W = 4
DIFF_HEADS = 4
DIFF_DK = 64
DIFF_DV = 128
ROT_DIM = DIFF_DK // 4
ROPE_THETA = 500000.0
GDN_QK = GDN_HEADS * GDN_DK
GDN_V = GDN_HEADS * GDN_DV
CONV_CH = 2 * GDN_QK + GDN_V
DIFF_QK = DIFF_HEADS * 2 * DIFF_DK
DIFF_V = DIFF_HEADS * DIFF_DV

LANES = 128
INV_BLOCK = 16
FF_CHUNK = 256
ROW_TILE = 256
CONV_PAD = 8
VMEM_LIMIT = 56 * 1024 * 1024
Q_SCALE = (DIFF_DK ** -0.5) * math.log2(math.e)


def _dot(a, b):
    return jnp.dot(a, b, preferred_element_type=F32)


def _dot_nt(a, b):
    return lax.dot_general(a, b, (((1,), (1,)), ((), ())), preferred_element_type=F32)


def _rms(x, g):
    return x * lax.rsqrt(jnp.mean(x * x, axis=-1, keepdims=True) + EPS) * g


def _silu(x):
    return x * jax.nn.sigmoid(x)


def _swiglu(xn, wg_ref, wu_ref, wd_ref):
    g = _dot(xn, wg_ref[...])
    u = _dot(xn, wu_ref[...])
    act = (_silu(g) * u).astype(BF16)
    return _dot(act, wd_ref[...])


def _const_spec(shape):
    nd = len(shape)
    return pl.BlockSpec(shape, lambda *_: (0,) * nd, pipeline_mode=pl.Buffered(1))


_COMPILER_1D = pltpu.CompilerParams(dimension_semantics=("arbitrary",), vmem_limit_bytes=VMEM_LIMIT)


PRE_WEIGHTS = ('n1', 'wg1', 'wu1', 'wd1', 'nm', 'w_conv', 'w_z', 'w_ba', 'w_q', 'w_k', 'w_v', 'qn', 'kn', 'gmat')


def _drain(stages):
    for _ in stages:
        pass


def _interleave(*stage_lists):
    live = [iter(s) for s in stage_lists]
    while live:
        for s in list(live):
            try:
                next(s)
            except StopIteration:
                live.remove(s)


def _pre_stages(out, x, cos, sa, sb, w):
    xn = _rms(x, w.n1[...]).astype(BF16)
    yield
    d_ff = w.wd1.shape[0]
    assert d_ff % FF_CHUNK == 0
    ffn = None
    for j in range(0, d_ff, FF_CHUNK):
        cols = slice(j, j + FF_CHUNK)
        act = (_silu(_dot(xn, w.wg1[:, cols])) * _dot(xn, w.wu1[:, cols])).astype(BF16)
        part = _dot(act, w.wd1[cols, :])
        ffn = part if ffn is None else ffn + part
        yield
    x1 = x + 0.5 * ffn
    out.x1 = x1
    h = _rms(x1, w.nm[...]).astype(BF16)
    yield
    conv_parts = []
    for j in range(0, CONV_CH, GDN_QK):
        conv_parts.append(_dot(h, w.w_conv[:, j:j + GDN_QK]))
        yield
    out.conv_in = jnp.concatenate(conv_parts, axis=1)
    out.z = _dot(h, w.w_z[...])
    out.ba = _dot(h, w.w_ba[...])
    yield
    v = _dot(h, w.w_v[...])
    out.v = v
    yield
    gm = w.gmat[...]

    def norm_rope(t, gw):
        ss = _dot((t * t).astype(BF16), gm)
        t = t * lax.rsqrt(ss * (1.0 / DIFF_DK) + EPS) * gw
        outs = []
        for hh in range(DIFF_HEADS):
            th = t[:, hh * LANES:(hh + 1) * LANES]
            up = pltpu.roll(th, LANES - ROT_DIM // 2, 1)
            dn = pltpu.roll(th, ROT_DIM // 2, 1)
            outs.append(th * cos + up * sa + dn * sb)
        return jnp.concatenate(outs, axis=1)

    out.qb = (norm_rope(_dot(h, w.w_q[...]), w.qn[...]) * Q_SCALE).astype(BF16)
    yield
    out.k = norm_rope(_dot(h, w.w_k[...]), w.kn[...])


def _pre_kernel(x_ref, cos_ref, sa_ref, sb_ref, *refs):
    nw = len(PRE_WEIGHTS)
    w = types.SimpleNamespace(**dict(zip(PRE_WEIGHTS, refs[:nw])))
    x1_ref, conv_ref, z_ref, ba_ref, qb_ref, vb_ref, kf_ref, vf_ref, kb_ref = refs[nw:]
    r = types.SimpleNamespace()
    _drain(_pre_stages(r, x_ref[...], cos_ref[...], sa_ref[...], sb_ref[...], w))
    x1_ref[...] = r.x1
    conv_ref[...] = r.conv_in
    z_ref[...] = r.z
    ba_ref[...] = r.ba
    qb_ref[...] = r.qb
    vb_ref[...] = r.v.astype(BF16)
    kf_ref[...] = r.k
    vf_ref[...] = r.v
    kb_ref[...] = r.k.astype(BF16)


def _pre_call(x2d, tabs, p, tm):
    n, d = x2d.shape
    row = lambda w: pl.BlockSpec((tm, w), lambda i: (i, 0))
    weights = [p[name] for name in PRE_WEIGHTS]
    outs = [(d, F32), (CONV_CH, F32), (GDN_V, F32), (2 * LANES, F32), (DIFF_QK, BF16), (DIFF_V, BF16),
            (DIFF_QK, F32), (DIFF_V, F32), (DIFF_QK, BF16)]
    return pl.pallas_call(
        _pre_kernel,
        grid=(n // tm,),
        in_specs=[row(d), row(LANES), row(LANES), row(LANES)] + [_const_spec(w.shape) for w in weights],
        out_specs=[row(w) for w, _ in outs],
        out_shape=[jax.ShapeDtypeStruct((n, w), dt) for w, dt in outs],
        compiler_params=_COMPILER_1D,
        name="pre",
    )(x2d, *tabs, *weights)


def _post_kernel(x1_ref, og_ref, od_ref, wo_ref, n2_ref, wg_ref, wu_ref, wd_ref, y_ref):
    mixed = jnp.concatenate([og_ref[...], od_ref[...]], axis=1)
    x2 = x1_ref[...] + _dot(mixed, wo_ref[...])
    xn = _rms(x2, n2_ref[...]).astype(BF16)
    y_ref[...] = x2 + 0.5 * _swiglu(xn, wg_ref, wu_ref, wd_ref)


def _post_call(x1, og, od, p, tm):
    n, d = x1.shape
    row = lambda w: pl.BlockSpec((tm, w), lambda i: (i, 0))
    weights = [p['wo'], p['n2'], p['wg2'], p['wu2'], p['wd2']]
    return pl.pallas_call(
        _post_kernel,
        grid=(n // tm,),
        in_specs=[row(d), row(GDN_V), row(DIFF_V)] + [_const_spec(w.shape) for w in weights],
        out_specs=row(d),
        out_shape=jax.ShapeDtypeStruct((n, d), F32),
        compiler_params=_COMPILER_1D,
        name="post",
    )(x1, og, od, *weights)


def _exact3(m01, x):
    x1 = x.astype(BF16)
    r1 = x - x1.astype(F32)
    x2 = r1.astype(BF16)
    x3 = (r1 - x2.astype(F32)).astype(BF16)
    return _dot(m01, x1) + (_dot(m01, x2) + _dot(m01, x3))


def _causal_conv(xp, rows, cw):
    y = xp[rows] * cw[CONV_W - 1:CONV_W, :]
    for k in range(1, CONV_W):
        y = y + pltpu.roll(xp, k, 0)[rows] * cw[CONV_W - 1 - k:CONV_W - k, :]
    return y


_HEADS = range(GDN_HEADS)


def _per_head(f, *lists):
    return [f(*args) for args in zip(*lists)]


def _gdn_prepare_stages(out, y, ba, alog, dtb, hm, tc, c):
    assert c % INV_BLOCK == 0 and tc % c == 0 and tc % LANES == 0
    yq = y[:, :GDN_QK]
    yk = y[:, GDN_QK:2 * GDN_QK]
    yq = yq * (lax.rsqrt(_dot((yq * yq).astype(BF16), hm) + EPS) * (GDN_DK ** -0.5))
    yk = yk * lax.rsqrt(_dot((yk * yk).astype(BF16), hm) + EPS)
    yield

    beta = jax.nn.sigmoid(ba[:, :LANES])
    g = -jnp.exp(alog) * jax.nn.softplus(ba[:, LANES:] + dtb)
    ri = lax.broadcasted_iota(jnp.int32, (tc, tc), 0)
    ci = lax.broadcasted_iota(jnp.int32, (tc, tc), 1)
    incl = ((ri // c) == (ci // c)) & (ci <= ri)
    same_blk = (ri // INV_BLOCK) == (ci // INV_BLOCK)
    diag = ri == ci
    eye = jnp.where(diag, 1.0, 0.0)
    gcum = _exact3(jnp.where(incl, 1.0, 0.0).astype(BF16), g)
    gcum_t = gcum.T
    yield

    qs = [yq[:, h * GDN_DK:(h + 1) * GDN_DK] for h in _HEADS]
    ks = [yk[:, h * GDN_DK:(h + 1) * GDN_DK] for h in _HEADS]
    vs = [y[:, 2 * GDN_QK + h * GDN_DV:2 * GDN_QK + (h + 1) * GDN_DV] for h in _HEADS]
    gcs = [gcum[:, h:h + 1] for h in _HEADS]
    bcs = [beta[:, h:h + 1] for h in _HEADS]
    decs = [jnp.exp(jnp.where(incl, gcs[h] - gcum_t[h:h + 1, :], -jnp.inf)) for h in _HEADS]
    kbs = _per_head(lambda k: k.astype(BF16), ks)
    yield
    a_s = _per_head(lambda bc, kb, dec: jnp.where(diag, 0.0, bc * _dot_nt(kb, kb) * dec), bcs, kbs, decs)
    yield
    qks = _per_head(lambda q, kb, dec: (_dot_nt(q.astype(BF16), kb) * dec).astype(BF16), qs, kbs, decs)
    yield

    assert INV_BLOCK == 16 and c in (INV_BLOCK, 4 * INV_BLOCK)
    bf = lambda xs: _per_head(lambda x: x.astype(BF16), xs)
    mm = lambda xs, ys: _per_head(_dot, xs, ys)
    ds = _per_head(lambda a: jnp.where(same_blk, a, 0.0), a_s) if c > INV_BLOCK else a_s
    sd = bf(ds)
    d2 = mm(sd, sd)
    yield
    s2 = bf(d2)
    d3 = mm(sd, s2)
    yield
    d4 = mm(s2, s2)
    yield
    s4 = bf(d4)
    n1 = _per_head(lambda d, x2, x3: eye - d + x2 - x3, ds, d2, d3)
    n2 = _per_head(jnp.add, n1, mm(bf(n1), s4))
    yield
    d8 = mm(s4, s4)
    yield
    tinv = _per_head(jnp.add, n2, mm(bf(n2), bf(d8)))
    yield
    if c > INV_BLOCK:
        std = bf(tinv)
        ms = mm(std, bf(_per_head(jnp.subtract, a_s, ds)))
        yield
        sm = bf(ms)
        m2 = mm(sm, sm)
        yield
        m3 = mm(sm, bf(m2))
        yield
        tinv = mm(bf(_per_head(lambda m, x2, x3: eye - m + x2 - x3, ms, m2, m3)), std)
        yield

    egs = _per_head(jnp.exp, gcs)
    rhs = _per_head(lambda v, k, bc, eg: jnp.concatenate([v * bc, k * (bc * eg)], axis=1), vs, ks, bcs, egs)
    sols = mm(bf(tinv), bf(rhs))
    out.c, out.ks, out.gcs, out.qks = c, ks, gcs, qks
    out.u0s = [sol[:, :GDN_DV] for sol in sols]
    out.wbs = [sol[:, GDN_DV:].astype(BF16) for sol in sols]
    out.qds = _per_head(lambda q, eg: (q * eg).astype(BF16), qs, egs)


def _gdn_prepare(y, ba, alog, dtb, hm, tc, c):
    out = types.SimpleNamespace()
    _drain(_gdn_prepare_stages(out, y, ba, alog, dtb, hm, tc, c))
    return out


def _gdn_chunk(prep, ic, states):
    c = prep.c
    rs = slice(ic * c, (ic + 1) * c)
    gends = [prep.gcs[h][(ic + 1) * c - 1:(ic + 1) * c, :] for h in _HEADS]
    kds = [(prep.ks[h][rs] * jnp.exp(gends[h] - prep.gcs[h][rs])).astype(BF16) for h in _HEADS]
    wss = [_dot(jnp.concatenate([prep.wbs[h][rs], prep.qds[h][rs]], axis=0), states[h].astype(BF16))
           for h in _HEADS]
    us = [(prep.u0s[h][rs] - wss[h][:c]).astype(BF16) for h in _HEADS]
    new = [states[h] * jnp.exp(gends[h]) + lax.dot_general(
        kds[h], us[h], (((0,), (0,)), ((), ())), preferred_element_type=F32) for h in _HEADS]
    return us, [ws[c:] for ws in wss], new


def _gdn_consecutive_stages(out, prep, states, tc):
    c = prep.c
    nchunk = tc // c
    us = [[] for _ in _HEADS]
    outs = [[] for _ in _HEADS]
    for ic in range(nchunk):
        rs = slice(ic * c, (ic + 1) * c)
        us_c, from_state, states = _gdn_chunk(prep, ic, states)
        yield
        zpad = [jnp.zeros((tc - (ic + 1) * c, GDN_DV), BF16)] if ic + 1 < nchunk else []
        for h in _HEADS:
            us[h].append(us_c[h])
            outs[h].append(from_state[h] + _dot(prep.qks[h][rs], jnp.concatenate(us[h] + zpad, axis=0)))
        yield
    out.o = [jnp.concatenate(o, axis=0) for o in outs]
    out.states = states


def _gdn_store(o, z_cols, on, o_ref):
    for h in _HEADS:
        cols = slice(h * GDN_DV, (h + 1) * GDN_DV)
        o_ref[:, cols] = (_rms(o[h], on) * _silu(z_cols(cols))).astype(o_ref.dtype)


def _gdn_kernel(x_ref, z_ref, ba_ref, hist_ref, s0_ref, cw_ref, alog_ref, dtb_ref, on_ref, hm_ref,
                o_ref, s_out_ref, xp_ref, *, tc, c):
    seqs = tc // c
    pad = CONV_PAD
    stride = pad + c
    x = x_ref[...]
    xp_ref[...] = jnp.zeros(xp_ref.shape, F32)
    for b in range(seqs):
        xp_ref[b * stride + pad - (CONV_W - 1):b * stride + pad, :] = hist_ref[b]
        xp_ref[b * stride + pad:(b + 1) * stride, :] = x[b * c:(b + 1) * c, :]
    yp = _silu(_causal_conv(xp_ref[...], slice(None), cw_ref[...]))
    y = jnp.concatenate([yp[b * stride + pad:(b + 1) * stride, :] for b in range(seqs)], axis=0)
    prep = _gdn_prepare(y, ba_ref[...], alog_ref[...], dtb_ref[...], hm_ref[...], tc, c)
    us = [[] for _ in _HEADS]
    outs = [[] for _ in _HEADS]
    for ic in range(seqs):
        us_c, from_state, new = _gdn_chunk(prep, ic, [s0_ref[ic, h] for h in _HEADS])
        for h in _HEADS:
            us[h].append(us_c[h])
            outs[h].append(from_state[h])
            s_out_ref[ic, h] = new[h]
    o = [jnp.concatenate(outs[h], axis=0) + _dot(prep.qks[h], jnp.concatenate(us[h], axis=0)) for h in _HEADS]
    _gdn_store(o, lambda cols: z_ref[:, cols], on_ref[...], o_ref)


GDN_SMALL = ('conv_w', 'a_log', 'dt_bias', 'gdn_on', 'head_mat')


def _gdn_call(conv, z, ba, hist, s0, p, tc):
    b, c, _ = conv.shape
    assert b * c == tc
    small = [p[name] for name in GDN_SMALL]
    whole = lambda a: pl.BlockSpec(a.shape, lambda i: (0,) * a.ndim)
    ins = [conv.reshape(tc, CONV_CH), z.reshape(tc, GDN_V), ba.reshape(tc, 2 * LANES), hist, s0] + small
    o, s_new = pl.pallas_call(
        functools.partial(_gdn_kernel, tc=tc, c=c),
        grid=(1,),
        in_specs=[whole(a) for a in ins],
        out_specs=[pl.BlockSpec((tc, GDN_V), lambda i: (0, 0)), whole(s0)],
        out_shape=[jax.ShapeDtypeStruct((tc, GDN_V), BF16), jax.ShapeDtypeStruct(s0.shape, F32)],
        scratch_shapes=[pltpu.VMEM((b * (c + CONV_PAD), CONV_CH), F32)],
        compiler_params=_COMPILER_1D,
        name="gdn",
    )(*ins)
    return o.reshape(b, c, GDN_V), s_new


def _pre_gdn_kernel(x_ref, cos_ref, sa_ref, sb_ref, *refs, tiles_per_seq, c):
    nw = len(PRE_WEIGHTS)
    w = types.SimpleNamespace(**dict(zip(PRE_WEIGHTS, refs[:nw])))
    cw_ref, alog_ref, dtb_ref, on_ref, hm_ref = refs[nw:nw + len(GDN_SMALL)]
    (x1_ref, qb_ref, vb_ref, kf_ref, vf_ref, last_ref, og_ref, s_out_ref,
     xp_scr, z_scr, ba_scr, s_scr) = refs[nw + len(GDN_SMALL):]
    i = pl.program_id(0)
    tm = x_ref.shape[0]
    pad = CONV_PAD

    @pl.when(i == 0)
    def _():
        xp_scr[...] = jnp.zeros(xp_scr.shape, F32)
        z_scr[...] = jnp.zeros(z_scr.shape, F32)
        ba_scr[...] = jnp.zeros(ba_scr.shape, F32)
        s_scr[...] = jnp.zeros(s_scr.shape, F32)

    xp = xp_scr[...]
    front = types.SimpleNamespace()

    def delta_rule():
        y = _silu(_causal_conv(xp, slice(pad, pad + tm), cw_ref[...]))
        yield
        prep = types.SimpleNamespace()
        yield from _gdn_prepare_stages(prep, y, ba_scr[...], alog_ref[...], dtb_ref[...], hm_ref[...], tm, c)
        starts_seq = (i - 1) % tiles_per_seq == 0
        res = types.SimpleNamespace()
        yield from _gdn_consecutive_stages(
            res, prep, [jnp.where(starts_seq, 0.0, s_scr[h]) for h in _HEADS], tm)
        for h in _HEADS:
            s_scr[h] = res.states[h]
            s_out_ref[h] = res.states[h]
        _gdn_store(res.o, lambda cols: z_scr[:, cols], on_ref[...], og_ref)

    def row_front():
        yield from _pre_stages(front, x_ref[...], cos_ref[...], sa_ref[...], sb_ref[...], w)
        x1_ref[...] = front.x1
        qb_ref[...] = front.qb
        vb_ref[...] = front.v.astype(BF16)
        for hh in range(DIFF_HEADS):
            vf_ref[pl.ds(hh, tm, stride=DIFF_HEADS), :] = front.v[:, hh * DIFF_DV:(hh + 1) * DIFF_DV]
        kf_ref[...] = front.k.T.reshape(DIFF_HEADS, LANES, tm)
        last_ref[...] = front.conv_in[tm - pad:tm, :]

    _interleave(row_front(), delta_rule())

    keep = xp[pad + tm - (CONV_W - 1):pad + tm, :]
    xp_scr[pad - (CONV_W - 1):pad, :] = jnp.where(i % tiles_per_seq == 0, 0.0, keep)
    xp_scr[pad:pad + tm, :] = front.conv_in
    z_scr[...] = front.z
    ba_scr[...] = front.ba


def _pre_gdn_call(x2d, tabs, p, tm, batch):
    n, d = x2d.shape
    l = n // batch
    tiles = l // tm
    nt = n // tm
    cur = lambda i: jnp.minimum(i, nt - 1)
    prev = lambda i: jnp.maximum(i - 1, 0)
    row = lambda w: pl.BlockSpec((tm, w), lambda i: (cur(i), 0))
    tab = pl.BlockSpec((tm, LANES), lambda i: (cur(i) % tiles, 0))
    weights = [p[name] for name in PRE_WEIGHTS + GDN_SMALL]
    state = (GDN_HEADS, GDN_DK, GDN_DV)
    return pl.pallas_call(
        functools.partial(_pre_gdn_kernel, tiles_per_seq=tiles, c=min(CHUNK, l)),
        grid=(nt + 1,),
        in_specs=[row(d), tab, tab, tab] + [_const_spec(w.shape) for w in weights],
        out_specs=[row(d), row(DIFF_QK), row(DIFF_V),
                   pl.BlockSpec((None, DIFF_HEADS, LANES, tm), lambda i: (cur(i) // tiles, 0, 0, cur(i) % tiles)),
                   pl.BlockSpec((tm * DIFF_HEADS, DIFF_DV), lambda i: (cur(i), 0)),
                   pl.BlockSpec((None, CONV_PAD, CONV_CH), lambda i: (cur(i) // tiles, 0, 0)),
                   pl.BlockSpec((tm, GDN_V), lambda i: (prev(i), 0)),
                   pl.BlockSpec((None,) + state, lambda i: (prev(i) // tiles, 0, 0, 0))],
        out_shape=[jax.ShapeDtypeStruct((n, d), F32), jax.ShapeDtypeStruct((n, DIFF_QK), BF16),
                   jax.ShapeDtypeStruct((n, DIFF_V), BF16),
                   jax.ShapeDtypeStruct((batch, DIFF_HEADS, LANES, l), F32),
                   jax.ShapeDtypeStruct((n * DIFF_HEADS, DIFF_DV), F32),
                   jax.ShapeDtypeStruct((batch, CONV_PAD, CONV_CH), F32),
                   jax.ShapeDtypeStruct((n, GDN_V), BF16),
                   jax.ShapeDtypeStruct((batch,) + state, F32)],
        scratch_shapes=[pltpu.VMEM((tm + CONV_PAD, CONV_CH), F32), pltpu.VMEM((tm, GDN_V), F32),
                        pltpu.VMEM((tm, 2 * LANES), F32), pltpu.VMEM(state, F32)],
        compiler_params=_COMPILER_1D,
        name="pre_gdn",
    )(x2d, *tabs, *weights)


def _stack_maps(q):
    lane = lax.broadcasted_iota(jnp.int32, q.shape, 1)
    zero = jnp.zeros_like(q)
    return jnp.concatenate([jnp.where(lane < DIFF_DK, q, zero), jnp.where(lane >= DIFF_DK, q, zero)],
                           axis=0)


def _diff_finish(acc, l, t, lq_ref, lk_ref, dn_ref, lam_init):
    lam_e = jnp.exp(jnp.sum(lq_ref[...] * lk_ref[...], axis=-1, keepdims=True))
    lam = lam_e[0:1] - lam_e[1:2] + lam_init
    o = acc[:t] / l[:t] - lam * (acc[t:] / l[t:])
    return _rms(o, dn_ref[...]) * (1.0 - lam_init)


def _halves_max(s):
    return jnp.maximum(s[:, :LANES], s[:, LANES:])


def _attn_prompt_kernel(q_ref, kt_ref, v_ref, bias_ref, lq_ref, lk_ref, dn_ref, o_ref,
                        kt_scr, vx_scr, s_scr, m_scr, acc_scr, *, tq, tk, lam_init):
    i = pl.program_id(2)
    assert tk == 2 * LANES and tq % tk == 0
    nsub = tq // tk

    @pl.when(i == 0)
    def _():
        for j in range(kt_scr.shape[0]):
            kt_scr[j] = kt_ref[:, j * tk:(j + 1) * tk].astype(BF16)
        vx_scr[:, :DIFF_DV] = v_ref[...]
        vx_scr[:, DIFF_DV:] = jnp.ones((vx_scr.shape[0], LANES), BF16)

    qq = _stack_maps(q_ref[...])

    def scores(j):
        return _dot(qq, kt_scr[j])

    def weighted_values(ss, blocks, mm):
        p = [jnp.exp2(s[:, half * LANES:(half + 1) * LANES] - mm) for s in ss for half in range(tk // LANES)]
        vx = [vx_scr[pl.ds(pl.multiple_of(j * tk, tk), tk), :] for j in blocks]
        return _dot(jnp.concatenate(p, axis=1).astype(BF16), jnp.concatenate(vx, axis=0))

    def pass1(blocks):
        ss = [scores(j) for j in blocks]
        for j, s in zip(blocks, ss):
            s_scr[j] = s
        m_scr[...] = jnp.maximum(m_scr[...], functools.reduce(jnp.maximum, [_halves_max(s) for s in ss]))

    def pass2(blocks):
        acc_scr[...] += weighted_values([s_scr[j] for j in blocks], blocks, m_scr[...])

    def over_off_diagonal(body):
        def trip(t, carry):
            body([t * 2 * nsub + u for u in range(2 * nsub)])
            return carry

        lax.fori_loop(0, i // 2, trip, 0)

        @pl.when(i % 2 == 1)
        def _():
            body([(i - 1) * nsub + u for u in range(nsub)])

    diag_blocks = [i * nsub + d for d in range(nsub)]
    s_diag = [scores(j) + bias_ref[d] for d, j in enumerate(diag_blocks)]
    m_scr[...] = functools.reduce(jnp.maximum, [_halves_max(s) for s in s_diag])
    over_off_diagonal(pass1)
    m = jnp.broadcast_to(jnp.max(m_scr[...], axis=-1, keepdims=True), m_scr.shape)
    m_scr[...] = m
    acc_scr[...] = weighted_values(s_diag, diag_blocks, m)
    over_off_diagonal(pass2)
    acc = acc_scr[...]
    o_ref[...] = _diff_finish(acc[:, :DIFF_DV], acc[:, DIFF_DV:], tq, lq_ref, lk_ref, dn_ref,
                              lam_init).astype(o_ref.dtype)


def _attn_prompt_call(qb, kt, vb, p, tq, tk, lam_init):
    b, l, _ = qb.shape
    small = [p['lambda_q'], p['lambda_k'], p['diff_on']]
    r = (jnp.arange(2 * tq) % tq)[None, :, None] // CHUNK
    cidx = (jnp.arange(tk)[None, None, :] + tk * jnp.arange(tq // tk)[:, None, None]) // CHUNK
    bias = jnp.where(cidx <= r, 0.0, -jnp.inf).astype(F32)
    return pl.pallas_call(
        functools.partial(_attn_prompt_kernel, tq=tq, tk=tk, lam_init=lam_init),
        grid=(b, DIFF_HEADS, l // tq),
        in_specs=[pl.BlockSpec((None, tq, LANES), lambda bi, h, i: (bi, i, h)),
                  pl.BlockSpec((None, None, LANES, l), lambda bi, h, i: (bi, h, 0, 0)),
                  pl.BlockSpec((None, l, LANES), lambda bi, h, i: (bi, 0, h)),
                  _const_spec(bias.shape)] + [
                      pl.BlockSpec(w.shape, lambda bi, h, i: (0, 0)) for w in small],
        out_specs=pl.BlockSpec((None, tq, LANES), lambda bi, h, i: (bi, i, h)),
        out_shape=jax.ShapeDtypeStruct((b, l, DIFF_V), BF16),
        scratch_shapes=[pltpu.VMEM((l // tk, LANES, tk), BF16),
                        pltpu.VMEM((l, DIFF_DV + LANES), BF16),
                        pltpu.VMEM((max((l - tq) // tk, 1), 2 * tq, tk), F32),
                        pltpu.VMEM((2 * tq, LANES), F32),
                        pltpu.VMEM((2 * tq, DIFF_DV + LANES), F32)],
        compiler_params=pltpu.CompilerParams(
            dimension_semantics=("arbitrary", "arbitrary", "arbitrary"),
            vmem_limit_bytes=VMEM_LIMIT),
        name="attn_prompt",
    )(qb, kt, vb, bias, *small)


def _attn_sample_kernel(q_ref, kn_ref, vn_ref, ckt_ref, cv_ref, lq_ref, lk_ref, dn_ref, o_ref,
                        *, t, lam_init):
    past = ckt_ref.shape[-1]
    for h in range(DIFF_HEADS):
        cols = slice(h * LANES, (h + 1) * LANES)
        qq = _stack_maps(q_ref[:, cols])
        s_c = _dot(qq, ckt_ref[h].astype(BF16))
        s_n = _dot_nt(qq, kn_ref[:, cols])
        m = jnp.maximum(jnp.max(s_c, axis=-1, keepdims=True), jnp.max(s_n, axis=-1, keepdims=True))
        p_c = jnp.exp2(s_c - m)
        p_n = jnp.exp2(s_n - m)
        l = jnp.sum(p_c, axis=-1, keepdims=True) + jnp.sum(p_n, axis=-1, keepdims=True)
        cv = cv_ref[pl.ds(h, past, stride=DIFF_HEADS), :].astype(BF16)
        acc = _dot(p_c.astype(BF16), cv) + _dot(p_n.astype(BF16), vn_ref[:, cols])
        o_ref[:, cols] = _diff_finish(acc, l, t, lq_ref, lk_ref, dn_ref, lam_init).astype(o_ref.dtype)


def _attn_sample_call(qb, kb, vb, ckt, cv, p, lam_init):
    b, t, _ = qb.shape
    past = ckt.shape[-1]
    small = [p['lambda_q'], p['lambda_k'], p['diff_on']]
    new = pl.BlockSpec((None, t, DIFF_V), lambda bi: (bi, 0, 0))
    return pl.pallas_call(
        functools.partial(_attn_sample_kernel, t=t, lam_init=lam_init),
        grid=(b,),
        in_specs=[new, new, new,
                  pl.BlockSpec((None, DIFF_HEADS, LANES, past), lambda bi: (bi, 0, 0, 0)),
                  pl.BlockSpec((None, past * DIFF_HEADS, DIFF_DV), lambda bi: (bi, 0, 0))] + [
                      pl.BlockSpec(w.shape, lambda bi: (0, 0)) for w in small],
        out_specs=new,
        out_shape=jax.ShapeDtypeStruct((b, t, DIFF_V), BF16),
        compiler_params=_COMPILER_1D,
        name="attn_sample",
    )(qb, kb, vb, ckt, cv, *small)


def _rope_tables(pos):
    half = ROT_DIM // 2
    inv = jnp.float32(ROPE_THETA) ** (-jnp.arange(half, dtype=F32) * 2.0 / ROT_DIM)
    ang = pos.astype(F32)[:, None] * inv[None, :]
    cos, sin = jnp.cos(ang), jnp.sin(ang)
    n = pos.shape[0]
    rest = DIFF_DK - ROT_DIM
    one = jnp.ones((n, rest), F32)
    zero = jnp.zeros((n, rest), F32)
    zh = jnp.zeros((n, half), F32)
    c64 = jnp.concatenate([cos, cos, one], axis=1)
    a64 = jnp.concatenate([-sin, zh, zero], axis=1)
    b64 = jnp.concatenate([zh, sin, zero], axis=1)
    rep = LANES // DIFF_DK
    return tuple(jnp.tile(t, (1, rep)) for t in (c64, a64, b64))


def _layer_params(w, l):
    d = w['w_in'].shape[1]
    d_ff = w['ffn1_w_down'].shape[1]
    win = w['w_in'][l]
    o = 0
    w_conv = win[:, o:o + CONV_CH]; o += CONV_CH
    w_b = win[:, o:o + GDN_HEADS]; o += GDN_HEADS
    w_a = win[:, o:o + GDN_HEADS]; o += GDN_HEADS
    w_z = win[:, o:o + GDN_V]; o += GDN_V
    w_q = win[:, o:o + DIFF_QK]; o += DIFF_QK
    w_k = win[:, o:o + DIFF_QK]; o += DIFF_QK
    w_v = win[:, o:o + DIFF_V]
    zpad = jnp.zeros((d, LANES - GDN_HEADS), win.dtype)
    w_ba = jnp.concatenate([w_b, zpad, w_a, zpad], axis=1)
    hpad = lambda v: jnp.concatenate([v.astype(F32), jnp.zeros((LANES - GDN_HEADS,), F32)])[None, :]
    grp = jnp.arange(DIFF_QK) // DIFF_DK
    bf = lambda t: t.astype(BF16)
    r2 = lambda v: v.astype(F32)[None, :]
    return dict(
        n1=r2(w['ffn1_norm'][l]), wg1=bf(w['ffn1_w_gu'][l][:, :d_ff]), wu1=bf(w['ffn1_w_gu'][l][:, d_ff:]),
        wd1=bf(w['ffn1_w_down'][l]),
        nm=r2(w['mix_norm'][l]), w_conv=bf(w_conv), w_z=bf(w_z), w_ba=bf(w_ba), w_q=bf(w_q), w_k=bf(w_k),
        w_v=bf(w_v),
        qn=r2(jnp.tile(w['q_norm'][l], DIFF_QK // DIFF_DK)), kn=r2(jnp.tile(w['k_norm'][l], DIFF_QK // DIFF_DK)),
        gmat=(grp[:, None] == grp[None, :]).astype(BF16),
        head_mat=(grp[:, None] // 2 == grp[None, :] // 2).astype(BF16),
        conv_w=w['conv_w'][l].astype(F32), a_log=hpad(w['a_log'][l]), dt_bias=hpad(w['dt_bias'][l]),
        gdn_on=r2(w['gdn_out_norm'][l]),
        lambda_q=w['lambda_q'][l].astype(F32), lambda_k=w['lambda_k'][l].astype(F32),
        diff_on=r2(w['diff_out_norm'][l]),
        wo=bf(w['w_out'][l]), n2=r2(w['ffn2_norm'][l]), wg2=bf(w['ffn2_w_gu'][l][:, :d_ff]),
        wu2=bf(w['ffn2_w_gu'][l][:, d_ff:]), wd2=bf(w['ffn2_w_down'][l]),
    )


def _prompt_layer(x, pos, p, lam_init):
    b, l, d = x.shape
    n = b * l
    tm = ROW_TILE
    tk = 2 * LANES
    tq = min(l, 2 * tk)
    assert l % tm == 0 and l % tq == 0
    x1, qb, vb, kf, vf, last, og, s_new = _pre_gdn_call(x.reshape(n, d), _rope_tables(pos), p, tm, b)
    od = _attn_prompt_call(qb.reshape(b, l, DIFF_QK), kf, vb.reshape(b, l, DIFF_V), p, tq, tk, lam_init)
    y = _post_call(x1, og, od.reshape(n, DIFF_V), p, tm)
    k_out = jnp.transpose(kf.reshape(b, DIFF_HEADS, 2, DIFF_DK, l), (0, 4, 1, 2, 3))
    return (y.reshape(b, l, d), k_out, vf.reshape(b, l, DIFF_HEADS, DIFF_DV), s_new,
            last[:, -(CONV_W - 1):])


def _sample_layer(x, pos, k_hist, v_hist, conv_hist, s0, p, lam_init):
    b, l, d = x.shape
    n = b * l
    assert n == ROW_TILE and l <= CHUNK
    tabs = tuple(jnp.tile(t, (b, 1)) for t in _rope_tables(pos))
    x1, conv, z, ba, qb, vb, kf, vf, kb = _pre_call(x.reshape(n, d), tabs, p, n)
    conv3 = conv.reshape(b, l, CONV_CH)
    conv_hist = conv_hist.astype(F32)
    og, s_new = _gdn_call(conv3, z.reshape(b, l, GDN_V), ba.reshape(b, l, 2 * LANES), conv_hist,
                          s0.astype(F32), p, n)
    conv_new = jnp.concatenate([conv_hist, conv3], axis=1)[:, -(CONV_W - 1):]
    past = k_hist.shape[1]
    ckt = jnp.transpose(k_hist, (0, 2, 3, 4, 1)).reshape(b, DIFF_HEADS, LANES, past)
    od = _attn_sample_call(qb.reshape(b, l, DIFF_QK), kb.reshape(b, l, DIFF_QK), vb.reshape(b, l, DIFF_V), ckt,
                           v_hist.reshape(b, past * DIFF_HEADS, DIFF_DV), p, lam_init)
    y = _post_call(x1, og.reshape(n, GDN_V), od.reshape(n, DIFF_V), p, n)
    return (y.reshape(b, l, d), kf.reshape(b, l, DIFF_HEADS, 2, DIFF_DK), vf.reshape(b, l, DIFF_HEADS, DIFF_DV),
            s_new, conv_new)


def kernel(x_prompt, x_sample, cache_k, cache_v, state_gdn, state_conv, ffn1_norm, ffn1_w_gu, ffn1_w_down,
           mix_norm, w_in, conv_w, a_log, dt_bias, gdn_out_norm, q_norm, k_norm, lambda_q, lambda_k,
           diff_out_norm, w_out, ffn2_norm, ffn2_w_gu, ffn2_w_down):
    w = dict(ffn1_norm=ffn1_norm, ffn1_w_gu=ffn1_w_gu, ffn1_w_down=ffn1_w_down, mix_norm=mix_norm, w_in=w_in,
             conv_w=conv_w, a_log=a_log, dt_bias=dt_bias, gdn_out_norm=gdn_out_norm, q_norm=q_norm,
             k_norm=k_norm, lambda_q=lambda_q, lambda_k=lambda_k, diff_out_norm=diff_out_norm, w_out=w_out,
             ffn2_norm=ffn2_norm, ffn2_w_gu=ffn2_w_gu, ffn2_w_down=ffn2_w_down)
    depth = w_in.shape[0]
    pos_p = jnp.arange(x_prompt.shape[1])
    pos_s = cache_k.shape[2] + jnp.arange(x_sample.shape[1])
    hp, hs = x_prompt, x_sample
    outs = [[] for _ in range(8)]
    for l in range(depth):
        lam_init = 0.8 - 0.6 * math.exp(-0.3 * l)
        p = _layer_params(w, l)
        hp, kp, vp, sp, cp = _prompt_layer(hp, pos_p, p, lam_init)
        hs, ks, vs, ss, cs = _sample_layer(hs, pos_s, cache_k[l], cache_v[l], state_conv[l], state_gdn[l], p,
                                           lam_init)
        for acc, val in zip(outs, (kp, vp, sp, cp, ks, vs, ss, cs)):
            acc.append(val)
    return (hp, hs) + tuple(jnp.stack(o) for o in outs)
```

```python
import functools
import math

import jax
import jax.numpy as jnp
from jax import lax
from jax.experimental import pallas as pl
from jax.experimental.pallas import tpu as pltpu

F32 = jnp.float32
BF16 = jnp.bfloat16

EPS = 1e-6
CHUNK = 64
GDN_HEADS = 4
GDN_DK = 128
GDN_DV = 128
CONV_W = 4
DIFF_HEADS = 4
DIFF_DK = 64
DIFF_DV = 128
ROT_DIM = DIFF_DK // 4
ROPE_THETA = 500000.0
GDN_QK = GDN_HEADS * GDN_DK
GDN_V = GDN_HEADS * GDN_DV
CONV_CH = 2 * GDN_QK + GDN_V
DIFF_QK = DIFF_HEADS * 2 * DIFF_DK
DIFF_V = DIFF_HEADS * DIFF_DV

LANES = 128
INV_BLOCK = 16
GDN_TILE = 256
VMEM_LIMIT = 56 * 1024 * 1024
Q_SCALE = (DIFF_DK ** -0.5) * math.log2(math.e)


def _dot(a, b):
    return jnp.dot(a, b, preferred_element_type=F32)


def _dot_nt(a, b):
    return lax.dot_general(a, b, (((1,), (1,)), ((), ())), preferred_element_type=F32)


def _rms(x, g):
    return x * lax.rsqrt(jnp.mean(x * x, axis=-1, keepdims=True) + EPS) * g


def _silu(x):
    return x * jax.nn.sigmoid(x)


def _swiglu(xn, wg_ref, wu_ref, wd_ref):
    g = _dot(xn, wg_ref[...])
    u = _dot(xn, wu_ref[...])
    act = (_silu(g) * u).astype(BF16)
    return _dot(act, wd_ref[...])


def _const_spec(shape):
    nd = len(shape)
    return pl.BlockSpec(shape, lambda *_: (0,) * nd, pipeline_mode=pl.Buffered(1))


def _pre_kernel(x_ref, cos_ref, sa_ref, sb_ref, n1_ref, wg_ref, wu_ref, wd_ref, nm_ref,
                wc_ref, wz_ref, wba_ref, wq_ref, wk_ref, wv_ref, qn_ref, kn_ref, gm_ref,
                x1_ref, conv_ref, z_ref, ba_ref, qb_ref, vb_ref, kf_ref, vf_ref, *maybe_kb_ref,
                k_pos_minor):
    x = x_ref[...]
    xn = _rms(x, n1_ref[...]).astype(BF16)
    x1 = x + 0.5 * _swiglu(xn, wg_ref, wu_ref, wd_ref)
    x1_ref[...] = x1
    h = _rms(x1, nm_ref[...]).astype(BF16)
    tm = x.shape[0]
    cos = cos_ref[...]
    sa = sa_ref[...]
    sb = sb_ref[...]
    gm = gm_ref[...]

    def norm_rope(t, gw):
        ss = _dot((t * t).astype(BF16), gm)
        t = t * lax.rsqrt(ss * (1.0 / DIFF_DK) + EPS) * gw
        outs = []
        for hh in range(DIFF_HEADS):
            th = t[:, hh * LANES:(hh + 1) * LANES]
            up = pltpu.roll(th, LANES - ROT_DIM // 2, 1)
            dn = pltpu.roll(th, ROT_DIM // 2, 1)
            outs.append(th * cos + up * sa + dn * sb)
        return jnp.concatenate(outs, axis=1)

    k = norm_rope(_dot(h, wk_ref[...]), kn_ref[...])
    if k_pos_minor:
        kf_ref[...] = k.T.reshape(DIFF_HEADS, LANES, tm)
    else:
        kf_ref[...] = k
        maybe_kb_ref[0][...] = k.astype(BF16)
    q = norm_rope(_dot(h, wq_ref[...]), qn_ref[...])
    qb_ref[...] = (q * Q_SCALE).astype(BF16)
    v = _dot(h, wv_ref[...])
    vb_ref[...] = v.astype(BF16)
    if k_pos_minor:
        for hh in range(DIFF_HEADS):
            vf_ref[pl.ds(hh, tm, stride=DIFF_HEADS), :] = v[:, hh * DIFF_DV:(hh + 1) * DIFF_DV]
    else:
        vf_ref[...] = v
    z_ref[...] = _dot(h, wz_ref[...])
    ba_ref[...] = _dot(h, wba_ref[...])
    conv_ref[...] = _dot(h, wc_ref[...])


def _pre_call(x2d, tabs, p, tm, n_pos_tiles, batch, k_pos_minor):
    n, d = x2d.shape
    grid = (n // tm,)
    row = lambda w: pl.BlockSpec((tm, w), lambda i: (i, 0))
    tab = pl.BlockSpec((tm, LANES), lambda i: (i % n_pos_tiles, 0))
    weights = [p['n1'], p['wg1'], p['wu1'], p['wd1'], p['nm'], p['w_conv'], p['w_z'], p['w_ba'],
               p['w_q'], p['w_k'], p['w_v'], p['qn'], p['kn'], p['gmat']]
    in_specs = [row(d), tab, tab, tab] + [_const_spec(w.shape) for w in weights]
    outs = [(d, F32), (CONV_CH, F32), (GDN_V, F32), (2 * LANES, F32), (DIFF_QK, BF16), (DIFF_V, BF16)]
    out_specs = [row(w) for w, _ in outs]
    out_shape = [jax.ShapeDtypeStruct((n, w), dt) for w, dt in outs]
    if k_pos_minor:
        l = n // batch
        tiles = l // tm
        out_specs += [pl.BlockSpec((None, DIFF_HEADS, LANES, tm), lambda i: (i // tiles, 0, 0, i % tiles)),
                      pl.BlockSpec((tm * DIFF_HEADS, DIFF_DV), lambda i: (i, 0))]
        out_shape += [jax.ShapeDtypeStruct((batch, DIFF_HEADS, LANES, l), F32),
                      jax.ShapeDtypeStruct((n * DIFF_HEADS, DIFF_DV), F32)]
    else:
        out_specs += [row(DIFF_QK), row(DIFF_V), row(DIFF_QK)]
        out_shape += [jax.ShapeDtypeStruct((n, DIFF_QK), F32), jax.ShapeDtypeStruct((n, DIFF_V), F32),
                      jax.ShapeDtypeStruct((n, DIFF_QK), BF16)]
    return pl.pallas_call(
        functools.partial(_pre_kernel, k_pos_minor=k_pos_minor),
        grid=grid,
        in_specs=in_specs,
        out_specs=out_specs,
        out_shape=out_shape,
        compiler_params=pltpu.CompilerParams(dimension_semantics=("arbitrary",),
                                             vmem_limit_bytes=VMEM_LIMIT),
        name="pre",
    )(x2d, *tabs, *weights)


def _post_kernel(x1_ref, og_ref, od_ref, wo_ref, n2_ref, wg_ref, wu_ref, wd_ref, y_ref):
    mixed = jnp.concatenate([og_ref[...], od_ref[...]], axis=1)
    x2 = x1_ref[...] + _dot(mixed, wo_ref[...])
    xn = _rms(x2, n2_ref[...]).astype(BF16)
    y_ref[...] = x2 + 0.5 * _swiglu(xn, wg_ref, wu_ref, wd_ref)


def _post_call(x1, og, od, p, tm):
    n, d = x1.shape
    row = lambda w: pl.BlockSpec((tm, w), lambda i: (i, 0))
    weights = [p['wo'], p['n2'], p['wg2'], p['wu2'], p['wd2']]
    return pl.pallas_call(
        _post_kernel,
        grid=(n // tm,),
        in_specs=[row(d), row(GDN_V), row(DIFF_V)] + [_const_spec(w.shape) for w in weights],
        out_specs=row(d),
        out_shape=jax.ShapeDtypeStruct((n, d), F32),
        compiler_params=pltpu.CompilerParams(dimension_semantics=("arbitrary",),
                                             vmem_limit_bytes=VMEM_LIMIT),
        name="post",
    )(x1, og, od, *weights)


def _exact3(m01, x):
    x1 = x.astype(BF16)
    r1 = x - x1.astype(F32)
    x2 = r1.astype(BF16)
    x3 = (r1 - x2.astype(F32)).astype(BF16)
    return _dot(m01, x1) + (_dot(m01, x2) + _dot(m01, x3))


CONV_PAD = 8


def _gdn_kernel(x_ref, z_ref, ba_ref, hist_ref, s0_ref, cw_ref, alog_ref, dtb_ref, on_ref, hm_ref,
                o_ref, s_out_ref, xp_ref, *maybe_s_ref, tc, c, seqs):
    pad = CONV_PAD
    nchunk = tc // c
    assert c % INV_BLOCK == 0 and tc % LANES == 0 and (seqs == 1 or seqs == nchunk)
    cw = cw_ref[...]
    x = x_ref[...]

    def conv(xp, rows):
        y = xp[rows] * cw[CONV_W - 1:CONV_W, :]
        for k in range(1, CONV_W):
            y = y + pltpu.roll(xp, k, 0)[rows] * cw[CONV_W - 1 - k:CONV_W - k, :]
        return y

    if seqs == 1:
        s_ref, = maybe_s_ref
        t = pl.program_id(1)

        @pl.when(t == 0)
        def _():
            xp_ref[0:pad, :] = jnp.zeros((pad, CONV_CH), F32)
            xp_ref[pad - (CONV_W - 1):pad, :] = hist_ref[...]
            s_ref[...] = s0_ref[...]

        xp_ref[pad:pad + tc, :] = x
        y = conv(xp_ref[...], slice(pad, pad + tc))
        xp_ref[pad - (CONV_W - 1):pad, :] = x[tc - (CONV_W - 1):tc, :]
    else:
        stride = pad + c
        xp_ref[...] = jnp.zeros(xp_ref.shape, F32)
        for b in range(seqs):
            xp_ref[b * stride + pad - (CONV_W - 1):b * stride + pad, :] = hist_ref[b]
            xp_ref[b * stride + pad:(b + 1) * stride, :] = x[b * c:(b + 1) * c, :]
        yp = conv(xp_ref[...], slice(None))
        y = jnp.concatenate([yp[b * stride + pad:(b + 1) * stride, :] for b in range(seqs)], axis=0)
    y = _silu(y)
    hm = hm_ref[...]
    yk = y[:, GDN_QK:2 * GDN_QK]
    yk = yk * lax.rsqrt(_dot((yk * yk).astype(BF16), hm) + EPS)

    ba = ba_ref[...]
    beta = jax.nn.sigmoid(ba[:, :LANES])
    g = -jnp.exp(alog_ref[...]) * jax.nn.softplus(ba[:, LANES:] + dtb_ref[...])
    ri = lax.broadcasted_iota(jnp.int32, (tc, tc), 0)
    ci = lax.broadcasted_iota(jnp.int32, (tc, tc), 1)
    incl = ((ri // c) == (ci // c)) & (ci <= ri)
    same_blk = (ri // INV_BLOCK) == (ci // INV_BLOCK)
    diag = ri == ci
    eye = jnp.where(diag, 1.0, 0.0)
    gcum = _exact3(jnp.where(incl, 1.0, 0.0).astype(BF16), g)
    gcum_t = gcum.T
    on = on_ref[...]

    heads = range(GDN_HEADS)

    def per_head(f, *lists):
        return [f(*args) for args in zip(*lists)]

    ks = [yk[:, h * GDN_DK:(h + 1) * GDN_DK] for h in heads]
    vs = [y[:, 2 * GDN_QK + h * GDN_DV:2 * GDN_QK + (h + 1) * GDN_DV] for h in heads]
    gcs = [gcum[:, h:h + 1] for h in heads]
    bcs = [beta[:, h:h + 1] for h in heads]
    decs = [jnp.exp(jnp.where(incl, gcs[h] - gcum_t[h:h + 1, :], -jnp.inf)) for h in heads]
    kbs = per_head(lambda k: k.astype(BF16), ks)
    a_s = per_head(lambda bc, kb, dec: jnp.where(diag, 0.0, bc * _dot_nt(kb, kb) * dec), bcs, kbs, decs)

    assert INV_BLOCK == 16 and c in (INV_BLOCK, 4 * INV_BLOCK)
    bf = lambda xs: per_head(lambda x: x.astype(BF16), xs)
    mm = lambda xs, ys: per_head(_dot, xs, ys)
    ds = per_head(lambda a: jnp.where(same_blk, a, 0.0), a_s) if c > INV_BLOCK else a_s
    sd = bf(ds)
    d2 = mm(sd, sd)
    s2 = bf(d2)
    d3 = mm(sd, s2)
    d4 = mm(s2, s2)
    s4 = bf(d4)
    n1 = per_head(lambda d, x2, x3: eye - d + x2 - x3, ds, d2, d3)
    n2 = per_head(jnp.add, n1, mm(bf(n1), s4))
    d8 = mm(s4, s4)
    tinv = per_head(jnp.add, n2, mm(bf(n2), bf(d8)))
    if c > INV_BLOCK:
        std = bf(tinv)
        ms = mm(std, bf(per_head(jnp.subtract, a_s, ds)))
        sm = bf(ms)
        m2 = mm(sm, sm)
        m3 = mm(sm, bf(m2))
        tinv = mm(bf(per_head(lambda m, x2, x3: eye - m + x2 - x3, ms, m2, m3)), std)

    egs = per_head(jnp.exp, gcs)
    rhs = per_head(lambda v, k, bc, eg: jnp.concatenate([v * bc, k * (bc * eg)], axis=1), vs, ks, bcs, egs)
    sols = mm(bf(tinv), bf(rhs))
    yq = y[:, :GDN_QK]
    yq = yq * (lax.rsqrt(_dot((yq * yq).astype(BF16), hm) + EPS) * (GDN_DK ** -0.5))
    qs = [yq[:, h * GDN_DK:(h + 1) * GDN_DK] for h in heads]
    qks = per_head(lambda q, kb, dec: (_dot_nt(q.astype(BF16), kb) * dec).astype(BF16), qs, kbs, decs)
    u0s = [sol[:, :GDN_DV] for sol in sols]
    wbs = [sol[:, GDN_DV:].astype(BF16) for sol in sols]
    qds = per_head(lambda q, eg: (q * eg).astype(BF16), qs, egs)

    def chunk_terms(ic):
        rs = slice(ic * c, (ic + 1) * c)
        gends = [gcs[h][(ic + 1) * c - 1:(ic + 1) * c, :] for h in heads]
        kds = [(ks[h][rs] * jnp.exp(gends[h] - gcs[h][rs])).astype(BF16) for h in heads]
        return rs, gends, kds

    def state_terms(rs, states):
        return [_dot(jnp.concatenate([wbs[h][rs], qds[h][rs]], axis=0), states[h].astype(BF16)) for h in heads]

    def next_state(states, gends, kds, us_c):
        return [states[h] * jnp.exp(gends[h]) + lax.dot_general(
            kds[h], us_c[h], (((0,), (0,)), ((), ())), preferred_element_type=F32) for h in heads]

    us = [[] for _ in heads]
    outs = [[] for _ in heads]
    if seqs == 1:
        trans = []
        for ic in range(nchunk):
            rs, gends, kds = chunk_terms(ic)
            kt_wu = [lax.dot_general(kds[h], jnp.concatenate([wbs[h][rs], u0s[h][rs].astype(BF16)], axis=1),
                                     (((0,), (0,)), ((), ())), preferred_element_type=F32) for h in heads]
            trans.append(([jnp.exp(g) for g in gends], [m[:, :GDN_DK].astype(BF16) for m in kt_wu],
                          [m[:, GDN_DK:] for m in kt_wu]))
        states = [s_ref[h] for h in heads]
        chunk_states = []
        for decay, kt_w, kt_u0 in trans:
            sb = [s.astype(BF16) for s in states]
            chunk_states.append(sb)
            states = [states[h] * decay[h] - _dot(kt_w[h], sb[h]) + kt_u0[h] for h in heads]
        for h in heads:
            s_ref[h] = states[h]
        for ic in range(nchunk):
            rs = slice(ic * c, (ic + 1) * c)
            wss = [_dot(jnp.concatenate([wbs[h][rs], qds[h][rs]], axis=0), chunk_states[ic][h]) for h in heads]
            for h in heads:
                us[h].append((u0s[h][rs] - wss[h][:c]).astype(BF16))
                outs[h].append(wss[h][c:])
        o = [jnp.concatenate(outs[h], axis=0) + _dot(qks[h], jnp.concatenate(us[h], axis=0)) for h in heads]

        @pl.when(t == pl.num_programs(1) - 1)
        def _():
            s_out_ref[...] = s_ref[...]
    else:
        for ic in range(nchunk):
            rs, gends, kds = chunk_terms(ic)
            states = [s0_ref[ic, h] for h in heads]
            wss = state_terms(rs, states)
            us_c = [(u0s[h][rs] - wss[h][:c]).astype(BF16) for h in heads]
            new = next_state(states, gends, kds, us_c)
            for h in heads:
                us[h].append(us_c[h])
                outs[h].append(wss[h][c:])
                s_out_ref[ic, h] = new[h]
        o = [jnp.concatenate(outs[h], axis=0) + _dot(qks[h], jnp.concatenate(us[h], axis=0)) for h in heads]
    for h in heads:
        cols = slice(h * GDN_DV, (h + 1) * GDN_DV)
        o_ref[:, cols] = (_rms(o[h], on) * _silu(z_ref[:, cols])).astype(o_ref.dtype)


def _gdn_call(conv, z, ba, hist, s0, p, tc, c):
    b, l, _ = conv.shape
    small = [p['conv_w'], p['a_log'], p['dt_bias'], p['gdn_on'], p['head_mat']]
    state = (GDN_HEADS, GDN_DK, GDN_DV)
    if l >= tc:
        seqs = 1
        grid = (b, l // tc)
        row = lambda w: pl.BlockSpec((None, tc, w), lambda i, j: (i, j, 0))
        per_b = lambda shape: pl.BlockSpec((None,) + shape, lambda i, j: (i,) + (0,) * len(shape))
        scratch = [pltpu.VMEM((tc + CONV_PAD, CONV_CH), F32), pltpu.VMEM(state, F32)]
    else:
        assert l == c and b * l == tc
        seqs = b
        grid = (1, 1)
        conv, z, ba = (t.reshape(1, tc, t.shape[-1]) for t in (conv, z, ba))
        row = lambda w: pl.BlockSpec((None, tc, w), lambda i, j: (0, 0, 0))
        per_b = lambda shape: pl.BlockSpec((b,) + shape, lambda i, j: (0,) * (len(shape) + 1))
        scratch = [pltpu.VMEM((seqs * (c + CONV_PAD), CONV_CH), F32)]
    o, s_new = pl.pallas_call(
        functools.partial(_gdn_kernel, tc=tc, c=c, seqs=seqs),
        grid=grid,
        in_specs=[row(CONV_CH), row(GDN_V), row(2 * LANES), per_b((CONV_W - 1, CONV_CH)), per_b(state)] + [
            pl.BlockSpec(w.shape, lambda i, j: (0, 0)) for w in small],
        out_specs=[row(GDN_V), per_b(state)],
        out_shape=[jax.ShapeDtypeStruct(conv.shape[:2] + (GDN_V,), BF16),
                   jax.ShapeDtypeStruct((b,) + state, F32)],
        scratch_shapes=scratch,
        compiler_params=pltpu.CompilerParams(dimension_semantics=("arbitrary", "arbitrary"),
                                             vmem_limit_bytes=VMEM_LIMIT),
        name="gdn",
    )(conv, z, ba, hist, s0, *small)
    return o.reshape(b, l, GDN_V), s_new


def _stack_maps(q):
    lane = lax.broadcasted_iota(jnp.int32, q.shape, 1)
    zero = jnp.zeros_like(q)
    return jnp.concatenate([jnp.where(lane < DIFF_DK, q, zero), jnp.where(lane >= DIFF_DK, q, zero)],
                           axis=0)


def _diff_finish(acc, l, t, lq_ref, lk_ref, dn_ref, lam_init):
    lam_e = jnp.exp(jnp.sum(lq_ref[...] * lk_ref[...], axis=-1, keepdims=True))
    lam = lam_e[0:1] - lam_e[1:2] + lam_init
    o = acc[:t] / l[:t] - lam * (acc[t:] / l[t:])
    return _rms(o, dn_ref[...]) * (1.0 - lam_init)


def _halves_max(s):
    return jnp.maximum(s[:, :LANES], s[:, LANES:])


def _attn_prompt_kernel(q_ref, kt_ref, v_ref, bias_ref, lq_ref, lk_ref, dn_ref, o_ref,
                        kt_scr, vx_scr, s_scr, m_scr, acc_scr, *, tq, tk, lam_init):
    i = pl.program_id(2)
    assert tk == 2 * LANES and tq % tk == 0
    nsub = tq // tk

    @pl.when(i == 0)
    def _():
        for j in range(kt_scr.shape[0]):
            kt_scr[j] = kt_ref[:, j * tk:(j + 1) * tk].astype(BF16)
        vx_scr[:, :DIFF_DV] = v_ref[...]
        vx_scr[:, DIFF_DV:] = jnp.ones((vx_scr.shape[0], LANES), BF16)

    qq = _stack_maps(q_ref[...])

    def scores(j):
        return _dot(qq, kt_scr[j])

    def weighted_values(ss, blocks, mm):
        p = [jnp.exp2(s[:, half * LANES:(half + 1) * LANES] - mm) for s in ss for half in range(tk // LANES)]
        vx = [vx_scr[pl.ds(pl.multiple_of(j * tk, tk), tk), :] for j in blocks]
        return _dot(jnp.concatenate(p, axis=1).astype(BF16), jnp.concatenate(vx, axis=0))

    def pass1(blocks):
        ss = [scores(j) for j in blocks]
        for j, s in zip(blocks, ss):
            s_scr[j] = s
        m_scr[...] = jnp.maximum(m_scr[...], functools.reduce(jnp.maximum, [_halves_max(s) for s in ss]))

    def pass2(blocks):
        acc_scr[...] += weighted_values([s_scr[j] for j in blocks], blocks, m_scr[...])

    def over_off_diagonal(body):
        def trip(t, carry):
            body([t * 2 * nsub + u for u in range(2 * nsub)])
            return carry

        lax.fori_loop(0, i // 2, trip, 0)

        @pl.when(i % 2 == 1)
        def _():
            body([(i - 1) * nsub + u for u in range(nsub)])

    diag_blocks = [i * nsub + d for d in range(nsub)]
    s_diag = [scores(j) + bias_ref[d] for d, j in enumerate(diag_blocks)]
    m_scr[...] = functools.reduce(jnp.maximum, [_halves_max(s) for s in s_diag])
    over_off_diagonal(pass1)
    m = jnp.broadcast_to(jnp.max(m_scr[...], axis=-1, keepdims=True), m_scr.shape)
    m_scr[...] = m
    acc_scr[...] = weighted_values(s_diag, diag_blocks, m)
    over_off_diagonal(pass2)
    acc = acc_scr[...]
    o_ref[...] = _diff_finish(acc[:, :DIFF_DV], acc[:, DIFF_DV:], tq, lq_ref, lk_ref, dn_ref,
                              lam_init).astype(o_ref.dtype)


def _attn_prompt_call(qb, kt, vb, p, tq, tk, lam_init):
    b, l, _ = qb.shape
    small = [p['lambda_q'], p['lambda_k'], p['diff_on']]
    r = (jnp.arange(2 * tq) % tq)[None, :, None] // CHUNK
    cidx = (jnp.arange(tk)[None, None, :] + tk * jnp.arange(tq // tk)[:, None, None]) // CHUNK
    bias = jnp.where(cidx <= r, 0.0, -jnp.inf).astype(F32)
    return pl.pallas_call(
        functools.partial(_attn_prompt_kernel, tq=tq, tk=tk, lam_init=lam_init),
        grid=(b, DIFF_HEADS, l // tq),
        in_specs=[pl.BlockSpec((None, tq, LANES), lambda bi, h, i: (bi, i, h)),
                  pl.BlockSpec((None, None, LANES, l), lambda bi, h, i: (bi, h, 0, 0)),
                  pl.BlockSpec((None, l, LANES), lambda bi, h, i: (bi, 0, h)),
                  _const_spec(bias.shape)] + [
                      pl.BlockSpec(w.shape, lambda bi, h, i: (0, 0)) for w in small],
        out_specs=pl.BlockSpec((None, tq, LANES), lambda bi, h, i: (bi, i, h)),
        out_shape=jax.ShapeDtypeStruct((b, l, DIFF_V), BF16),
        scratch_shapes=[pltpu.VMEM((l // tk, LANES, tk), BF16),
                        pltpu.VMEM((l, DIFF_DV + LANES), BF16),
                        pltpu.VMEM((max((l - tq) // tk, 1), 2 * tq, tk), F32),
                        pltpu.VMEM((2 * tq, LANES), F32),
                        pltpu.VMEM((2 * tq, DIFF_DV + LANES), F32)],
        compiler_params=pltpu.CompilerParams(
            dimension_semantics=("arbitrary", "arbitrary", "arbitrary"),
            vmem_limit_bytes=VMEM_LIMIT),
        name="attn_prompt",
    )(qb, kt, vb, bias, *small)


def _attn_sample_kernel(q_ref, kn_ref, vn_ref, ckt_ref, cv_ref, lq_ref, lk_ref, dn_ref, o_ref,
                        *, t, lam_init):
    past = ckt_ref.shape[-1]
    for h in range(DIFF_HEADS):
        cols = slice(h * LANES, (h + 1) * LANES)
        qq = _stack_maps(q_ref[:, cols])
        s_c = _dot(qq, ckt_ref[h].astype(BF16))
        s_n = _dot_nt(qq, kn_ref[:, cols])
        m = jnp.maximum(jnp.max(s_c, axis=-1, keepdims=True), jnp.max(s_n, axis=-1, keepdims=True))
        p_c = jnp.exp2(s_c - m)
        p_n = jnp.exp2(s_n - m)
        l = jnp.sum(p_c, axis=-1, keepdims=True) + jnp.sum(p_n, axis=-1, keepdims=True)
        cv = cv_ref[pl.ds(h, past, stride=DIFF_HEADS), :].astype(BF16)
        acc = _dot(p_c.astype(BF16), cv) + _dot(p_n.astype(BF16), vn_ref[:, cols])
        o_ref[:, cols] = _diff_finish(acc, l, t, lq_ref, lk_ref, dn_ref, lam_init).astype(o_ref.dtype)


def _attn_sample_call(qb, kb, vb, ckt, cv, p, lam_init):
    b, t, _ = qb.shape
    past = ckt.shape[-1]
    small = [p['lambda_q'], p['lambda_k'], p['diff_on']]
    new = pl.BlockSpec((None, t, DIFF_V), lambda bi: (bi, 0, 0))
    return pl.pallas_call(
        functools.partial(_attn_sample_kernel, t=t, lam_init=lam_init),
        grid=(b,),
        in_specs=[new, new, new,
                  pl.BlockSpec((None, DIFF_HEADS, LANES, past), lambda bi: (bi, 0, 0, 0)),
                  pl.BlockSpec((None, past * DIFF_HEADS, DIFF_DV), lambda bi: (bi, 0, 0))] + [
                      pl.BlockSpec(w.shape, lambda bi: (0, 0)) for w in small],
        out_specs=new,
        out_shape=jax.ShapeDtypeStruct((b, t, DIFF_V), BF16),
        compiler_params=pltpu.CompilerParams(dimension_semantics=("arbitrary",),
                                             vmem_limit_bytes=VMEM_LIMIT),
        name="attn_sample",
    )(qb, kb, vb, ckt, cv, *small)


def _rope_tables(pos):
    half = ROT_DIM // 2
    inv = jnp.float32(ROPE_THETA) ** (-jnp.arange(half, dtype=F32) * 2.0 / ROT_DIM)
    ang = pos.astype(F32)[:, None] * inv[None, :]
    cos, sin = jnp.cos(ang), jnp.sin(ang)
    n = pos.shape[0]
    rest = DIFF_DK - ROT_DIM
    one = jnp.ones((n, rest), F32)
    zero = jnp.zeros((n, rest), F32)
    zh = jnp.zeros((n, half), F32)
    c64 = jnp.concatenate([cos, cos, one], axis=1)
    a64 = jnp.concatenate([-sin, zh, zero], axis=1)
    b64 = jnp.concatenate([zh, sin, zero], axis=1)
    rep = LANES // DIFF_DK
    return tuple(jnp.tile(t, (1, rep)) for t in (c64, a64, b64))


def _layer_params(w, l):
    d = w['w_in'].shape[1]
    d_ff = w['ffn1_w_down'].shape[1]
    win = w['w_in'][l]
    o = 0
    w_conv = win[:, o:o + CONV_CH]; o += CONV_CH
    w_b = win[:, o:o + GDN_HEADS]; o += GDN_HEADS
    w_a = win[:, o:o + GDN_HEADS]; o += GDN_HEADS
    w_z = win[:, o:o + GDN_V]; o += GDN_V
    w_q = win[:, o:o + DIFF_QK]; o += DIFF_QK
    w_k = win[:, o:o + DIFF_QK]; o += DIFF_QK
    w_v = win[:, o:o + DIFF_V]
    zpad = jnp.zeros((d, LANES - GDN_HEADS), win.dtype)
    w_ba = jnp.concatenate([w_b, zpad, w_a, zpad], axis=1)
    hpad = lambda v: jnp.concatenate([v.astype(F32), jnp.zeros((LANES - GDN_HEADS,), F32)])[None, :]
    grp = jnp.arange(DIFF_QK) // DIFF_DK
    bf = lambda t: t.astype(BF16)
    r2 = lambda v: v.astype(F32)[None, :]
    return dict(
        n1=r2(w['ffn1_norm'][l]), wg1=bf(w['ffn1_w_gu'][l][:, :d_ff]), wu1=bf(w['ffn1_w_gu'][l][:, d_ff:]),
        wd1=bf(w['ffn1_w_down'][l]),
        nm=r2(w['mix_norm'][l]), w_conv=bf(w_conv), w_z=bf(w_z), w_ba=bf(w_ba), w_q=bf(w_q), w_k=bf(w_k),
        w_v=bf(w_v),
        qn=r2(jnp.tile(w['q_norm'][l], DIFF_QK // DIFF_DK)), kn=r2(jnp.tile(w['k_norm'][l], DIFF_QK // DIFF_DK)),
        gmat=(grp[:, None] == grp[None, :]).astype(BF16),
        head_mat=(grp[:, None] // 2 == grp[None, :] // 2).astype(BF16),
        conv_w=w['conv_w'][l].astype(F32), a_log=hpad(w['a_log'][l]), dt_bias=hpad(w['dt_bias'][l]),
        gdn_on=r2(w['gdn_out_norm'][l]),
        lambda_q=w['lambda_q'][l].astype(F32), lambda_k=w['lambda_k'][l].astype(F32),
        diff_on=r2(w['diff_out_norm'][l]),
        wo=bf(w['w_out'][l]), n2=r2(w['ffn2_norm'][l]), wg2=bf(w['ffn2_w_gu'][l][:, :d_ff]),
        wu2=bf(w['ffn2_w_gu'][l][:, d_ff:]), wd2=bf(w['ffn2_w_down'][l]),
    )


def _pick_tile(n, pref):
    t = min(n, pref)
    assert n % t == 0
    return t


def _layer(x, pos, k_hist, v_hist, conv_hist, s0, p, lam_init):
    b, l, d = x.shape
    n = b * l
    tm = _pick_tile(n, 256)
    tabs = _rope_tables(pos)
    if l >= tm:
        assert l % tm == 0
        n_pos_tiles = l // tm
    else:
        assert tm % l == 0
        tabs = tuple(jnp.tile(t, (tm // l, 1)) for t in tabs)
        n_pos_tiles = 1
    prompt = k_hist is None
    pre = _pre_call(x.reshape(n, d), tabs, p, tm, n_pos_tiles, b, prompt)
    x1, conv, z, ba, qb, vb, kf, vf = pre[:8]
    c = min(CHUNK, l)
    assert l % c == 0
    tc = GDN_TILE
    assert l % tc == 0 or n == tc
    if conv_hist is None:
        conv_hist = jnp.zeros((b, CONV_W - 1, CONV_CH), F32)
        s0 = jnp.zeros((b, GDN_HEADS, GDN_DK, GDN_DV), F32)
    conv3 = conv.reshape(b, l, CONV_CH)
    og, s_new = _gdn_call(conv3, z.reshape(b, l, GDN_V), ba.reshape(b, l, 2 * LANES),
                          conv_hist.astype(F32), s0.astype(F32), p, tc, c)
    conv_new = jnp.concatenate([conv_hist.astype(F32), conv3], axis=1)[:, -(CONV_W - 1):]
    q3 = qb.reshape(b, l, DIFF_QK)
    v3 = vb.reshape(b, l, DIFF_V)
    if prompt:
        tk = 2 * LANES
        od = _attn_prompt_call(q3, kf, v3, p, _pick_tile(l, 2 * tk), tk, lam_init)
        k_out = jnp.transpose(kf.reshape(b, DIFF_HEADS, 2, DIFF_DK, l), (0, 4, 1, 2, 3))
    else:
        past = k_hist.shape[1]
        ckt = jnp.transpose(k_hist, (0, 2, 3, 4, 1)).reshape(b, DIFF_HEADS, LANES, past)
        od = _attn_sample_call(q3, pre[8].reshape(b, l, DIFF_QK), v3, ckt,
                               v_hist.reshape(b, past * DIFF_HEADS, DIFF_DV), p, lam_init)
        k_out = kf.reshape(b, l, DIFF_HEADS, 2, DIFF_DK)
    y = _post_call(x1, og.reshape(n, GDN_V), od.reshape(n, DIFF_V), p, tm)
    return (y.reshape(b, l, d), k_out, vf.reshape(b, l, DIFF_HEADS, DIFF_DV), s_new, conv_new)


def kernel(x_prompt, x_sample, cache_k, cache_v, state_gdn, state_conv, ffn1_norm, ffn1_w_gu, ffn1_w_down,
           mix_norm, w_in, conv_w, a_log, dt_bias, gdn_out_norm, q_norm, k_norm, lambda_q, lambda_k,
           diff_out_norm, w_out, ffn2_norm, ffn2_w_gu, ffn2_w_down):
    w = dict(ffn1_norm=ffn1_norm, ffn1_w_gu=ffn1_w_gu, ffn1_w_down=ffn1_w_down, mix_norm=mix_norm, w_in=w_in,
             conv_w=conv_w, a_log=a_log, dt_bias=dt_bias, gdn_out_norm=gdn_out_norm, q_norm=q_norm,
             k_norm=k_norm, lambda_q=lambda_q, lambda_k=lambda_k, diff_out_norm=diff_out_norm, w_out=w_out,
             ffn2_norm=ffn2_norm, ffn2_w_gu=ffn2_w_gu, ffn2_w_down=ffn2_w_down)
    depth = w_in.shape[0]
    pos_p = jnp.arange(x_prompt.shape[1])
    pos_s = cache_k.shape[2] + jnp.arange(x_sample.shape[1])
    hp, hs = x_prompt, x_sample
    outs = [[] for _ in range(8)]
    for l in range(depth):
        lam_init = 0.8 - 0.6 * math.exp(-0.3 * l)
        p = _layer_params(w, l)
        hp, kp, vp, sp, cp = _layer(hp, pos_p, None, None, None, None, p, lam_init)
        hs, ks, vs, ss, cs = _layer(hs, pos_s, cache_k[l], cache_v[l], state_conv[l], state_gdn[l], p, lam_init)
        for acc, val in zip(outs, (kp, vp, sp, cp, ks, vs, ss, cs)):
            acc.append(val)
    return (hp, hs) + tuple(jnp.stack(o) for o in outs)
```

```python
import functools
import math

import jax
import jax.numpy as jnp
from jax import lax
from jax.experimental import pallas as pl
from jax.experimental.pallas import tpu as pltpu

F32 = jnp.float32
BF16 = jnp.bfloat16

EPS = 1e-6
CHUNK = 64
GDN_HEADS = 4
GDN_DK = 128
GDN_DV = 128
CONV_W = 4
DIFF_HEADS = 4
DIFF_DK = 64
DIFF_DV = 128
ROT_DIM = DIFF_DK // 4
ROPE_THETA = 500000.0
GDN_QK = GDN_HEADS * GDN_DK
GDN_V = GDN_HEADS * GDN_DV
CONV_CH = 2 * GDN_QK + GDN_V
DIFF_QK = DIFF_HEADS * 2 * DIFF_DK
DIFF_V = DIFF_HEADS * DIFF_DV

LANES = 128
INV_BLOCK = 16
GDN_TILE = 128
GDN_BLOCK_TILES = 4
VMEM_LIMIT = 56 * 1024 * 1024
Q_SCALE = (DIFF_DK ** -0.5) * math.log2(math.e)


def _dot(a, b):
    return jnp.dot(a, b, preferred_element_type=F32)


def _dot_nt(a, b):
    return lax.dot_general(a, b, (((1,), (1,)), ((), ())), preferred_element_type=F32)


def _rms(x, g):
    return x * lax.rsqrt(jnp.mean(x * x, axis=-1, keepdims=True) + EPS) * g


def _silu(x):
    return x * jax.nn.sigmoid(x)


def _swiglu(xn, wg_ref, wu_ref, wd_ref):
    g = _dot(xn, wg_ref[...])
    u = _dot(xn, wu_ref[...])
    act = (_silu(g) * u).astype(BF16)
    return _dot(act, wd_ref[...])


def _const_spec(shape):
    nd = len(shape)
    return pl.BlockSpec(shape, lambda *_: (0,) * nd, pipeline_mode=pl.Buffered(1))


def _pre_kernel(x_ref, cos_ref, sa_ref, sb_ref, n1_ref, wg_ref, wu_ref, wd_ref, nm_ref,
                wc_ref, wz_ref, wba_ref, wq_ref, wk_ref, wv_ref, qn_ref, kn_ref, gm_ref,
                x1_ref, conv_ref, z_ref, ba_ref, qb_ref, vb_ref, kf_ref, vf_ref, *maybe_kb_ref,
                k_pos_minor):
    x = x_ref[...]
    xn = _rms(x, n1_ref[...]).astype(BF16)
    x1 = x + 0.5 * _swiglu(xn, wg_ref, wu_ref, wd_ref)
    x1_ref[...] = x1
    h = _rms(x1, nm_ref[...]).astype(BF16)
    tm = x.shape[0]
    cos = cos_ref[...]
    sa = sa_ref[...]
    sb = sb_ref[...]
    gm = gm_ref[...]

    def norm_rope(t, gw):
        ss = _dot((t * t).astype(BF16), gm)
        t = t * lax.rsqrt(ss * (1.0 / DIFF_DK) + EPS) * gw
        outs = []
        for hh in range(DIFF_HEADS):
            th = t[:, hh * LANES:(hh + 1) * LANES]
            up = pltpu.roll(th, LANES - ROT_DIM // 2, 1)
            dn = pltpu.roll(th, ROT_DIM // 2, 1)
            outs.append(th * cos + up * sa + dn * sb)
        return jnp.concatenate(outs, axis=1)

    k = norm_rope(_dot(h, wk_ref[...]), kn_ref[...])
    if k_pos_minor:
        kf_ref[...] = k.T.reshape(DIFF_HEADS, LANES, tm)
    else:
        kf_ref[...] = k
        maybe_kb_ref[0][...] = k.astype(BF16)
    q = norm_rope(_dot(h, wq_ref[...]), qn_ref[...])
    qb_ref[...] = (q * Q_SCALE).astype(BF16)
    v = _dot(h, wv_ref[...])
    vb_ref[...] = v.astype(BF16)
    if k_pos_minor:
        for hh in range(DIFF_HEADS):
            vf_ref[pl.ds(hh, tm, stride=DIFF_HEADS), :] = v[:, hh * DIFF_DV:(hh + 1) * DIFF_DV]
    else:
        vf_ref[...] = v
    z_ref[...] = _dot(h, wz_ref[...])
    ba_ref[...] = _dot(h, wba_ref[...])
    conv_ref[...] = _dot(h, wc_ref[...])


def _pre_call(x2d, tabs, p, tm, n_pos_tiles, batch, k_pos_minor):
    n, d = x2d.shape
    grid = (n // tm,)
    row = lambda w: pl.BlockSpec((tm, w), lambda i: (i, 0))
    tab = pl.BlockSpec((tm, LANES), lambda i: (i % n_pos_tiles, 0))
    weights = [p['n1'], p['wg1'], p['wu1'], p['wd1'], p['nm'], p['w_conv'], p['w_z'], p['w_ba'],
               p['w_q'], p['w_k'], p['w_v'], p['qn'], p['kn'], p['gmat']]
    in_specs = [row(d), tab, tab, tab] + [_const_spec(w.shape) for w in weights]
    outs = [(d, F32), (CONV_CH, F32), (GDN_V, F32), (2 * LANES, F32), (DIFF_QK, BF16), (DIFF_V, BF16)]
    out_specs = [row(w) for w, _ in outs]
    out_shape = [jax.ShapeDtypeStruct((n, w), dt) for w, dt in outs]
    if k_pos_minor:
        l = n // batch
        tiles = l // tm
        out_specs += [pl.BlockSpec((None, DIFF_HEADS, LANES, tm), lambda i: (i // tiles, 0, 0, i % tiles)),
                      pl.BlockSpec((tm * DIFF_HEADS, DIFF_DV), lambda i: (i, 0))]
        out_shape += [jax.ShapeDtypeStruct((batch, DIFF_HEADS, LANES, l), F32),
                      jax.ShapeDtypeStruct((n * DIFF_HEADS, DIFF_DV), F32)]
    else:
        out_specs += [row(DIFF_QK), row(DIFF_V), row(DIFF_QK)]
        out_shape += [jax.ShapeDtypeStruct((n, DIFF_QK), F32), jax.ShapeDtypeStruct((n, DIFF_V), F32),
                      jax.ShapeDtypeStruct((n, DIFF_QK), BF16)]
    return pl.pallas_call(
        functools.partial(_pre_kernel, k_pos_minor=k_pos_minor),
        grid=grid,
        in_specs=in_specs,
        out_specs=out_specs,
        out_shape=out_shape,
        compiler_params=pltpu.CompilerParams(dimension_semantics=("arbitrary",),
                                             vmem_limit_bytes=VMEM_LIMIT),
        name="pre",
    )(x2d, *tabs, *weights)


def _post_kernel(x1_ref, og_ref, od_ref, wo_ref, n2_ref, wg_ref, wu_ref, wd_ref, y_ref):
    mixed = jnp.concatenate([og_ref[...], od_ref[...]], axis=1)
    x2 = x1_ref[...] + _dot(mixed, wo_ref[...])
    xn = _rms(x2, n2_ref[...]).astype(BF16)
    y_ref[...] = x2 + 0.5 * _swiglu(xn, wg_ref, wu_ref, wd_ref)


def _post_call(x1, og, od, p, tm):
    n, d = x1.shape
    row = lambda w: pl.BlockSpec((tm, w), lambda i: (i, 0))
    weights = [p['wo'], p['n2'], p['wg2'], p['wu2'], p['wd2']]
    return pl.pallas_call(
        _post_kernel,
        grid=(n // tm,),
        in_specs=[row(d), row(GDN_V), row(DIFF_V)] + [_const_spec(w.shape) for w in weights],
        out_specs=row(d),
        out_shape=jax.ShapeDtypeStruct((n, d), F32),
        compiler_params=pltpu.CompilerParams(dimension_semantics=("arbitrary",),
                                             vmem_limit_bytes=VMEM_LIMIT),
        name="post",
    )(x1, og, od, *weights)


def _exact3(m01, x):
    x1 = x.astype(BF16)
    r1 = x - x1.astype(F32)
    x2 = r1.astype(BF16)
    x3 = (r1 - x2.astype(F32)).astype(BF16)
    return _dot(m01, x1) + (_dot(m01, x2) + _dot(m01, x3))


CONV_PAD = 8


def _gdn_kernel(x_ref, z_ref, ba_ref, hist_ref, s0_ref, cw_ref, alog_ref, dtb_ref, on_ref, hm_ref,
                o_ref, s_out_ref, xp_ref, *maybe_s_ref, tc, tile, c, seqs):
    pad = CONV_PAD
    nchunk = tile // c
    ntile = tc // tile
    assert c % INV_BLOCK == 0 and tile % LANES == 0 and tc % tile == 0
    assert seqs == 1 or (seqs == nchunk and ntile == 1)
    cw = cw_ref[...]
    x = x_ref[...]

    def conv(xp, rows):
        y = xp[rows] * cw[CONV_W - 1:CONV_W, :]
        for k in range(1, CONV_W):
            y = y + pltpu.roll(xp, k, 0)[rows] * cw[CONV_W - 1 - k:CONV_W - k, :]
        return y

    if seqs == 1:
        s_ref, = maybe_s_ref
        t = pl.program_id(1)

        @pl.when(t == 0)
        def _():
            xp_ref[0:pad, :] = jnp.zeros((pad, CONV_CH), F32)
            xp_ref[pad - (CONV_W - 1):pad, :] = hist_ref[...]
            s_ref[...] = s0_ref[...]

        xp_ref[pad:pad + tc, :] = x
        y = conv(xp_ref[...], slice(pad, pad + tc))
        xp_ref[pad - (CONV_W - 1):pad, :] = x[tc - (CONV_W - 1):tc, :]
    else:
        stride = pad + c
        xp_ref[...] = jnp.zeros(xp_ref.shape, F32)
        for b in range(seqs):
            xp_ref[b * stride + pad - (CONV_W - 1):b * stride + pad, :] = hist_ref[b]
            xp_ref[b * stride + pad:(b + 1) * stride, :] = x[b * c:(b + 1) * c, :]
        yp = conv(xp_ref[...], slice(None))
        y = jnp.concatenate([yp[b * stride + pad:(b + 1) * stride, :] for b in range(seqs)], axis=0)
    y = _silu(y)
    hm = hm_ref[...]
    yk = y[:, GDN_QK:2 * GDN_QK]
    yk = yk * lax.rsqrt(_dot((yk * yk).astype(BF16), hm) + EPS)

    ba = ba_ref[...]
    beta = jax.nn.sigmoid(ba[:, :LANES])
    g = -jnp.exp(alog_ref[...]) * jax.nn.softplus(ba[:, LANES:] + dtb_ref[...])
    ri = lax.broadcasted_iota(jnp.int32, (tile, tile), 0)
    ci = lax.broadcasted_iota(jnp.int32, (tile, tile), 1)
    incl = ((ri // c) == (ci // c)) & (ci <= ri)
    same_blk = (ri // INV_BLOCK) == (ci // INV_BLOCK)
    diag = ri == ci
    eye = jnp.where(diag, 1.0, 0.0)
    incl01 = jnp.where(incl, 1.0, 0.0).astype(BF16)
    tile_rows = [slice(ti * tile, (ti + 1) * tile) for ti in range(ntile)]
    gcums = [_exact3(incl01, g[rows]) for rows in tile_rows]
    gcum_ts = [gc.T for gc in gcums]
    on = on_ref[...]

    heads = range(GDN_HEADS)
    units = [(ti, h) for ti in range(ntile) for h in heads]

    def per_head(f, *lists):
        return [f(*args) for args in zip(*lists)]

    def head_cols(h, width):
        return slice(h * width, (h + 1) * width)

    ks = [yk[tile_rows[ti], head_cols(h, GDN_DK)] for ti, h in units]
    vs = [y[tile_rows[ti], 2 * GDN_QK + h * GDN_DV:2 * GDN_QK + (h + 1) * GDN_DV] for ti, h in units]
    gcs = [gcums[ti][:, h:h + 1] for ti, h in units]
    bcs = [beta[tile_rows[ti], h:h + 1] for ti, h in units]
    decs = [jnp.exp(jnp.where(incl, gcs[u] - gcum_ts[ti][h:h + 1, :], -jnp.inf))
            for u, (ti, h) in enumerate(units)]
    kbs = per_head(lambda k: k.astype(BF16), ks)
    a_s = per_head(lambda bc, kb, dec: jnp.where(diag, 0.0, bc * _dot_nt(kb, kb) * dec), bcs, kbs, decs)

    assert INV_BLOCK == 16 and c in (INV_BLOCK, 4 * INV_BLOCK)
    bf = lambda xs: per_head(lambda x: x.astype(BF16), xs)
    mm = lambda xs, ys: per_head(_dot, xs, ys)
    ds = per_head(lambda a: jnp.where(same_blk, a, 0.0), a_s) if c > INV_BLOCK else a_s
    sd = bf(ds)
    d2 = mm(sd, sd)
    s2 = bf(d2)
    d3 = mm(sd, s2)
    d4 = mm(s2, s2)
    s4 = bf(d4)
    n1 = per_head(lambda d, x2, x3: eye - d + x2 - x3, ds, d2, d3)
    n2 = per_head(jnp.add, n1, mm(bf(n1), s4))
    d8 = mm(s4, s4)
    tinv = per_head(jnp.add, n2, mm(bf(n2), bf(d8)))
    if c > INV_BLOCK:
        std = bf(tinv)
        ms = mm(std, bf(per_head(jnp.subtract, a_s, ds)))
        sm = bf(ms)
        m2 = mm(sm, sm)
        m3 = mm(sm, bf(m2))
        tinv = mm(bf(per_head(lambda m, x2, x3: eye - m + x2 - x3, ms, m2, m3)), std)

    egs = per_head(jnp.exp, gcs)
    rhs = per_head(lambda v, k, bc, eg: jnp.concatenate([v * bc, k * (bc * eg)], axis=1), vs, ks, bcs, egs)
    sols = mm(bf(tinv), bf(rhs))
    yq = y[:, :GDN_QK]
    yq = yq * (lax.rsqrt(_dot((yq * yq).astype(BF16), hm) + EPS) * (GDN_DK ** -0.5))
    qs = [yq[tile_rows[ti], head_cols(h, GDN_DK)] for ti, h in units]
    qks = per_head(lambda q, kb, dec: (_dot_nt(q.astype(BF16), kb) * dec).astype(BF16), qs, kbs, decs)
    u0s = [sol[:, :GDN_DV] for sol in sols]
    wbs = [sol[:, GDN_DV:].astype(BF16) for sol in sols]
    qds = per_head(lambda q, eg: (q * eg).astype(BF16), qs, egs)

    unit = lambda ti, h: ti * GDN_HEADS + h

    def chunk_terms(ti, ic):
        rs = slice(ic * c, (ic + 1) * c)
        gends = [gcs[unit(ti, h)][(ic + 1) * c - 1:(ic + 1) * c, :] for h in heads]
        kds = [(ks[unit(ti, h)][rs] * jnp.exp(gends[h] - gcs[unit(ti, h)][rs])).astype(BF16) for h in heads]
        return rs, gends, kds

    def state_terms(rs, states):
        return [_dot(jnp.concatenate([wbs[h][rs], qds[h][rs]], axis=0), states[h].astype(BF16)) for h in heads]

    def next_state(states, gends, kds, us_c):
        return [states[h] * jnp.exp(gends[h]) + lax.dot_general(
            kds[h], us_c[h], (((0,), (0,)), ((), ())), preferred_element_type=F32) for h in heads]

    us = [[] for _ in units]
    outs = [[] for _ in units]
    if seqs == 1:
        trans = []
        for ti in range(ntile):
            for ic in range(nchunk):
                rs, gends, kds = chunk_terms(ti, ic)
                kt_wu = [lax.dot_general(
                    kds[h], jnp.concatenate([wbs[unit(ti, h)][rs], u0s[unit(ti, h)][rs].astype(BF16)], axis=1),
                    (((0,), (0,)), ((), ())), preferred_element_type=F32) for h in heads]
                trans.append(([jnp.exp(g) for g in gends], [m[:, :GDN_DK].astype(BF16) for m in kt_wu],
                              [m[:, GDN_DK:] for m in kt_wu]))
        states = [s_ref[h] for h in heads]
        chunk_states = []
        for decay, kt_w, kt_u0 in trans:
            sb = [s.astype(BF16) for s in states]
            chunk_states.append(sb)
            states = [states[h] * decay[h] - _dot(kt_w[h], sb[h]) + kt_u0[h] for h in heads]
        for h in heads:
            s_ref[h] = states[h]
        for ti in range(ntile):
            for ic in range(nchunk):
                rs = slice(ic * c, (ic + 1) * c)
                sb = chunk_states[ti * nchunk + ic]
                wss = [_dot(jnp.concatenate([wbs[unit(ti, h)][rs], qds[unit(ti, h)][rs]], axis=0), sb[h])
                       for h in heads]
                for h in heads:
                    us[unit(ti, h)].append((u0s[unit(ti, h)][rs] - wss[h][:c]).astype(BF16))
                    outs[unit(ti, h)].append(wss[h][c:])
        o = [jnp.concatenate(outs[u], axis=0) + _dot(qks[u], jnp.concatenate(us[u], axis=0))
             for u in range(len(units))]

        @pl.when(t == pl.num_programs(1) - 1)
        def _():
            s_out_ref[...] = s_ref[...]
    else:
        for ic in range(nchunk):
            rs, gends, kds = chunk_terms(0, ic)
            states = [s0_ref[ic, h] for h in heads]
            wss = state_terms(rs, states)
            us_c = [(u0s[h][rs] - wss[h][:c]).astype(BF16) for h in heads]
            new = next_state(states, gends, kds, us_c)
            for h in heads:
                us[h].append(us_c[h])
                outs[h].append(wss[h][c:])
                s_out_ref[ic, h] = new[h]
        o = [jnp.concatenate(outs[h], axis=0) + _dot(qks[h], jnp.concatenate(us[h], axis=0)) for h in heads]
    for u, (ti, h) in enumerate(units):
        rows, cols = tile_rows[ti], head_cols(h, GDN_DV)
        o_ref[rows, cols] = (_rms(o[u], on) * _silu(z_ref[rows, cols])).astype(o_ref.dtype)


def _gdn_call(conv, z, ba, hist, s0, p, tile, c):
    b, l, _ = conv.shape
    small = [p['conv_w'], p['a_log'], p['dt_bias'], p['gdn_on'], p['head_mat']]
    state = (GDN_HEADS, GDN_DK, GDN_DV)
    if l >= tile:
        tc = _pick_tile(l, GDN_BLOCK_TILES * tile)
        seqs = 1
        grid = (b, l // tc)
        row = lambda w: pl.BlockSpec((None, tc, w), lambda i, j: (i, j, 0))
        per_b = lambda shape: pl.BlockSpec((None,) + shape, lambda i, j: (i,) + (0,) * len(shape))
        scratch = [pltpu.VMEM((tc + CONV_PAD, CONV_CH), F32), pltpu.VMEM(state, F32)]
    else:
        tc = tile = b * l
        assert l == c and tc % LANES == 0
        seqs = b
        grid = (1, 1)
        conv, z, ba = (t.reshape(1, tc, t.shape[-1]) for t in (conv, z, ba))
        row = lambda w: pl.BlockSpec((None, tc, w), lambda i, j: (0, 0, 0))
        per_b = lambda shape: pl.BlockSpec((b,) + shape, lambda i, j: (0,) * (len(shape) + 1))
        scratch = [pltpu.VMEM((seqs * (c + CONV_PAD), CONV_CH), F32)]
    o, s_new = pl.pallas_call(
        functools.partial(_gdn_kernel, tc=tc, tile=tile, c=c, seqs=seqs),
        grid=grid,
        in_specs=[row(CONV_CH), row(GDN_V), row(2 * LANES), per_b((CONV_W - 1, CONV_CH)), per_b(state)] + [
            pl.BlockSpec(w.shape, lambda i, j: (0, 0)) for w in small],
        out_specs=[row(GDN_V), per_b(state)],
        out_shape=[jax.ShapeDtypeStruct(conv.shape[:2] + (GDN_V,), BF16),
                   jax.ShapeDtypeStruct((b,) + state, F32)],
        scratch_shapes=scratch,
        compiler_params=pltpu.CompilerParams(dimension_semantics=("arbitrary", "arbitrary"),
                                             vmem_limit_bytes=VMEM_LIMIT),
        name="gdn",
    )(conv, z, ba, hist, s0, *small)
    return o.reshape(b, l, GDN_V), s_new


def _stack_maps(q):
    lane = lax.broadcasted_iota(jnp.int32, q.shape, 1)
    zero = jnp.zeros_like(q)
    return jnp.concatenate([jnp.where(lane < DIFF_DK, q, zero), jnp.where(lane >= DIFF_DK, q, zero)],
                           axis=0)


def _diff_finish(acc, l, t, lq_ref, lk_ref, dn_ref, lam_init):
    lam_e = jnp.exp(jnp.sum(lq_ref[...] * lk_ref[...], axis=-1, keepdims=True))
    lam = lam_e[0:1] - lam_e[1:2] + lam_init
    o = acc[:t] / l[:t] - lam * (acc[t:] / l[t:])
    return _rms(o, dn_ref[...]) * (1.0 - lam_init)


def _halves_max(s):
    return jnp.maximum(s[:, :LANES], s[:, LANES:])


def _attn_prompt_kernel(q_ref, kt_ref, v_ref, bias_ref, lq_ref, lk_ref, dn_ref, o_ref,
                        kt_scr, vx_scr, s_scr, m_scr, acc_scr, *, tq, tk, lam_init):
    i = pl.program_id(2)
    assert tk == 2 * LANES and tq % tk == 0
    nsub = tq // tk

    @pl.when(i == 0)
    def _():
        for j in range(kt_scr.shape[0]):
            kt_scr[j] = kt_ref[:, j * tk:(j + 1) * tk].astype(BF16)
        vx_scr[:, :DIFF_DV] = v_ref[...]
        vx_scr[:, DIFF_DV:] = jnp.ones((vx_scr.shape[0], LANES), BF16)

    qq = _stack_maps(q_ref[...])

    def scores(j):
        return _dot(qq, kt_scr[j])

    def weighted_values(ss, blocks, mm):
        p = [jnp.exp2(s[:, half * LANES:(half + 1) * LANES] - mm) for s in ss for half in range(tk // LANES)]
        vx = [vx_scr[pl.ds(pl.multiple_of(j * tk, tk), tk), :] for j in blocks]
        return _dot(jnp.concatenate(p, axis=1).astype(BF16), jnp.concatenate(vx, axis=0))

    def pass1(blocks):
        ss = [scores(j) for j in blocks]
        for j, s in zip(blocks, ss):
            s_scr[j] = s
        m_scr[...] = jnp.maximum(m_scr[...], functools.reduce(jnp.maximum, [_halves_max(s) for s in ss]))

    def pass2(blocks):
        acc_scr[...] += weighted_values([s_scr[j] for j in blocks], blocks, m_scr[...])

    def over_off_diagonal(body):
        def trip(t, carry):
            body([t * 2 * nsub + u for u in range(2 * nsub)])
            return carry

        lax.fori_loop(0, i // 2, trip, 0)

        @pl.when(i % 2 == 1)
        def _():
            body([(i - 1) * nsub + u for u in range(nsub)])

    diag_blocks = [i * nsub + d for d in range(nsub)]
    s_diag = [scores(j) + bias_ref[d] for d, j in enumerate(diag_blocks)]
    m_scr[...] = functools.reduce(jnp.maximum, [_halves_max(s) for s in s_diag])
    over_off_diagonal(pass1)
    m = jnp.broadcast_to(jnp.max(m_scr[...], axis=-1, keepdims=True), m_scr.shape)
    m_scr[...] = m
    acc_scr[...] = weighted_values(s_diag, diag_blocks, m)
    over_off_diagonal(pass2)
    acc = acc_scr[...]
    o_ref[...] = _diff_finish(acc[:, :DIFF_DV], acc[:, DIFF_DV:], tq, lq_ref, lk_ref, dn_ref,
                              lam_init).astype(o_ref.dtype)


def _attn_prompt_call(qb, kt, vb, p, tq, tk, lam_init):
    b, l, _ = qb.shape
    small = [p['lambda_q'], p['lambda_k'], p['diff_on']]
    r = (jnp.arange(2 * tq) % tq)[None, :, None] // CHUNK
    cidx = (jnp.arange(tk)[None, None, :] + tk * jnp.arange(tq // tk)[:, None, None]) // CHUNK
    bias = jnp.where(cidx <= r, 0.0, -jnp.inf).astype(F32)
    return pl.pallas_call(
        functools.partial(_attn_prompt_kernel, tq=tq, tk=tk, lam_init=lam_init),
        grid=(b, DIFF_HEADS, l // tq),
        in_specs=[pl.BlockSpec((None, tq, LANES), lambda bi, h, i: (bi, i, h)),
                  pl.BlockSpec((None, None, LANES, l), lambda bi, h, i: (bi, h, 0, 0)),
                  pl.BlockSpec((None, l, LANES), lambda bi, h, i: (bi, 0, h)),
                  _const_spec(bias.shape)] + [
                      pl.BlockSpec(w.shape, lambda bi, h, i: (0, 0)) for w in small],
        out_specs=pl.BlockSpec((None, tq, LANES), lambda bi, h, i: (bi, i, h)),
        out_shape=jax.ShapeDtypeStruct((b, l, DIFF_V), BF16),
        scratch_shapes=[pltpu.VMEM((l // tk, LANES, tk), BF16),
                        pltpu.VMEM((l, DIFF_DV + LANES), BF16),
                        pltpu.VMEM((max((l - tq) // tk, 1), 2 * tq, tk), F32),
                        pltpu.VMEM((2 * tq, LANES), F32),
                        pltpu.VMEM((2 * tq, DIFF_DV + LANES), F32)],
        compiler_params=pltpu.CompilerParams(
            dimension_semantics=("arbitrary", "arbitrary", "arbitrary"),
            vmem_limit_bytes=VMEM_LIMIT),
        name="attn_prompt",
    )(qb, kt, vb, bias, *small)


def _attn_sample_kernel(q_ref, kn_ref, vn_ref, ckt_ref, cv_ref, lq_ref, lk_ref, dn_ref, o_ref,
                        *, t, lam_init):
    past = ckt_ref.shape[-1]
    for h in range(DIFF_HEADS):
        cols = slice(h * LANES, (h + 1) * LANES)
        qq = _stack_maps(q_ref[:, cols])
        s_c = _dot(qq, ckt_ref[h].astype(BF16))
        s_n = _dot_nt(qq, kn_ref[:, cols])
        m = jnp.maximum(jnp.max(s_c, axis=-1, keepdims=True), jnp.max(s_n, axis=-1, keepdims=True))
        p_c = jnp.exp2(s_c - m)
        p_n = jnp.exp2(s_n - m)
        l = jnp.sum(p_c, axis=-1, keepdims=True) + jnp.sum(p_n, axis=-1, keepdims=True)
        cv = cv_ref[pl.ds(h, past, stride=DIFF_HEADS), :].astype(BF16)
        acc = _dot(p_c.astype(BF16), cv) + _dot(p_n.astype(BF16), vn_ref[:, cols])
        o_ref[:, cols] = _diff_finish(acc, l, t, lq_ref, lk_ref, dn_ref, lam_init).astype(o_ref.dtype)


def _attn_sample_call(qb, kb, vb, ckt, cv, p, lam_init):
    b, t, _ = qb.shape
    past = ckt.shape[-1]
    small = [p['lambda_q'], p['lambda_k'], p['diff_on']]
    new = pl.BlockSpec((None, t, DIFF_V), lambda bi: (bi, 0, 0))
    return pl.pallas_call(
        functools.partial(_attn_sample_kernel, t=t, lam_init=lam_init),
        grid=(b,),
        in_specs=[new, new, new,
                  pl.BlockSpec((None, DIFF_HEADS, LANES, past), lambda bi: (bi, 0, 0, 0)),
                  pl.BlockSpec((None, past * DIFF_HEADS, DIFF_DV), lambda bi: (bi, 0, 0))] + [
                      pl.BlockSpec(w.shape, lambda bi: (0, 0)) for w in small],
        out_specs=new,
        out_shape=jax.ShapeDtypeStruct((b, t, DIFF_V), BF16),
        compiler_params=pltpu.CompilerParams(dimension_semantics=("arbitrary",),
                                             vmem_limit_bytes=VMEM_LIMIT),
        name="attn_sample",
    )(qb, kb, vb, ckt, cv, *small)


def _rope_tables(pos):
    half = ROT_DIM // 2
    inv = jnp.float32(ROPE_THETA) ** (-jnp.arange(half, dtype=F32) * 2.0 / ROT_DIM)
    ang = pos.astype(F32)[:, None] * inv[None, :]
    cos, sin = jnp.cos(ang), jnp.sin(ang)
    n = pos.shape[0]
    rest = DIFF_DK - ROT_DIM
    one = jnp.ones((n, rest), F32)
    zero = jnp.zeros((n, rest), F32)
    zh = jnp.zeros((n, half), F32)
    c64 = jnp.concatenate([cos, cos, one], axis=1)
    a64 = jnp.concatenate([-sin, zh, zero], axis=1)
    b64 = jnp.concatenate([zh, sin, zero], axis=1)
    rep = LANES // DIFF_DK
    return tuple(jnp.tile(t, (1, rep)) for t in (c64, a64, b64))


def _layer_params(w, l):
    d = w['w_in'].shape[1]
    d_ff = w['ffn1_w_down'].shape[1]
    win = w['w_in'][l]
    o = 0
    w_conv = win[:, o:o + CONV_CH]; o += CONV_CH
    w_b = win[:, o:o + GDN_HEADS]; o += GDN_HEADS
    w_a = win[:, o:o + GDN_HEADS]; o += GDN_HEADS
    w_z = win[:, o:o + GDN_V]; o += GDN_V
    w_q = win[:, o:o + DIFF_QK]; o += DIFF_QK
    w_k = win[:, o:o + DIFF_QK]; o += DIFF_QK
    w_v = win[:, o:o + DIFF_V]
    zpad = jnp.zeros((d, LANES - GDN_HEADS), win.dtype)
    w_ba = jnp.concatenate([w_b, zpad, w_a, zpad], axis=1)
    hpad = lambda v: jnp.concatenate([v.astype(F32), jnp.zeros((LANES - GDN_HEADS,), F32)])[None, :]
    grp = jnp.arange(DIFF_QK) // DIFF_DK
    bf = lambda t: t.astype(BF16)
    r2 = lambda v: v.astype(F32)[None, :]
    return dict(
        n1=r2(w['ffn1_norm'][l]), wg1=bf(w['ffn1_w_gu'][l][:, :d_ff]), wu1=bf(w['ffn1_w_gu'][l][:, d_ff:]),
        wd1=bf(w['ffn1_w_down'][l]),
        nm=r2(w['mix_norm'][l]), w_conv=bf(w_conv), w_z=bf(w_z), w_ba=bf(w_ba), w_q=bf(w_q), w_k=bf(w_k),
        w_v=bf(w_v),
        qn=r2(jnp.tile(w['q_norm'][l], DIFF_QK // DIFF_DK)), kn=r2(jnp.tile(w['k_norm'][l], DIFF_QK // DIFF_DK)),
        gmat=(grp[:, None] == grp[None, :]).astype(BF16),
        head_mat=(grp[:, None] // 2 == grp[None, :] // 2).astype(BF16),
        conv_w=w['conv_w'][l].astype(F32), a_log=hpad(w['a_log'][l]), dt_bias=hpad(w['dt_bias'][l]),
        gdn_on=r2(w['gdn_out_norm'][l]),
        lambda_q=w['lambda_q'][l].astype(F32), lambda_k=w['lambda_k'][l].astype(F32),
        diff_on=r2(w['diff_out_norm'][l]),
        wo=bf(w['w_out'][l]), n2=r2(w['ffn2_norm'][l]), wg2=bf(w['ffn2_w_gu'][l][:, :d_ff]),
        wu2=bf(w['ffn2_w_gu'][l][:, d_ff:]), wd2=bf(w['ffn2_w_down'][l]),
    )


def _pick_tile(n, pref):
    t = min(n, pref)
    assert n % t == 0
    return t


def _layer(x, pos, k_hist, v_hist, conv_hist, s0, p, lam_init):
    b, l, d = x.shape
    n = b * l
    tm = _pick_tile(n, 256)
    tabs = _rope_tables(pos)
    if l >= tm:
        assert l % tm == 0
        n_pos_tiles = l // tm
    else:
        assert tm % l == 0
        tabs = tuple(jnp.tile(t, (tm // l, 1)) for t in tabs)
        n_pos_tiles = 1
    prompt = k_hist is None
    pre = _pre_call(x.reshape(n, d), tabs, p, tm, n_pos_tiles, b, prompt)
    x1, conv, z, ba, qb, vb, kf, vf = pre[:8]
    c = min(CHUNK, l)
    assert l % c == 0
    tc = GDN_TILE
    assert l % tc == 0 or l == c
    if conv_hist is None:
        conv_hist = jnp.zeros((b, CONV_W - 1, CONV_CH), F32)
        s0 = jnp.zeros((b, GDN_HEADS, GDN_DK, GDN_DV), F32)
    conv3 = conv.reshape(b, l, CONV_CH)
    og, s_new = _gdn_call(conv3, z.reshape(b, l, GDN_V), ba.reshape(b, l, 2 * LANES),
                          conv_hist.astype(F32), s0.astype(F32), p, tc, c)
    conv_new = jnp.concatenate([conv_hist.astype(F32), conv3], axis=1)[:, -(CONV_W - 1):]
    q3 = qb.reshape(b, l, DIFF_QK)
    v3 = vb.reshape(b, l, DIFF_V)
    if prompt:
        tk = 2 * LANES
        od = _attn_prompt_call(q3, kf, v3, p, _pick_tile(l, 2 * tk), tk, lam_init)
        k_out = jnp.transpose(kf.reshape(b, DIFF_HEADS, 2, DIFF_DK, l), (0, 4, 1, 2, 3))
    else:
        past = k_hist.shape[1]
        ckt = jnp.transpose(k_hist, (0, 2, 3, 4, 1)).reshape(b, DIFF_HEADS, LANES, past)
        od = _attn_sample_call(q3, pre[8].reshape(b, l, DIFF_QK), v3, ckt,
                               v_hist.reshape(b, past * DIFF_HEADS, DIFF_DV), p, lam_init)
        k_out = kf.reshape(b, l, DIFF_HEADS, 2, DIFF_DK)
    y = _post_call(x1, og.reshape(n, GDN_V), od.reshape(n, DIFF_V), p, tm)
    return (y.reshape(b, l, d), k_out, vf.reshape(b, l, DIFF_HEADS, DIFF_DV), s_new, conv_new)


def kernel(x_prompt, x_sample, cache_k, cache_v, state_gdn, state_conv, ffn1_norm, ffn1_w_gu, ffn1_w_down,
           mix_norm, w_in, conv_w, a_log, dt_bias, gdn_out_norm, q_norm, k_norm, lambda_q, lambda_k,
           diff_out_norm, w_out, ffn2_norm, ffn2_w_gu, ffn2_w_down):
    w = dict(ffn1_norm=ffn1_norm, ffn1_w_gu=ffn1_w_gu, ffn1_w_down=ffn1_w_down, mix_norm=mix_norm, w_in=w_in,
             conv_w=conv_w, a_log=a_log, dt_bias=dt_bias, gdn_out_norm=gdn_out_norm, q_norm=q_norm,
             k_norm=k_norm, lambda_q=lambda_q, lambda_k=lambda_k, diff_out_norm=diff_out_norm, w_out=w_out,
             ffn2_norm=ffn2_norm, ffn2_w_gu=ffn2_w_gu, ffn2_w_down=ffn2_w_down)
    depth = w_in.shape[0]
    pos_p = jnp.arange(x_prompt.shape[1])
    pos_s = cache_k.shape[2] + jnp.arange(x_sample.shape[1])
    hp, hs = x_prompt, x_sample
    outs = [[] for _ in range(8)]
    for l in range(depth):
        lam_init = 0.8 - 0.6 * math.exp(-0.3 * l)
        p = _layer_params(w, l)
        hp, kp, vp, sp, cp = _layer(hp, pos_p, None, None, None, None, p, lam_init)
        hs, ks, vs, ss, cs = _layer(hs, pos_s, cache_k[l], cache_v[l], state_conv[l], state_gdn[l], p, lam_init)
        for acc, val in zip(outs, (kp, vp, sp, cp, ks, vs, ss, cs)):
            acc.append(val)
    return (hp, hs) + tuple(jnp.stack(o) for o in outs)
```

```python
import functools
import math

import jax
import jax.numpy as jnp
from jax import lax
from jax.experimental import pallas as pl
from jax.experimental.pallas import tpu as pltpu

F32 = jnp.float32
BF16 = jnp.bfloat16

EPS = 1e-6
CHUNK = 64
GDN_HEADS = 4
GDN_DK = 128
GDN_DV = 128
CONV_W = 4
DIFF_HEADS = 4
DIFF_DK = 64
DIFF_DV = 128
ROT_DIM = DIFF_DK // 4
ROPE_THETA = 500000.0
GDN_QK = GDN_HEADS * GDN_DK
GDN_V = GDN_HEADS * GDN_DV
CONV_CH = 2 * GDN_QK + GDN_V
DIFF_QK = DIFF_HEADS * 2 * DIFF_DK
DIFF_V = DIFF_HEADS * DIFF_DV

LANES = 128
INV_BLOCK = 16
GDN_TILE = 128
GDN_BLOCK_TILES = 4
VMEM_LIMIT = 56 * 1024 * 1024
Q_SCALE = (DIFF_DK ** -0.5) * math.log2(math.e)


def _dot(a, b):
    return jnp.dot(a, b, preferred_element_type=F32)


def _dot_nt(a, b):
    return lax.dot_general(a, b, (((1,), (1,)), ((), ())), preferred_element_type=F32)


def _rms(x, g):
    return x * lax.rsqrt(jnp.mean(x * x, axis=-1, keepdims=True) + EPS) * g


def _silu(x):
    return x * jax.nn.sigmoid(x)


def _rms_parts(x, g):
    return (x * g).astype(BF16), lax.rsqrt(jnp.mean(x * x, axis=-1, keepdims=True) + EPS)


def _swiglu(xg, r, wg_ref, wu_ref, wd_ref):
    g = _dot(xg, wg_ref[...]) * r
    u = _dot(xg, wu_ref[...]) * r
    act = (_silu(g) * u).astype(BF16)
    return _dot(act, wd_ref[...])


def _const_spec(shape):
    nd = len(shape)
    return pl.BlockSpec(shape, lambda *_: (0,) * nd, pipeline_mode=pl.Buffered(1))


def _pre_kernel(x_ref, cos_ref, sa_ref, sb_ref, n1_ref, wg_ref, wu_ref, wd_ref, nm_ref,
                wc_ref, wz_ref, wba_ref, wq_ref, wk_ref, wv_ref, qn_ref, kn_ref, gm_ref,
                x1_ref, conv_ref, z_ref, ba_ref, qb_ref, vb_ref, kf_ref, vf_ref, *maybe_kb_ref,
                k_pos_minor):
    x = x_ref[...]
    xg, r = _rms_parts(x, n1_ref[...])
    x1 = x + 0.5 * _swiglu(xg, r, wg_ref, wu_ref, wd_ref)
    x1_ref[...] = x1
    h, rh = _rms_parts(x1, nm_ref[...])
    tm = x.shape[0]
    cos = cos_ref[...]
    sa = sa_ref[...]
    sb = sb_ref[...]
    gm = gm_ref[...]

    def norm_rope(t, gw):
        ss = _dot((t * t).astype(BF16), gm)
        t = t * lax.rsqrt(ss * (1.0 / DIFF_DK) + EPS) * gw
        outs = []
        for hh in range(DIFF_HEADS):
            th = t[:, hh * LANES:(hh + 1) * LANES]
            up = pltpu.roll(th, LANES - ROT_DIM // 2, 1)
            dn = pltpu.roll(th, ROT_DIM // 2, 1)
            outs.append(th * cos + up * sa + dn * sb)
        return jnp.concatenate(outs, axis=1)

    k = norm_rope(_dot(h, wk_ref[...]) * rh, kn_ref[...])
    if k_pos_minor:
        kf_ref[...] = k.T.reshape(DIFF_HEADS, LANES, tm)
    else:
        kf_ref[...] = k
        maybe_kb_ref[0][...] = k.astype(BF16)
    q = norm_rope(_dot(h, wq_ref[...]) * rh, qn_ref[...])
    qb_ref[...] = (q * Q_SCALE).astype(BF16)
    v = _dot(h, wv_ref[...]) * rh
    vb_ref[...] = v.astype(BF16)
    if k_pos_minor:
        for hh in range(DIFF_HEADS):
            vf_ref[pl.ds(hh, tm, stride=DIFF_HEADS), :] = v[:, hh * DIFF_DV:(hh + 1) * DIFF_DV]
    else:
        vf_ref[...] = v
    z_ref[...] = _dot(h, wz_ref[...]) * rh
    ba_ref[...] = _dot(h, wba_ref[...]) * rh
    conv_ref[...] = _dot(h, wc_ref[...]) * rh


def _pre_call(x2d, tabs, p, tm, n_pos_tiles, batch, k_pos_minor):
    n, d = x2d.shape
    grid = (n // tm,)
    row = lambda w: pl.BlockSpec((tm, w), lambda i: (i, 0))
    tab = pl.BlockSpec((tm, LANES), lambda i: (i % n_pos_tiles, 0))
    weights = [p['n1'], p['wg1'], p['wu1'], p['wd1'], p['nm'], p['w_conv'], p['w_z'], p['w_ba'],
               p['w_q'], p['w_k'], p['w_v'], p['qn'], p['kn'], p['gmat']]
    in_specs = [row(d), tab, tab, tab] + [_const_spec(w.shape) for w in weights]
    outs = [(d, F32), (CONV_CH, F32), (GDN_V, F32), (2 * LANES, F32), (DIFF_QK, BF16), (DIFF_V, BF16)]
    out_specs = [row(w) for w, _ in outs]
    out_shape = [jax.ShapeDtypeStruct((n, w), dt) for w, dt in outs]
    if k_pos_minor:
        l = n // batch
        tiles = l // tm
        out_specs += [pl.BlockSpec((None, DIFF_HEADS, LANES, tm), lambda i: (i // tiles, 0, 0, i % tiles)),
                      pl.BlockSpec((tm * DIFF_HEADS, DIFF_DV), lambda i: (i, 0))]
        out_shape += [jax.ShapeDtypeStruct((batch, DIFF_HEADS, LANES, l), F32),
                      jax.ShapeDtypeStruct((n * DIFF_HEADS, DIFF_DV), F32)]
    else:
        out_specs += [row(DIFF_QK), row(DIFF_V), row(DIFF_QK)]
        out_shape += [jax.ShapeDtypeStruct((n, DIFF_QK), F32), jax.ShapeDtypeStruct((n, DIFF_V), F32),
                      jax.ShapeDtypeStruct((n, DIFF_QK), BF16)]
    return pl.pallas_call(
        functools.partial(_pre_kernel, k_pos_minor=k_pos_minor),
        grid=grid,
        in_specs=in_specs,
        out_specs=out_specs,
        out_shape=out_shape,
        compiler_params=pltpu.CompilerParams(dimension_semantics=("arbitrary",),
                                             vmem_limit_bytes=VMEM_LIMIT),
        name="pre",
    )(x2d, *tabs, *weights)


def _post_kernel(x1_ref, og_ref, od_ref, wo_ref, n2_ref, wg_ref, wu_ref, wd_ref, y_ref):
    mixed = jnp.concatenate([og_ref[...], od_ref[...]], axis=1)
    x2 = x1_ref[...] + _dot(mixed, wo_ref[...])
    xg, r = _rms_parts(x2, n2_ref[...])
    y_ref[...] = x2 + 0.5 * _swiglu(xg, r, wg_ref, wu_ref, wd_ref)


def _post_call(x1, og, od, p, tm):
    n, d = x1.shape
    row = lambda w: pl.BlockSpec((tm, w), lambda i: (i, 0))
    weights = [p['wo'], p['n2'], p['wg2'], p['wu2'], p['wd2']]
    return pl.pallas_call(
        _post_kernel,
        grid=(n // tm,),
        in_specs=[row(d), row(GDN_V), row(DIFF_V)] + [_const_spec(w.shape) for w in weights],
        out_specs=row(d),
        out_shape=jax.ShapeDtypeStruct((n, d), F32),
        compiler_params=pltpu.CompilerParams(dimension_semantics=("arbitrary",),
                                             vmem_limit_bytes=VMEM_LIMIT),
        name="post",
    )(x1, og, od, *weights)


def _exact3(m01, x):
    x1 = x.astype(BF16)
    r1 = x - x1.astype(F32)
    x2 = r1.astype(BF16)
    x3 = (r1 - x2.astype(F32)).astype(BF16)
    return _dot(m01, x1) + (_dot(m01, x2) + _dot(m01, x3))


CONV_PAD = 8


def _gdn_kernel(x_ref, z_ref, ba_ref, hist_ref, s0_ref, cw_ref, alog_ref, dtb_ref, on_ref, hm_ref,
                o_ref, s_out_ref, xp_ref, *maybe_s_ref, tc, tile, c, seqs):
    pad = CONV_PAD
    nchunk = tile // c
    ntile = tc // tile
    assert c % INV_BLOCK == 0 and tile % LANES == 0 and tc % tile == 0
    assert seqs == 1 or (seqs == nchunk and ntile == 1)
    cw = cw_ref[...]
    x = x_ref[...]

    def conv(xp, rows):
        y = xp[rows] * cw[CONV_W - 1:CONV_W, :]
        for k in range(1, CONV_W):
            y = y + pltpu.roll(xp, k, 0)[rows] * cw[CONV_W - 1 - k:CONV_W - k, :]
        return y

    if seqs == 1:
        s_ref, = maybe_s_ref
        t = pl.program_id(1)

        @pl.when(t == 0)
        def _():
            xp_ref[0:pad, :] = jnp.zeros((pad, CONV_CH), F32)
            xp_ref[pad - (CONV_W - 1):pad, :] = hist_ref[...]
            s_ref[...] = s0_ref[...]

        xp_ref[pad:pad + tc, :] = x
        y = conv(xp_ref[...], slice(pad, pad + tc))
        xp_ref[pad - (CONV_W - 1):pad, :] = x[tc - (CONV_W - 1):tc, :]
    else:
        stride = pad + c
        xp_ref[...] = jnp.zeros(xp_ref.shape, F32)
        for b in range(seqs):
            xp_ref[b * stride + pad - (CONV_W - 1):b * stride + pad, :] = hist_ref[b]
            xp_ref[b * stride + pad:(b + 1) * stride, :] = x[b * c:(b + 1) * c, :]
        yp = conv(xp_ref[...], slice(None))
        y = jnp.concatenate([yp[b * stride + pad:(b + 1) * stride, :] for b in range(seqs)], axis=0)
    y = _silu(y)
    hm = hm_ref[...]
    yk = y[:, GDN_QK:2 * GDN_QK]
    yk = yk * lax.rsqrt(_dot((yk * yk).astype(BF16), hm) + EPS)

    ba = ba_ref[...]
    beta = jax.nn.sigmoid(ba[:, :LANES])
    g = -jnp.exp(alog_ref[...]) * jax.nn.softplus(ba[:, LANES:] + dtb_ref[...])
    ri = lax.broadcasted_iota(jnp.int32, (tile, tile), 0)
    ci = lax.broadcasted_iota(jnp.int32, (tile, tile), 1)
    incl = ((ri // c) == (ci // c)) & (ci <= ri)
    same_blk = (ri // INV_BLOCK) == (ci // INV_BLOCK)
    diag = ri == ci
    eye = jnp.where(diag, 1.0, 0.0)
    incl01 = jnp.where(incl, 1.0, 0.0).astype(BF16)
    tile_rows = [slice(ti * tile, (ti + 1) * tile) for ti in range(ntile)]
    gcums = [_exact3(incl01, g[rows]) for rows in tile_rows]
    gcum_ts = [gc.T for gc in gcums]
    on = on_ref[...]

    heads = range(GDN_HEADS)
    units = [(ti, h) for ti in range(ntile) for h in heads]

    def per_head(f, *lists):
        return [f(*args) for args in zip(*lists)]

    def head_cols(h, width):
        return slice(h * width, (h + 1) * width)

    ks = [yk[tile_rows[ti], head_cols(h, GDN_DK)] for ti, h in units]
    vs = [y[tile_rows[ti], 2 * GDN_QK + h * GDN_DV:2 * GDN_QK + (h + 1) * GDN_DV] for ti, h in units]
    gcs = [gcums[ti][:, h:h + 1] for ti, h in units]
    bcs = [beta[tile_rows[ti], h:h + 1] for ti, h in units]
    decs = [jnp.exp(jnp.where(incl, gcs[u] - gcum_ts[ti][h:h + 1, :], -jnp.inf))
            for u, (ti, h) in enumerate(units)]
    kbs = per_head(lambda k: k.astype(BF16), ks)
    a_s = per_head(lambda bc, kb, dec: jnp.where(diag, 0.0, bc * _dot_nt(kb, kb) * dec), bcs, kbs, decs)

    assert INV_BLOCK == 16 and c in (INV_BLOCK, 4 * INV_BLOCK)
    bf = lambda xs: per_head(lambda x: x.astype(BF16), xs)
    mm = lambda xs, ys: per_head(_dot, xs, ys)
    ds = per_head(lambda a: jnp.where(same_blk, a, 0.0), a_s) if c > INV_BLOCK else a_s
    sd = bf(ds)
    d2 = mm(sd, sd)
    s2 = bf(d2)
    d3 = mm(sd, s2)
    d4 = mm(s2, s2)
    s4 = bf(d4)
    n1 = per_head(lambda d, x2, x3: eye - d + x2 - x3, ds, d2, d3)
    n2 = per_head(jnp.add, n1, mm(bf(n1), s4))
    d8 = mm(s4, s4)
    tinv = per_head(jnp.add, n2, mm(bf(n2), bf(d8)))
    if c > INV_BLOCK:
        std = bf(tinv)
        ms = mm(std, bf(per_head(jnp.subtract, a_s, ds)))
        sm = bf(ms)
        m2 = mm(sm, sm)
        m3 = mm(sm, bf(m2))
        tinv = mm(bf(per_head(lambda m, x2, x3: eye - m + x2 - x3, ms, m2, m3)), std)

    egs = per_head(jnp.exp, gcs)
    rhs = per_head(lambda v, k, bc, eg: jnp.concatenate([v * bc, k * (bc * eg)], axis=1), vs, ks, bcs, egs)
    sols = mm(bf(tinv), bf(rhs))
    yq = y[:, :GDN_QK]
    yq = yq * (lax.rsqrt(_dot((yq * yq).astype(BF16), hm) + EPS) * (GDN_DK ** -0.5))
    qs = [yq[tile_rows[ti], head_cols(h, GDN_DK)] for ti, h in units]
    qks = per_head(lambda q, kb, dec: (_dot_nt(q.astype(BF16), kb) * dec).astype(BF16), qs, kbs, decs)
    u0s = [sol[:, :GDN_DV] for sol in sols]
    wbs = [sol[:, GDN_DV:].astype(BF16) for sol in sols]
    qds = per_head(lambda q, eg: (q * eg).astype(BF16), qs, egs)

    unit = lambda ti, h: ti * GDN_HEADS + h

    def chunk_terms(ti, ic):
        rs = slice(ic * c, (ic + 1) * c)
        gends = [gcs[unit(ti, h)][(ic + 1) * c - 1:(ic + 1) * c, :] for h in heads]
        kds = [(ks[unit(ti, h)][rs] * jnp.exp(gends[h] - gcs[unit(ti, h)][rs])).astype(BF16) for h in heads]
        return rs, gends, kds

    def state_terms(rs, states):
        return [_dot(jnp.concatenate([wbs[h][rs], qds[h][rs]], axis=0), states[h].astype(BF16)) for h in heads]

    def next_state(states, gends, kds, us_c):
        return [states[h] * jnp.exp(gends[h]) + lax.dot_general(
            kds[h], us_c[h], (((0,), (0,)), ((), ())), preferred_element_type=F32) for h in heads]

    us = [[] for _ in units]
    outs = [[] for _ in units]
    if seqs == 1:
        trans = []
        for ti in range(ntile):
            for ic in range(nchunk):
                rs, gends, kds = chunk_terms(ti, ic)
                kt_wu = [lax.dot_general(
                    kds[h], jnp.concatenate([wbs[unit(ti, h)][rs], u0s[unit(ti, h)][rs].astype(BF16)], axis=1),
                    (((0,), (0,)), ((), ())), preferred_element_type=F32) for h in heads]
                trans.append(([jnp.exp(g) for g in gends], [m[:, :GDN_DK].astype(BF16) for m in kt_wu],
                              [m[:, GDN_DK:] for m in kt_wu]))
        states = [s_ref[h] for h in heads]
        chunk_states = []
        for decay, kt_w, kt_u0 in trans:
            sb = [s.astype(BF16) for s in states]
            chunk_states.append(sb)
            states = [states[h] * decay[h] - _dot(kt_w[h], sb[h]) + kt_u0[h] for h in heads]
        for h in heads:
            s_ref[h] = states[h]
        for ti in range(ntile):
            for ic in range(nchunk):
                rs = slice(ic * c, (ic + 1) * c)
                sb = chunk_states[ti * nchunk + ic]
                wss = [_dot(jnp.concatenate([wbs[unit(ti, h)][rs], qds[unit(ti, h)][rs]], axis=0), sb[h])
                       for h in heads]
                for h in heads:
                    us[unit(ti, h)].append((u0s[unit(ti, h)][rs] - wss[h][:c]).astype(BF16))
                    outs[unit(ti, h)].append(wss[h][c:])
        o = [jnp.concatenate(outs[u], axis=0) + _dot(qks[u], jnp.concatenate(us[u], axis=0))
             for u in range(len(units))]

        @pl.when(t == pl.num_programs(1) - 1)
        def _():
            s_out_ref[...] = s_ref[...]
    else:
        for ic in range(nchunk):
            rs, gends, kds = chunk_terms(0, ic)
            states = [s0_ref[ic, h] for h in heads]
            wss = state_terms(rs, states)
            us_c = [(u0s[h][rs] - wss[h][:c]).astype(BF16) for h in heads]
            new = next_state(states, gends, kds, us_c)
            for h in heads:
                us[h].append(us_c[h])
                outs[h].append(wss[h][c:])
                s_out_ref[ic, h] = new[h]
        o = [jnp.concatenate(outs[h], axis=0) + _dot(qks[h], jnp.concatenate(us[h], axis=0)) for h in heads]
    for u, (ti, h) in enumerate(units):
        rows, cols = tile_rows[ti], head_cols(h, GDN_DV)
        o_ref[rows, cols] = (_rms(o[u], on) * _silu(z_ref[rows, cols])).astype(o_ref.dtype)


def _gdn_call(conv, z, ba, hist, s0, p, tile, c):
    b, l, _ = conv.shape
    small = [p['conv_w'], p['a_log'], p['dt_bias'], p['gdn_on'], p['head_mat']]
    state = (GDN_HEADS, GDN_DK, GDN_DV)
    if l >= tile:
        tc = _pick_tile(l, GDN_BLOCK_TILES * tile)
        seqs = 1
        grid = (b, l // tc)
        row = lambda w: pl.BlockSpec((None, tc, w), lambda i, j: (i, j, 0))
        per_b = lambda shape: pl.BlockSpec((None,) + shape, lambda i, j: (i,) + (0,) * len(shape))
        scratch = [pltpu.VMEM((tc + CONV_PAD, CONV_CH), F32), pltpu.VMEM(state, F32)]
    else:
        tc = tile = b * l
        assert l == c and tc % LANES == 0
        seqs = b
        grid = (1, 1)
        conv, z, ba = (t.reshape(1, tc, t.shape[-1]) for t in (conv, z, ba))
        row = lambda w: pl.BlockSpec((None, tc, w), lambda i, j: (0, 0, 0))
        per_b = lambda shape: pl.BlockSpec((b,) + shape, lambda i, j: (0,) * (len(shape) + 1))
        scratch = [pltpu.VMEM((seqs * (c + CONV_PAD), CONV_CH), F32)]
    o, s_new = pl.pallas_call(
        functools.partial(_gdn_kernel, tc=tc, tile=tile, c=c, seqs=seqs),
        grid=grid,
        in_specs=[row(CONV_CH), row(GDN_V), row(2 * LANES), per_b((CONV_W - 1, CONV_CH)), per_b(state)] + [
            pl.BlockSpec(w.shape, lambda i, j: (0, 0)) for w in small],
        out_specs=[row(GDN_V), per_b(state)],
        out_shape=[jax.ShapeDtypeStruct(conv.shape[:2] + (GDN_V,), BF16),
                   jax.ShapeDtypeStruct((b,) + state, F32)],
        scratch_shapes=scratch,
        compiler_params=pltpu.CompilerParams(dimension_semantics=("arbitrary", "arbitrary"),
                                             vmem_limit_bytes=VMEM_LIMIT),
        name="gdn",
    )(conv, z, ba, hist, s0, *small)
    return o.reshape(b, l, GDN_V), s_new


def _stack_maps(q):
    lane = lax.broadcasted_iota(jnp.int32, q.shape, 1)
    zero = jnp.zeros_like(q)
    return jnp.concatenate([jnp.where(lane < DIFF_DK, q, zero), jnp.where(lane >= DIFF_DK, q, zero)],
                           axis=0)


def _diff_finish(acc, l, t, lq_ref, lk_ref, dn_ref, lam_init):
    lam_e = jnp.exp(jnp.sum(lq_ref[...] * lk_ref[...], axis=-1, keepdims=True))
    lam = lam_e[0:1] - lam_e[1:2] + lam_init
    o = acc[:t] / l[:t] - lam * (acc[t:] / l[t:])
    return _rms(o, dn_ref[...]) * (1.0 - lam_init)


def _halves_max(s):
    return jnp.maximum(s[:, :LANES], s[:, LANES:])


def _attn_prompt_kernel(q_ref, kt_ref, v_ref, bias_ref, lq_ref, lk_ref, dn_ref, o_ref,
                        kt_scr, vx_scr, s_scr, m_scr, acc_scr, *, tq, tk, lam_init):
    i = pl.program_id(2)
    assert tk == 2 * LANES and tq % tk == 0
    nsub = tq // tk

    @pl.when(i == 0)
    def _():
        for j in range(kt_scr.shape[0]):
            kt_scr[j] = kt_ref[:, j * tk:(j + 1) * tk].astype(BF16)
        vx_scr[:, :DIFF_DV] = v_ref[...]
        vx_scr[:, DIFF_DV:] = jnp.ones((vx_scr.shape[0], LANES), BF16)

    qq = _stack_maps(q_ref[...])

    def scores(j):
        return _dot(qq, kt_scr[j])

    def weighted_values(ss, blocks, mm):
        p = [jnp.exp2(s[:, half * LANES:(half + 1) * LANES] - mm) for s in ss for half in range(tk // LANES)]
        vx = [vx_scr[pl.ds(pl.multiple_of(j * tk, tk), tk), :] for j in blocks]
        return _dot(jnp.concatenate(p, axis=1).astype(BF16), jnp.concatenate(vx, axis=0))

    def pass1(blocks, biases):
        ss = [scores(j) if b is None else scores(j) + b for j, b in zip(blocks, biases)]
        for j, s in zip(blocks, ss):
            s_scr[j] = s
        m_scr[...] = jnp.maximum(m_scr[...], functools.reduce(jnp.maximum, [_halves_max(s) for s in ss]))

    def pass2(blocks, biases):
        del biases
        acc_scr[...] += weighted_values([s_scr[j] for j in blocks], blocks, m_scr[...])

    def over_visible(body):
        def group(g):
            return [g * nsub + u for u in range(nsub)]

        def trip(t, carry):
            second = group(2 * t + 1)
            body(group(2 * t) + second,
                 [None] * nsub + [bias_ref[jnp.maximum(j - i * nsub + 1, 0)] for j in second])
            return carry

        lax.fori_loop(0, (i + 1) // 2, trip, 0)

        @pl.when(i % 2 == 0)
        def _():
            body(group(i), [bias_ref[1 + d] for d in range(nsub)])

    m_scr[...] = jnp.full(m_scr.shape, -jnp.inf, F32)
    over_visible(pass1)
    m_scr[...] = jnp.broadcast_to(jnp.max(m_scr[...], axis=-1, keepdims=True), m_scr.shape)
    acc_scr[...] = jnp.zeros(acc_scr.shape, F32)
    over_visible(pass2)
    acc = acc_scr[...]
    o_ref[...] = _diff_finish(acc[:, :DIFF_DV], acc[:, DIFF_DV:], tq, lq_ref, lk_ref, dn_ref,
                              lam_init).astype(o_ref.dtype)


def _attn_prompt_call(qb, kt, vb, p, tq, tk, lam_init):
    b, l, _ = qb.shape
    small = [p['lambda_q'], p['lambda_k'], p['diff_on']]
    r = (jnp.arange(2 * tq) % tq)[None, :, None] // CHUNK
    cidx = (jnp.arange(tk)[None, None, :] + tk * jnp.arange(tq // tk)[:, None, None]) // CHUNK
    bias = jnp.where(cidx <= r, 0.0, -jnp.inf).astype(F32)
    bias = jnp.concatenate([jnp.zeros_like(bias[:1]), bias], axis=0)
    return pl.pallas_call(
        functools.partial(_attn_prompt_kernel, tq=tq, tk=tk, lam_init=lam_init),
        grid=(b, DIFF_HEADS, l // tq),
        in_specs=[pl.BlockSpec((None, tq, LANES), lambda bi, h, i: (bi, i, h)),
                  pl.BlockSpec((None, None, LANES, l), lambda bi, h, i: (bi, h, 0, 0)),
                  pl.BlockSpec((None, l, LANES), lambda bi, h, i: (bi, 0, h)),
                  _const_spec(bias.shape)] + [
                      pl.BlockSpec(w.shape, lambda bi, h, i: (0, 0)) for w in small],
        out_specs=pl.BlockSpec((None, tq, LANES), lambda bi, h, i: (bi, i, h)),
        out_shape=jax.ShapeDtypeStruct((b, l, DIFF_V), BF16),
        scratch_shapes=[pltpu.VMEM((l // tk, LANES, tk), BF16),
                        pltpu.VMEM((l, DIFF_DV + LANES), BF16),
                        pltpu.VMEM((l // tk, 2 * tq, tk), F32),
                        pltpu.VMEM((2 * tq, LANES), F32),
                        pltpu.VMEM((2 * tq, DIFF_DV + LANES), F32)],
        compiler_params=pltpu.CompilerParams(
            dimension_semantics=("arbitrary", "arbitrary", "arbitrary"),
            vmem_limit_bytes=VMEM_LIMIT),
        name="attn_prompt",
    )(qb, kt, vb, bias, *small)


def _attn_sample_kernel(q_ref, kn_ref, vn_ref, ckt_ref, cv_ref, lq_ref, lk_ref, dn_ref, o_ref,
                        *, t, lam_init):
    past = ckt_ref.shape[-1]
    for h in range(DIFF_HEADS):
        cols = slice(h * LANES, (h + 1) * LANES)
        qq = _stack_maps(q_ref[:, cols])
        s_c = _dot(qq, ckt_ref[h].astype(BF16))
        s_n = _dot_nt(qq, kn_ref[:, cols])
        m = jnp.maximum(jnp.max(s_c, axis=-1, keepdims=True), jnp.max(s_n, axis=-1, keepdims=True))
        p_c = jnp.exp2(s_c - m)
        p_n = jnp.exp2(s_n - m)
        l = jnp.sum(p_c, axis=-1, keepdims=True) + jnp.sum(p_n, axis=-1, keepdims=True)
        cv = cv_ref[pl.ds(h, past, stride=DIFF_HEADS), :].astype(BF16)
        acc = _dot(p_c.astype(BF16), cv) + _dot(p_n.astype(BF16), vn_ref[:, cols])
        o_ref[:, cols] = _diff_finish(acc, l, t, lq_ref, lk_ref, dn_ref, lam_init).astype(o_ref.dtype)


def _attn_sample_call(qb, kb, vb, ckt, cv, p, lam_init):
    b, t, _ = qb.shape
    past = ckt.shape[-1]
    small = [p['lambda_q'], p['lambda_k'], p['diff_on']]
    new = pl.BlockSpec((None, t, DIFF_V), lambda bi: (bi, 0, 0))
    return pl.pallas_call(
        functools.partial(_attn_sample_kernel, t=t, lam_init=lam_init),
        grid=(b,),
        in_specs=[new, new, new,
                  pl.BlockSpec((None, DIFF_HEADS, LANES, past), lambda bi: (bi, 0, 0, 0)),
                  pl.BlockSpec((None, past * DIFF_HEADS, DIFF_DV), lambda bi: (bi, 0, 0))] + [
                      pl.BlockSpec(w.shape, lambda bi: (0, 0)) for w in small],
        out_specs=new,
        out_shape=jax.ShapeDtypeStruct((b, t, DIFF_V), BF16),
        compiler_params=pltpu.CompilerParams(dimension_semantics=("arbitrary",),
                                             vmem_limit_bytes=VMEM_LIMIT),
        name="attn_sample",
    )(qb, kb, vb, ckt, cv, *small)


def _rope_tables(pos):
    half = ROT_DIM // 2
    inv = jnp.float32(ROPE_THETA) ** (-jnp.arange(half, dtype=F32) * 2.0 / ROT_DIM)
    ang = pos.astype(F32)[:, None] * inv[None, :]
    cos, sin = jnp.cos(ang), jnp.sin(ang)
    n = pos.shape[0]
    rest = DIFF_DK - ROT_DIM
    one = jnp.ones((n, rest), F32)
    zero = jnp.zeros((n, rest), F32)
    zh = jnp.zeros((n, half), F32)
    c64 = jnp.concatenate([cos, cos, one], axis=1)
    a64 = jnp.concatenate([-sin, zh, zero], axis=1)
    b64 = jnp.concatenate([zh, sin, zero], axis=1)
    rep = LANES // DIFF_DK
    return tuple(jnp.tile(t, (1, rep)) for t in (c64, a64, b64))


def _layer_params(w, l):
    d = w['w_in'].shape[1]
    d_ff = w['ffn1_w_down'].shape[1]
    win = w['w_in'][l]
    o = 0
    w_conv = win[:, o:o + CONV_CH]; o += CONV_CH
    w_b = win[:, o:o + GDN_HEADS]; o += GDN_HEADS
    w_a = win[:, o:o + GDN_HEADS]; o += GDN_HEADS
    w_z = win[:, o:o + GDN_V]; o += GDN_V
    w_q = win[:, o:o + DIFF_QK]; o += DIFF_QK
    w_k = win[:, o:o + DIFF_QK]; o += DIFF_QK
    w_v = win[:, o:o + DIFF_V]
    zpad = jnp.zeros((d, LANES - GDN_HEADS), win.dtype)
    w_ba = jnp.concatenate([w_b, zpad, w_a, zpad], axis=1)
    hpad = lambda v: jnp.concatenate([v.astype(F32), jnp.zeros((LANES - GDN_HEADS,), F32)])[None, :]
    grp = jnp.arange(DIFF_QK) // DIFF_DK
    bf = lambda t: t.astype(BF16)
    r2 = lambda v: v.astype(F32)[None, :]
    return dict(
        n1=r2(w['ffn1_norm'][l]), wg1=bf(w['ffn1_w_gu'][l][:, :d_ff]), wu1=bf(w['ffn1_w_gu'][l][:, d_ff:]),
        wd1=bf(w['ffn1_w_down'][l]),
        nm=r2(w['mix_norm'][l]), w_conv=bf(w_conv), w_z=bf(w_z), w_ba=bf(w_ba), w_q=bf(w_q), w_k=bf(w_k),
        w_v=bf(w_v),
        qn=r2(jnp.tile(w['q_norm'][l], DIFF_QK // DIFF_DK)), kn=r2(jnp.tile(w['k_norm'][l], DIFF_QK // DIFF_DK)),
        gmat=(grp[:, None] == grp[None, :]).astype(BF16),
        head_mat=(grp[:, None] // 2 == grp[None, :] // 2).astype(BF16),
        conv_w=w['conv_w'][l].astype(F32), a_log=hpad(w['a_log'][l]), dt_bias=hpad(w['dt_bias'][l]),
        gdn_on=r2(w['gdn_out_norm'][l]),
        lambda_q=w['lambda_q'][l].astype(F32), lambda_k=w['lambda_k'][l].astype(F32),
        diff_on=r2(w['diff_out_norm'][l]),
        wo=bf(w['w_out'][l]), n2=r2(w['ffn2_norm'][l]), wg2=bf(w['ffn2_w_gu'][l][:, :d_ff]),
        wu2=bf(w['ffn2_w_gu'][l][:, d_ff:]), wd2=bf(w['ffn2_w_down'][l]),
    )


def _pick_tile(n, pref):
    t = min(n, pref)
    assert n % t == 0
    return t


def _layer(x, pos, k_hist, v_hist, conv_hist, s0, p, lam_init):
    b, l, d = x.shape
    n = b * l
    tm = _pick_tile(n, 256)
    tabs = _rope_tables(pos)
    if l >= tm:
        assert l % tm == 0
        n_pos_tiles = l // tm
    else:
        assert tm % l == 0
        tabs = tuple(jnp.tile(t, (tm // l, 1)) for t in tabs)
        n_pos_tiles = 1
    prompt = k_hist is None
    pre = _pre_call(x.reshape(n, d), tabs, p, tm, n_pos_tiles, b, prompt)
    x1, conv, z, ba, qb, vb, kf, vf = pre[:8]
    c = min(CHUNK, l)
    assert l % c == 0
    tc = GDN_TILE
    assert l % tc == 0 or l == c
    if conv_hist is None:
        conv_hist = jnp.zeros((b, CONV_W - 1, CONV_CH), F32)
        s0 = jnp.zeros((b, GDN_HEADS, GDN_DK, GDN_DV), F32)
    conv3 = conv.reshape(b, l, CONV_CH)
    og, s_new = _gdn_call(conv3, z.reshape(b, l, GDN_V), ba.reshape(b, l, 2 * LANES),
                          conv_hist.astype(F32), s0.astype(F32), p, tc, c)
    conv_new = jnp.concatenate([conv_hist.astype(F32), conv3], axis=1)[:, -(CONV_W - 1):]
    q3 = qb.reshape(b, l, DIFF_QK)
    v3 = vb.reshape(b, l, DIFF_V)
    if prompt:
        tk = 2 * LANES
        od = _attn_prompt_call(q3, kf, v3, p, _pick_tile(l, 2 * tk), tk, lam_init)
        k_out = jnp.transpose(kf.reshape(b, DIFF_HEADS, 2, DIFF_DK, l), (0, 4, 1, 2, 3))
    else:
        past = k_hist.shape[1]
        ckt = jnp.transpose(k_hist, (0, 2, 3, 4, 1)).reshape(b, DIFF_HEADS, LANES, past)
        od = _attn_sample_call(q3, pre[8].reshape(b, l, DIFF_QK), v3, ckt,
                               v_hist.reshape(b, past * DIFF_HEADS, DIFF_DV), p, lam_init)
        k_out = kf.reshape(b, l, DIFF_HEADS, 2, DIFF_DK)
    y = _post_call(x1, og.reshape(n, GDN_V), od.reshape(n, DIFF_V), p, tm)
    return (y.reshape(b, l, d), k_out, vf.reshape(b, l, DIFF_HEADS, DIFF_DV), s_new, conv_new)


def kernel(x_prompt, x_sample, cache_k, cache_v, state_gdn, state_conv, ffn1_norm, ffn1_w_gu, ffn1_w_down,
           mix_norm, w_in, conv_w, a_log, dt_bias, gdn_out_norm, q_norm, k_norm, lambda_q, lambda_k,
           diff_out_norm, w_out, ffn2_norm, ffn2_w_gu, ffn2_w_down):
    w = dict(ffn1_norm=ffn1_norm, ffn1_w_gu=ffn1_w_gu, ffn1_w_down=ffn1_w_down, mix_norm=mix_norm, w_in=w_in,
             conv_w=conv_w, a_log=a_log, dt_bias=dt_bias, gdn_out_norm=gdn_out_norm, q_norm=q_norm,
             k_norm=k_norm, lambda_q=lambda_q, lambda_k=lambda_k, diff_out_norm=diff_out_norm, w_out=w_out,
             ffn2_norm=ffn2_norm, ffn2_w_gu=ffn2_w_gu, ffn2_w_down=ffn2_w_down)
    depth = w_in.shape[0]
    pos_p = jnp.arange(x_prompt.shape[1])
    pos_s = cache_k.shape[2] + jnp.arange(x_sample.shape[1])
    hp, hs = x_prompt, x_sample
    outs = [[] for _ in range(8)]
    for l in range(depth):
        lam_init = 0.8 - 0.6 * math.exp(-0.3 * l)
        p = _layer_params(w, l)
        hp, kp, vp, sp, cp = _layer(hp, pos_p, None, None, None, None, p, lam_init)
        hs, ks, vs, ss, cs = _layer(hs, pos_s, cache_k[l], cache_v[l], state_conv[l], state_gdn[l], p, lam_init)
        for acc, val in zip(outs, (kp, vp, sp, cp, ks, vs, ss, cs)):
            acc.append(val)
    return (hp, hs) + tuple(jnp.stack(o) for o in outs)
```

```python
import functools
import math

import jax
import jax.numpy as jnp
from jax import lax
from jax.experimental import pallas as pl
from jax.experimental.pallas import tpu as pltpu

F32 = jnp.float32
BF16 = jnp.bfloat16

EPS = 1e-6
CHUNK = 64
GDN_HEADS = 4
GDN_DK = 128
GDN_DV = 128
CONV_W = 4
DIFF_HEADS = 4
DIFF_DK = 64
DIFF_DV = 128
ROT_DIM = DIFF_DK // 4
ROPE_THETA = 500000.0
GDN_QK = GDN_HEADS * GDN_DK
GDN_V = GDN_HEADS * GDN_DV
CONV_CH = 2 * GDN_QK + GDN_V
DIFF_QK = DIFF_HEADS * 2 * DIFF_DK
DIFF_V = DIFF_HEADS * DIFF_DV

LANES = 128
INV_BLOCK = 16
GDN_TILE = 128
GDN_BLOCK_TILES = 4
VMEM_LIMIT = 56 * 1024 * 1024
Q_SCALE = (DIFF_DK ** -0.5) * math.log2(math.e)


def _dot(a, b):
    return jnp.dot(a, b, preferred_element_type=F32)


def _dot_nt(a, b):
    return lax.dot_general(a, b, (((1,), (1,)), ((), ())), preferred_element_type=F32)


def _rms(x, g):
    return x * lax.rsqrt(jnp.mean(x * x, axis=-1, keepdims=True) + EPS) * g


def _silu(x):
    return x * jax.nn.sigmoid(x)


def _rms_parts(x, g):
    return (x * g).astype(BF16), lax.rsqrt(jnp.mean(x * x, axis=-1, keepdims=True) + EPS)


def _swiglu(xg, r, wgu_ref, wd_ref):
    d_ff = wd_ref.shape[0]
    g = _dot(xg, wgu_ref[:, :d_ff]) * r
    u = _dot(xg, wgu_ref[:, d_ff:]) * r
    act = (_silu(g) * u).astype(BF16)
    return _dot(act, wd_ref[...])


def _const_spec(shape):
    nd = len(shape)
    return pl.BlockSpec(shape, lambda *_: (0,) * nd, pipeline_mode=pl.Buffered(1))


_WIN = dict(zip(('conv', 'z', 'q', 'k', 'v', 'ba'),
                (slice(a, b) for a, b in zip((0, 1536, 2048, 2560, 3072, 3584), (1536, 2048, 2560, 3072, 3584, 3840)))))
assert _WIN['conv'].stop == CONV_CH and _WIN['ba'].stop - _WIN['ba'].start == 2 * LANES


def _pre_kernel(x_ref, cos_ref, sa_ref, sb_ref, n1_ref, wgu_ref, wd_ref, nm_ref, win_ref, qn_ref, kn_ref, gm_ref,
                x1_ref, conv_ref, z_ref, ba_ref, qb_ref, vb_ref, kf_ref, vf_ref, *maybe_kb_ref,
                k_pos_minor):
    x = x_ref[...]
    xg, r = _rms_parts(x, n1_ref[...])
    x1 = x + 0.5 * _swiglu(xg, r, wgu_ref, wd_ref)
    x1_ref[...] = x1
    h, rh = _rms_parts(x1, nm_ref[...])
    tm = x.shape[0]
    cos = cos_ref[...]
    sa = sa_ref[...]
    sb = sb_ref[...]
    gm = gm_ref[...]

    def norm_rope(t, gw):
        ss = _dot((t * t).astype(BF16), gm)
        t = t * lax.rsqrt(ss * (1.0 / DIFF_DK) + EPS) * gw
        outs = []
        for hh in range(DIFF_HEADS):
            th = t[:, hh * LANES:(hh + 1) * LANES]
            up = pltpu.roll(th, LANES - ROT_DIM // 2, 1)
            dn = pltpu.roll(th, ROT_DIM // 2, 1)
            outs.append(th * cos + up * sa + dn * sb)
        return jnp.concatenate(outs, axis=1)

    k = norm_rope(_dot(h, win_ref[:, _WIN['k']]) * rh, kn_ref[...])
    if k_pos_minor:
        kf_ref[...] = k.T.reshape(DIFF_HEADS, LANES, tm)
    else:
        kf_ref[...] = k
        maybe_kb_ref[0][...] = k.astype(BF16)
    q = norm_rope(_dot(h, win_ref[:, _WIN['q']]) * rh, qn_ref[...])
    qb_ref[...] = (q * Q_SCALE).astype(BF16)
    v = _dot(h, win_ref[:, _WIN['v']]) * rh
    vb_ref[...] = v.astype(BF16)
    if k_pos_minor:
        for hh in range(DIFF_HEADS):
            vf_ref[pl.ds(hh, tm, stride=DIFF_HEADS), :] = v[:, hh * DIFF_DV:(hh + 1) * DIFF_DV]
    else:
        vf_ref[...] = v
    z_ref[...] = _dot(h, win_ref[:, _WIN['z']]) * rh
    ba_ref[...] = _dot(h, win_ref[:, _WIN['ba']]) * rh
    conv_ref[...] = _dot(h, win_ref[:, _WIN['conv']]) * rh


def _pre_call(x2d, tabs, p, tm, n_pos_tiles, batch, k_pos_minor):
    n, d = x2d.shape
    grid = (n // tm,)
    row = lambda w: pl.BlockSpec((tm, w), lambda i: (i, 0))
    tab = pl.BlockSpec((tm, LANES), lambda i: (i % n_pos_tiles, 0))
    weights = [p['n1'], p['wgu1'], p['wd1'], p['nm'], p['w_in'], p['qn'], p['kn'], p['gmat']]
    in_specs = [row(d), tab, tab, tab] + [_const_spec(w.shape) for w in weights]
    outs = [(d, F32), (CONV_CH, F32), (GDN_V, F32), (2 * LANES, F32), (DIFF_QK, BF16), (DIFF_V, BF16)]
    out_specs = [row(w) for w, _ in outs]
    out_shape = [jax.ShapeDtypeStruct((n, w), dt) for w, dt in outs]
    if k_pos_minor:
        l = n // batch
        tiles = l // tm
        out_specs += [pl.BlockSpec((None, DIFF_HEADS, LANES, tm), lambda i: (i // tiles, 0, 0, i % tiles)),
                      pl.BlockSpec((tm * DIFF_HEADS, DIFF_DV), lambda i: (i, 0))]
        out_shape += [jax.ShapeDtypeStruct((batch, DIFF_HEADS, LANES, l), F32),
                      jax.ShapeDtypeStruct((n * DIFF_HEADS, DIFF_DV), F32)]
    else:
        out_specs += [row(DIFF_QK), row(DIFF_V), row(DIFF_QK)]
        out_shape += [jax.ShapeDtypeStruct((n, DIFF_QK), F32), jax.ShapeDtypeStruct((n, DIFF_V), F32),
                      jax.ShapeDtypeStruct((n, DIFF_QK), BF16)]
    return pl.pallas_call(
        functools.partial(_pre_kernel, k_pos_minor=k_pos_minor),
        grid=grid,
        in_specs=in_specs,
        out_specs=out_specs,
        out_shape=out_shape,
        compiler_params=pltpu.CompilerParams(dimension_semantics=("arbitrary",),
                                             vmem_limit_bytes=VMEM_LIMIT),
        name="pre",
    )(x2d, *tabs, *weights)


def _post_kernel(x1_ref, og_ref, od_ref, wo_ref, n2_ref, wgu_ref, wd_ref, y_ref):
    mixed = jnp.concatenate([og_ref[...], od_ref[...]], axis=1)
    x2 = x1_ref[...] + _dot(mixed, wo_ref[...])
    xg, r = _rms_parts(x2, n2_ref[...])
    y_ref[...] = x2 + 0.5 * _swiglu(xg, r, wgu_ref, wd_ref)


def _post_call(x1, og, od, p, tm):
    n, d = x1.shape
    row = lambda w: pl.BlockSpec((tm, w), lambda i: (i, 0))
    weights = [p['wo'], p['n2'], p['wgu2'], p['wd2']]
    return pl.pallas_call(
        _post_kernel,
        grid=(n // tm,),
        in_specs=[row(d), row(GDN_V), row(DIFF_V)] + [_const_spec(w.shape) for w in weights],
        out_specs=row(d),
        out_shape=jax.ShapeDtypeStruct((n, d), F32),
        compiler_params=pltpu.CompilerParams(dimension_semantics=("arbitrary",),
                                             vmem_limit_bytes=VMEM_LIMIT),
        name="post",
    )(x1, og, od, *weights)


def _exact3(m01, x):
    x1 = x.astype(BF16)
    r1 = x - x1.astype(F32)
    x2 = r1.astype(BF16)
    x3 = (r1 - x2.astype(F32)).astype(BF16)
    return _dot(m01, x1) + (_dot(m01, x2) + _dot(m01, x3))


CONV_PAD = 8


def _gdn_kernel(x_ref, z_ref, ba_ref, hist_ref, s0_ref, cw_ref, alog_ref, dtb_ref, on_ref, hm_ref,
                o_ref, s_out_ref, xp_ref, *maybe_s_ref, tc, tile, c, seqs):
    pad = CONV_PAD
    nchunk = tile // c
    ntile = tc // tile
    assert c % INV_BLOCK == 0 and tile % LANES == 0 and tc % tile == 0
    assert seqs == 1 or (seqs == nchunk and ntile == 1)
    cw = cw_ref[...]
    x = x_ref[...]

    def conv(xp, rows):
        y = xp[rows] * cw[CONV_W - 1:CONV_W, :]
        for k in range(1, CONV_W):
            y = y + pltpu.roll(xp, k, 0)[rows] * cw[CONV_W - 1 - k:CONV_W - k, :]
        return y

    if seqs == 1:
        s_ref, = maybe_s_ref
        t = pl.program_id(1)

        @pl.when(t == 0)
        def _():
            xp_ref[0:pad, :] = jnp.zeros((pad, CONV_CH), F32)
            xp_ref[pad - (CONV_W - 1):pad, :] = hist_ref[...]
            s_ref[...] = s0_ref[...]

        xp_ref[pad:pad + tc, :] = x
        y = conv(xp_ref[...], slice(pad, pad + tc))
        xp_ref[pad - (CONV_W - 1):pad, :] = x[tc - (CONV_W - 1):tc, :]
    else:
        stride = pad + c
        xp_ref[...] = jnp.zeros(xp_ref.shape, F32)
        for b in range(seqs):
            xp_ref[b * stride + pad - (CONV_W - 1):b * stride + pad, :] = hist_ref[b]
            xp_ref[b * stride + pad:(b + 1) * stride, :] = x[b * c:(b + 1) * c, :]
        yp = conv(xp_ref[...], slice(None))
        y = jnp.concatenate([yp[b * stride + pad:(b + 1) * stride, :] for b in range(seqs)], axis=0)
    y = _silu(y)
    hm = hm_ref[...]
    yk = y[:, GDN_QK:2 * GDN_QK]
    yk = yk * lax.rsqrt(_dot((yk * yk).astype(BF16), hm) + EPS)

    ba = ba_ref[...]
    beta = jax.nn.sigmoid(ba[:, :LANES])
    g = -jnp.exp(alog_ref[...]) * jax.nn.softplus(ba[:, LANES:] + dtb_ref[...])
    ri = lax.broadcasted_iota(jnp.int32, (tile, tile), 0)
    ci = lax.broadcasted_iota(jnp.int32, (tile, tile), 1)
    incl = ((ri // c) == (ci // c)) & (ci <= ri)
    same_blk = (ri // INV_BLOCK) == (ci // INV_BLOCK)
    diag = ri == ci
    eye = jnp.where(diag, 1.0, 0.0)
    incl01 = jnp.where(incl, 1.0, 0.0).astype(BF16)
    tile_rows = [slice(ti * tile, (ti + 1) * tile) for ti in range(ntile)]
    gcums = [_exact3(incl01, g[rows]) for rows in tile_rows]
    gcum_ts = [gc.T for gc in gcums]
    on = on_ref[...]

    heads = range(GDN_HEADS)
    units = [(ti, h) for ti in range(ntile) for h in heads]

    def per_head(f, *lists):
        return [f(*args) for args in zip(*lists)]

    def head_cols(h, width):
        return slice(h * width, (h + 1) * width)

    ks = [yk[tile_rows[ti], head_cols(h, GDN_DK)] for ti, h in units]
    vs = [y[tile_rows[ti], 2 * GDN_QK + h * GDN_DV:2 * GDN_QK + (h + 1) * GDN_DV] for ti, h in units]
    gcs = [gcums[ti][:, h:h + 1] for ti, h in units]
    bcs = [beta[tile_rows[ti], h:h + 1] for ti, h in units]
    decs = [jnp.exp(jnp.where(incl, gcs[u] - gcum_ts[ti][h:h + 1, :], -jnp.inf))
            for u, (ti, h) in enumerate(units)]
    kbs = per_head(lambda k: k.astype(BF16), ks)
    a_s = per_head(lambda bc, kb, dec: jnp.where(diag, 0.0, bc * _dot_nt(kb, kb) * dec), bcs, kbs, decs)

    assert INV_BLOCK == 16 and c in (INV_BLOCK, 4 * INV_BLOCK)
    bf = lambda xs: per_head(lambda x: x.astype(BF16), xs)
    mm = lambda xs, ys: per_head(_dot, xs, ys)
    ds = per_head(lambda a: jnp.where(same_blk, a, 0.0), a_s) if c > INV_BLOCK else a_s
    sd = bf(ds)
    d2 = mm(sd, sd)
    s2 = bf(d2)
    d3 = mm(sd, s2)
    d4 = mm(s2, s2)
    s4 = bf(d4)
    n1 = per_head(lambda d, x2, x3: eye - d + x2 - x3, ds, d2, d3)
    n2 = per_head(jnp.add, n1, mm(bf(n1), s4))
    d8 = mm(s4, s4)
    tinv = per_head(jnp.add, n2, mm(bf(n2), bf(d8)))
    if c > INV_BLOCK:
        std = bf(tinv)
        ms = mm(std, bf(per_head(jnp.subtract, a_s, ds)))
        sm = bf(ms)
        m2 = mm(sm, sm)
        m3 = mm(sm, bf(m2))
        tinv = mm(bf(per_head(lambda m, x2, x3: eye - m + x2 - x3, ms, m2, m3)), std)

    egs = per_head(jnp.exp, gcs)
    rhs = per_head(lambda v, k, bc, eg: jnp.concatenate([v * bc, k * (bc * eg)], axis=1), vs, ks, bcs, egs)
    sols = mm(bf(tinv), bf(rhs))
    yq = y[:, :GDN_QK]
    yq = yq * (lax.rsqrt(_dot((yq * yq).astype(BF16), hm) + EPS) * (GDN_DK ** -0.5))
    qs = [yq[tile_rows[ti], head_cols(h, GDN_DK)] for ti, h in units]
    qks = per_head(lambda q, kb, dec: (_dot_nt(q.astype(BF16), kb) * dec).astype(BF16), qs, kbs, decs)
    u0s = [sol[:, :GDN_DV] for sol in sols]
    wbs = [sol[:, GDN_DV:].astype(BF16) for sol in sols]
    qds = per_head(lambda q, eg: (q * eg).astype(BF16), qs, egs)

    unit = lambda ti, h: ti * GDN_HEADS + h

    def chunk_terms(ti, ic):
        rs = slice(ic * c, (ic + 1) * c)
        gends = [gcs[unit(ti, h)][(ic + 1) * c - 1:(ic + 1) * c, :] for h in heads]
        kds = [(ks[unit(ti, h)][rs] * jnp.exp(gends[h] - gcs[unit(ti, h)][rs])).astype(BF16) for h in heads]
        return rs, gends, kds

    def state_terms(rs, states):
        return [_dot(jnp.concatenate([wbs[h][rs], qds[h][rs]], axis=0), states[h].astype(BF16)) for h in heads]

    def next_state(states, gends, kds, us_c):
        return [states[h] * jnp.exp(gends[h]) + lax.dot_general(
            kds[h], us_c[h], (((0,), (0,)), ((), ())), preferred_element_type=F32) for h in heads]

    us = [[] for _ in units]
    outs = [[] for _ in units]
    if seqs == 1:
        trans = []
        for ti in range(ntile):
            for ic in range(nchunk):
                rs, gends, kds = chunk_terms(ti, ic)
                kt_wu = [lax.dot_general(
                    kds[h], jnp.concatenate([wbs[unit(ti, h)][rs], u0s[unit(ti, h)][rs].astype(BF16)], axis=1),
                    (((0,), (0,)), ((), ())), preferred_element_type=F32) for h in heads]
                trans.append(([jnp.exp(g) for g in gends], [m[:, :GDN_DK].astype(BF16) for m in kt_wu],
                              [m[:, GDN_DK:] for m in kt_wu]))
        states = [s_ref[h] for h in heads]
        chunk_states = []
        for decay, kt_w, kt_u0 in trans:
            sb = [s.astype(BF16) for s in states]
            chunk_states.append(sb)
            states = [states[h] * decay[h] - _dot(kt_w[h], sb[h]) + kt_u0[h] for h in heads]
        for h in heads:
            s_ref[h] = states[h]
        for ti in range(ntile):
            for ic in range(nchunk):
                rs = slice(ic * c, (ic + 1) * c)
                sb = chunk_states[ti * nchunk + ic]
                wss = [_dot(jnp.concatenate([wbs[unit(ti, h)][rs], qds[unit(ti, h)][rs]], axis=0), sb[h])
                       for h in heads]
                for h in heads:
                    us[unit(ti, h)].append((u0s[unit(ti, h)][rs] - wss[h][:c]).astype(BF16))
                    outs[unit(ti, h)].append(wss[h][c:])
        o = [jnp.concatenate(outs[u], axis=0) + _dot(qks[u], jnp.concatenate(us[u], axis=0))
             for u in range(len(units))]

        @pl.when(t == pl.num_programs(1) - 1)
        def _():
            s_out_ref[...] = s_ref[...]
    else:
        for ic in range(nchunk):
            rs, gends, kds = chunk_terms(0, ic)
            states = [s0_ref[ic, h] for h in heads]
            wss = state_terms(rs, states)
            us_c = [(u0s[h][rs] - wss[h][:c]).astype(BF16) for h in heads]
            new = next_state(states, gends, kds, us_c)
            for h in heads:
                us[h].append(us_c[h])
                outs[h].append(wss[h][c:])
                s_out_ref[ic, h] = new[h]
        o = [jnp.concatenate(outs[h], axis=0) + _dot(qks[h], jnp.concatenate(us[h], axis=0)) for h in heads]
    for u, (ti, h) in enumerate(units):
        rows, cols = tile_rows[ti], head_cols(h, GDN_DV)
        o_ref[rows, cols] = (_rms(o[u], on) * _silu(z_ref[rows, cols])).astype(o_ref.dtype)


def _gdn_call(conv, z, ba, hist, s0, p, tile, c):
    b, l, _ = conv.shape
    small = [p['conv_w'], p['a_log'], p['dt_bias'], p['gdn_on'], p['head_mat']]
    state = (GDN_HEADS, GDN_DK, GDN_DV)
    if l >= tile:
        tc = _pick_tile(l, GDN_BLOCK_TILES * tile)
        seqs = 1
        grid = (b, l // tc)
        row = lambda w: pl.BlockSpec((None, tc, w), lambda i, j: (i, j, 0))
        per_b = lambda shape: pl.BlockSpec((None,) + shape, lambda i, j: (i,) + (0,) * len(shape))
        scratch = [pltpu.VMEM((tc + CONV_PAD, CONV_CH), F32), pltpu.VMEM(state, F32)]
    else:
        tc = tile = b * l
        assert l == c and tc % LANES == 0
        seqs = b
        grid = (1, 1)
        conv, z, ba = (t.reshape(1, tc, t.shape[-1]) for t in (conv, z, ba))
        row = lambda w: pl.BlockSpec((None, tc, w), lambda i, j: (0, 0, 0))
        per_b = lambda shape: pl.BlockSpec((b,) + shape, lambda i, j: (0,) * (len(shape) + 1))
        scratch = [pltpu.VMEM((seqs * (c + CONV_PAD), CONV_CH), F32)]
    o, s_new = pl.pallas_call(
        functools.partial(_gdn_kernel, tc=tc, tile=tile, c=c, seqs=seqs),
        grid=grid,
        in_specs=[row(CONV_CH), row(GDN_V), row(2 * LANES), per_b((CONV_W - 1, CONV_CH)), per_b(state)] + [
            pl.BlockSpec(w.shape, lambda i, j: (0, 0)) for w in small],
        out_specs=[row(GDN_V), per_b(state)],
        out_shape=[jax.ShapeDtypeStruct(conv.shape[:2] + (GDN_V,), BF16),
                   jax.ShapeDtypeStruct((b,) + state, F32)],
        scratch_shapes=scratch,
        compiler_params=pltpu.CompilerParams(dimension_semantics=("arbitrary", "arbitrary"),
                                             vmem_limit_bytes=VMEM_LIMIT),
        name="gdn",
    )(conv, z, ba, hist, s0, *small)
    return o.reshape(b, l, GDN_V), s_new


def _stack_maps(q):
    lane = lax.broadcasted_iota(jnp.int32, q.shape, 1)
    zero = jnp.zeros_like(q)
    return jnp.concatenate([jnp.where(lane < DIFF_DK, q, zero), jnp.where(lane >= DIFF_DK, q, zero)],
                           axis=0)


def _diff_finish(acc, l, t, lq_ref, lk_ref, dn_ref, lam_init):
    lam_e = jnp.exp(jnp.sum(lq_ref[...] * lk_ref[...], axis=-1, keepdims=True))
    lam = lam_e[0:1] - lam_e[1:2] + lam_init
    o = acc[:t] / l[:t] - lam * (acc[t:] / l[t:])
    return _rms(o, dn_ref[...]) * (1.0 - lam_init)


def _halves_max(s):
    return jnp.maximum(s[:, :LANES], s[:, LANES:])


def _attn_prompt_kernel(q_ref, kt_ref, v_ref, bias_ref, lq_ref, lk_ref, dn_ref, o_ref,
                        kt_scr, vx_scr, s_scr, m_scr, acc_scr, *, tq, tk, lam_init):
    i = pl.program_id(2)
    assert tk == 2 * LANES and tq % tk == 0
    nsub = tq // tk

    @pl.when(i == 0)
    def _():
        for j in range(kt_scr.shape[0]):
            kt_scr[j] = kt_ref[:, j * tk:(j + 1) * tk].astype(BF16)
        vx_scr[:, :DIFF_DV] = v_ref[...]
        vx_scr[:, DIFF_DV:] = jnp.ones((vx_scr.shape[0], LANES), BF16)

    qq = _stack_maps(q_ref[...])

    def scores(j):
        return _dot(qq, kt_scr[j])

    def weighted_values(ss, blocks, mm):
        p = [jnp.exp2(s[:, half * LANES:(half + 1) * LANES] - mm) for s in ss for half in range(tk // LANES)]
        vx = [vx_scr[pl.ds(pl.multiple_of(j * tk, tk), tk), :] for j in blocks]
        return _dot(jnp.concatenate(p, axis=1).astype(BF16), jnp.concatenate(vx, axis=0))

    def pass1(blocks, biases):
        ss = [scores(j) if b is None else scores(j) + b for j, b in zip(blocks, biases)]
        for j, s in zip(blocks, ss):
            s_scr[j] = s
        m_scr[...] = jnp.maximum(m_scr[...], functools.reduce(jnp.maximum, [_halves_max(s) for s in ss]))

    def pass2(blocks, biases):
        del biases
        acc_scr[...] += weighted_values([s_scr[j] for j in blocks], blocks, m_scr[...])

    def over_visible(body):
        def group(g):
            return [g * nsub + u for u in range(nsub)]

        def trip(t, carry):
            second = group(2 * t + 1)
            body(group(2 * t) + second,
                 [None] * nsub + [bias_ref[jnp.maximum(j - i * nsub + 1, 0)] for j in second])
            return carry

        lax.fori_loop(0, (i + 1) // 2, trip, 0)

        @pl.when(i % 2 == 0)
        def _():
            body(group(i), [bias_ref[1 + d] for d in range(nsub)])

    m_scr[...] = jnp.full(m_scr.shape, -jnp.inf, F32)
    over_visible(pass1)
    m_scr[...] = jnp.broadcast_to(jnp.max(m_scr[...], axis=-1, keepdims=True), m_scr.shape)
    acc_scr[...] = jnp.zeros(acc_scr.shape, F32)
    over_visible(pass2)
    acc = acc_scr[...]
    o_ref[...] = _diff_finish(acc[:, :DIFF_DV], acc[:, DIFF_DV:], tq, lq_ref, lk_ref, dn_ref,
                              lam_init).astype(o_ref.dtype)


def _attn_prompt_call(qb, kt, vb, p, tq, tk, lam_init):
    b, l, _ = qb.shape
    small = [p['lambda_q'], p['lambda_k'], p['diff_on']]
    r = (jnp.arange(2 * tq) % tq)[None, :, None] // CHUNK
    cidx = (jnp.arange(tk)[None, None, :] + tk * jnp.arange(tq // tk)[:, None, None]) // CHUNK
    bias = jnp.where(cidx <= r, 0.0, -jnp.inf).astype(F32)
    bias = jnp.concatenate([jnp.zeros_like(bias[:1]), bias], axis=0)
    return pl.pallas_call(
        functools.partial(_attn_prompt_kernel, tq=tq, tk=tk, lam_init=lam_init),
        grid=(b, DIFF_HEADS, l // tq),
        in_specs=[pl.BlockSpec((None, tq, LANES), lambda bi, h, i: (bi, i, h)),
                  pl.BlockSpec((None, None, LANES, l), lambda bi, h, i: (bi, h, 0, 0)),
                  pl.BlockSpec((None, l, LANES), lambda bi, h, i: (bi, 0, h)),
                  _const_spec(bias.shape)] + [
                      pl.BlockSpec(w.shape, lambda bi, h, i: (0, 0)) for w in small],
        out_specs=pl.BlockSpec((None, tq, LANES), lambda bi, h, i: (bi, i, h)),
        out_shape=jax.ShapeDtypeStruct((b, l, DIFF_V), BF16),
        scratch_shapes=[pltpu.VMEM((l // tk, LANES, tk), BF16),
                        pltpu.VMEM((l, DIFF_DV + LANES), BF16),
                        pltpu.VMEM((l // tk, 2 * tq, tk), F32),
                        pltpu.VMEM((2 * tq, LANES), F32),
                        pltpu.VMEM((2 * tq, DIFF_DV + LANES), F32)],
        compiler_params=pltpu.CompilerParams(
            dimension_semantics=("arbitrary", "arbitrary", "arbitrary"),
            vmem_limit_bytes=VMEM_LIMIT),
        name="attn_prompt",
    )(qb, kt, vb, bias, *small)


def _attn_sample_kernel(q_ref, kn_ref, vn_ref, ckt_ref, cv_ref, lq_ref, lk_ref, dn_ref, o_ref,
                        *, t, lam_init):
    past = ckt_ref.shape[-1]
    for h in range(DIFF_HEADS):
        cols = slice(h * LANES, (h + 1) * LANES)
        qq = _stack_maps(q_ref[:, cols])
        s_c = _dot(qq, ckt_ref[h].astype(BF16))
        s_n = _dot_nt(qq, kn_ref[:, cols])
        m = jnp.maximum(jnp.max(s_c, axis=-1, keepdims=True), jnp.max(s_n, axis=-1, keepdims=True))
        p_c = jnp.exp2(s_c - m)
        p_n = jnp.exp2(s_n - m)
        l = jnp.sum(p_c, axis=-1, keepdims=True) + jnp.sum(p_n, axis=-1, keepdims=True)
        cv = cv_ref[pl.ds(h, past, stride=DIFF_HEADS), :].astype(BF16)
        acc = _dot(p_c.astype(BF16), cv) + _dot(p_n.astype(BF16), vn_ref[:, cols])
        o_ref[:, cols] = _diff_finish(acc, l, t, lq_ref, lk_ref, dn_ref, lam_init).astype(o_ref.dtype)


def _attn_sample_call(qb, kb, vb, ckt, cv, p, lam_init):
    b, t, _ = qb.shape
    past = ckt.shape[-1]
    small = [p['lambda_q'], p['lambda_k'], p['diff_on']]
    new = pl.BlockSpec((None, t, DIFF_V), lambda bi: (bi, 0, 0))
    return pl.pallas_call(
        functools.partial(_attn_sample_kernel, t=t, lam_init=lam_init),
        grid=(b,),
        in_specs=[new, new, new,
                  pl.BlockSpec((None, DIFF_HEADS, LANES, past), lambda bi: (bi, 0, 0, 0)),
                  pl.BlockSpec((None, past * DIFF_HEADS, DIFF_DV), lambda bi: (bi, 0, 0))] + [
                      pl.BlockSpec(w.shape, lambda bi: (0, 0)) for w in small],
        out_specs=new,
        out_shape=jax.ShapeDtypeStruct((b, t, DIFF_V), BF16),
        compiler_params=pltpu.CompilerParams(dimension_semantics=("arbitrary",),
                                             vmem_limit_bytes=VMEM_LIMIT),
        name="attn_sample",
    )(qb, kb, vb, ckt, cv, *small)


def _rope_tables(pos):
    half = ROT_DIM // 2
    inv = jnp.float32(ROPE_THETA) ** (-jnp.arange(half, dtype=F32) * 2.0 / ROT_DIM)
    ang = pos.astype(F32)[:, None] * inv[None, :]
    cos, sin = jnp.cos(ang), jnp.sin(ang)
    n = pos.shape[0]
    rest = DIFF_DK - ROT_DIM
    one = jnp.ones((n, rest), F32)
    zero = jnp.zeros((n, rest), F32)
    zh = jnp.zeros((n, half), F32)
    c64 = jnp.concatenate([cos, cos, one], axis=1)
    a64 = jnp.concatenate([-sin, zh, zero], axis=1)
    b64 = jnp.concatenate([zh, sin, zero], axis=1)
    rep = LANES // DIFF_DK
    return tuple(jnp.tile(t, (1, rep)) for t in (c64, a64, b64))


def _layer_params(w, l):
    d = w['w_in'].shape[1]
    win = w['w_in'][l]
    o = 0
    w_conv = win[:, o:o + CONV_CH]; o += CONV_CH
    w_b = win[:, o:o + GDN_HEADS]; o += GDN_HEADS
    w_a = win[:, o:o + GDN_HEADS]; o += GDN_HEADS
    w_z = win[:, o:o + GDN_V]; o += GDN_V
    w_q = win[:, o:o + DIFF_QK]; o += DIFF_QK
    w_k = win[:, o:o + DIFF_QK]; o += DIFF_QK
    w_v = win[:, o:o + DIFF_V]
    zpad = jnp.zeros((d, LANES - GDN_HEADS), win.dtype)
    w_in = jnp.concatenate([w_conv, w_z, w_q, w_k, w_v, w_b, zpad, w_a, zpad], axis=1)
    hpad = lambda v: jnp.concatenate([v.astype(F32), jnp.zeros((LANES - GDN_HEADS,), F32)])[None, :]
    grp = jnp.arange(DIFF_QK) // DIFF_DK
    bf = lambda t: t.astype(BF16)
    r2 = lambda v: v.astype(F32)[None, :]
    return dict(
        n1=r2(w['ffn1_norm'][l]), wgu1=bf(w['ffn1_w_gu'][l]), wd1=bf(w['ffn1_w_down'][l]),
        nm=r2(w['mix_norm'][l]), w_in=bf(w_in),
        qn=r2(jnp.tile(w['q_norm'][l], DIFF_QK // DIFF_DK)), kn=r2(jnp.tile(w['k_norm'][l], DIFF_QK // DIFF_DK)),
        gmat=(grp[:, None] == grp[None, :]).astype(BF16),
        head_mat=(grp[:, None] // 2 == grp[None, :] // 2).astype(BF16),
        conv_w=w['conv_w'][l].astype(F32), a_log=hpad(w['a_log'][l]), dt_bias=hpad(w['dt_bias'][l]),
        gdn_on=r2(w['gdn_out_norm'][l]),
        lambda_q=w['lambda_q'][l].astype(F32), lambda_k=w['lambda_k'][l].astype(F32),
        diff_on=r2(w['diff_out_norm'][l]),
        wo=bf(w['w_out'][l]), n2=r2(w['ffn2_norm'][l]), wgu2=bf(w['ffn2_w_gu'][l]), wd2=bf(w['ffn2_w_down'][l]),
    )


def _pick_tile(n, pref):
    t = min(n, pref)
    assert n % t == 0
    return t


def _layer(x, pos, k_hist, v_hist, conv_hist, s0, p, lam_init):
    b, l, d = x.shape
    n = b * l
    tm = _pick_tile(n, 256)
    tabs = _rope_tables(pos)
    if l >= tm:
        assert l % tm == 0
        n_pos_tiles = l // tm
    else:
        assert tm % l == 0
        tabs = tuple(jnp.tile(t, (tm // l, 1)) for t in tabs)
        n_pos_tiles = 1
    prompt = k_hist is None
    pre = _pre_call(x.reshape(n, d), tabs, p, tm, n_pos_tiles, b, prompt)
    x1, conv, z, ba, qb, vb, kf, vf = pre[:8]
    c = min(CHUNK, l)
    assert l % c == 0
    tc = GDN_TILE
    assert l % tc == 0 or l == c
    if conv_hist is None:
        conv_hist = jnp.zeros((b, CONV_W - 1, CONV_CH), F32)
        s0 = jnp.zeros((b, GDN_HEADS, GDN_DK, GDN_DV), F32)
    conv3 = conv.reshape(b, l, CONV_CH)
    og, s_new = _gdn_call(conv3, z.reshape(b, l, GDN_V), ba.reshape(b, l, 2 * LANES),
                          conv_hist.astype(F32), s0.astype(F32), p, tc, c)
    conv_new = jnp.concatenate([conv_hist.astype(F32), conv3], axis=1)[:, -(CONV_W - 1):]
    q3 = qb.reshape(b, l, DIFF_QK)
    v3 = vb.reshape(b, l, DIFF_V)
    if prompt:
        tk = 2 * LANES
        od = _attn_prompt_call(q3, kf, v3, p, _pick_tile(l, 2 * tk), tk, lam_init)
        k_out = jnp.transpose(kf.reshape(b, DIFF_HEADS, 2, DIFF_DK, l), (0, 4, 1, 2, 3))
    else:
        past = k_hist.shape[1]
        ckt = jnp.transpose(k_hist, (0, 2, 3, 4, 1)).reshape(b, DIFF_HEADS, LANES, past)
        od = _attn_sample_call(q3, pre[8].reshape(b, l, DIFF_QK), v3, ckt,
                               v_hist.reshape(b, past * DIFF_HEADS, DIFF_DV), p, lam_init)
        k_out = kf.reshape(b, l, DIFF_HEADS, 2, DIFF_DK)
    y = _post_call(x1, og.reshape(n, GDN_V), od.reshape(n, DIFF_V), p, tm)
    return (y.reshape(b, l, d), k_out, vf.reshape(b, l, DIFF_HEADS, DIFF_DV), s_new, conv_new)


def kernel(x_prompt, x_sample, cache_k, cache_v, state_gdn, state_conv, ffn1_norm, ffn1_w_gu, ffn1_w_down,
           mix_norm, w_in, conv_w, a_log, dt_bias, gdn_out_norm, q_norm, k_norm, lambda_q, lambda_k,
           diff_out_norm, w_out, ffn2_norm, ffn2_w_gu, ffn2_w_down):
    w = dict(ffn1_norm=ffn1_norm, ffn1_w_gu=ffn1_w_gu, ffn1_w_down=ffn1_w_down, mix_norm=mix_norm, w_in=w_in,
             conv_w=conv_w, a_log=a_log, dt_bias=dt_bias, gdn_out_norm=gdn_out_norm, q_norm=q_norm,
             k_norm=k_norm, lambda_q=lambda_q, lambda_k=lambda_k, diff_out_norm=diff_out_norm, w_out=w_out,
             ffn2_norm=ffn2_norm, ffn2_w_gu=ffn2_w_gu, ffn2_w_down=ffn2_w_down)
    depth = w_in.shape[0]
    pos_p = jnp.arange(x_prompt.shape[1])
    pos_s = cache_k.shape[2] + jnp.arange(x_sample.shape[1])
    hp, hs = x_prompt, x_sample
    outs = [[] for _ in range(8)]
    for l in range(depth):
        lam_init = 0.8 - 0.6 * math.exp(-0.3 * l)
        p = _layer_params(w, l)
        hp, kp, vp, sp, cp = _layer(hp, pos_p, None, None, None, None, p, lam_init)
        hs, ks, vs, ss, cs = _layer(hs, pos_s, cache_k[l], cache_v[l], state_conv[l], state_gdn[l], p, lam_init)
        for acc, val in zip(outs, (kp, vp, sp, cp, ks, vs, ss, cs)):
            acc.append(val)
    return (hp, hs) + tuple(jnp.stack(o) for o in outs)
```

```python
import functools
import math

import jax
import jax.numpy as jnp
from jax import lax
from jax.experimental import pallas as pl
from jax.experimental.pallas import tpu as pltpu

F32 = jnp.float32
BF16 = jnp.bfloat16

EPS = 1e-6
CHUNK = 64
GDN_HEADS = 4
GDN_DK = 128
GDN_DV = 128
CONV_W = 4
DIFF_HEADS = 4
DIFF_DK = 64
DIFF_DV = 128
ROT_DIM = DIFF_DK // 4
ROPE_THETA = 500000.0
GDN_QK = GDN_HEADS * GDN_DK
GDN_V = GDN_HEADS * GDN_DV
CONV_CH = 2 * GDN_QK + GDN_V
DIFF_QK = DIFF_HEADS * 2 * DIFF_DK
DIFF_V = DIFF_HEADS * DIFF_DV

LANES = 128
INV_BLOCK = 16
GDN_TILE = 128
GDN_BLOCK_TILES = 4
VMEM_LIMIT = 56 * 1024 * 1024
Q_SCALE = (DIFF_DK ** -0.5) * math.log2(math.e)


def _dot(a, b):
    return jnp.dot(a, b, preferred_element_type=F32)


def _dot_nt(a, b):
    return lax.dot_general(a, b, (((1,), (1,)), ((), ())), preferred_element_type=F32)


def _rms(x, g):
    return x * lax.rsqrt(jnp.mean(x * x, axis=-1, keepdims=True) + EPS) * g


def _silu(x):
    return x * jax.nn.sigmoid(x)


def _rms_parts(x, g):
    return (x * g).astype(BF16), lax.rsqrt(jnp.mean(x * x, axis=-1, keepdims=True) + EPS)


def _swiglu(xg, r, wgu_ref, wd_ref):
    d_ff = wd_ref.shape[0]
    g = _dot(xg, wgu_ref[:, :d_ff]) * r
    u = _dot(xg, wgu_ref[:, d_ff:]) * r
    act = (_silu(g) * u).astype(BF16)
    return _dot(act, wd_ref[...])


def _const_spec(shape):
    nd = len(shape)
    return pl.BlockSpec(shape, lambda *_: (0,) * nd, pipeline_mode=pl.Buffered(1))


_WIN = dict(zip(('conv', 'z', 'q', 'k', 'v', 'ba'),
                (slice(a, b) for a, b in zip((0, 1536, 2048, 2560, 3072, 3584), (1536, 2048, 2560, 3072, 3584, 3840)))))
assert _WIN['conv'].stop == CONV_CH and _WIN['ba'].stop - _WIN['ba'].start == 2 * LANES


def _pre_kernel(x_ref, cos_ref, sa_ref, sb_ref, n1_ref, wgu_ref, wd_ref, nm_ref, win_ref, qn_ref, kn_ref, gm_ref,
                x1_ref, conv_ref, z_ref, ba_ref, qb_ref, vb_ref, kf_ref, vf_ref, *maybe_kb_ref,
                k_pos_minor):
    x = x_ref[...]
    xg, r = _rms_parts(x, n1_ref[...])
    x1 = x + 0.5 * _swiglu(xg, r, wgu_ref, wd_ref)
    x1_ref[...] = x1
    h, rh = _rms_parts(x1, nm_ref[...])
    tm = x.shape[0]
    cos = cos_ref[...]
    sa = sa_ref[...]
    sb = sb_ref[...]
    gm = gm_ref[...]

    def norm_rope(t, gw):
        ss = _dot((t * t).astype(BF16), gm)
        t = t * lax.rsqrt(ss * (1.0 / DIFF_DK) + EPS) * gw
        outs = []
        for hh in range(DIFF_HEADS):
            th = t[:, hh * LANES:(hh + 1) * LANES]
            up = pltpu.roll(th, LANES - ROT_DIM // 2, 1)
            dn = pltpu.roll(th, ROT_DIM // 2, 1)
            outs.append(th * cos + up * sa + dn * sb)
        return jnp.concatenate(outs, axis=1)

    k = norm_rope(_dot_nt(h, win_ref[_WIN['k'], :]) * rh, kn_ref[...])
    if k_pos_minor:
        kf_ref[...] = k.T.reshape(DIFF_HEADS, LANES, tm)
    else:
        kf_ref[...] = k
        maybe_kb_ref[0][...] = k.astype(BF16)
    q = norm_rope(_dot_nt(h, win_ref[_WIN['q'], :]) * rh, qn_ref[...])
    qb_ref[...] = (q * Q_SCALE).astype(BF16)
    v = _dot_nt(h, win_ref[_WIN['v'], :]) * rh
    vb_ref[...] = v.astype(BF16)
    if k_pos_minor:
        for hh in range(DIFF_HEADS):
            vf_ref[pl.ds(hh, tm, stride=DIFF_HEADS), :] = v[:, hh * DIFF_DV:(hh + 1) * DIFF_DV]
    else:
        vf_ref[...] = v
    z_ref[...] = _dot_nt(h, win_ref[_WIN['z'], :]) * rh
    ba_ref[...] = _dot_nt(h, win_ref[_WIN['ba'], :]) * rh
    conv_ref[...] = _dot_nt(h, win_ref[_WIN['conv'], :]) * rh


def _pre_call(x2d, tabs, p, tm, n_pos_tiles, batch, k_pos_minor):
    n, d = x2d.shape
    grid = (n // tm,)
    row = lambda w: pl.BlockSpec((tm, w), lambda i: (i, 0))
    tab = pl.BlockSpec((tm, LANES), lambda i: (i % n_pos_tiles, 0))
    weights = [p['n1'], p['wgu1'], p['wd1'], p['nm'], p['w_in'], p['qn'], p['kn'], p['gmat']]
    in_specs = [row(d), tab, tab, tab] + [_const_spec(w.shape) for w in weights]
    outs = [(d, F32), (CONV_CH, F32), (GDN_V, F32), (2 * LANES, F32), (DIFF_QK, BF16), (DIFF_V, BF16)]
    out_specs = [row(w) for w, _ in outs]
    out_shape = [jax.ShapeDtypeStruct((n, w), dt) for w, dt in outs]
    if k_pos_minor:
        l = n // batch
        tiles = l // tm
        out_specs += [pl.BlockSpec((None, DIFF_HEADS, LANES, tm), lambda i: (i // tiles, 0, 0, i % tiles)),
                      pl.BlockSpec((tm * DIFF_HEADS, DIFF_DV), lambda i: (i, 0))]
        out_shape += [jax.ShapeDtypeStruct((batch, DIFF_HEADS, LANES, l), F32),
                      jax.ShapeDtypeStruct((n * DIFF_HEADS, DIFF_DV), F32)]
    else:
        out_specs += [row(DIFF_QK), row(DIFF_V), row(DIFF_QK)]
        out_shape += [jax.ShapeDtypeStruct((n, DIFF_QK), F32), jax.ShapeDtypeStruct((n, DIFF_V), F32),
                      jax.ShapeDtypeStruct((n, DIFF_QK), BF16)]
    return pl.pallas_call(
        functools.partial(_pre_kernel, k_pos_minor=k_pos_minor),
        grid=grid,
        in_specs=in_specs,
        out_specs=out_specs,
        out_shape=out_shape,
        compiler_params=pltpu.CompilerParams(dimension_semantics=("arbitrary",),
                                             vmem_limit_bytes=VMEM_LIMIT),
        name="pre",
    )(x2d, *tabs, *weights)


def _post_kernel(x1_ref, og_ref, od_ref, wo_ref, n2_ref, wgu_ref, wd_ref, y_ref):
    mixed = jnp.concatenate([og_ref[...], od_ref[...]], axis=1)
    x2 = x1_ref[...] + _dot(mixed, wo_ref[...])
    xg, r = _rms_parts(x2, n2_ref[...])
    y_ref[...] = x2 + 0.5 * _swiglu(xg, r, wgu_ref, wd_ref)


def _post_call(x1, og, od, p, tm):
    n, d = x1.shape
    row = lambda w: pl.BlockSpec((tm, w), lambda i: (i, 0))
    weights = [p['wo'], p['n2'], p['wgu2'], p['wd2']]
    return pl.pallas_call(
        _post_kernel,
        grid=(n // tm,),
        in_specs=[row(d), row(GDN_V), row(DIFF_V)] + [_const_spec(w.shape) for w in weights],
        out_specs=row(d),
        out_shape=jax.ShapeDtypeStruct((n, d), F32),
        compiler_params=pltpu.CompilerParams(dimension_semantics=("arbitrary",),
                                             vmem_limit_bytes=VMEM_LIMIT),
        name="post",
    )(x1, og, od, *weights)


def _exact3(m01, x):
    x1 = x.astype(BF16)
    r1 = x - x1.astype(F32)
    x2 = r1.astype(BF16)
    x3 = (r1 - x2.astype(F32)).astype(BF16)
    return _dot(m01, x1) + (_dot(m01, x2) + _dot(m01, x3))


CONV_PAD = 8


def _gdn_kernel(x_ref, z_ref, ba_ref, hist_ref, s0_ref, cw_ref, alog_ref, dtb_ref, on_ref, hm_ref,
                o_ref, s_out_ref, xp_ref, *maybe_s_ref, tc, tile, c, seqs):
    pad = CONV_PAD
    nchunk = tile // c
    ntile = tc // tile
    assert c % INV_BLOCK == 0 and tile % LANES == 0 and tc % tile == 0
    assert seqs == 1 or (seqs == nchunk and ntile == 1)
    cw = cw_ref[...]
    x = x_ref[...]

    def conv(xp, rows):
        y = xp[rows] * cw[CONV_W - 1:CONV_W, :]
        for k in range(1, CONV_W):
            y = y + pltpu.roll(xp, k, 0)[rows] * cw[CONV_W - 1 - k:CONV_W - k, :]
        return y

    if seqs == 1:
        s_ref, = maybe_s_ref
        t = pl.program_id(1)

        @pl.when(t == 0)
        def _():
            xp_ref[0:pad, :] = jnp.zeros((pad, CONV_CH), F32)
            xp_ref[pad - (CONV_W - 1):pad, :] = hist_ref[...]
            s_ref[...] = s0_ref[...]

        xp_ref[pad:pad + tc, :] = x
        y = conv(xp_ref[...], slice(pad, pad + tc))
        xp_ref[pad - (CONV_W - 1):pad, :] = x[tc - (CONV_W - 1):tc, :]
    else:
        stride = pad + c
        xp_ref[...] = jnp.zeros(xp_ref.shape, F32)
        for b in range(seqs):
            xp_ref[b * stride + pad - (CONV_W - 1):b * stride + pad, :] = hist_ref[b]
            xp_ref[b * stride + pad:(b + 1) * stride, :] = x[b * c:(b + 1) * c, :]
        yp = conv(xp_ref[...], slice(None))
        y = jnp.concatenate([yp[b * stride + pad:(b + 1) * stride, :] for b in range(seqs)], axis=0)
    y = _silu(y)
    hm = hm_ref[...]
    yk = y[:, GDN_QK:2 * GDN_QK]
    yk = yk * lax.rsqrt(_dot((yk * yk).astype(BF16), hm) + EPS)

    ba = ba_ref[...]
    beta = jax.nn.sigmoid(ba[:, :LANES])
    g = -jnp.exp(alog_ref[...]) * jax.nn.softplus(ba[:, LANES:] + dtb_ref[...])
    ri = lax.broadcasted_iota(jnp.int32, (tile, tile), 0)
    ci = lax.broadcasted_iota(jnp.int32, (tile, tile), 1)
    incl = ((ri // c) == (ci // c)) & (ci <= ri)
    same_blk = (ri // INV_BLOCK) == (ci // INV_BLOCK)
    diag = ri == ci
    eye = jnp.where(diag, 1.0, 0.0)
    incl01 = jnp.where(incl, 1.0, 0.0).astype(BF16)
    tile_rows = [slice(ti * tile, (ti + 1) * tile) for ti in range(ntile)]
    gcums = [_exact3(incl01, g[rows]) for rows in tile_rows]
    gcum_ts = [gc.T for gc in gcums]
    on = on_ref[...]

    heads = range(GDN_HEADS)
    units = [(ti, h) for ti in range(ntile) for h in heads]

    def per_head(f, *lists):
        return [f(*args) for args in zip(*lists)]

    def head_cols(h, width):
        return slice(h * width, (h + 1) * width)

    ks = [yk[tile_rows[ti], head_cols(h, GDN_DK)] for ti, h in units]
    vs = [y[tile_rows[ti], 2 * GDN_QK + h * GDN_DV:2 * GDN_QK + (h + 1) * GDN_DV] for ti, h in units]
    gcs = [gcums[ti][:, h:h + 1] for ti, h in units]
    bcs = [beta[tile_rows[ti], h:h + 1] for ti, h in units]
    decs = [jnp.exp(jnp.where(incl, gcs[u] - gcum_ts[ti][h:h + 1, :], -jnp.inf))
            for u, (ti, h) in enumerate(units)]
    kbs = per_head(lambda k: k.astype(BF16), ks)
    a_s = per_head(lambda bc, kb, dec: jnp.where(diag, 0.0, bc * _dot_nt(kb, kb) * dec), bcs, kbs, decs)

    assert INV_BLOCK == 16 and c in (INV_BLOCK, 4 * INV_BLOCK)
    bf = lambda xs: per_head(lambda x: x.astype(BF16), xs)
    mm = lambda xs, ys: per_head(_dot, xs, ys)
    ds = per_head(lambda a: jnp.where(same_blk, a, 0.0), a_s) if c > INV_BLOCK else a_s
    sd = bf(ds)
    d2 = mm(sd, sd)
    s2 = bf(d2)
    d3 = mm(sd, s2)
    d4 = mm(s2, s2)
    s4 = bf(d4)
    n1 = per_head(lambda d, x2, x3: eye - d + x2 - x3, ds, d2, d3)
    n2 = per_head(jnp.add, n1, mm(bf(n1), s4))
    d8 = mm(s4, s4)
    tinv = per_head(jnp.add, n2, mm(bf(n2), bf(d8)))
    if c > INV_BLOCK:
        std = bf(tinv)
        ms = mm(std, bf(per_head(jnp.subtract, a_s, ds)))
        sm = bf(ms)
        m2 = mm(sm, sm)
        m3 = mm(sm, bf(m2))
        tinv = mm(bf(per_head(lambda m, x2, x3: eye - m + x2 - x3, ms, m2, m3)), std)

    egs = per_head(jnp.exp, gcs)
    rhs = per_head(lambda v, k, bc, eg: jnp.concatenate([v * bc, k * (bc * eg)], axis=1), vs, ks, bcs, egs)
    sols = mm(bf(tinv), bf(rhs))
    yq = y[:, :GDN_QK]
    yq = yq * (lax.rsqrt(_dot((yq * yq).astype(BF16), hm) + EPS) * (GDN_DK ** -0.5))
    qs = [yq[tile_rows[ti], head_cols(h, GDN_DK)] for ti, h in units]
    qks = per_head(lambda q, kb, dec: (_dot_nt(q.astype(BF16), kb) * dec).astype(BF16), qs, kbs, decs)
    u0s = [sol[:, :GDN_DV] for sol in sols]
    wbs = [sol[:, GDN_DV:].astype(BF16) for sol in sols]
    qds = per_head(lambda q, eg: (q * eg).astype(BF16), qs, egs)

    unit = lambda ti, h: ti * GDN_HEADS + h

    def chunk_terms(ti, ic):
        rs = slice(ic * c, (ic + 1) * c)
        gends = [gcs[unit(ti, h)][(ic + 1) * c - 1:(ic + 1) * c, :] for h in heads]
        kds = [(ks[unit(ti, h)][rs] * jnp.exp(gends[h] - gcs[unit(ti, h)][rs])).astype(BF16) for h in heads]
        return rs, gends, kds

    def state_terms(rs, states):
        return [_dot(jnp.concatenate([wbs[h][rs], qds[h][rs]], axis=0), states[h].astype(BF16)) for h in heads]

    def next_state(states, gends, kds, us_c):
        return [states[h] * jnp.exp(gends[h]) + lax.dot_general(
            kds[h], us_c[h], (((0,), (0,)), ((), ())), preferred_element_type=F32) for h in heads]

    us = [[] for _ in units]
    outs = [[] for _ in units]
    if seqs == 1:
        trans = []
        for ti in range(ntile):
            for ic in range(nchunk):
                rs, gends, kds = chunk_terms(ti, ic)
                kt_wu = [lax.dot_general(
                    kds[h], jnp.concatenate([wbs[unit(ti, h)][rs], u0s[unit(ti, h)][rs].astype(BF16)], axis=1),
                    (((0,), (0,)), ((), ())), preferred_element_type=F32) for h in heads]
                trans.append(([jnp.exp(g) for g in gends], [m[:, :GDN_DK].astype(BF16) for m in kt_wu],
                              [m[:, GDN_DK:] for m in kt_wu]))
        states = [s_ref[h] for h in heads]
        chunk_states = []
        for decay, kt_w, kt_u0 in trans:
            sb = [s.astype(BF16) for s in states]
            chunk_states.append(sb)
            states = [states[h] * decay[h] - _dot(kt_w[h], sb[h]) + kt_u0[h] for h in heads]
        for h in heads:
            s_ref[h] = states[h]
        for ti in range(ntile):
            for ic in range(nchunk):
                rs = slice(ic * c, (ic + 1) * c)
                sb = chunk_states[ti * nchunk + ic]
                wss = [_dot(jnp.concatenate([wbs[unit(ti, h)][rs], qds[unit(ti, h)][rs]], axis=0), sb[h])
                       for h in heads]
                for h in heads:
                    us[unit(ti, h)].append((u0s[unit(ti, h)][rs] - wss[h][:c]).astype(BF16))
                    outs[unit(ti, h)].append(wss[h][c:])
        o = [jnp.concatenate(outs[u], axis=0) + _dot(qks[u], jnp.concatenate(us[u], axis=0))
             for u in range(len(units))]

        @pl.when(t == pl.num_programs(1) - 1)
        def _():
            s_out_ref[...] = s_ref[...]
    else:
        for ic in range(nchunk):
            rs, gends, kds = chunk_terms(0, ic)
            states = [s0_ref[ic, h] for h in heads]
            wss = state_terms(rs, states)
            us_c = [(u0s[h][rs] - wss[h][:c]).astype(BF16) for h in heads]
            new = next_state(states, gends, kds, us_c)
            for h in heads:
                us[h].append(us_c[h])
                outs[h].append(wss[h][c:])
                s_out_ref[ic, h] = new[h]
        o = [jnp.concatenate(outs[h], axis=0) + _dot(qks[h], jnp.concatenate(us[h], axis=0)) for h in heads]
    for u, (ti, h) in enumerate(units):
        rows, cols = tile_rows[ti], head_cols(h, GDN_DV)
        o_ref[rows, cols] = (_rms(o[u], on) * _silu(z_ref[rows, cols])).astype(o_ref.dtype)


def _gdn_call(conv, z, ba, hist, s0, p, tile, c):
    b, l, _ = conv.shape
    small = [p['conv_w'], p['a_log'], p['dt_bias'], p['gdn_on'], p['head_mat']]
    state = (GDN_HEADS, GDN_DK, GDN_DV)
    if l >= tile:
        tc = _pick_tile(l, GDN_BLOCK_TILES * tile)
        seqs = 1
        grid = (b, l // tc)
        row = lambda w: pl.BlockSpec((None, tc, w), lambda i, j: (i, j, 0))
        per_b = lambda shape: pl.BlockSpec((None,) + shape, lambda i, j: (i,) + (0,) * len(shape))
        scratch = [pltpu.VMEM((tc + CONV_PAD, CONV_CH), F32), pltpu.VMEM(state, F32)]
    else:
        tc = tile = b * l
        assert l == c and tc % LANES == 0
        seqs = b
        grid = (1, 1)
        conv, z, ba = (t.reshape(1, tc, t.shape[-1]) for t in (conv, z, ba))
        row = lambda w: pl.BlockSpec((None, tc, w), lambda i, j: (0, 0, 0))
        per_b = lambda shape: pl.BlockSpec((b,) + shape, lambda i, j: (0,) * (len(shape) + 1))
        scratch = [pltpu.VMEM((seqs * (c + CONV_PAD), CONV_CH), F32)]
    o, s_new = pl.pallas_call(
        functools.partial(_gdn_kernel, tc=tc, tile=tile, c=c, seqs=seqs),
        grid=grid,
        in_specs=[row(CONV_CH), row(GDN_V), row(2 * LANES), per_b((CONV_W - 1, CONV_CH)), per_b(state)] + [
            pl.BlockSpec(w.shape, lambda i, j: (0, 0)) for w in small],
        out_specs=[row(GDN_V), per_b(state)],
        out_shape=[jax.ShapeDtypeStruct(conv.shape[:2] + (GDN_V,), BF16),
                   jax.ShapeDtypeStruct((b,) + state, F32)],
        scratch_shapes=scratch,
        compiler_params=pltpu.CompilerParams(dimension_semantics=("arbitrary", "arbitrary"),
                                             vmem_limit_bytes=VMEM_LIMIT),
        name="gdn",
    )(conv, z, ba, hist, s0, *small)
    return o.reshape(b, l, GDN_V), s_new


def _stack_maps(q):
    lane = lax.broadcasted_iota(jnp.int32, q.shape, 1)
    zero = jnp.zeros_like(q)
    return jnp.concatenate([jnp.where(lane < DIFF_DK, q, zero), jnp.where(lane >= DIFF_DK, q, zero)],
                           axis=0)


def _diff_finish(acc, l, t, lq_ref, lk_ref, dn_ref, lam_init):
    lam_e = jnp.exp(jnp.sum(lq_ref[...] * lk_ref[...], axis=-1, keepdims=True))
    lam = lam_e[0:1] - lam_e[1:2] + lam_init
    o = acc[:t] / l[:t] - lam * (acc[t:] / l[t:])
    return _rms(o, dn_ref[...]) * (1.0 - lam_init)


def _halves_max(s):
    return jnp.maximum(s[:, :LANES], s[:, LANES:])


def _attn_prompt_kernel(q_ref, kt_ref, v_ref, bias_ref, lq_ref, lk_ref, dn_ref, o_ref,
                        kt_scr, vx_scr, s_scr, m_scr, acc_scr, *, tq, tk, lam_init):
    i = pl.program_id(2)
    assert tk == 2 * LANES and tq % tk == 0
    nsub = tq // tk

    @pl.when(i == 0)
    def _():
        for j in range(kt_scr.shape[0]):
            kt_scr[j] = kt_ref[:, j * tk:(j + 1) * tk].astype(BF16)
        vx_scr[:, :DIFF_DV] = v_ref[...]
        vx_scr[:, DIFF_DV:] = jnp.ones((vx_scr.shape[0], LANES), BF16)

    qq = _stack_maps(q_ref[...])

    def scores(j):
        return _dot(qq, kt_scr[j])

    def weighted_values(ss, blocks, mm):
        p = [jnp.exp2(s[:, half * LANES:(half + 1) * LANES] - mm) for s in ss for half in range(tk // LANES)]
        vx = [vx_scr[pl.ds(pl.multiple_of(j * tk, tk), tk), :] for j in blocks]
        return _dot(jnp.concatenate(p, axis=1).astype(BF16), jnp.concatenate(vx, axis=0))

    def pass1(blocks, biases):
        ss = [scores(j) if b is None else scores(j) + b for j, b in zip(blocks, biases)]
        for j, s in zip(blocks, ss):
            s_scr[j] = s
        m_scr[...] = jnp.maximum(m_scr[...], functools.reduce(jnp.maximum, [_halves_max(s) for s in ss]))

    def pass2(blocks, biases):
        del biases
        acc_scr[...] += weighted_values([s_scr[j] for j in blocks], blocks, m_scr[...])

    def over_visible(body):
        def group(g):
            return [g * nsub + u for u in range(nsub)]

        def trip(t, carry):
            second = group(2 * t + 1)
            body(group(2 * t) + second,
                 [None] * nsub + [bias_ref[jnp.maximum(j - i * nsub + 1, 0)] for j in second])
            return carry

        lax.fori_loop(0, (i + 1) // 2, trip, 0)

        @pl.when(i % 2 == 0)
        def _():
            body(group(i), [bias_ref[1 + d] for d in range(nsub)])

    m_scr[...] = jnp.full(m_scr.shape, -jnp.inf, F32)
    over_visible(pass1)
    m_scr[...] = jnp.broadcast_to(jnp.max(m_scr[...], axis=-1, keepdims=True), m_scr.shape)
    acc_scr[...] = jnp.zeros(acc_scr.shape, F32)
    over_visible(pass2)
    acc = acc_scr[...]
    o_ref[...] = _diff_finish(acc[:, :DIFF_DV], acc[:, DIFF_DV:], tq, lq_ref, lk_ref, dn_ref,
                              lam_init).astype(o_ref.dtype)


def _attn_prompt_call(qb, kt, vb, p, tq, tk, lam_init):
    b, l, _ = qb.shape
    small = [p['lambda_q'], p['lambda_k'], p['diff_on']]
    r = (jnp.arange(2 * tq) % tq)[None, :, None] // CHUNK
    cidx = (jnp.arange(tk)[None, None, :] + tk * jnp.arange(tq // tk)[:, None, None]) // CHUNK
    bias = jnp.where(cidx <= r, 0.0, -jnp.inf).astype(F32)
    bias = jnp.concatenate([jnp.zeros_like(bias[:1]), bias], axis=0)
    return pl.pallas_call(
        functools.partial(_attn_prompt_kernel, tq=tq, tk=tk, lam_init=lam_init),
        grid=(b, DIFF_HEADS, l // tq),
        in_specs=[pl.BlockSpec((None, tq, LANES), lambda bi, h, i: (bi, i, h)),
                  pl.BlockSpec((None, None, LANES, l), lambda bi, h, i: (bi, h, 0, 0)),
                  pl.BlockSpec((None, l, LANES), lambda bi, h, i: (bi, 0, h)),
                  _const_spec(bias.shape)] + [
                      pl.BlockSpec(w.shape, lambda bi, h, i: (0, 0)) for w in small],
        out_specs=pl.BlockSpec((None, tq, LANES), lambda bi, h, i: (bi, i, h)),
        out_shape=jax.ShapeDtypeStruct((b, l, DIFF_V), BF16),
        scratch_shapes=[pltpu.VMEM((l // tk, LANES, tk), BF16),
                        pltpu.VMEM((l, DIFF_DV + LANES), BF16),
                        pltpu.VMEM((l // tk, 2 * tq, tk), F32),
                        pltpu.VMEM((2 * tq, LANES), F32),
                        pltpu.VMEM((2 * tq, DIFF_DV + LANES), F32)],
        compiler_params=pltpu.CompilerParams(
            dimension_semantics=("arbitrary", "arbitrary", "arbitrary"),
            vmem_limit_bytes=VMEM_LIMIT),
        name="attn_prompt",
    )(qb, kt, vb, bias, *small)


def _attn_sample_kernel(q_ref, kn_ref, vn_ref, ckt_ref, cv_ref, lq_ref, lk_ref, dn_ref, o_ref,
                        *, t, lam_init):
    past = ckt_ref.shape[-1]
    for h in range(DIFF_HEADS):
        cols = slice(h * LANES, (h + 1) * LANES)
        qq = _stack_maps(q_ref[:, cols])
        s_c = _dot(qq, ckt_ref[h].astype(BF16))
        s_n = _dot_nt(qq, kn_ref[:, cols])
        m = jnp.maximum(jnp.max(s_c, axis=-1, keepdims=True), jnp.max(s_n, axis=-1, keepdims=True))
        p_c = jnp.exp2(s_c - m)
        p_n = jnp.exp2(s_n - m)
        l = jnp.sum(p_c, axis=-1, keepdims=True) + jnp.sum(p_n, axis=-1, keepdims=True)
        cv = cv_ref[pl.ds(h, past, stride=DIFF_HEADS), :].astype(BF16)
        acc = _dot(p_c.astype(BF16), cv) + _dot(p_n.astype(BF16), vn_ref[:, cols])
        o_ref[:, cols] = _diff_finish(acc, l, t, lq_ref, lk_ref, dn_ref, lam_init).astype(o_ref.dtype)


def _attn_sample_call(qb, kb, vb, ckt, cv, p, lam_init):
    b, t, _ = qb.shape
    past = ckt.shape[-1]
    small = [p['lambda_q'], p['lambda_k'], p['diff_on']]
    new = pl.BlockSpec((None, t, DIFF_V), lambda bi: (bi, 0, 0))
    return pl.pallas_call(
        functools.partial(_attn_sample_kernel, t=t, lam_init=lam_init),
        grid=(b,),
        in_specs=[new, new, new,
                  pl.BlockSpec((None, DIFF_HEADS, LANES, past), lambda bi: (bi, 0, 0, 0)),
                  pl.BlockSpec((None, past * DIFF_HEADS, DIFF_DV), lambda bi: (bi, 0, 0))] + [
                      pl.BlockSpec(w.shape, lambda bi: (0, 0)) for w in small],
        out_specs=new,
        out_shape=jax.ShapeDtypeStruct((b, t, DIFF_V), BF16),
        compiler_params=pltpu.CompilerParams(dimension_semantics=("arbitrary",),
                                             vmem_limit_bytes=VMEM_LIMIT),
        name="attn_sample",
    )(qb, kb, vb, ckt, cv, *small)


def _rope_tables(pos):
    half = ROT_DIM // 2
    inv = jnp.float32(ROPE_THETA) ** (-jnp.arange(half, dtype=F32) * 2.0 / ROT_DIM)
    ang = pos.astype(F32)[:, None] * inv[None, :]
    cos, sin = jnp.cos(ang), jnp.sin(ang)
    n = pos.shape[0]
    rest = DIFF_DK - ROT_DIM
    one = jnp.ones((n, rest), F32)
    zero = jnp.zeros((n, rest), F32)
    zh = jnp.zeros((n, half), F32)
    c64 = jnp.concatenate([cos, cos, one], axis=1)
    a64 = jnp.concatenate([-sin, zh, zero], axis=1)
    b64 = jnp.concatenate([zh, sin, zero], axis=1)
    rep = LANES // DIFF_DK
    return tuple(jnp.tile(t, (1, rep)) for t in (c64, a64, b64))


def _layer_params(w, l):
    d = w['w_in'].shape[1]
    win = jnp.transpose(w['w_in'][l])
    o = 0
    w_conv = win[o:o + CONV_CH]; o += CONV_CH
    w_b = win[o:o + GDN_HEADS]; o += GDN_HEADS
    w_a = win[o:o + GDN_HEADS]; o += GDN_HEADS
    w_z = win[o:o + GDN_V]; o += GDN_V
    w_q = win[o:o + DIFF_QK]; o += DIFF_QK
    w_k = win[o:o + DIFF_QK]; o += DIFF_QK
    w_v = win[o:o + DIFF_V]
    zpad = jnp.zeros((LANES - GDN_HEADS, d), win.dtype)
    w_in = jnp.concatenate([w_conv, w_z, w_q, w_k, w_v, w_b, zpad, w_a, zpad], axis=0)
    hpad = lambda v: jnp.concatenate([v.astype(F32), jnp.zeros((LANES - GDN_HEADS,), F32)])[None, :]
    grp = jnp.arange(DIFF_QK) // DIFF_DK
    bf = lambda t: t.astype(BF16)
    r2 = lambda v: v.astype(F32)[None, :]
    return dict(
        n1=r2(w['ffn1_norm'][l]), wgu1=bf(w['ffn1_w_gu'][l]), wd1=bf(w['ffn1_w_down'][l]),
        nm=r2(w['mix_norm'][l]), w_in=bf(w_in),
        qn=r2(jnp.tile(w['q_norm'][l], DIFF_QK // DIFF_DK)), kn=r2(jnp.tile(w['k_norm'][l], DIFF_QK // DIFF_DK)),
        gmat=(grp[:, None] == grp[None, :]).astype(BF16),
        head_mat=(grp[:, None] // 2 == grp[None, :] // 2).astype(BF16),
        conv_w=w['conv_w'][l].astype(F32), a_log=hpad(w['a_log'][l]), dt_bias=hpad(w['dt_bias'][l]),
        gdn_on=r2(w['gdn_out_norm'][l]),
        lambda_q=w['lambda_q'][l].astype(F32), lambda_k=w['lambda_k'][l].astype(F32),
        diff_on=r2(w['diff_out_norm'][l]),
        wo=bf(w['w_out'][l]), n2=r2(w['ffn2_norm'][l]), wgu2=bf(w['ffn2_w_gu'][l]), wd2=bf(w['ffn2_w_down'][l]),
    )


def _pick_tile(n, pref):
    t = min(n, pref)
    assert n % t == 0
    return t


def _layer(x, pos, k_hist, v_hist, conv_hist, s0, p, lam_init):
    b, l, d = x.shape
    n = b * l
    tm = _pick_tile(n, 256)
    tabs = _rope_tables(pos)
    if l >= tm:
        assert l % tm == 0
        n_pos_tiles = l // tm
    else:
        assert tm % l == 0
        tabs = tuple(jnp.tile(t, (tm // l, 1)) for t in tabs)
        n_pos_tiles = 1
    prompt = k_hist is None
    pre = _pre_call(x.reshape(n, d), tabs, p, tm, n_pos_tiles, b, prompt)
    x1, conv, z, ba, qb, vb, kf, vf = pre[:8]
    c = min(CHUNK, l)
    assert l % c == 0
    tc = GDN_TILE
    assert l % tc == 0 or l == c
    if conv_hist is None:
        conv_hist = jnp.zeros((b, CONV_W - 1, CONV_CH), F32)
        s0 = jnp.zeros((b, GDN_HEADS, GDN_DK, GDN_DV), F32)
    conv3 = conv.reshape(b, l, CONV_CH)
    og, s_new = _gdn_call(conv3, z.reshape(b, l, GDN_V), ba.reshape(b, l, 2 * LANES),
                          conv_hist.astype(F32), s0.astype(F32), p, tc, c)
    conv_new = jnp.concatenate([conv_hist.astype(F32), conv3], axis=1)[:, -(CONV_W - 1):]
    q3 = qb.reshape(b, l, DIFF_QK)
    v3 = vb.reshape(b, l, DIFF_V)
    if prompt:
        tk = 2 * LANES
        od = _attn_prompt_call(q3, kf, v3, p, _pick_tile(l, 2 * tk), tk, lam_init)
        k_out = jnp.transpose(kf.reshape(b, DIFF_HEADS, 2, DIFF_DK, l), (0, 4, 1, 2, 3))
    else:
        past = k_hist.shape[1]
        ckt = jnp.transpose(k_hist, (0, 2, 3, 4, 1)).reshape(b, DIFF_HEADS, LANES, past)
        od = _attn_sample_call(q3, pre[8].reshape(b, l, DIFF_QK), v3, ckt,
                               v_hist.reshape(b, past * DIFF_HEADS, DIFF_DV), p, lam_init)
        k_out = kf.reshape(b, l, DIFF_HEADS, 2, DIFF_DK)
    y = _post_call(x1, og.reshape(n, GDN_V), od.reshape(n, DIFF_V), p, tm)
    return (y.reshape(b, l, d), k_out, vf.reshape(b, l, DIFF_HEADS, DIFF_DV), s_new, conv_new)


def kernel(x_prompt, x_sample, cache_k, cache_v, state_gdn, state_conv, ffn1_norm, ffn1_w_gu, ffn1_w_down,
           mix_norm, w_in, conv_w, a_log, dt_bias, gdn_out_norm, q_norm, k_norm, lambda_q, lambda_k,
           diff_out_norm, w_out, ffn2_norm, ffn2_w_gu, ffn2_w_down):
    w = dict(ffn1_norm=ffn1_norm, ffn1_w_gu=ffn1_w_gu, ffn1_w_down=ffn1_w_down, mix_norm=mix_norm, w_in=w_in,
             conv_w=conv_w, a_log=a_log, dt_bias=dt_bias, gdn_out_norm=gdn_out_norm, q_norm=q_norm,
             k_norm=k_norm, lambda_q=lambda_q, lambda_k=lambda_k, diff_out_norm=diff_out_norm, w_out=w_out,
             ffn2_norm=ffn2_norm, ffn2_w_gu=ffn2_w_gu, ffn2_w_down=ffn2_w_down)
    depth = w_in.shape[0]
    pos_p = jnp.arange(x_prompt.shape[1])
    pos_s = cache_k.shape[2] + jnp.arange(x_sample.shape[1])
    hp, hs = x_prompt, x_sample
    outs = [[] for _ in range(8)]
    for l in range(depth):
        lam_init = 0.8 - 0.6 * math.exp(-0.3 * l)
        p = _layer_params(w, l)
        hp, kp, vp, sp, cp = _layer(hp, pos_p, None, None, None, None, p, lam_init)
        hs, ks, vs, ss, cs = _layer(hs, pos_s, cache_k[l], cache_v[l], state_conv[l], state_gdn[l], p, lam_init)
        for acc, val in zip(outs, (kp, vp, sp, cp, ks, vs, ss, cs)):
            acc.append(val)
    return (hp, hs) + tuple(jnp.stack(o) for o in outs)
```

```python
import functools
import math

import jax
import jax.numpy as jnp
from jax import lax
from jax.experimental import pallas as pl
from jax.experimental.pallas import tpu as pltpu

F32 = jnp.float32
BF16 = jnp.bfloat16

EPS = 1e-6
CHUNK = 64
GDN_HEADS = 4
GDN_DK = 128
GDN_DV = 128
CONV_W = 4
DIFF_HEADS = 4
DIFF_DK = 64
DIFF_DV = 128
ROT_DIM = DIFF_DK // 4
ROPE_THETA = 500000.0
GDN_QK = GDN_HEADS * GDN_DK
GDN_V = GDN_HEADS * GDN_DV
CONV_CH = 2 * GDN_QK + GDN_V
DIFF_QK = DIFF_HEADS * 2 * DIFF_DK
DIFF_V = DIFF_HEADS * DIFF_DV

LANES = 128
INV_BLOCK = 16
GDN_TILE = 128
GDN_BLOCK_TILES = 4
VMEM_LIMIT = 56 * 1024 * 1024
Q_SCALE = (DIFF_DK ** -0.5) * math.log2(math.e)


def _dot(a, b):
    return jnp.dot(a, b, preferred_element_type=F32)


def _dot_nt(a, b):
    return lax.dot_general(a, b, (((1,), (1,)), ((), ())), preferred_element_type=F32)


def _rms(x, g):
    return x * lax.rsqrt(jnp.mean(x * x, axis=-1, keepdims=True) + EPS) * g


def _silu(x):
    return x * jax.nn.sigmoid(x)


def _rms_parts(x, g):
    return (x * g).astype(BF16), lax.rsqrt(jnp.mean(x * x, axis=-1, keepdims=True) + EPS)


def _swiglu(xg, r, wgu_ref, wd_ref):
    d_ff = wd_ref.shape[0]
    g = _dot(xg, wgu_ref[:, :d_ff]) * r
    u = _dot(xg, wgu_ref[:, d_ff:]) * r
    act = (_silu(g) * u).astype(BF16)
    return _dot(act, wd_ref[...])


def _const_spec(shape):
    nd = len(shape)
    return pl.BlockSpec(shape, lambda *_: (0,) * nd, pipeline_mode=pl.Buffered(1))


_WREST = dict(z=slice(0, GDN_V), q=slice(GDN_V, GDN_V + DIFF_QK), k=slice(GDN_V + DIFF_QK, GDN_V + 2 * DIFF_QK),
              v=slice(GDN_V + 2 * DIFF_QK, GDN_V + 2 * DIFF_QK + DIFF_V))


def _pre_kernel(x_ref, cos_ref, sa_ref, sb_ref, n1_ref, wgu_ref, wd_ref, nm_ref, wconv_ref, wrest_ref, wba_ref, qn_ref, kn_ref,
                gm_ref,
                x1_ref, conv_ref, z_ref, ba_ref, qb_ref, vb_ref, kf_ref, vf_ref, *maybe_kb_ref,
                k_pos_minor):
    x = x_ref[...]
    xg, r = _rms_parts(x, n1_ref[...])
    x1 = x + 0.5 * _swiglu(xg, r, wgu_ref, wd_ref)
    x1_ref[...] = x1
    h, rh = _rms_parts(x1, nm_ref[...])
    tm = x.shape[0]
    cos = cos_ref[...]
    sa = sa_ref[...]
    sb = sb_ref[...]
    gm = gm_ref[...]

    def norm_rope(t, gw):
        ss = _dot((t * t).astype(BF16), gm)
        t = t * lax.rsqrt(ss * (1.0 / DIFF_DK) + EPS) * gw
        outs = []
        for hh in range(DIFF_HEADS):
            th = t[:, hh * LANES:(hh + 1) * LANES]
            up = pltpu.roll(th, LANES - ROT_DIM // 2, 1)
            dn = pltpu.roll(th, ROT_DIM // 2, 1)
            outs.append(th * cos + up * sa + dn * sb)
        return jnp.concatenate(outs, axis=1)

    k = norm_rope(_dot_nt(h, wrest_ref[_WREST['k'], :]) * rh, kn_ref[...])
    if k_pos_minor:
        kf_ref[...] = k.T.reshape(DIFF_HEADS, LANES, tm)
    else:
        kf_ref[...] = k
        maybe_kb_ref[0][...] = k.astype(BF16)
    q = norm_rope(_dot_nt(h, wrest_ref[_WREST['q'], :]) * rh, qn_ref[...])
    qb_ref[...] = (q * Q_SCALE).astype(BF16)
    v = _dot_nt(h, wrest_ref[_WREST['v'], :]) * rh
    vb_ref[...] = v.astype(BF16)
    if k_pos_minor:
        for hh in range(DIFF_HEADS):
            vf_ref[pl.ds(hh, tm, stride=DIFF_HEADS), :] = v[:, hh * DIFF_DV:(hh + 1) * DIFF_DV]
    else:
        vf_ref[...] = v
    z_ref[...] = _dot_nt(h, wrest_ref[_WREST['z'], :]) * rh
    ba_ref[...] = _dot_nt(h, wba_ref[...]) * rh
    conv_ref[...] = _dot_nt(h, wconv_ref[...]) * rh


def _pre_call(x2d, tabs, p, tm, n_pos_tiles, batch, k_pos_minor):
    n, d = x2d.shape
    grid = (n // tm,)
    row = lambda w: pl.BlockSpec((tm, w), lambda i: (i, 0))
    tab = pl.BlockSpec((tm, LANES), lambda i: (i % n_pos_tiles, 0))
    weights = [p['n1'], p['wgu1'], p['wd1'], p['nm'], p['w_conv_t'], p['w_rest_t'], p['w_ba_t'], p['qn'], p['kn'], p['gmat']]
    in_specs = [row(d), tab, tab, tab] + [_const_spec(w.shape) for w in weights]
    outs = [(d, F32), (CONV_CH, F32), (GDN_V, F32), (2 * LANES, F32), (DIFF_QK, BF16), (DIFF_V, BF16)]
    out_specs = [row(w) for w, _ in outs]
    out_shape = [jax.ShapeDtypeStruct((n, w), dt) for w, dt in outs]
    if k_pos_minor:
        l = n // batch
        tiles = l // tm
        out_specs += [pl.BlockSpec((None, DIFF_HEADS, LANES, tm), lambda i: (i // tiles, 0, 0, i % tiles)),
                      pl.BlockSpec((tm * DIFF_HEADS, DIFF_DV), lambda i: (i, 0))]
        out_shape += [jax.ShapeDtypeStruct((batch, DIFF_HEADS, LANES, l), F32),
                      jax.ShapeDtypeStruct((n * DIFF_HEADS, DIFF_DV), F32)]
    else:
        out_specs += [row(DIFF_QK), row(DIFF_V), row(DIFF_QK)]
        out_shape += [jax.ShapeDtypeStruct((n, DIFF_QK), F32), jax.ShapeDtypeStruct((n, DIFF_V), F32),
                      jax.ShapeDtypeStruct((n, DIFF_QK), BF16)]
    return pl.pallas_call(
        functools.partial(_pre_kernel, k_pos_minor=k_pos_minor),
        grid=grid,
        in_specs=in_specs,
        out_specs=out_specs,
        out_shape=out_shape,
        compiler_params=pltpu.CompilerParams(dimension_semantics=("arbitrary",),
                                             vmem_limit_bytes=VMEM_LIMIT),
        name="pre",
    )(x2d, *tabs, *weights)


def _post_kernel(x1_ref, og_ref, od_ref, wo_ref, n2_ref, wgu_ref, wd_ref, y_ref):
    mixed = jnp.concatenate([og_ref[...], od_ref[...]], axis=1)
    x2 = x1_ref[...] + _dot(mixed, wo_ref[...])
    xg, r = _rms_parts(x2, n2_ref[...])
    y_ref[...] = x2 + 0.5 * _swiglu(xg, r, wgu_ref, wd_ref)


def _post_call(x1, og, od, p, tm):
    n, d = x1.shape
    row = lambda w: pl.BlockSpec((tm, w), lambda i: (i, 0))
    weights = [p['wo'], p['n2'], p['wgu2'], p['wd2']]
    return pl.pallas_call(
        _post_kernel,
        grid=(n // tm,),
        in_specs=[row(d), row(GDN_V), row(DIFF_V)] + [_const_spec(w.shape) for w in weights],
        out_specs=row(d),
        out_shape=jax.ShapeDtypeStruct((n, d), F32),
        compiler_params=pltpu.CompilerParams(dimension_semantics=("arbitrary",),
                                             vmem_limit_bytes=VMEM_LIMIT),
        name="post",
    )(x1, og, od, *weights)


def _exact3(m01, x):
    x1 = x.astype(BF16)
    r1 = x - x1.astype(F32)
    x2 = r1.astype(BF16)
    x3 = (r1 - x2.astype(F32)).astype(BF16)
    return _dot(m01, x1) + (_dot(m01, x2) + _dot(m01, x3))


CONV_PAD = 8


def _gdn_kernel(x_ref, z_ref, ba_ref, hist_ref, s0_ref, cw_ref, alog_ref, dtb_ref, on_ref, hm_ref,
                o_ref, s_out_ref, xp_ref, *maybe_s_ref, tc, tile, c, seqs):
    pad = CONV_PAD
    nchunk = tile // c
    ntile = tc // tile
    assert c % INV_BLOCK == 0 and tile % LANES == 0 and tc % tile == 0
    assert seqs == 1 or (seqs == nchunk and ntile == 1)
    cw = cw_ref[...]
    x = x_ref[...]

    def conv(xp, rows):
        y = xp[rows] * cw[CONV_W - 1:CONV_W, :]
        for k in range(1, CONV_W):
            y = y + pltpu.roll(xp, k, 0)[rows] * cw[CONV_W - 1 - k:CONV_W - k, :]
        return y

    if seqs == 1:
        s_ref, = maybe_s_ref
        t = pl.program_id(1)

        @pl.when(t == 0)
        def _():
            xp_ref[0:pad, :] = jnp.zeros((pad, CONV_CH), F32)
            xp_ref[pad - (CONV_W - 1):pad, :] = hist_ref[...]
            s_ref[...] = s0_ref[...]

        xp_ref[pad:pad + tc, :] = x
        y = conv(xp_ref[...], slice(pad, pad + tc))
        xp_ref[pad - (CONV_W - 1):pad, :] = x[tc - (CONV_W - 1):tc, :]
    else:
        stride = pad + c
        xp_ref[...] = jnp.zeros(xp_ref.shape, F32)
        for b in range(seqs):
            xp_ref[b * stride + pad - (CONV_W - 1):b * stride + pad, :] = hist_ref[b]
            xp_ref[b * stride + pad:(b + 1) * stride, :] = x[b * c:(b + 1) * c, :]
        yp = conv(xp_ref[...], slice(None))
        y = jnp.concatenate([yp[b * stride + pad:(b + 1) * stride, :] for b in range(seqs)], axis=0)
    y = _silu(y)
    hm = hm_ref[...]
    yk = y[:, GDN_QK:2 * GDN_QK]
    yk = yk * lax.rsqrt(_dot((yk * yk).astype(BF16), hm) + EPS)

    ba = ba_ref[...]
    beta = jax.nn.sigmoid(ba[:, :LANES])
    g = -jnp.exp(alog_ref[...]) * jax.nn.softplus(ba[:, LANES:] + dtb_ref[...])
    ri = lax.broadcasted_iota(jnp.int32, (tile, tile), 0)
    ci = lax.broadcasted_iota(jnp.int32, (tile, tile), 1)
    incl = ((ri // c) == (ci // c)) & (ci <= ri)
    same_blk = (ri // INV_BLOCK) == (ci // INV_BLOCK)
    diag = ri == ci
    eye = jnp.where(diag, 1.0, 0.0)
    incl01 = jnp.where(incl, 1.0, 0.0).astype(BF16)
    tile_rows = [slice(ti * tile, (ti + 1) * tile) for ti in range(ntile)]
    gcums = [_exact3(incl01, g[rows]) for rows in tile_rows]
    gcum_ts = [gc.T for gc in gcums]
    on = on_ref[...]

    heads = range(GDN_HEADS)
    units = [(ti, h) for ti in range(ntile) for h in heads]

    def per_head(f, *lists):
        return [f(*args) for args in zip(*lists)]

    def head_cols(h, width):
        return slice(h * width, (h + 1) * width)

    ks = [yk[tile_rows[ti], head_cols(h, GDN_DK)] for ti, h in units]
    vs = [y[tile_rows[ti], 2 * GDN_QK + h * GDN_DV:2 * GDN_QK + (h + 1) * GDN_DV] for ti, h in units]
    gcs = [gcums[ti][:, h:h + 1] for ti, h in units]
    bcs = [beta[tile_rows[ti], h:h + 1] for ti, h in units]
    decs = [jnp.exp(jnp.where(incl, gcs[u] - gcum_ts[ti][h:h + 1, :], -jnp.inf))
            for u, (ti, h) in enumerate(units)]
    kbs = per_head(lambda k: k.astype(BF16), ks)
    a_s = per_head(lambda bc, kb, dec: jnp.where(diag, 0.0, bc * _dot_nt(kb, kb) * dec), bcs, kbs, decs)

    assert INV_BLOCK == 16 and c in (INV_BLOCK, 4 * INV_BLOCK)
    bf = lambda xs: per_head(lambda x: x.astype(BF16), xs)
    mm = lambda xs, ys: per_head(_dot, xs, ys)
    ds = per_head(lambda a: jnp.where(same_blk, a, 0.0), a_s) if c > INV_BLOCK else a_s
    sd = bf(ds)
    d2 = mm(sd, sd)
    s2 = bf(d2)
    d3 = mm(sd, s2)
    d4 = mm(s2, s2)
    s4 = bf(d4)
    n1 = per_head(lambda d, x2, x3: eye - d + x2 - x3, ds, d2, d3)
    n2 = per_head(jnp.add, n1, mm(bf(n1), s4))
    d8 = mm(s4, s4)
    tinv = per_head(jnp.add, n2, mm(bf(n2), bf(d8)))
    if c > INV_BLOCK:
        std = bf(tinv)
        ms = mm(std, bf(per_head(jnp.subtract, a_s, ds)))
        sm = bf(ms)
        m2 = mm(sm, sm)
        m3 = mm(sm, bf(m2))
        tinv = mm(bf(per_head(lambda m, x2, x3: eye - m + x2 - x3, ms, m2, m3)), std)

    egs = per_head(jnp.exp, gcs)
    rhs = per_head(lambda v, k, bc, eg: jnp.concatenate([v * bc, k * (bc * eg)], axis=1), vs, ks, bcs, egs)
    sols = mm(bf(tinv), bf(rhs))
    yq = y[:, :GDN_QK]
    yq = yq * (lax.rsqrt(_dot((yq * yq).astype(BF16), hm) + EPS) * (GDN_DK ** -0.5))
    qs = [yq[tile_rows[ti], head_cols(h, GDN_DK)] for ti, h in units]
    qks = per_head(lambda q, kb, dec: (_dot_nt(q.astype(BF16), kb) * dec).astype(BF16), qs, kbs, decs)
    u0s = [sol[:, :GDN_DV] for sol in sols]
    wbs = [sol[:, GDN_DV:].astype(BF16) for sol in sols]
    qds = per_head(lambda q, eg: (q * eg).astype(BF16), qs, egs)

    unit = lambda ti, h: ti * GDN_HEADS + h

    def chunk_terms(ti, ic):
        rs = slice(ic * c, (ic + 1) * c)
        gends = [gcs[unit(ti, h)][(ic + 1) * c - 1:(ic + 1) * c, :] for h in heads]
        kds = [(ks[unit(ti, h)][rs] * jnp.exp(gends[h] - gcs[unit(ti, h)][rs])).astype(BF16) for h in heads]
        return rs, gends, kds

    def state_terms(rs, states):
        return [_dot(jnp.concatenate([wbs[h][rs], qds[h][rs]], axis=0), states[h].astype(BF16)) for h in heads]

    def next_state(states, gends, kds, us_c):
        return [states[h] * jnp.exp(gends[h]) + lax.dot_general(
            kds[h], us_c[h], (((0,), (0,)), ((), ())), preferred_element_type=F32) for h in heads]

    us = [[] for _ in units]
    outs = [[] for _ in units]
    if seqs == 1:
        trans = []
        for ti in range(ntile):
            for ic in range(nchunk):
                rs, gends, kds = chunk_terms(ti, ic)
                kt_wu = [lax.dot_general(
                    kds[h], jnp.concatenate([wbs[unit(ti, h)][rs], u0s[unit(ti, h)][rs].astype(BF16)], axis=1),
                    (((0,), (0,)), ((), ())), preferred_element_type=F32) for h in heads]
                trans.append(([jnp.exp(g) for g in gends], [m[:, :GDN_DK].astype(BF16) for m in kt_wu],
                              [m[:, GDN_DK:] for m in kt_wu]))
        states = [s_ref[h] for h in heads]
        chunk_states = []
        for decay, kt_w, kt_u0 in trans:
            sb = [s.astype(BF16) for s in states]
            chunk_states.append(sb)
            states = [states[h] * decay[h] - _dot(kt_w[h], sb[h]) + kt_u0[h] for h in heads]
        for h in heads:
            s_ref[h] = states[h]
        for ti in range(ntile):
            for ic in range(nchunk):
                rs = slice(ic * c, (ic + 1) * c)
                sb = chunk_states[ti * nchunk + ic]
                wss = [_dot(jnp.concatenate([wbs[unit(ti, h)][rs], qds[unit(ti, h)][rs]], axis=0), sb[h])
                       for h in heads]
                for h in heads:
                    us[unit(ti, h)].append((u0s[unit(ti, h)][rs] - wss[h][:c]).astype(BF16))
                    outs[unit(ti, h)].append(wss[h][c:])
        o = [jnp.concatenate(outs[u], axis=0) + _dot(qks[u], jnp.concatenate(us[u], axis=0))
             for u in range(len(units))]

        @pl.when(t == pl.num_programs(1) - 1)
        def _():
            s_out_ref[...] = s_ref[...]
    else:
        for ic in range(nchunk):
            rs, gends, kds = chunk_terms(0, ic)
            states = [s0_ref[ic, h] for h in heads]
            wss = state_terms(rs, states)
            us_c = [(u0s[h][rs] - wss[h][:c]).astype(BF16) for h in heads]
            new = next_state(states, gends, kds, us_c)
            for h in heads:
                us[h].append(us_c[h])
                outs[h].append(wss[h][c:])
                s_out_ref[ic, h] = new[h]
        o = [jnp.concatenate(outs[h], axis=0) + _dot(qks[h], jnp.concatenate(us[h], axis=0)) for h in heads]
    for u, (ti, h) in enumerate(units):
        rows, cols = tile_rows[ti], head_cols(h, GDN_DV)
        o_ref[rows, cols] = (_rms(o[u], on) * _silu(z_ref[rows, cols])).astype(o_ref.dtype)


def _gdn_call(conv, z, ba, hist, s0, p, tile, c):
    b, l, _ = conv.shape
    small = [p['conv_w'], p['a_log'], p['dt_bias'], p['gdn_on'], p['head_mat']]
    state = (GDN_HEADS, GDN_DK, GDN_DV)
    if l >= tile:
        tc = _pick_tile(l, GDN_BLOCK_TILES * tile)
        seqs = 1
        grid = (b, l // tc)
        row = lambda w: pl.BlockSpec((None, tc, w), lambda i, j: (i, j, 0))
        per_b = lambda shape: pl.BlockSpec((None,) + shape, lambda i, j: (i,) + (0,) * len(shape))
        scratch = [pltpu.VMEM((tc + CONV_PAD, CONV_CH), F32), pltpu.VMEM(state, F32)]
    else:
        tc = tile = b * l
        assert l == c and tc % LANES == 0
        seqs = b
        grid = (1, 1)
        conv, z, ba = (t.reshape(1, tc, t.shape[-1]) for t in (conv, z, ba))
        row = lambda w: pl.BlockSpec((None, tc, w), lambda i, j: (0, 0, 0))
        per_b = lambda shape: pl.BlockSpec((b,) + shape, lambda i, j: (0,) * (len(shape) + 1))
        scratch = [pltpu.VMEM((seqs * (c + CONV_PAD), CONV_CH), F32)]
    o, s_new = pl.pallas_call(
        functools.partial(_gdn_kernel, tc=tc, tile=tile, c=c, seqs=seqs),
        grid=grid,
        in_specs=[row(CONV_CH), row(GDN_V), row(2 * LANES), per_b((CONV_W - 1, CONV_CH)), per_b(state)] + [
            pl.BlockSpec(w.shape, lambda i, j: (0, 0)) for w in small],
        out_specs=[row(GDN_V), per_b(state)],
        out_shape=[jax.ShapeDtypeStruct(conv.shape[:2] + (GDN_V,), BF16),
                   jax.ShapeDtypeStruct((b,) + state, F32)],
        scratch_shapes=scratch,
        compiler_params=pltpu.CompilerParams(dimension_semantics=("arbitrary", "arbitrary"),
                                             vmem_limit_bytes=VMEM_LIMIT),
        name="gdn",
    )(conv, z, ba, hist, s0, *small)
    return o.reshape(b, l, GDN_V), s_new


def _stack_maps(q):
    lane = lax.broadcasted_iota(jnp.int32, q.shape, 1)
    zero = jnp.zeros_like(q)
    return jnp.concatenate([jnp.where(lane < DIFF_DK, q, zero), jnp.where(lane >= DIFF_DK, q, zero)],
                           axis=0)


def _diff_finish(acc, l, t, lq_ref, lk_ref, dn_ref, lam_init):
    lam_e = jnp.exp(jnp.sum(lq_ref[...] * lk_ref[...], axis=-1, keepdims=True))
    lam = lam_e[0:1] - lam_e[1:2] + lam_init
    o = acc[:t] / l[:t] - lam * (acc[t:] / l[t:])
    return _rms(o, dn_ref[...]) * (1.0 - lam_init)


def _halves_max(s):
    return jnp.maximum(s[:, :LANES], s[:, LANES:])


def _attn_prompt_kernel(q_ref, kt_ref, v_ref, bias_ref, lq_ref, lk_ref, dn_ref, o_ref,
                        kt_scr, vx_scr, s_scr, m_scr, acc_scr, *, tq, tk, lam_init):
    i = pl.program_id(2)
    assert tk == 2 * LANES and tq % tk == 0
    nsub = tq // tk

    @pl.when(i == 0)
    def _():
        for j in range(kt_scr.shape[0]):
            kt_scr[j] = kt_ref[:, j * tk:(j + 1) * tk].astype(BF16)
        vx_scr[:, :DIFF_DV] = v_ref[...]
        vx_scr[:, DIFF_DV:] = jnp.ones((vx_scr.shape[0], LANES), BF16)

    qq = _stack_maps(q_ref[...])

    def scores(j):
        return _dot(qq, kt_scr[j])

    def weighted_values(ss, blocks, mm):
        p = [jnp.exp2(s[:, half * LANES:(half + 1) * LANES] - mm) for s in ss for half in range(tk // LANES)]
        vx = [vx_scr[pl.ds(pl.multiple_of(j * tk, tk), tk), :] for j in blocks]
        return _dot(jnp.concatenate(p, axis=1).astype(BF16), jnp.concatenate(vx, axis=0))

    def pass1(blocks, biases):
        ss = [scores(j) if b is None else scores(j) + b for j, b in zip(blocks, biases)]
        for j, s in zip(blocks, ss):
            s_scr[j] = s
        m_scr[...] = jnp.maximum(m_scr[...], functools.reduce(jnp.maximum, [_halves_max(s) for s in ss]))

    def pass2(blocks, biases):
        del biases
        acc_scr[...] += weighted_values([s_scr[j] for j in blocks], blocks, m_scr[...])

    def over_visible(body):
        def group(g):
            return [g * nsub + u for u in range(nsub)]

        def trip(t, carry):
            second = group(2 * t + 1)
            body(group(2 * t) + second,
                 [None] * nsub + [bias_ref[jnp.maximum(j - i * nsub + 1, 0)] for j in second])
            return carry

        lax.fori_loop(0, (i + 1) // 2, trip, 0)

        @pl.when(i % 2 == 0)
        def _():
            body(group(i), [bias_ref[1 + d] for d in range(nsub)])

    m_scr[...] = jnp.full(m_scr.shape, -jnp.inf, F32)
    over_visible(pass1)
    m_scr[...] = jnp.broadcast_to(jnp.max(m_scr[...], axis=-1, keepdims=True), m_scr.shape)
    acc_scr[...] = jnp.zeros(acc_scr.shape, F32)
    over_visible(pass2)
    acc = acc_scr[...]
    o_ref[...] = _diff_finish(acc[:, :DIFF_DV], acc[:, DIFF_DV:], tq, lq_ref, lk_ref, dn_ref,
                              lam_init).astype(o_ref.dtype)


def _attn_prompt_call(qb, kt, vb, p, tq, tk, lam_init):
    b, l, _ = qb.shape
    small = [p['lambda_q'], p['lambda_k'], p['diff_on']]
    r = (jnp.arange(2 * tq) % tq)[None, :, None] // CHUNK
    cidx = (jnp.arange(tk)[None, None, :] + tk * jnp.arange(tq // tk)[:, None, None]) // CHUNK
    bias = jnp.where(cidx <= r, 0.0, -jnp.inf).astype(F32)
    bias = jnp.concatenate([jnp.zeros_like(bias[:1]), bias], axis=0)
    return pl.pallas_call(
        functools.partial(_attn_prompt_kernel, tq=tq, tk=tk, lam_init=lam_init),
        grid=(b, DIFF_HEADS, l // tq),
        in_specs=[pl.BlockSpec((None, tq, LANES), lambda bi, h, i: (bi, i, h)),
                  pl.BlockSpec((None, None, LANES, l), lambda bi, h, i: (bi, h, 0, 0)),
                  pl.BlockSpec((None, l, LANES), lambda bi, h, i: (bi, 0, h)),
                  _const_spec(bias.shape)] + [
                      pl.BlockSpec(w.shape, lambda bi, h, i: (0, 0)) for w in small],
        out_specs=pl.BlockSpec((None, tq, LANES), lambda bi, h, i: (bi, i, h)),
        out_shape=jax.ShapeDtypeStruct((b, l, DIFF_V), BF16),
        scratch_shapes=[pltpu.VMEM((l // tk, LANES, tk), BF16),
                        pltpu.VMEM((l, DIFF_DV + LANES), BF16),
                        pltpu.VMEM((l // tk, 2 * tq, tk), F32),
                        pltpu.VMEM((2 * tq, LANES), F32),
                        pltpu.VMEM((2 * tq, DIFF_DV + LANES), F32)],
        compiler_params=pltpu.CompilerParams(
            dimension_semantics=("arbitrary", "arbitrary", "arbitrary"),
            vmem_limit_bytes=VMEM_LIMIT),
        name="attn_prompt",
    )(qb, kt, vb, bias, *small)


def _attn_sample_kernel(q_ref, kn_ref, vn_ref, ckt_ref, cv_ref, lq_ref, lk_ref, dn_ref, o_ref,
                        *, t, lam_init):
    past = ckt_ref.shape[-1]
    for h in range(DIFF_HEADS):
        cols = slice(h * LANES, (h + 1) * LANES)
        qq = _stack_maps(q_ref[:, cols])
        s_c = _dot(qq, ckt_ref[h].astype(BF16))
        s_n = _dot_nt(qq, kn_ref[:, cols])
        m = jnp.maximum(jnp.max(s_c, axis=-1, keepdims=True), jnp.max(s_n, axis=-1, keepdims=True))
        p_c = jnp.exp2(s_c - m)
        p_n = jnp.exp2(s_n - m)
        l = jnp.sum(p_c, axis=-1, keepdims=True) + jnp.sum(p_n, axis=-1, keepdims=True)
        cv = cv_ref[pl.ds(h, past, stride=DIFF_HEADS), :].astype(BF16)
        acc = _dot(p_c.astype(BF16), cv) + _dot(p_n.astype(BF16), vn_ref[:, cols])
        o_ref[:, cols] = _diff_finish(acc, l, t, lq_ref, lk_ref, dn_ref, lam_init).astype(o_ref.dtype)


def _attn_sample_call(qb, kb, vb, ckt, cv, p, lam_init):
    b, t, _ = qb.shape
    past = ckt.shape[-1]
    small = [p['lambda_q'], p['lambda_k'], p['diff_on']]
    new = pl.BlockSpec((None, t, DIFF_V), lambda bi: (bi, 0, 0))
    return pl.pallas_call(
        functools.partial(_attn_sample_kernel, t=t, lam_init=lam_init),
        grid=(b,),
        in_specs=[new, new, new,
                  pl.BlockSpec((None, DIFF_HEADS, LANES, past), lambda bi: (bi, 0, 0, 0)),
                  pl.BlockSpec((None, past * DIFF_HEADS, DIFF_DV), lambda bi: (bi, 0, 0))] + [
                      pl.BlockSpec(w.shape, lambda bi: (0, 0)) for w in small],
        out_specs=new,
        out_shape=jax.ShapeDtypeStruct((b, t, DIFF_V), BF16),
        compiler_params=pltpu.CompilerParams(dimension_semantics=("arbitrary",),
                                             vmem_limit_bytes=VMEM_LIMIT),
        name="attn_sample",
    )(qb, kb, vb, ckt, cv, *small)


def _rope_tables(pos):
    half = ROT_DIM // 2
    inv = jnp.float32(ROPE_THETA) ** (-jnp.arange(half, dtype=F32) * 2.0 / ROT_DIM)
    ang = pos.astype(F32)[:, None] * inv[None, :]
    cos, sin = jnp.cos(ang), jnp.sin(ang)
    n = pos.shape[0]
    rest = DIFF_DK - ROT_DIM
    one = jnp.ones((n, rest), F32)
    zero = jnp.zeros((n, rest), F32)
    zh = jnp.zeros((n, half), F32)
    c64 = jnp.concatenate([cos, cos, one], axis=1)
    a64 = jnp.concatenate([-sin, zh, zero], axis=1)
    b64 = jnp.concatenate([zh, sin, zero], axis=1)
    rep = LANES // DIFF_DK
    return tuple(jnp.tile(t, (1, rep)) for t in (c64, a64, b64))


def _layer_params(w, l):
    d = w['w_in'].shape[1]
    win = jnp.transpose(w['w_in'][l])
    w_b = win[CONV_CH:CONV_CH + GDN_HEADS]
    w_a = win[CONV_CH + GDN_HEADS:CONV_CH + 2 * GDN_HEADS]
    zpad = jnp.zeros((LANES - GDN_HEADS, d), win.dtype)
    hpad = lambda v: jnp.concatenate([v.astype(F32), jnp.zeros((LANES - GDN_HEADS,), F32)])[None, :]
    grp = jnp.arange(DIFF_QK) // DIFF_DK
    bf = lambda t: t.astype(BF16)
    r2 = lambda v: v.astype(F32)[None, :]
    return dict(
        n1=r2(w['ffn1_norm'][l]), wgu1=bf(w['ffn1_w_gu'][l]), wd1=bf(w['ffn1_w_down'][l]),
        nm=r2(w['mix_norm'][l]), w_conv_t=bf(win[:CONV_CH]), w_rest_t=bf(win[CONV_CH + 2 * GDN_HEADS:]),
        w_ba_t=bf(jnp.concatenate([w_b, zpad, w_a, zpad], axis=0)),
        qn=r2(jnp.tile(w['q_norm'][l], DIFF_QK // DIFF_DK)), kn=r2(jnp.tile(w['k_norm'][l], DIFF_QK // DIFF_DK)),
        gmat=(grp[:, None] == grp[None, :]).astype(BF16),
        head_mat=(grp[:, None] // 2 == grp[None, :] // 2).astype(BF16),
        conv_w=w['conv_w'][l].astype(F32), a_log=hpad(w['a_log'][l]), dt_bias=hpad(w['dt_bias'][l]),
        gdn_on=r2(w['gdn_out_norm'][l]),
        lambda_q=w['lambda_q'][l].astype(F32), lambda_k=w['lambda_k'][l].astype(F32),
        diff_on=r2(w['diff_out_norm'][l]),
        wo=bf(w['w_out'][l]), n2=r2(w['ffn2_norm'][l]), wgu2=bf(w['ffn2_w_gu'][l]), wd2=bf(w['ffn2_w_down'][l]),
    )


def _pick_tile(n, pref):
    t = min(n, pref)
    assert n % t == 0
    return t


def _layer(x, pos, k_hist, v_hist, conv_hist, s0, p, lam_init):
    b, l, d = x.shape
    n = b * l
    tm = _pick_tile(n, 256)
    tabs = _rope_tables(pos)
    if l >= tm:
        assert l % tm == 0
        n_pos_tiles = l // tm
    else:
        assert tm % l == 0
        tabs = tuple(jnp.tile(t, (tm // l, 1)) for t in tabs)
        n_pos_tiles = 1
    prompt = k_hist is None
    pre = _pre_call(x.reshape(n, d), tabs, p, tm, n_pos_tiles, b, prompt)
    x1, conv, z, ba, qb, vb, kf, vf = pre[:8]
    c = min(CHUNK, l)
    assert l % c == 0
    tc = GDN_TILE
    assert l % tc == 0 or l == c
    if conv_hist is None:
        conv_hist = jnp.zeros((b, CONV_W - 1, CONV_CH), F32)
        s0 = jnp.zeros((b, GDN_HEADS, GDN_DK, GDN_DV), F32)
    conv3 = conv.reshape(b, l, CONV_CH)
    og, s_new = _gdn_call(conv3, z.reshape(b, l, GDN_V), ba.reshape(b, l, 2 * LANES),
                          conv_hist.astype(F32), s0.astype(F32), p, tc, c)
    conv_new = jnp.concatenate([conv_hist.astype(F32), conv3], axis=1)[:, -(CONV_W - 1):]
    q3 = qb.reshape(b, l, DIFF_QK)
    v3 = vb.reshape(b, l, DIFF_V)
    if prompt:
        tk = 2 * LANES
        od = _attn_prompt_call(q3, kf, v3, p, _pick_tile(l, 2 * tk), tk, lam_init)
        k_out = jnp.transpose(kf.reshape(b, DIFF_HEADS, 2, DIFF_DK, l), (0, 4, 1, 2, 3))
    else:
        past = k_hist.shape[1]
        ckt = jnp.transpose(k_hist, (0, 2, 3, 4, 1)).reshape(b, DIFF_HEADS, LANES, past)
        od = _attn_sample_call(q3, pre[8].reshape(b, l, DIFF_QK), v3, ckt,
                               v_hist.reshape(b, past * DIFF_HEADS, DIFF_DV), p, lam_init)
        k_out = kf.reshape(b, l, DIFF_HEADS, 2, DIFF_DK)
    y = _post_call(x1, og.reshape(n, GDN_V), od.reshape(n, DIFF_V), p, tm)
    return (y.reshape(b, l, d), k_out, vf.reshape(b, l, DIFF_HEADS, DIFF_DV), s_new, conv_new)


def kernel(x_prompt, x_sample, cache_k, cache_v, state_gdn, state_conv, ffn1_norm, ffn1_w_gu, ffn1_w_down,
           mix_norm, w_in, conv_w, a_log, dt_bias, gdn_out_norm, q_norm, k_norm, lambda_q, lambda_k,
           diff_out_norm, w_out, ffn2_norm, ffn2_w_gu, ffn2_w_down):
    w = dict(ffn1_norm=ffn1_norm, ffn1_w_gu=ffn1_w_gu, ffn1_w_down=ffn1_w_down, mix_norm=mix_norm, w_in=w_in,
             conv_w=conv_w, a_log=a_log, dt_bias=dt_bias, gdn_out_norm=gdn_out_norm, q_norm=q_norm,
             k_norm=k_norm, lambda_q=lambda_q, lambda_k=lambda_k, diff_out_norm=diff_out_norm, w_out=w_out,
             ffn2_norm=ffn2_norm, ffn2_w_gu=ffn2_w_gu, ffn2_w_down=ffn2_w_down)
    depth = w_in.shape[0]
    pos_p = jnp.arange(x_prompt.shape[1])
    pos_s = cache_k.shape[2] + jnp.arange(x_sample.shape[1])
    hp, hs = x_prompt, x_sample
    outs = [[] for _ in range(8)]
    for l in range(depth):
        lam_init = 0.8 - 0.6 * math.exp(-0.3 * l)
        p = _layer_params(w, l)
        hp, kp, vp, sp, cp = _layer(hp, pos_p, None, None, None, None, p, lam_init)
        hs, ks, vs, ss, cs = _layer(hs, pos_s, cache_k[l], cache_v[l], state_conv[l], state_gdn[l], p, lam_init)
        for acc, val in zip(outs, (kp, vp, sp, cp, ks, vs, ss, cs)):
            acc.append(val)
    return (hp, hs) + tuple(jnp.stack(o) for o in outs)
```

```python
import functools
import math

import jax
import jax.numpy as jnp
from jax import lax
from jax.experimental import pallas as pl
from jax.experimental.pallas import tpu as pltpu

F32 = jnp.float32
BF16 = jnp.bfloat16

EPS = 1e-6
CHUNK = 64
GDN_HEADS = 4
GDN_DK = 128
GDN_DV = 128
CONV_W = 4
DIFF_HEADS = 4
DIFF_DK = 64
DIFF_DV = 128
ROT_DIM = DIFF_DK // 4
ROPE_THETA = 500000.0
GDN_QK = GDN_HEADS * GDN_DK
GDN_V = GDN_HEADS * GDN_DV
CONV_CH = 2 * GDN_QK + GDN_V
DIFF_QK = DIFF_HEADS * 2 * DIFF_DK
DIFF_V = DIFF_HEADS * DIFF_DV

LANES = 128
INV_BLOCK = 16
POST_TILE = 512
GDN_TILE = 128
GDN_BLOCK_TILES = 4
VMEM_LIMIT = 60 * 1024 * 1024
Q_SCALE = (DIFF_DK ** -0.5) * math.log2(math.e)


def _dot(a, b):
    return jnp.dot(a, b, preferred_element_type=F32)


def _dot_nt(a, b):
    return lax.dot_general(a, b, (((1,), (1,)), ((), ())), preferred_element_type=F32)


def _rms(x, g):
    return x * lax.rsqrt(jnp.mean(x * x, axis=-1, keepdims=True) + EPS) * g


def _silu(x):
    return x * jax.nn.sigmoid(x)


def _rms_parts(x, g):
    return (x * g).astype(BF16), lax.rsqrt(jnp.mean(x * x, axis=-1, keepdims=True) + EPS)


def _swiglu(xg, r, wgu_ref, wd_ref):
    d_ff = wd_ref.shape[0]
    g = _dot(xg, wgu_ref[:, :d_ff]) * r
    u = _dot(xg, wgu_ref[:, d_ff:]) * r
    act = (_silu(g) * u).astype(BF16)
    return _dot(act, wd_ref[...])


def _const_spec(shape):
    nd = len(shape)
    return pl.BlockSpec(shape, lambda *_: (0,) * nd, pipeline_mode=pl.Buffered(1))


_WREST = dict(z=slice(0, GDN_V), q=slice(GDN_V, GDN_V + DIFF_QK), k=slice(GDN_V + DIFF_QK, GDN_V + 2 * DIFF_QK),
              v=slice(GDN_V + 2 * DIFF_QK, GDN_V + 2 * DIFF_QK + DIFF_V))


def _pre_kernel(x_ref, cos_ref, sa_ref, sb_ref, n1_ref, wgu_ref, wd_ref, nm_ref, wconv_ref, wrest_ref, wba_ref, qn_ref, kn_ref,
                gm_ref,
                x1_ref, conv_ref, z_ref, ba_ref, qb_ref, vb_ref, kf_ref, vf_ref, *maybe_kb_ref,
                k_pos_minor):
    x = x_ref[...]
    xg, r = _rms_parts(x, n1_ref[...])
    x1 = x + 0.5 * _swiglu(xg, r, wgu_ref, wd_ref)
    x1_ref[...] = x1
    h, rh = _rms_parts(x1, nm_ref[...])
    tm = x.shape[0]
    cos = cos_ref[...]
    sa = sa_ref[...]
    sb = sb_ref[...]
    gm = gm_ref[...]

    def norm_rope(t, gw):
        ss = _dot((t * t).astype(BF16), gm)
        t = t * lax.rsqrt(ss * (1.0 / DIFF_DK) + EPS) * gw
        outs = []
        for hh in range(DIFF_HEADS):
            th = t[:, hh * LANES:(hh + 1) * LANES]
            up = pltpu.roll(th, LANES - ROT_DIM // 2, 1)
            dn = pltpu.roll(th, ROT_DIM // 2, 1)
            outs.append(th * cos + up * sa + dn * sb)
        return jnp.concatenate(outs, axis=1)

    k = norm_rope(_dot_nt(h, wrest_ref[_WREST['k'], :]) * rh, kn_ref[...])
    if k_pos_minor:
        kf_ref[...] = k.T.reshape(DIFF_HEADS, LANES, tm)
    else:
        kf_ref[...] = k
        maybe_kb_ref[0][...] = k.astype(BF16)
    q = norm_rope(_dot_nt(h, wrest_ref[_WREST['q'], :]) * rh, qn_ref[...])
    qb_ref[...] = (q * Q_SCALE).astype(BF16)
    v = _dot_nt(h, wrest_ref[_WREST['v'], :]) * rh
    vb_ref[...] = v.astype(BF16)
    if k_pos_minor:
        for hh in range(DIFF_HEADS):
            vf_ref[pl.ds(hh, tm, stride=DIFF_HEADS), :] = v[:, hh * DIFF_DV:(hh + 1) * DIFF_DV]
    else:
        vf_ref[...] = v
    z_ref[...] = _dot_nt(h, wrest_ref[_WREST['z'], :]) * rh
    ba_ref[...] = _dot_nt(h, wba_ref[...]) * rh
    conv_ref[...] = _dot_nt(h, wconv_ref[...]) * rh


def _pre_call(x2d, tabs, p, tm, n_pos_tiles, batch, k_pos_minor):
    n, d = x2d.shape
    grid = (n // tm,)
    row = lambda w: pl.BlockSpec((tm, w), lambda i: (i, 0))
    tab = pl.BlockSpec((tm, LANES), lambda i: (i % n_pos_tiles, 0))
    weights = [p['n1'], p['wgu1'], p['wd1'], p['nm'], p['w_conv_t'], p['w_rest_t'], p['w_ba_t'], p['qn'], p['kn'], p['gmat']]
    in_specs = [row(d), tab, tab, tab] + [_const_spec(w.shape) for w in weights]
    outs = [(d, F32), (CONV_CH, F32), (GDN_V, F32), (2 * LANES, F32), (DIFF_QK, BF16), (DIFF_V, BF16)]
    out_specs = [row(w) for w, _ in outs]
    out_shape = [jax.ShapeDtypeStruct((n, w), dt) for w, dt in outs]
    if k_pos_minor:
        l = n // batch
        tiles = l // tm
        out_specs += [pl.BlockSpec((None, DIFF_HEADS, LANES, tm), lambda i: (i // tiles, 0, 0, i % tiles)),
                      pl.BlockSpec((tm * DIFF_HEADS, DIFF_DV), lambda i: (i, 0))]
        out_shape += [jax.ShapeDtypeStruct((batch, DIFF_HEADS, LANES, l), F32),
                      jax.ShapeDtypeStruct((n * DIFF_HEADS, DIFF_DV), F32)]
    else:
        out_specs += [row(DIFF_QK), row(DIFF_V), row(DIFF_QK)]
        out_shape += [jax.ShapeDtypeStruct((n, DIFF_QK), F32), jax.ShapeDtypeStruct((n, DIFF_V), F32),
                      jax.ShapeDtypeStruct((n, DIFF_QK), BF16)]
    return pl.pallas_call(
        functools.partial(_pre_kernel, k_pos_minor=k_pos_minor),
        grid=grid,
        in_specs=in_specs,
        out_specs=out_specs,
        out_shape=out_shape,
        compiler_params=pltpu.CompilerParams(dimension_semantics=("arbitrary",),
                                             vmem_limit_bytes=VMEM_LIMIT),
        name="pre",
    )(x2d, *tabs, *weights)


def _post_kernel(x1_ref, og_ref, od_ref, wo_ref, n2_ref, wgu_ref, wd_ref, y_ref):
    mixed = jnp.concatenate([og_ref[...], od_ref[...]], axis=1)
    x2 = x1_ref[...] + _dot(mixed, wo_ref[...])
    xg, r = _rms_parts(x2, n2_ref[...])
    y_ref[...] = x2 + 0.5 * _swiglu(xg, r, wgu_ref, wd_ref)


def _post_call(x1, og, od, p, tm):
    n, d = x1.shape
    row = lambda w: pl.BlockSpec((tm, w), lambda i: (i, 0))
    weights = [p['wo'], p['n2'], p['wgu2'], p['wd2']]
    return pl.pallas_call(
        _post_kernel,
        grid=(n // tm,),
        in_specs=[row(d), row(GDN_V), row(DIFF_V)] + [_const_spec(w.shape) for w in weights],
        out_specs=row(d),
        out_shape=jax.ShapeDtypeStruct((n, d), F32),
        compiler_params=pltpu.CompilerParams(dimension_semantics=("arbitrary",),
                                             vmem_limit_bytes=VMEM_LIMIT),
        name="post",
    )(x1, og, od, *weights)


def _exact3(m01, x):
    x1 = x.astype(BF16)
    r1 = x - x1.astype(F32)
    x2 = r1.astype(BF16)
    x3 = (r1 - x2.astype(F32)).astype(BF16)
    return _dot(m01, x1) + (_dot(m01, x2) + _dot(m01, x3))


CONV_PAD = 8


def _gdn_kernel(x_ref, z_ref, ba_ref, hist_ref, s0_ref, cw_ref, alog_ref, dtb_ref, on_ref, hm_ref,
                o_ref, s_out_ref, xp_ref, *maybe_s_ref, tc, tile, c, seqs):
    pad = CONV_PAD
    nchunk = tile // c
    ntile = tc // tile
    assert c % INV_BLOCK == 0 and tile % LANES == 0 and tc % tile == 0
    assert seqs == 1 or (seqs == nchunk and ntile == 1)
    cw = cw_ref[...]
    x = x_ref[...]

    def conv(xp, rows):
        y = xp[rows] * cw[CONV_W - 1:CONV_W, :]
        for k in range(1, CONV_W):
            y = y + pltpu.roll(xp, k, 0)[rows] * cw[CONV_W - 1 - k:CONV_W - k, :]
        return y

    if seqs == 1:
        s_ref, = maybe_s_ref
        t = pl.program_id(1)

        @pl.when(t == 0)
        def _():
            xp_ref[0:pad, :] = jnp.zeros((pad, CONV_CH), F32)
            xp_ref[pad - (CONV_W - 1):pad, :] = hist_ref[...]
            s_ref[...] = s0_ref[...]

        xp_ref[pad:pad + tc, :] = x
        y = conv(xp_ref[...], slice(pad, pad + tc))
        xp_ref[pad - (CONV_W - 1):pad, :] = x[tc - (CONV_W - 1):tc, :]
    else:
        stride = pad + c
        xp_ref[...] = jnp.zeros(xp_ref.shape, F32)
        for b in range(seqs):
            xp_ref[b * stride + pad - (CONV_W - 1):b * stride + pad, :] = hist_ref[b]
            xp_ref[b * stride + pad:(b + 1) * stride, :] = x[b * c:(b + 1) * c, :]
        yp = conv(xp_ref[...], slice(None))
        y = jnp.concatenate([yp[b * stride + pad:(b + 1) * stride, :] for b in range(seqs)], axis=0)
    y = _silu(y)
    hm = hm_ref[...]
    yk = y[:, GDN_QK:2 * GDN_QK]
    yk = yk * lax.rsqrt(_dot((yk * yk).astype(BF16), hm) + EPS)

    ba = ba_ref[...]
    beta = jax.nn.sigmoid(ba[:, :LANES])
    g = -jnp.exp(alog_ref[...]) * jax.nn.softplus(ba[:, LANES:] + dtb_ref[...])
    ri = lax.broadcasted_iota(jnp.int32, (tile, tile), 0)
    ci = lax.broadcasted_iota(jnp.int32, (tile, tile), 1)
    incl = ((ri // c) == (ci // c)) & (ci <= ri)
    same_blk = (ri // INV_BLOCK) == (ci // INV_BLOCK)
    diag = ri == ci
    eye = jnp.where(diag, 1.0, 0.0)
    incl01 = jnp.where(incl, 1.0, 0.0).astype(BF16)
    tile_rows = [slice(ti * tile, (ti + 1) * tile) for ti in range(ntile)]
    gcums = [_exact3(incl01, g[rows]) for rows in tile_rows]
    gcum_ts = [gc.T for gc in gcums]
    on = on_ref[...]

    heads = range(GDN_HEADS)
    units = [(ti, h) for ti in range(ntile) for h in heads]

    def per_head(f, *lists):
        return [f(*args) for args in zip(*lists)]

    def head_cols(h, width):
        return slice(h * width, (h + 1) * width)

    ks = [yk[tile_rows[ti], head_cols(h, GDN_DK)] for ti, h in units]
    vs = [y[tile_rows[ti], 2 * GDN_QK + h * GDN_DV:2 * GDN_QK + (h + 1) * GDN_DV] for ti, h in units]
    gcs = [gcums[ti][:, h:h + 1] for ti, h in units]
    bcs = [beta[tile_rows[ti], h:h + 1] for ti, h in units]
    decs = [jnp.exp(jnp.where(incl, gcs[u] - gcum_ts[ti][h:h + 1, :], -jnp.inf))
            for u, (ti, h) in enumerate(units)]
    kbs = per_head(lambda k: k.astype(BF16), ks)
    a_s = per_head(lambda bc, kb, dec: jnp.where(diag, 0.0, bc * _dot_nt(kb, kb) * dec), bcs, kbs, decs)

    assert INV_BLOCK == 16 and c in (INV_BLOCK, 4 * INV_BLOCK)
    bf = lambda xs: per_head(lambda x: x.astype(BF16), xs)
    mm = lambda xs, ys: per_head(_dot, xs, ys)
    ds = per_head(lambda a: jnp.where(same_blk, a, 0.0), a_s) if c > INV_BLOCK else a_s
    sd = bf(ds)
    d2 = mm(sd, sd)
    s2 = bf(d2)
    d3 = mm(sd, s2)
    d4 = mm(s2, s2)
    s4 = bf(d4)
    n1 = per_head(lambda d, x2, x3: eye - d + x2 - x3, ds, d2, d3)
    n2 = per_head(jnp.add, n1, mm(bf(n1), s4))
    d8 = mm(s4, s4)
    tinv = per_head(jnp.add, n2, mm(bf(n2), bf(d8)))
    if c > INV_BLOCK:
        std = bf(tinv)
        ms = mm(std, bf(per_head(jnp.subtract, a_s, ds)))
        sm = bf(ms)
        m2 = mm(sm, sm)
        m3 = mm(sm, bf(m2))
        tinv = mm(bf(per_head(lambda m, x2, x3: eye - m + x2 - x3, ms, m2, m3)), std)

    egs = per_head(jnp.exp, gcs)
    rhs = per_head(lambda v, k, bc, eg: jnp.concatenate([v * bc, k * (bc * eg)], axis=1), vs, ks, bcs, egs)
    sols = mm(bf(tinv), bf(rhs))
    yq = y[:, :GDN_QK]
    yq = yq * (lax.rsqrt(_dot((yq * yq).astype(BF16), hm) + EPS) * (GDN_DK ** -0.5))
    qs = [yq[tile_rows[ti], head_cols(h, GDN_DK)] for ti, h in units]
    qks = per_head(lambda q, kb, dec: (_dot_nt(q.astype(BF16), kb) * dec).astype(BF16), qs, kbs, decs)
    u0s = [sol[:, :GDN_DV] for sol in sols]
    wbs = [sol[:, GDN_DV:].astype(BF16) for sol in sols]
    qds = per_head(lambda q, eg: (q * eg).astype(BF16), qs, egs)

    unit = lambda ti, h: ti * GDN_HEADS + h

    def chunk_terms(ti, ic):
        rs = slice(ic * c, (ic + 1) * c)
        gends = [gcs[unit(ti, h)][(ic + 1) * c - 1:(ic + 1) * c, :] for h in heads]
        kds = [(ks[unit(ti, h)][rs] * jnp.exp(gends[h] - gcs[unit(ti, h)][rs])).astype(BF16) for h in heads]
        return rs, gends, kds

    def state_terms(rs, states):
        return [_dot(jnp.concatenate([wbs[h][rs], qds[h][rs]], axis=0), states[h].astype(BF16)) for h in heads]

    def next_state(states, gends, kds, us_c):
        return [states[h] * jnp.exp(gends[h]) + lax.dot_general(
            kds[h], us_c[h], (((0,), (0,)), ((), ())), preferred_element_type=F32) for h in heads]

    us = [[] for _ in units]
    outs = [[] for _ in units]
    if seqs == 1:
        trans = []
        for ti in range(ntile):
            for ic in range(nchunk):
                rs, gends, kds = chunk_terms(ti, ic)
                kt_wu = [lax.dot_general(
                    kds[h], jnp.concatenate([wbs[unit(ti, h)][rs], u0s[unit(ti, h)][rs].astype(BF16)], axis=1),
                    (((0,), (0,)), ((), ())), preferred_element_type=F32) for h in heads]
                trans.append(([jnp.exp(g) for g in gends], [m[:, :GDN_DK].astype(BF16) for m in kt_wu],
                              [m[:, GDN_DK:] for m in kt_wu]))
        states = [s_ref[h] for h in heads]
        chunk_states = []
        for decay, kt_w, kt_u0 in trans:
            sb = [s.astype(BF16) for s in states]
            chunk_states.append(sb)
            states = [states[h] * decay[h] - _dot(kt_w[h], sb[h]) + kt_u0[h] for h in heads]
        for h in heads:
            s_ref[h] = states[h]
        for ti in range(ntile):
            for ic in range(nchunk):
                rs = slice(ic * c, (ic + 1) * c)
                sb = chunk_states[ti * nchunk + ic]
                wss = [_dot(jnp.concatenate([wbs[unit(ti, h)][rs], qds[unit(ti, h)][rs]], axis=0), sb[h])
                       for h in heads]
                for h in heads:
                    us[unit(ti, h)].append((u0s[unit(ti, h)][rs] - wss[h][:c]).astype(BF16))
                    outs[unit(ti, h)].append(wss[h][c:])
        o = [jnp.concatenate(outs[u], axis=0) + _dot(qks[u], jnp.concatenate(us[u], axis=0))
             for u in range(len(units))]

        @pl.when(t == pl.num_programs(1) - 1)
        def _():
            s_out_ref[...] = s_ref[...]
    else:
        for ic in range(nchunk):
            rs, gends, kds = chunk_terms(0, ic)
            states = [s0_ref[ic, h] for h in heads]
            wss = state_terms(rs, states)
            us_c = [(u0s[h][rs] - wss[h][:c]).astype(BF16) for h in heads]
            new = next_state(states, gends, kds, us_c)
            for h in heads:
                us[h].append(us_c[h])
                outs[h].append(wss[h][c:])
                s_out_ref[ic, h] = new[h]
        o = [jnp.concatenate(outs[h], axis=0) + _dot(qks[h], jnp.concatenate(us[h], axis=0)) for h in heads]
    for u, (ti, h) in enumerate(units):
        rows, cols = tile_rows[ti], head_cols(h, GDN_DV)
        o_ref[rows, cols] = (_rms(o[u], on) * _silu(z_ref[rows, cols])).astype(o_ref.dtype)


def _gdn_call(conv, z, ba, hist, s0, p, tile, c):
    b, l, _ = conv.shape
    small = [p['conv_w'], p['a_log'], p['dt_bias'], p['gdn_on'], p['head_mat']]
    state = (GDN_HEADS, GDN_DK, GDN_DV)
    if l >= tile:
        tc = _pick_tile(l, GDN_BLOCK_TILES * tile)
        seqs = 1
        grid = (b, l // tc)
        row = lambda w: pl.BlockSpec((None, tc, w), lambda i, j: (i, j, 0))
        per_b = lambda shape: pl.BlockSpec((None,) + shape, lambda i, j: (i,) + (0,) * len(shape))
        scratch = [pltpu.VMEM((tc + CONV_PAD, CONV_CH), F32), pltpu.VMEM(state, F32)]
    else:
        tc = tile = b * l
        assert l == c and tc % LANES == 0
        seqs = b
        grid = (1, 1)
        conv, z, ba = (t.reshape(1, tc, t.shape[-1]) for t in (conv, z, ba))
        row = lambda w: pl.BlockSpec((None, tc, w), lambda i, j: (0, 0, 0))
        per_b = lambda shape: pl.BlockSpec((b,) + shape, lambda i, j: (0,) * (len(shape) + 1))
        scratch = [pltpu.VMEM((seqs * (c + CONV_PAD), CONV_CH), F32)]
    o, s_new = pl.pallas_call(
        functools.partial(_gdn_kernel, tc=tc, tile=tile, c=c, seqs=seqs),
        grid=grid,
        in_specs=[row(CONV_CH), row(GDN_V), row(2 * LANES), per_b((CONV_W - 1, CONV_CH)), per_b(state)] + [
            pl.BlockSpec(w.shape, lambda i, j: (0, 0)) for w in small],
        out_specs=[row(GDN_V), per_b(state)],
        out_shape=[jax.ShapeDtypeStruct(conv.shape[:2] + (GDN_V,), BF16),
                   jax.ShapeDtypeStruct((b,) + state, F32)],
        scratch_shapes=scratch,
        compiler_params=pltpu.CompilerParams(dimension_semantics=("arbitrary", "arbitrary"),
                                             vmem_limit_bytes=VMEM_LIMIT),
        name="gdn",
    )(conv, z, ba, hist, s0, *small)
    return o.reshape(b, l, GDN_V), s_new


def _stack_maps(q):
    lane = lax.broadcasted_iota(jnp.int32, q.shape, 1)
    zero = jnp.zeros_like(q)
    return jnp.concatenate([jnp.where(lane < DIFF_DK, q, zero), jnp.where(lane >= DIFF_DK, q, zero)],
                           axis=0)


def _diff_finish(acc, l, t, lq_ref, lk_ref, dn_ref, lam_init):
    lam_e = jnp.exp(jnp.sum(lq_ref[...] * lk_ref[...], axis=-1, keepdims=True))
    lam = lam_e[0:1] - lam_e[1:2] + lam_init
    o = acc[:t] / l[:t] - lam * (acc[t:] / l[t:])
    return _rms(o, dn_ref[...]) * (1.0 - lam_init)


def _halves_max(s):
    return jnp.maximum(s[:, :LANES], s[:, LANES:])


def _attn_prompt_kernel(q_ref, kt_ref, v_ref, bias_ref, lq_ref, lk_ref, dn_ref, o_ref,
                        kt_scr, vx_scr, s_scr, m_scr, acc_scr, *, tq, tk, lam_init):
    i = pl.program_id(2)
    assert tk == 2 * LANES and tq % tk == 0
    nsub = tq // tk

    @pl.when(i == 0)
    def _():
        for j in range(kt_scr.shape[0]):
            kt_scr[j] = kt_ref[:, j * tk:(j + 1) * tk].astype(BF16)
        vx_scr[:, :DIFF_DV] = v_ref[...]
        vx_scr[:, DIFF_DV:] = jnp.ones((vx_scr.shape[0], LANES), BF16)

    qq = _stack_maps(q_ref[...])

    def scores(j):
        return _dot(qq, kt_scr[j])

    def weighted_values(ss, blocks, mm):
        p = [jnp.exp2(s[:, half * LANES:(half + 1) * LANES] - mm) for s in ss for half in range(tk // LANES)]
        vx = [vx_scr[pl.ds(pl.multiple_of(j * tk, tk), tk), :] for j in blocks]
        return _dot(jnp.concatenate(p, axis=1).astype(BF16), jnp.concatenate(vx, axis=0))

    def pass1(blocks, biases):
        ss = [scores(j) if b is None else scores(j) + b for j, b in zip(blocks, biases)]
        for j, s in zip(blocks, ss):
            s_scr[j] = s
        m_scr[...] = jnp.maximum(m_scr[...], functools.reduce(jnp.maximum, [_halves_max(s) for s in ss]))

    def pass2(blocks, biases):
        del biases
        acc_scr[...] += weighted_values([s_scr[j] for j in blocks], blocks, m_scr[...])

    def over_visible(body):
        def group(g):
            return [g * nsub + u for u in range(nsub)]

        def trip(t, carry):
            second = group(2 * t + 1)
            body(group(2 * t) + second,
                 [None] * nsub + [bias_ref[jnp.maximum(j - i * nsub + 1, 0)] for j in second])
            return carry

        lax.fori_loop(0, (i + 1) // 2, trip, 0)

        @pl.when(i % 2 == 0)
        def _():
            body(group(i), [bias_ref[1 + d] for d in range(nsub)])

    m_scr[...] = jnp.full(m_scr.shape, -jnp.inf, F32)
    over_visible(pass1)
    m_scr[...] = jnp.broadcast_to(jnp.max(m_scr[...], axis=-1, keepdims=True), m_scr.shape)
    acc_scr[...] = jnp.zeros(acc_scr.shape, F32)
    over_visible(pass2)
    acc = acc_scr[...]
    o_ref[...] = _diff_finish(acc[:, :DIFF_DV], acc[:, DIFF_DV:], tq, lq_ref, lk_ref, dn_ref,
                              lam_init).astype(o_ref.dtype)


def _attn_prompt_call(qb, kt, vb, p, tq, tk, lam_init):
    b, l, _ = qb.shape
    small = [p['lambda_q'], p['lambda_k'], p['diff_on']]
    r = (jnp.arange(2 * tq) % tq)[None, :, None] // CHUNK
    cidx = (jnp.arange(tk)[None, None, :] + tk * jnp.arange(tq // tk)[:, None, None]) // CHUNK
    bias = jnp.where(cidx <= r, 0.0, -jnp.inf).astype(F32)
    bias = jnp.concatenate([jnp.zeros_like(bias[:1]), bias], axis=0)
    return pl.pallas_call(
        functools.partial(_attn_prompt_kernel, tq=tq, tk=tk, lam_init=lam_init),
        grid=(b, DIFF_HEADS, l // tq),
        in_specs=[pl.BlockSpec((None, tq, LANES), lambda bi, h, i: (bi, i, h)),
                  pl.BlockSpec((None, None, LANES, l), lambda bi, h, i: (bi, h, 0, 0)),
                  pl.BlockSpec((None, l, LANES), lambda bi, h, i: (bi, 0, h)),
                  _const_spec(bias.shape)] + [
                      pl.BlockSpec(w.shape, lambda bi, h, i: (0, 0)) for w in small],
        out_specs=pl.BlockSpec((None, tq, LANES), lambda bi, h, i: (bi, i, h)),
        out_shape=jax.ShapeDtypeStruct((b, l, DIFF_V), BF16),
        scratch_shapes=[pltpu.VMEM((l // tk, LANES, tk), BF16),
                        pltpu.VMEM((l, DIFF_DV + LANES), BF16),
                        pltpu.VMEM((l // tk, 2 * tq, tk), F32),
                        pltpu.VMEM((2 * tq, LANES), F32),
                        pltpu.VMEM((2 * tq, DIFF_DV + LANES), F32)],
        compiler_params=pltpu.CompilerParams(
            dimension_semantics=("arbitrary", "arbitrary", "arbitrary"),
            vmem_limit_bytes=VMEM_LIMIT),
        name="attn_prompt",
    )(qb, kt, vb, bias, *small)


def _attn_sample_kernel(q_ref, kn_ref, vn_ref, ckt_ref, cv_ref, lq_ref, lk_ref, dn_ref, o_ref,
                        *, t, lam_init):
    past = ckt_ref.shape[-1]
    for h in range(DIFF_HEADS):
        cols = slice(h * LANES, (h + 1) * LANES)
        qq = _stack_maps(q_ref[:, cols])
        s_c = _dot(qq, ckt_ref[h].astype(BF16))
        s_n = _dot_nt(qq, kn_ref[:, cols])
        m = jnp.maximum(jnp.max(s_c, axis=-1, keepdims=True), jnp.max(s_n, axis=-1, keepdims=True))
        p_c = jnp.exp2(s_c - m)
        p_n = jnp.exp2(s_n - m)
        l = jnp.sum(p_c, axis=-1, keepdims=True) + jnp.sum(p_n, axis=-1, keepdims=True)
        cv = cv_ref[pl.ds(h, past, stride=DIFF_HEADS), :].astype(BF16)
        acc = _dot(p_c.astype(BF16), cv) + _dot(p_n.astype(BF16), vn_ref[:, cols])
        o_ref[:, cols] = _diff_finish(acc, l, t, lq_ref, lk_ref, dn_ref, lam_init).astype(o_ref.dtype)


def _attn_sample_call(qb, kb, vb, ckt, cv, p, lam_init):
    b, t, _ = qb.shape
    past = ckt.shape[-1]
    small = [p['lambda_q'], p['lambda_k'], p['diff_on']]
    new = pl.BlockSpec((None, t, DIFF_V), lambda bi: (bi, 0, 0))
    return pl.pallas_call(
        functools.partial(_attn_sample_kernel, t=t, lam_init=lam_init),
        grid=(b,),
        in_specs=[new, new, new,
                  pl.BlockSpec((None, DIFF_HEADS, LANES, past), lambda bi: (bi, 0, 0, 0)),
                  pl.BlockSpec((None, past * DIFF_HEADS, DIFF_DV), lambda bi: (bi, 0, 0))] + [
                      pl.BlockSpec(w.shape, lambda bi: (0, 0)) for w in small],
        out_specs=new,
        out_shape=jax.ShapeDtypeStruct((b, t, DIFF_V), BF16),
        compiler_params=pltpu.CompilerParams(dimension_semantics=("arbitrary",),
                                             vmem_limit_bytes=VMEM_LIMIT),
        name="attn_sample",
    )(qb, kb, vb, ckt, cv, *small)


def _rope_tables(pos):
    half = ROT_DIM // 2
    inv = jnp.float32(ROPE_THETA) ** (-jnp.arange(half, dtype=F32) * 2.0 / ROT_DIM)
    ang = pos.astype(F32)[:, None] * inv[None, :]
    cos, sin = jnp.cos(ang), jnp.sin(ang)
    n = pos.shape[0]
    rest = DIFF_DK - ROT_DIM
    one = jnp.ones((n, rest), F32)
    zero = jnp.zeros((n, rest), F32)
    zh = jnp.zeros((n, half), F32)
    c64 = jnp.concatenate([cos, cos, one], axis=1)
    a64 = jnp.concatenate([-sin, zh, zero], axis=1)
    b64 = jnp.concatenate([zh, sin, zero], axis=1)
    rep = LANES // DIFF_DK
    return tuple(jnp.tile(t, (1, rep)) for t in (c64, a64, b64))


def _layer_params(w, l):
    d = w['w_in'].shape[1]
    win = jnp.transpose(w['w_in'][l])
    w_b = win[CONV_CH:CONV_CH + GDN_HEADS]
    w_a = win[CONV_CH + GDN_HEADS:CONV_CH + 2 * GDN_HEADS]
    zpad = jnp.zeros((LANES - GDN_HEADS, d), win.dtype)
    hpad = lambda v: jnp.concatenate([v.astype(F32), jnp.zeros((LANES - GDN_HEADS,), F32)])[None, :]
    grp = jnp.arange(DIFF_QK) // DIFF_DK
    bf = lambda t: t.astype(BF16)
    r2 = lambda v: v.astype(F32)[None, :]
    return dict(
        n1=r2(w['ffn1_norm'][l]), wgu1=bf(w['ffn1_w_gu'][l]), wd1=bf(w['ffn1_w_down'][l]),
        nm=r2(w['mix_norm'][l]), w_conv_t=bf(win[:CONV_CH]), w_rest_t=bf(win[CONV_CH + 2 * GDN_HEADS:]),
        w_ba_t=bf(jnp.concatenate([w_b, zpad, w_a, zpad], axis=0)),
        qn=r2(jnp.tile(w['q_norm'][l], DIFF_QK // DIFF_DK)), kn=r2(jnp.tile(w['k_norm'][l], DIFF_QK // DIFF_DK)),
        gmat=(grp[:, None] == grp[None, :]).astype(BF16),
        head_mat=(grp[:, None] // 2 == grp[None, :] // 2).astype(BF16),
        conv_w=w['conv_w'][l].astype(F32), a_log=hpad(w['a_log'][l]), dt_bias=hpad(w['dt_bias'][l]),
        gdn_on=r2(w['gdn_out_norm'][l]),
        lambda_q=w['lambda_q'][l].astype(F32), lambda_k=w['lambda_k'][l].astype(F32),
        diff_on=r2(w['diff_out_norm'][l]),
        wo=bf(w['w_out'][l]), n2=r2(w['ffn2_norm'][l]), wgu2=bf(w['ffn2_w_gu'][l]), wd2=bf(w['ffn2_w_down'][l]),
    )


def _pick_tile(n, pref):
    t = min(n, pref)
    assert n % t == 0
    return t


def _layer(x, pos, k_hist, v_hist, conv_hist, s0, p, lam_init):
    b, l, d = x.shape
    n = b * l
    tm = _pick_tile(n, 512)
    tabs = _rope_tables(pos)
    if l >= tm:
        assert l % tm == 0
        n_pos_tiles = l // tm
    else:
        assert tm % l == 0
        tabs = tuple(jnp.tile(t, (tm // l, 1)) for t in tabs)
        n_pos_tiles = 1
    prompt = k_hist is None
    pre = _pre_call(x.reshape(n, d), tabs, p, tm, n_pos_tiles, b, prompt)
    x1, conv, z, ba, qb, vb, kf, vf = pre[:8]
    c = min(CHUNK, l)
    assert l % c == 0
    tc = GDN_TILE
    assert l % tc == 0 or l == c
    if conv_hist is None:
        conv_hist = jnp.zeros((b, CONV_W - 1, CONV_CH), F32)
        s0 = jnp.zeros((b, GDN_HEADS, GDN_DK, GDN_DV), F32)
    conv3 = conv.reshape(b, l, CONV_CH)
    og, s_new = _gdn_call(conv3, z.reshape(b, l, GDN_V), ba.reshape(b, l, 2 * LANES),
                          conv_hist.astype(F32), s0.astype(F32), p, tc, c)
    conv_new = jnp.concatenate([conv_hist.astype(F32), conv3], axis=1)[:, -(CONV_W - 1):]
    q3 = qb.reshape(b, l, DIFF_QK)
    v3 = vb.reshape(b, l, DIFF_V)
    if prompt:
        tk = 2 * LANES
        od = _attn_prompt_call(q3, kf, v3, p, _pick_tile(l, 2 * tk), tk, lam_init)
        k_out = jnp.transpose(kf.reshape(b, DIFF_HEADS, 2, DIFF_DK, l), (0, 4, 1, 2, 3))
    else:
        past = k_hist.shape[1]
        ckt = jnp.transpose(k_hist, (0, 2, 3, 4, 1)).reshape(b, DIFF_HEADS, LANES, past)
        od = _attn_sample_call(q3, pre[8].reshape(b, l, DIFF_QK), v3, ckt,
                               v_hist.reshape(b, past * DIFF_HEADS, DIFF_DV), p, lam_init)
        k_out = kf.reshape(b, l, DIFF_HEADS, 2, DIFF_DK)
    y = _post_call(x1, og.reshape(n, GDN_V), od.reshape(n, DIFF_V), p, _pick_tile(n, POST_TILE))
    return (y.reshape(b, l, d), k_out, vf.reshape(b, l, DIFF_HEADS, DIFF_DV), s_new, conv_new)


def kernel(x_prompt, x_sample, cache_k, cache_v, state_gdn, state_conv, ffn1_norm, ffn1_w_gu, ffn1_w_down,
           mix_norm, w_in, conv_w, a_log, dt_bias, gdn_out_norm, q_norm, k_norm, lambda_q, lambda_k,
           diff_out_norm, w_out, ffn2_norm, ffn2_w_gu, ffn2_w_down):
    w = dict(ffn1_norm=ffn1_norm, ffn1_w_gu=ffn1_w_gu, ffn1_w_down=ffn1_w_down, mix_norm=mix_norm, w_in=w_in,
             conv_w=conv_w, a_log=a_log, dt_bias=dt_bias, gdn_out_norm=gdn_out_norm, q_norm=q_norm,
             k_norm=k_norm, lambda_q=lambda_q, lambda_k=lambda_k, diff_out_norm=diff_out_norm, w_out=w_out,
             ffn2_norm=ffn2_norm, ffn2_w_gu=ffn2_w_gu, ffn2_w_down=ffn2_w_down)
    depth = w_in.shape[0]
    pos_p = jnp.arange(x_prompt.shape[1])
    pos_s = cache_k.shape[2] + jnp.arange(x_sample.shape[1])
    hp, hs = x_prompt, x_sample
    outs = [[] for _ in range(8)]
    for l in range(depth):
        lam_init = 0.8 - 0.6 * math.exp(-0.3 * l)
        p = _layer_params(w, l)
        hp, kp, vp, sp, cp = _layer(hp, pos_p, None, None, None, None, p, lam_init)
        hs, ks, vs, ss, cs = _layer(hs, pos_s, cache_k[l], cache_v[l], state_conv[l], state_gdn[l], p, lam_init)
        for acc, val in zip(outs, (kp, vp, sp, cp, ks, vs, ss, cs)):
            acc.append(val)
    return (hp, hs) + tuple(jnp.stack(o) for o in outs)
```

```python
import functools
import math

import jax
import jax.numpy as jnp
from jax import lax
from jax.experimental import pallas as pl
from jax.experimental.pallas import tpu as pltpu

F32 = jnp.float32
BF16 = jnp.bfloat16

EPS = 1e-6
CHUNK = 64
GDN_HEADS = 4
GDN_DK = 128
GDN_DV = 128
CONV_W = 4
DIFF_HEADS = 4
DIFF_DK = 64
DIFF_DV = 128
ROT_DIM = DIFF_DK // 4
ROPE_THETA = 500000.0
GDN_QK = GDN_HEADS * GDN_DK
GDN_V = GDN_HEADS * GDN_DV
CONV_CH = 2 * GDN_QK + GDN_V
DIFF_QK = DIFF_HEADS * 2 * DIFF_DK
DIFF_V = DIFF_HEADS * DIFF_DV

LANES = 128
INV_BLOCK = 16
POST_TILE = 512
GDN_TILE = 128
GDN_BLOCK_TILES = 4
VMEM_LIMIT = 60 * 1024 * 1024
Q_SCALE = (DIFF_DK ** -0.5) * math.log2(math.e)


def _dot(a, b):
    return jnp.dot(a, b, preferred_element_type=F32)


def _dot_nt(a, b):
    return lax.dot_general(a, b, (((1,), (1,)), ((), ())), preferred_element_type=F32)


def _rms(x, g):
    return x * lax.rsqrt(jnp.mean(x * x, axis=-1, keepdims=True) + EPS) * g


def _silu(x):
    return x * jax.nn.sigmoid(x)


def _rms_parts(x, g):
    return (x * g).astype(BF16), lax.rsqrt(jnp.mean(x * x, axis=-1, keepdims=True) + EPS)


def _swiglu(xg, r, wgu_ref, wd_ref):
    d_ff = wd_ref.shape[0]
    g = _dot(xg, wgu_ref[:, :d_ff]) * r
    u = _dot(xg, wgu_ref[:, d_ff:]) * r
    act = (_silu(g) * u).astype(BF16)
    return _dot(act, wd_ref[...])


def _const_spec(shape):
    nd = len(shape)
    return pl.BlockSpec(shape, lambda *_: (0,) * nd, pipeline_mode=pl.Buffered(1))


_WREST = dict(z=slice(0, GDN_V), q=slice(GDN_V, GDN_V + DIFF_QK), k=slice(GDN_V + DIFF_QK, GDN_V + 2 * DIFF_QK),
              v=slice(GDN_V + 2 * DIFF_QK, GDN_V + 2 * DIFF_QK + DIFF_V))


def _pre_kernel(x_ref, cos_ref, sa_ref, sb_ref, n1_ref, wgu_ref, wd_ref, nm_ref, wconv_ref, wrest_ref, wba_ref, qn_ref, kn_ref,
                gm_ref,
                x1_ref, conv_ref, z_ref, ba_ref, qb_ref, vb_ref, kf_ref, vf_ref, *maybe_kb_ref,
                k_pos_minor):
    x = x_ref[...]
    xg, r = _rms_parts(x, n1_ref[...])
    x1 = x + 0.5 * _swiglu(xg, r, wgu_ref, wd_ref)
    x1_ref[...] = x1
    h, rh = _rms_parts(x1, nm_ref[...])
    tm = x.shape[0]
    cos = cos_ref[...]
    sa = sa_ref[...]
    sb = sb_ref[...]
    gm = gm_ref[...]

    def norm_rope(t, gw):
        ss = _dot((t * t).astype(BF16), gm)
        t = t * lax.rsqrt(ss * (1.0 / DIFF_DK) + EPS) * gw
        outs = []
        for hh in range(DIFF_HEADS):
            th = t[:, hh * LANES:(hh + 1) * LANES]
            up = pltpu.roll(th, LANES - ROT_DIM // 2, 1)
            dn = pltpu.roll(th, ROT_DIM // 2, 1)
            outs.append(th * cos + up * sa + dn * sb)
        return jnp.concatenate(outs, axis=1)

    k = norm_rope(_dot_nt(h, wrest_ref[_WREST['k'], :]) * rh, kn_ref[...])
    if k_pos_minor:
        kf_ref[...] = k.T.reshape(DIFF_HEADS, LANES, tm)
    else:
        kf_ref[...] = k
        maybe_kb_ref[0][...] = k.astype(BF16)
    q = norm_rope(_dot_nt(h, wrest_ref[_WREST['q'], :]) * rh, qn_ref[...])
    qb_ref[...] = (q * Q_SCALE).astype(BF16)
    v = _dot_nt(h, wrest_ref[_WREST['v'], :]) * rh
    vb_ref[...] = v.astype(BF16)
    if k_pos_minor:
        for hh in range(DIFF_HEADS):
            vf_ref[pl.ds(hh, tm, stride=DIFF_HEADS), :] = v[:, hh * DIFF_DV:(hh + 1) * DIFF_DV]
    else:
        vf_ref[...] = v
    z_ref[...] = _dot_nt(h, wrest_ref[_WREST['z'], :]) * rh
    ba_ref[...] = _dot_nt(h, wba_ref[...]) * rh
    conv_ref[...] = _dot_nt(h, wconv_ref[...]) * rh


def _pre_call(x2d, tabs, p, tm, n_pos_tiles, batch, k_pos_minor):
    n, d = x2d.shape
    grid = (n // tm,)
    row = lambda w: pl.BlockSpec((tm, w), lambda i: (i, 0))
    tab = pl.BlockSpec((tm, LANES), lambda i: (i % n_pos_tiles, 0))
    weights = [p['n1'], p['wgu1'], p['wd1'], p['nm'], p['w_conv_t'], p['w_rest_t'], p['w_ba_t'], p['qn'], p['kn'], p['gmat']]
    in_specs = [row(d), tab, tab, tab] + [_const_spec(w.shape) for w in weights]
    outs = [(d, F32), (CONV_CH, F32), (GDN_V, F32), (2 * LANES, F32), (DIFF_QK, BF16), (DIFF_V, BF16)]
    out_specs = [row(w) for w, _ in outs]
    out_shape = [jax.ShapeDtypeStruct((n, w), dt) for w, dt in outs]
    if k_pos_minor:
        l = n // batch
        tiles = l // tm
        out_specs += [pl.BlockSpec((None, DIFF_HEADS, LANES, tm), lambda i: (i // tiles, 0, 0, i % tiles)),
                      pl.BlockSpec((tm * DIFF_HEADS, DIFF_DV), lambda i: (i, 0))]
        out_shape += [jax.ShapeDtypeStruct((batch, DIFF_HEADS, LANES, l), F32),
                      jax.ShapeDtypeStruct((n * DIFF_HEADS, DIFF_DV), F32)]
    else:
        out_specs += [row(DIFF_QK), row(DIFF_V), row(DIFF_QK)]
        out_shape += [jax.ShapeDtypeStruct((n, DIFF_QK), F32), jax.ShapeDtypeStruct((n, DIFF_V), F32),
                      jax.ShapeDtypeStruct((n, DIFF_QK), BF16)]
    return pl.pallas_call(
        functools.partial(_pre_kernel, k_pos_minor=k_pos_minor),
        grid=grid,
        in_specs=in_specs,
        out_specs=out_specs,
        out_shape=out_shape,
        compiler_params=pltpu.CompilerParams(dimension_semantics=("arbitrary",),
                                             vmem_limit_bytes=VMEM_LIMIT),
        name="pre",
    )(x2d, *tabs, *weights)


def _post_kernel(x1_ref, og_ref, od_ref, wo_ref, n2_ref, wgu_ref, wd_ref, y_ref):
    mixed = jnp.concatenate([og_ref[...], od_ref[...]], axis=1)
    x2 = x1_ref[...] + _dot(mixed, wo_ref[...])
    xg, r = _rms_parts(x2, n2_ref[...])
    y_ref[...] = x2 + 0.5 * _swiglu(xg, r, wgu_ref, wd_ref)


def _post_call(x1, og, od, p, tm):
    n, d = x1.shape
    row = lambda w: pl.BlockSpec((tm, w), lambda i: (i, 0))
    weights = [p['wo'], p['n2'], p['wgu2'], p['wd2']]
    return pl.pallas_call(
        _post_kernel,
        grid=(n // tm,),
        in_specs=[row(d), row(GDN_V), row(DIFF_V)] + [_const_spec(w.shape) for w in weights],
        out_specs=row(d),
        out_shape=jax.ShapeDtypeStruct((n, d), F32),
        compiler_params=pltpu.CompilerParams(dimension_semantics=("arbitrary",),
                                             vmem_limit_bytes=VMEM_LIMIT),
        name="post",
    )(x1, og, od, *weights)


def _exact3(m01, x):
    x1 = x.astype(BF16)
    r1 = x - x1.astype(F32)
    x2 = r1.astype(BF16)
    x3 = (r1 - x2.astype(F32)).astype(BF16)
    return _dot(m01, x1) + (_dot(m01, x2) + _dot(m01, x3))


CONV_PAD = 8


def _gdn_kernel(x_ref, z_ref, ba_ref, hist_ref, s0_ref, cw_ref, alog_ref, dtb_ref, on_ref, hm_ref,
                o_ref, s_out_ref, xp_ref, *maybe_s_ref, tc, tile, c, seqs):
    pad = CONV_PAD
    nchunk = tile // c
    ntile = tc // tile
    assert c % INV_BLOCK == 0 and tile % LANES == 0 and tc % tile == 0
    assert seqs == 1 or (seqs == nchunk and ntile == 1)
    cw = cw_ref[...]
    x = x_ref[...]

    def conv(xp, rows):
        y = xp[rows] * cw[CONV_W - 1:CONV_W, :]
        for k in range(1, CONV_W):
            y = y + pltpu.roll(xp, k, 0)[rows] * cw[CONV_W - 1 - k:CONV_W - k, :]
        return y

    if seqs == 1:
        s_ref, = maybe_s_ref
        t = pl.program_id(1)

        @pl.when(t == 0)
        def _():
            xp_ref[0:pad, :] = jnp.zeros((pad, CONV_CH), F32)
            xp_ref[pad - (CONV_W - 1):pad, :] = hist_ref[...]
            s_ref[...] = s0_ref[...]

        xp_ref[pad:pad + tc, :] = x
        y = conv(xp_ref[...], slice(pad, pad + tc))
        xp_ref[pad - (CONV_W - 1):pad, :] = x[tc - (CONV_W - 1):tc, :]
    else:
        stride = pad + c
        xp_ref[...] = jnp.zeros(xp_ref.shape, F32)
        for b in range(seqs):
            xp_ref[b * stride + pad - (CONV_W - 1):b * stride + pad, :] = hist_ref[b]
            xp_ref[b * stride + pad:(b + 1) * stride, :] = x[b * c:(b + 1) * c, :]
        yp = conv(xp_ref[...], slice(None))
        y = jnp.concatenate([yp[b * stride + pad:(b + 1) * stride, :] for b in range(seqs)], axis=0)
    y = _silu(y)
    hm = hm_ref[...]
    yk = y[:, GDN_QK:2 * GDN_QK]
    yk = yk * lax.rsqrt(_dot((yk * yk).astype(BF16), hm) + EPS)

    ba = ba_ref[...]
    beta = jax.nn.sigmoid(ba[:, :LANES])
    g = -jnp.exp(alog_ref[...]) * jax.nn.softplus(ba[:, LANES:] + dtb_ref[...])
    ri = lax.broadcasted_iota(jnp.int32, (tile, tile), 0)
    ci = lax.broadcasted_iota(jnp.int32, (tile, tile), 1)
    incl = ((ri // c) == (ci // c)) & (ci <= ri)
    same_blk = (ri // INV_BLOCK) == (ci // INV_BLOCK)
    diag = ri == ci
    eye = jnp.where(diag, 1.0, 0.0)
    incl01 = jnp.where(incl, 1.0, 0.0).astype(BF16)
    tile_rows = [slice(ti * tile, (ti + 1) * tile) for ti in range(ntile)]
    gcums = [_exact3(incl01, g[rows]) for rows in tile_rows]
    gcum_ts = [gc.T for gc in gcums]
    on = on_ref[...]

    heads = range(GDN_HEADS)
    units = [(ti, h) for ti in range(ntile) for h in heads]

    def per_head(f, *lists):
        return [f(*args) for args in zip(*lists)]

    def head_cols(h, width):
        return slice(h * width, (h + 1) * width)

    ks = [yk[tile_rows[ti], head_cols(h, GDN_DK)] for ti, h in units]
    vs = [y[tile_rows[ti], 2 * GDN_QK + h * GDN_DV:2 * GDN_QK + (h + 1) * GDN_DV] for ti, h in units]
    gcs = [gcums[ti][:, h:h + 1] for ti, h in units]
    bcs = [beta[tile_rows[ti], h:h + 1] for ti, h in units]
    decs = [jnp.exp(jnp.where(incl, gcs[u] - gcum_ts[ti][h:h + 1, :], -jnp.inf))
            for u, (ti, h) in enumerate(units)]
    kbs = per_head(lambda k: k.astype(BF16), ks)
    a_s = per_head(lambda bc, kb, dec: jnp.where(diag, 0.0, bc * _dot_nt(kb, kb) * dec), bcs, kbs, decs)

    assert INV_BLOCK == 16 and c in (INV_BLOCK, 4 * INV_BLOCK)
    bf = lambda xs: per_head(lambda x: x.astype(BF16), xs)
    mm = lambda xs, ys: per_head(_dot, xs, ys)
    ds = per_head(lambda a: jnp.where(same_blk, a, 0.0), a_s) if c > INV_BLOCK else a_s
    sd = bf(ds)
    d2 = mm(sd, sd)
    s2 = bf(d2)
    d3 = mm(sd, s2)
    d4 = mm(s2, s2)
    s4 = bf(d4)
    n1 = per_head(lambda d, x2, x3: eye - d + x2 - x3, ds, d2, d3)
    n2 = per_head(jnp.add, n1, mm(bf(n1), s4))
    d8 = mm(s4, s4)
    tinv = per_head(jnp.add, n2, mm(bf(n2), bf(d8)))
    if c > INV_BLOCK:
        std = bf(tinv)
        ms = mm(std, bf(per_head(jnp.subtract, a_s, ds)))
        sm = bf(ms)
        m2 = mm(sm, sm)
        m3 = mm(sm, bf(m2))
        tinv = mm(bf(per_head(lambda m, x2, x3: eye - m + x2 - x3, ms, m2, m3)), std)

    egs = per_head(jnp.exp, gcs)
    rhs = per_head(lambda v, k, bc, eg: jnp.concatenate([v * bc, k * (bc * eg)], axis=1), vs, ks, bcs, egs)
    sols = mm(bf(tinv), bf(rhs))
    yq = y[:, :GDN_QK]
    yq = yq * (lax.rsqrt(_dot((yq * yq).astype(BF16), hm) + EPS) * (GDN_DK ** -0.5))
    qs = [yq[tile_rows[ti], head_cols(h, GDN_DK)] for ti, h in units]
    qks = per_head(lambda q, kb, dec: (_dot_nt(q.astype(BF16), kb) * dec).astype(BF16), qs, kbs, decs)
    u0s = [sol[:, :GDN_DV] for sol in sols]
    wbs = [sol[:, GDN_DV:].astype(BF16) for sol in sols]
    qds = per_head(lambda q, eg: (q * eg).astype(BF16), qs, egs)

    unit = lambda ti, h: ti * GDN_HEADS + h

    def chunk_terms(ti, ic):
        rs = slice(ic * c, (ic + 1) * c)
        gends = [gcs[unit(ti, h)][(ic + 1) * c - 1:(ic + 1) * c, :] for h in heads]
        kds = [(ks[unit(ti, h)][rs] * jnp.exp(gends[h] - gcs[unit(ti, h)][rs])).astype(BF16) for h in heads]
        return rs, gends, kds

    def state_terms(rs, states):
        return [_dot(jnp.concatenate([wbs[h][rs], qds[h][rs]], axis=0), states[h].astype(BF16)) for h in heads]

    def next_state(states, gends, kds, us_c):
        return [states[h] * jnp.exp(gends[h]) + lax.dot_general(
            kds[h], us_c[h], (((0,), (0,)), ((), ())), preferred_element_type=F32) for h in heads]

    us = [[] for _ in units]
    outs = [[] for _ in units]
    if seqs == 1:
        trans = []
        for ti in range(ntile):
            for ic in range(nchunk):
                rs, gends, kds = chunk_terms(ti, ic)
                kt_wu = [lax.dot_general(
                    kds[h], jnp.concatenate([wbs[unit(ti, h)][rs], u0s[unit(ti, h)][rs].astype(BF16)], axis=1),
                    (((0,), (0,)), ((), ())), preferred_element_type=F32) for h in heads]
                trans.append(([jnp.exp(g) for g in gends], [m[:, :GDN_DK].astype(BF16) for m in kt_wu],
                              [m[:, GDN_DK:] for m in kt_wu]))
        states = [s_ref[h] for h in heads]
        chunk_states = []
        for decay, kt_w, kt_u0 in trans:
            sb = [s.astype(BF16) for s in states]
            chunk_states.append(sb)
            states = [states[h] * decay[h] - _dot(kt_w[h], sb[h]) + kt_u0[h] for h in heads]
        for h in heads:
            s_ref[h] = states[h]
        for ti in range(ntile):
            for ic in range(nchunk):
                rs = slice(ic * c, (ic + 1) * c)
                sb = chunk_states[ti * nchunk + ic]
                wss = [_dot(jnp.concatenate([wbs[unit(ti, h)][rs], qds[unit(ti, h)][rs]], axis=0), sb[h])
                       for h in heads]
                for h in heads:
                    us[unit(ti, h)].append((u0s[unit(ti, h)][rs] - wss[h][:c]).astype(BF16))
                    outs[unit(ti, h)].append(wss[h][c:])
        o = [jnp.concatenate(outs[u], axis=0) + _dot(qks[u], jnp.concatenate(us[u], axis=0))
             for u in range(len(units))]

        @pl.when(t == pl.num_programs(1) - 1)
        def _():
            s_out_ref[...] = s_ref[...]
    else:
        for ic in range(nchunk):
            rs, gends, kds = chunk_terms(0, ic)
            states = [s0_ref[ic, h] for h in heads]
            wss = state_terms(rs, states)
            us_c = [(u0s[h][rs] - wss[h][:c]).astype(BF16) for h in heads]
            new = next_state(states, gends, kds, us_c)
            for h in heads:
                us[h].append(us_c[h])
                outs[h].append(wss[h][c:])
                s_out_ref[ic, h] = new[h]
        o = [jnp.concatenate(outs[h], axis=0) + _dot(qks[h], jnp.concatenate(us[h], axis=0)) for h in heads]
    for u, (ti, h) in enumerate(units):
        rows, cols = tile_rows[ti], head_cols(h, GDN_DV)
        o_ref[rows, cols] = (_rms(o[u], on) * _silu(z_ref[rows, cols])).astype(o_ref.dtype)


def _gdn_call(conv, z, ba, hist, s0, p, tile, c):
    b, l, _ = conv.shape
    small = [p['conv_w'], p['a_log'], p['dt_bias'], p['gdn_on'], p['head_mat']]
    state = (GDN_HEADS, GDN_DK, GDN_DV)
    if l >= tile:
        tc = _pick_tile(l, GDN_BLOCK_TILES * tile)
        seqs = 1
        grid = (b, l // tc)
        row = lambda w: pl.BlockSpec((None, tc, w), lambda i, j: (i, j, 0))
        per_b = lambda shape: pl.BlockSpec((None,) + shape, lambda i, j: (i,) + (0,) * len(shape))
        scratch = [pltpu.VMEM((tc + CONV_PAD, CONV_CH), F32), pltpu.VMEM(state, F32)]
    else:
        tc = tile = b * l
        assert l == c and tc % LANES == 0
        seqs = b
        grid = (1, 1)
        conv, z, ba = (t.reshape(1, tc, t.shape[-1]) for t in (conv, z, ba))
        row = lambda w: pl.BlockSpec((None, tc, w), lambda i, j: (0, 0, 0))
        per_b = lambda shape: pl.BlockSpec((b,) + shape, lambda i, j: (0,) * (len(shape) + 1))
        scratch = [pltpu.VMEM((seqs * (c + CONV_PAD), CONV_CH), F32)]
    o, s_new = pl.pallas_call(
        functools.partial(_gdn_kernel, tc=tc, tile=tile, c=c, seqs=seqs),
        grid=grid,
        in_specs=[row(CONV_CH), row(GDN_V), row(2 * LANES), per_b((CONV_W - 1, CONV_CH)), per_b(state)] + [
            pl.BlockSpec(w.shape, lambda i, j: (0, 0)) for w in small],
        out_specs=[row(GDN_V), per_b(state)],
        out_shape=[jax.ShapeDtypeStruct(conv.shape[:2] + (GDN_V,), BF16),
                   jax.ShapeDtypeStruct((b,) + state, F32)],
        scratch_shapes=scratch,
        compiler_params=pltpu.CompilerParams(dimension_semantics=("arbitrary", "arbitrary"),
                                             vmem_limit_bytes=VMEM_LIMIT),
        name="gdn",
    )(conv, z, ba, hist, s0, *small)
    return o.reshape(b, l, GDN_V), s_new


def _stack_maps(q):
    lane = lax.broadcasted_iota(jnp.int32, q.shape, 1)
    zero = jnp.zeros_like(q)
    return jnp.concatenate([jnp.where(lane < DIFF_DK, q, zero), jnp.where(lane >= DIFF_DK, q, zero)],
                           axis=0)


def _diff_finish(acc, l, t, lq_ref, lk_ref, dn_ref, lam_init):
    lam_e = jnp.exp(jnp.sum(lq_ref[...] * lk_ref[...], axis=-1, keepdims=True))
    lam = lam_e[0:1] - lam_e[1:2] + lam_init
    o = acc[:t] / l[:t] - lam * (acc[t:] / l[t:])
    return _rms(o, dn_ref[...]) * (1.0 - lam_init)


def _halves_max(s):
    return jnp.maximum(s[:, :LANES], s[:, LANES:])


def _attn_prompt_kernel(q_ref, kt_ref, v_ref, bias_ref, lq_ref, lk_ref, dn_ref, o_ref,
                        kt_scr, vx_scr, s_scr, m_scr, acc_scr, *, tq, tk, lam_init):
    i = pl.program_id(2)
    assert tk == 2 * LANES and tq % tk == 0
    nsub = tq // tk

    @pl.when(i == 0)
    def _():
        for j in range(kt_scr.shape[0]):
            kt_scr[j] = kt_ref[:, j * tk:(j + 1) * tk].astype(BF16)
        vx_scr[:, :DIFF_DV] = v_ref[...]
        vx_scr[:, DIFF_DV:] = jnp.ones((vx_scr.shape[0], LANES), BF16)

    qq = _stack_maps(q_ref[...])

    def scores(j):
        return _dot(qq, kt_scr[j])

    def weighted_values(ss, blocks, mm):
        p = [jnp.exp2(s[:, half * LANES:(half + 1) * LANES] - mm) for s in ss for half in range(tk // LANES)]
        vx = [vx_scr[pl.ds(pl.multiple_of(j * tk, tk), tk), :] for j in blocks]
        return _dot(jnp.concatenate(p, axis=1).astype(BF16), jnp.concatenate(vx, axis=0))

    def pass1(blocks, biases):
        ss = [scores(j) if b is None else scores(j) + b for j, b in zip(blocks, biases)]
        for j, s in zip(blocks, ss):
            s_scr[j] = s
        m_scr[...] = jnp.maximum(m_scr[...], functools.reduce(jnp.maximum, [_halves_max(s) for s in ss]))

    def pass2(blocks, biases):
        del biases
        acc_scr[...] += weighted_values([s_scr[j] for j in blocks], blocks, m_scr[...])

    def over_visible(body):
        ngroups = i + 1

        def group(g):
            return [g * nsub + u for u in range(nsub)]

        def run(first, count, diagonal_last):
            last = group(first + count - 1)
            if diagonal_last:
                biases = [bias_ref[1 + d] for d in range(nsub)]
            else:
                biases = [bias_ref[jnp.maximum(j - i * nsub + 1, 0)] for j in last]
            body([j for g in range(count - 1) for j in group(first + g)] + last,
                 [None] * (nsub * (count - 1)) + biases)

        def trip(t, carry):
            run(4 * t, 4, False)
            return carry

        lax.fori_loop(0, ngroups // 4, trip, 0)

        @pl.when(ngroups % 4 >= 2)
        def _():
            run((ngroups // 4) * 4, 2, False)

        @pl.when(ngroups % 2 == 1)
        def _():
            run(i, 1, True)

    m_scr[...] = jnp.full(m_scr.shape, -jnp.inf, F32)
    over_visible(pass1)
    m_scr[...] = jnp.broadcast_to(jnp.max(m_scr[...], axis=-1, keepdims=True), m_scr.shape)
    acc_scr[...] = jnp.zeros(acc_scr.shape, F32)
    over_visible(pass2)
    acc = acc_scr[...]
    o_ref[...] = _diff_finish(acc[:, :DIFF_DV], acc[:, DIFF_DV:], tq, lq_ref, lk_ref, dn_ref,
                              lam_init).astype(o_ref.dtype)


def _attn_prompt_call(qb, kt, vb, p, tq, tk, lam_init):
    b, l, _ = qb.shape
    small = [p['lambda_q'], p['lambda_k'], p['diff_on']]
    r = (jnp.arange(2 * tq) % tq)[None, :, None] // CHUNK
    cidx = (jnp.arange(tk)[None, None, :] + tk * jnp.arange(tq // tk)[:, None, None]) // CHUNK
    bias = jnp.where(cidx <= r, 0.0, -jnp.inf).astype(F32)
    bias = jnp.concatenate([jnp.zeros_like(bias[:1]), bias], axis=0)
    return pl.pallas_call(
        functools.partial(_attn_prompt_kernel, tq=tq, tk=tk, lam_init=lam_init),
        grid=(b, DIFF_HEADS, l // tq),
        in_specs=[pl.BlockSpec((None, tq, LANES), lambda bi, h, i: (bi, i, h)),
                  pl.BlockSpec((None, None, LANES, l), lambda bi, h, i: (bi, h, 0, 0)),
                  pl.BlockSpec((None, l, LANES), lambda bi, h, i: (bi, 0, h)),
                  _const_spec(bias.shape)] + [
                      pl.BlockSpec(w.shape, lambda bi, h, i: (0, 0)) for w in small],
        out_specs=pl.BlockSpec((None, tq, LANES), lambda bi, h, i: (bi, i, h)),
        out_shape=jax.ShapeDtypeStruct((b, l, DIFF_V), BF16),
        scratch_shapes=[pltpu.VMEM((l // tk, LANES, tk), BF16),
                        pltpu.VMEM((l, DIFF_DV + LANES), BF16),
                        pltpu.VMEM((l // tk, 2 * tq, tk), F32),
                        pltpu.VMEM((2 * tq, LANES), F32),
                        pltpu.VMEM((2 * tq, DIFF_DV + LANES), F32)],
        compiler_params=pltpu.CompilerParams(
            dimension_semantics=("arbitrary", "arbitrary", "arbitrary"),
            vmem_limit_bytes=VMEM_LIMIT),
        name="attn_prompt",
    )(qb, kt, vb, bias, *small)


def _attn_sample_kernel(q_ref, kn_ref, vn_ref, ckt_ref, cv_ref, lq_ref, lk_ref, dn_ref, o_ref,
                        *, t, lam_init):
    past = ckt_ref.shape[-1]
    for h in range(DIFF_HEADS):
        cols = slice(h * LANES, (h + 1) * LANES)
        qq = _stack_maps(q_ref[:, cols])
        s_c = _dot(qq, ckt_ref[h].astype(BF16))
        s_n = _dot_nt(qq, kn_ref[:, cols])
        m = jnp.maximum(jnp.max(s_c, axis=-1, keepdims=True), jnp.max(s_n, axis=-1, keepdims=True))
        p_c = jnp.exp2(s_c - m)
        p_n = jnp.exp2(s_n - m)
        l = jnp.sum(p_c, axis=-1, keepdims=True) + jnp.sum(p_n, axis=-1, keepdims=True)
        cv = cv_ref[pl.ds(h, past, stride=DIFF_HEADS), :].astype(BF16)
        acc = _dot(p_c.astype(BF16), cv) + _dot(p_n.astype(BF16), vn_ref[:, cols])
        o_ref[:, cols] = _diff_finish(acc, l, t, lq_ref, lk_ref, dn_ref, lam_init).astype(o_ref.dtype)


def _attn_sample_call(qb, kb, vb, ckt, cv, p, lam_init):
    b, t, _ = qb.shape
    past = ckt.shape[-1]
    small = [p['lambda_q'], p['lambda_k'], p['diff_on']]
    new = pl.BlockSpec((None, t, DIFF_V), lambda bi: (bi, 0, 0))
    return pl.pallas_call(
        functools.partial(_attn_sample_kernel, t=t, lam_init=lam_init),
        grid=(b,),
        in_specs=[new, new, new,
                  pl.BlockSpec((None, DIFF_HEADS, LANES, past), lambda bi: (bi, 0, 0, 0)),
                  pl.BlockSpec((None, past * DIFF_HEADS, DIFF_DV), lambda bi: (bi, 0, 0))] + [
                      pl.BlockSpec(w.shape, lambda bi: (0, 0)) for w in small],
        out_specs=new,
        out_shape=jax.ShapeDtypeStruct((b, t, DIFF_V), BF16),
        compiler_params=pltpu.CompilerParams(dimension_semantics=("arbitrary",),
                                             vmem_limit_bytes=VMEM_LIMIT),
        name="attn_sample",
    )(qb, kb, vb, ckt, cv, *small)


def _rope_tables(pos):
    half = ROT_DIM // 2
    inv = jnp.float32(ROPE_THETA) ** (-jnp.arange(half, dtype=F32) * 2.0 / ROT_DIM)
    ang = pos.astype(F32)[:, None] * inv[None, :]
    cos, sin = jnp.cos(ang), jnp.sin(ang)
    n = pos.shape[0]
    rest = DIFF_DK - ROT_DIM
    one = jnp.ones((n, rest), F32)
    zero = jnp.zeros((n, rest), F32)
    zh = jnp.zeros((n, half), F32)
    c64 = jnp.concatenate([cos, cos, one], axis=1)
    a64 = jnp.concatenate([-sin, zh, zero], axis=1)
    b64 = jnp.concatenate([zh, sin, zero], axis=1)
    rep = LANES // DIFF_DK
    return tuple(jnp.tile(t, (1, rep)) for t in (c64, a64, b64))


def _layer_params(w, l):
    d = w['w_in'].shape[1]
    win = jnp.transpose(w['w_in'][l])
    w_b = win[CONV_CH:CONV_CH + GDN_HEADS]
    w_a = win[CONV_CH + GDN_HEADS:CONV_CH + 2 * GDN_HEADS]
    zpad = jnp.zeros((LANES - GDN_HEADS, d), win.dtype)
    hpad = lambda v: jnp.concatenate([v.astype(F32), jnp.zeros((LANES - GDN_HEADS,), F32)])[None, :]
    grp = jnp.arange(DIFF_QK) // DIFF_DK
    bf = lambda t: t.astype(BF16)
    r2 = lambda v: v.astype(F32)[None, :]
    return dict(
        n1=r2(w['ffn1_norm'][l]), wgu1=bf(w['ffn1_w_gu'][l]), wd1=bf(w['ffn1_w_down'][l]),
        nm=r2(w['mix_norm'][l]), w_conv_t=bf(win[:CONV_CH]), w_rest_t=bf(win[CONV_CH + 2 * GDN_HEADS:]),
        w_ba_t=bf(jnp.concatenate([w_b, zpad, w_a, zpad], axis=0)),
        qn=r2(jnp.tile(w['q_norm'][l], DIFF_QK // DIFF_DK)), kn=r2(jnp.tile(w['k_norm'][l], DIFF_QK // DIFF_DK)),
        gmat=(grp[:, None] == grp[None, :]).astype(BF16),
        head_mat=(grp[:, None] // 2 == grp[None, :] // 2).astype(BF16),
        conv_w=w['conv_w'][l].astype(F32), a_log=hpad(w['a_log'][l]), dt_bias=hpad(w['dt_bias'][l]),
        gdn_on=r2(w['gdn_out_norm'][l]),
        lambda_q=w['lambda_q'][l].astype(F32), lambda_k=w['lambda_k'][l].astype(F32),
        diff_on=r2(w['diff_out_norm'][l]),
        wo=bf(w['w_out'][l]), n2=r2(w['ffn2_norm'][l]), wgu2=bf(w['ffn2_w_gu'][l]), wd2=bf(w['ffn2_w_down'][l]),
    )


def _pick_tile(n, pref):
    t = min(n, pref)
    assert n % t == 0
    return t


def _layer(x, pos, k_hist, v_hist, conv_hist, s0, p, lam_init):
    b, l, d = x.shape
    n = b * l
    tm = _pick_tile(n, 512)
    tabs = _rope_tables(pos)
    if l >= tm:
        assert l % tm == 0
        n_pos_tiles = l // tm
    else:
        assert tm % l == 0
        tabs = tuple(jnp.tile(t, (tm // l, 1)) for t in tabs)
        n_pos_tiles = 1
    prompt = k_hist is None
    pre = _pre_call(x.reshape(n, d), tabs, p, tm, n_pos_tiles, b, prompt)
    x1, conv, z, ba, qb, vb, kf, vf = pre[:8]
    c = min(CHUNK, l)
    assert l % c == 0
    tc = GDN_TILE
    assert l % tc == 0 or l == c
    if conv_hist is None:
        conv_hist = jnp.zeros((b, CONV_W - 1, CONV_CH), F32)
        s0 = jnp.zeros((b, GDN_HEADS, GDN_DK, GDN_DV), F32)
    conv3 = conv.reshape(b, l, CONV_CH)
    og, s_new = _gdn_call(conv3, z.reshape(b, l, GDN_V), ba.reshape(b, l, 2 * LANES),
                          conv_hist.astype(F32), s0.astype(F32), p, tc, c)
    conv_new = jnp.concatenate([conv_hist.astype(F32), conv3], axis=1)[:, -(CONV_W - 1):]
    q3 = qb.reshape(b, l, DIFF_QK)
    v3 = vb.reshape(b, l, DIFF_V)
    if prompt:
        tk = 2 * LANES
        od = _attn_prompt_call(q3, kf, v3, p, _pick_tile(l, 2 * tk), tk, lam_init)
        k_out = jnp.transpose(kf.reshape(b, DIFF_HEADS, 2, DIFF_DK, l), (0, 4, 1, 2, 3))
    else:
        past = k_hist.shape[1]
        ckt = jnp.transpose(k_hist, (0, 2, 3, 4, 1)).reshape(b, DIFF_HEADS, LANES, past)
        od = _attn_sample_call(q3, pre[8].reshape(b, l, DIFF_QK), v3, ckt,
                               v_hist.reshape(b, past * DIFF_HEADS, DIFF_DV), p, lam_init)
        k_out = kf.reshape(b, l, DIFF_HEADS, 2, DIFF_DK)
    y = _post_call(x1, og.reshape(n, GDN_V), od.reshape(n, DIFF_V), p, _pick_tile(n, POST_TILE))
    return (y.reshape(b, l, d), k_out, vf.reshape(b, l, DIFF_HEADS, DIFF_DV), s_new, conv_new)


def kernel(x_prompt, x_sample, cache_k, cache_v, state_gdn, state_conv, ffn1_norm, ffn1_w_gu, ffn1_w_down,
           mix_norm, w_in, conv_w, a_log, dt_bias, gdn_out_norm, q_norm, k_norm, lambda_q, lambda_k,
           diff_out_norm, w_out, ffn2_norm, ffn2_w_gu, ffn2_w_down):
    w = dict(ffn1_norm=ffn1_norm, ffn1_w_gu=ffn1_w_gu, ffn1_w_down=ffn1_w_down, mix_norm=mix_norm, w_in=w_in,
             conv_w=conv_w, a_log=a_log, dt_bias=dt_bias, gdn_out_norm=gdn_out_norm, q_norm=q_norm,
             k_norm=k_norm, lambda_q=lambda_q, lambda_k=lambda_k, diff_out_norm=diff_out_norm, w_out=w_out,
             ffn2_norm=ffn2_norm, ffn2_w_gu=ffn2_w_gu, ffn2_w_down=ffn2_w_down)
    depth = w_in.shape[0]
    pos_p = jnp.arange(x_prompt.shape[1])
    pos_s = cache_k.shape[2] + jnp.arange(x_sample.shape[1])
    hp, hs = x_prompt, x_sample
    outs = [[] for _ in range(8)]
    for l in range(depth):
        lam_init = 0.8 - 0.6 * math.exp(-0.3 * l)
        p = _layer_params(w, l)
        hp, kp, vp, sp, cp = _layer(hp, pos_p, None, None, None, None, p, lam_init)
        hs, ks, vs, ss, cs = _layer(hs, pos_s, cache_k[l], cache_v[l], state_conv[l], state_gdn[l], p, lam_init)
        for acc, val in zip(outs, (kp, vp, sp, cp, ks, vs, ss, cs)):
            acc.append(val)
    return (hp, hs) + tuple(jnp.stack(o) for o in outs)
```

```python
import functools
import math

import jax
import jax.numpy as jnp
from jax import lax
from jax.experimental import pallas as pl
from jax.experimental.pallas import tpu as pltpu

F32 = jnp.float32
BF16 = jnp.bfloat16

EPS = 1e-6
CHUNK = 64
GDN_HEADS = 4
GDN_DK = 128
GDN_DV = 128
CONV_W = 4
DIFF_HEADS = 4
DIFF_DK = 64
DIFF_DV = 128
ROT_DIM = DIFF_DK // 4
ROPE_THETA = 500000.0
GDN_QK = GDN_HEADS * GDN_DK
GDN_V = GDN_HEADS * GDN_DV
CONV_CH = 2 * GDN_QK + GDN_V
DIFF_QK = DIFF_HEADS * 2 * DIFF_DK
DIFF_V = DIFF_HEADS * DIFF_DV

LANES = 128
INV_BLOCK = 16
ROW_TILE = 512
GDN_TILE = 128
GDN_BLOCK_TILES = 4
VMEM_LIMIT = 60 * 1024 * 1024
Q_SCALE = (DIFF_DK ** -0.5) * math.log2(math.e)


def _dot(a, b):
    return jnp.dot(a, b, preferred_element_type=F32)


def _dot_nt(a, b):
    return lax.dot_general(a, b, (((1,), (1,)), ((), ())), preferred_element_type=F32)


def _rms(x, g):
    return x * lax.rsqrt(jnp.mean(x * x, axis=-1, keepdims=True) + EPS) * g


def _silu(x):
    return x * jax.nn.sigmoid(x)


def _rms_parts(x, g):
    return (x * g).astype(BF16), lax.rsqrt(jnp.mean(x * x, axis=-1, keepdims=True) + EPS)


def _swiglu(xg, r, wgu_ref, wd_ref):
    d_ff = wd_ref.shape[0]
    g = _dot(xg, wgu_ref[:, :d_ff]) * r
    u = _dot(xg, wgu_ref[:, d_ff:]) * r
    act = (_silu(g) * u).astype(BF16)
    return _dot(act, wd_ref[...])


def _const_spec(shape):
    nd = len(shape)
    return pl.BlockSpec(shape, lambda *_: (0,) * nd, pipeline_mode=pl.Buffered(1))


_WREST = dict(z=slice(0, GDN_V), q=slice(GDN_V, GDN_V + DIFF_QK), k=slice(GDN_V + DIFF_QK, GDN_V + 2 * DIFF_QK),
              v=slice(GDN_V + 2 * DIFF_QK, GDN_V + 2 * DIFF_QK + DIFF_V))


def _pre_kernel(x_ref, cos_ref, sa_ref, sb_ref, n1_ref, wgu_ref, wd_ref, nm_ref, wconv_ref, wrest_ref, wba_ref, qn_ref, kn_ref,
                gm_ref,
                x1_ref, conv_ref, z_ref, ba_ref, qb_ref, vb_ref, kf_ref, vf_ref, *maybe_kb_ref,
                k_pos_minor):
    x = x_ref[...]
    xg, r = _rms_parts(x, n1_ref[...])
    x1 = x + 0.5 * _swiglu(xg, r, wgu_ref, wd_ref)
    x1_ref[...] = x1
    h, rh = _rms_parts(x1, nm_ref[...])
    tm = x.shape[0]
    cos = cos_ref[...]
    sa = sa_ref[...]
    sb = sb_ref[...]
    gm = gm_ref[...]

    def norm_rope(t, gw):
        ss = _dot((t * t).astype(BF16), gm)
        t = t * lax.rsqrt(ss * (1.0 / DIFF_DK) + EPS) * gw
        outs = []
        for hh in range(DIFF_HEADS):
            th = t[:, hh * LANES:(hh + 1) * LANES]
            up = pltpu.roll(th, LANES - ROT_DIM // 2, 1)
            dn = pltpu.roll(th, ROT_DIM // 2, 1)
            outs.append(th * cos + up * sa + dn * sb)
        return jnp.concatenate(outs, axis=1)

    k = norm_rope(_dot_nt(h, wrest_ref[_WREST['k'], :]) * rh, kn_ref[...])
    if k_pos_minor:
        kf_ref[...] = k.T.reshape(DIFF_HEADS, LANES, tm)
    else:
        kf_ref[...] = k
        maybe_kb_ref[0][...] = k.astype(BF16)
    q = norm_rope(_dot_nt(h, wrest_ref[_WREST['q'], :]) * rh, qn_ref[...])
    qb_ref[...] = (q * Q_SCALE).astype(BF16)
    v = _dot_nt(h, wrest_ref[_WREST['v'], :]) * rh
    vb_ref[...] = v.astype(BF16)
    if k_pos_minor:
        for hh in range(DIFF_HEADS):
            vf_ref[pl.ds(hh, tm, stride=DIFF_HEADS), :] = v[:, hh * DIFF_DV:(hh + 1) * DIFF_DV]
    else:
        vf_ref[...] = v
    z_ref[...] = _dot_nt(h, wrest_ref[_WREST['z'], :]) * rh
    ba_ref[...] = _dot_nt(h, wba_ref[...]) * rh
    conv_ref[...] = _dot_nt(h, wconv_ref[...]) * rh


def _pre_call(x2d, tabs, p, tm, n_pos_tiles, batch, k_pos_minor):
    n, d = x2d.shape
    grid = (n // tm,)
    row = lambda w: pl.BlockSpec((tm, w), lambda i: (i, 0))
    tab = pl.BlockSpec((tm, LANES), lambda i: (i % n_pos_tiles, 0))
    weights = [p['n1'], p['wgu1'], p['wd1'], p['nm'], p['w_conv_t'], p['w_rest_t'], p['w_ba_t'], p['qn'], p['kn'], p['gmat']]
    in_specs = [row(d), tab, tab, tab] + [_const_spec(w.shape) for w in weights]
    outs = [(d, F32), (CONV_CH, F32), (GDN_V, F32), (2 * LANES, F32), (DIFF_QK, BF16), (DIFF_V, BF16)]
    out_specs = [row(w) for w, _ in outs]
    out_shape = [jax.ShapeDtypeStruct((n, w), dt) for w, dt in outs]
    if k_pos_minor:
        l = n // batch
        tiles = l // tm
        out_specs += [pl.BlockSpec((None, DIFF_HEADS, LANES, tm), lambda i: (i // tiles, 0, 0, i % tiles)),
                      pl.BlockSpec((tm * DIFF_HEADS, DIFF_DV), lambda i: (i, 0))]
        out_shape += [jax.ShapeDtypeStruct((batch, DIFF_HEADS, LANES, l), F32),
                      jax.ShapeDtypeStruct((n * DIFF_HEADS, DIFF_DV), F32)]
    else:
        out_specs += [row(DIFF_QK), row(DIFF_V), row(DIFF_QK)]
        out_shape += [jax.ShapeDtypeStruct((n, DIFF_QK), F32), jax.ShapeDtypeStruct((n, DIFF_V), F32),
                      jax.ShapeDtypeStruct((n, DIFF_QK), BF16)]
    return pl.pallas_call(
        functools.partial(_pre_kernel, k_pos_minor=k_pos_minor),
        grid=grid,
        in_specs=in_specs,
        out_specs=out_specs,
        out_shape=out_shape,
        compiler_params=pltpu.CompilerParams(dimension_semantics=("arbitrary",),
                                             vmem_limit_bytes=VMEM_LIMIT),
        name="pre",
    )(x2d, *tabs, *weights)


def _post_kernel(x1_ref, og_ref, od_ref, wo_ref, n2_ref, wgu_ref, wd_ref, y_ref):
    mixed = jnp.concatenate([og_ref[...], od_ref[...]], axis=1)
    x2 = x1_ref[...] + _dot(mixed, wo_ref[...])
    xg, r = _rms_parts(x2, n2_ref[...])
    y_ref[...] = x2 + 0.5 * _swiglu(xg, r, wgu_ref, wd_ref)


def _post_call(x1, og, od, p, tm):
    n, d = x1.shape
    row = lambda w: pl.BlockSpec((tm, w), lambda i: (i, 0))
    weights = [p['wo'], p['n2'], p['wgu2'], p['wd2']]
    return pl.pallas_call(
        _post_kernel,
        grid=(n // tm,),
        in_specs=[row(d), row(GDN_V), row(DIFF_V)] + [_const_spec(w.shape) for w in weights],
        out_specs=row(d),
        out_shape=jax.ShapeDtypeStruct((n, d), F32),
        compiler_params=pltpu.CompilerParams(dimension_semantics=("arbitrary",),
                                             vmem_limit_bytes=VMEM_LIMIT),
        name="post",
    )(x1, og, od, *weights)


def _exact3(m01, x):
    x1 = x.astype(BF16)
    r1 = x - x1.astype(F32)
    x2 = r1.astype(BF16)
    x3 = (r1 - x2.astype(F32)).astype(BF16)
    return _dot(m01, x1) + (_dot(m01, x2) + _dot(m01, x3))


CONV_PAD = 8


def _gdn_kernel(x_ref, z_ref, ba_ref, hist_ref, s0_ref, cw_ref, alog_ref, dtb_ref, on_ref, hm_ref,
                o_ref, s_out_ref, xp_ref, *maybe_s_ref, tc, tile, c, seqs):
    pad = CONV_PAD
    nchunk = tile // c
    ntile = tc // tile
    assert c % INV_BLOCK == 0 and tile % LANES == 0 and tc % tile == 0
    assert seqs == 1 or (seqs == nchunk and ntile == 1)
    cw = cw_ref[...]
    x = x_ref[...]

    def conv(xp, rows):
        y = xp[rows] * cw[CONV_W - 1:CONV_W, :]
        for k in range(1, CONV_W):
            y = y + pltpu.roll(xp, k, 0)[rows] * cw[CONV_W - 1 - k:CONV_W - k, :]
        return y

    if seqs == 1:
        s_ref, = maybe_s_ref
        t = pl.program_id(1)

        @pl.when(t == 0)
        def _():
            xp_ref[0:pad, :] = jnp.zeros((pad, CONV_CH), F32)
            xp_ref[pad - (CONV_W - 1):pad, :] = hist_ref[...]
            s_ref[...] = s0_ref[...]

        xp_ref[pad:pad + tc, :] = x
        y = conv(xp_ref[...], slice(pad, pad + tc))
        xp_ref[pad - (CONV_W - 1):pad, :] = x[tc - (CONV_W - 1):tc, :]
    else:
        stride = pad + c
        xp_ref[...] = jnp.zeros(xp_ref.shape, F32)
        for b in range(seqs):
            xp_ref[b * stride + pad - (CONV_W - 1):b * stride + pad, :] = hist_ref[b]
            xp_ref[b * stride + pad:(b + 1) * stride, :] = x[b * c:(b + 1) * c, :]
        yp = conv(xp_ref[...], slice(None))
        y = jnp.concatenate([yp[b * stride + pad:(b + 1) * stride, :] for b in range(seqs)], axis=0)
    y = _silu(y)
    hm = hm_ref[...]
    yk = y[:, GDN_QK:2 * GDN_QK]
    yk = yk * lax.rsqrt(_dot((yk * yk).astype(BF16), hm) + EPS)

    ba = ba_ref[...]
    beta = jax.nn.sigmoid(ba[:, :LANES])
    g = -jnp.exp(alog_ref[...]) * jax.nn.softplus(ba[:, LANES:] + dtb_ref[...])
    ri = lax.broadcasted_iota(jnp.int32, (tile, tile), 0)
    ci = lax.broadcasted_iota(jnp.int32, (tile, tile), 1)
    incl = ((ri // c) == (ci // c)) & (ci <= ri)
    same_blk = (ri // INV_BLOCK) == (ci // INV_BLOCK)
    diag = ri == ci
    eye = jnp.where(diag, 1.0, 0.0)
    incl01 = jnp.where(incl, 1.0, 0.0).astype(BF16)
    tile_rows = [slice(ti * tile, (ti + 1) * tile) for ti in range(ntile)]
    gcums = [_exact3(incl01, g[rows]) for rows in tile_rows]
    gcum_ts = [gc.T for gc in gcums]
    on = on_ref[...]

    heads = range(GDN_HEADS)
    units = [(ti, h) for ti in range(ntile) for h in heads]

    def per_head(f, *lists):
        return [f(*args) for args in zip(*lists)]

    def head_cols(h, width):
        return slice(h * width, (h + 1) * width)

    ks = [yk[tile_rows[ti], head_cols(h, GDN_DK)] for ti, h in units]
    vs = [y[tile_rows[ti], 2 * GDN_QK + h * GDN_DV:2 * GDN_QK + (h + 1) * GDN_DV] for ti, h in units]
    gcs = [gcums[ti][:, h:h + 1] for ti, h in units]
    bcs = [beta[tile_rows[ti], h:h + 1] for ti, h in units]
    decs = [jnp.exp(jnp.where(incl, gcs[u] - gcum_ts[ti][h:h + 1, :], -jnp.inf))
            for u, (ti, h) in enumerate(units)]
    kbs = per_head(lambda k: k.astype(BF16), ks)
    a_s = per_head(lambda bc, kb, dec: jnp.where(diag, 0.0, bc * _dot_nt(kb, kb) * dec), bcs, kbs, decs)

    assert INV_BLOCK == 16 and c in (INV_BLOCK, 4 * INV_BLOCK)
    bf = lambda xs: per_head(lambda x: x.astype(BF16), xs)
    mm = lambda xs, ys: per_head(_dot, xs, ys)
    ds = per_head(lambda a: jnp.where(same_blk, a, 0.0), a_s) if c > INV_BLOCK else a_s
    sd = bf(ds)
    d2 = mm(sd, sd)
    s2 = bf(d2)
    d3 = mm(sd, s2)
    d4 = mm(s2, s2)
    s4 = bf(d4)
    n1 = per_head(lambda d, x2, x3: eye - d + x2 - x3, ds, d2, d3)
    n2 = per_head(jnp.add, n1, mm(bf(n1), s4))
    d8 = mm(s4, s4)
    tinv = per_head(jnp.add, n2, mm(bf(n2), bf(d8)))
    if c > INV_BLOCK:
        std = bf(tinv)
        ms = mm(std, bf(per_head(jnp.subtract, a_s, ds)))
        sm = bf(ms)
        m2 = mm(sm, sm)
        m3 = mm(sm, bf(m2))
        tinv = mm(bf(per_head(lambda m, x2, x3: eye - m + x2 - x3, ms, m2, m3)), std)

    egs = per_head(jnp.exp, gcs)
    rhs = per_head(lambda v, k, bc, eg: jnp.concatenate([v * bc, k * (bc * eg)], axis=1), vs, ks, bcs, egs)
    sols = mm(bf(tinv), bf(rhs))
    yq = y[:, :GDN_QK]
    yq = yq * (lax.rsqrt(_dot((yq * yq).astype(BF16), hm) + EPS) * (GDN_DK ** -0.5))
    qs = [yq[tile_rows[ti], head_cols(h, GDN_DK)] for ti, h in units]
    qks = per_head(lambda q, kb, dec: (_dot_nt(q.astype(BF16), kb) * dec).astype(BF16), qs, kbs, decs)
    u0s = [sol[:, :GDN_DV] for sol in sols]
    wbs = [sol[:, GDN_DV:].astype(BF16) for sol in sols]
    qds = per_head(lambda q, eg: (q * eg).astype(BF16), qs, egs)

    unit = lambda ti, h: ti * GDN_HEADS + h

    def chunk_terms(ti, ic):
        rs = slice(ic * c, (ic + 1) * c)
        gends = [gcs[unit(ti, h)][(ic + 1) * c - 1:(ic + 1) * c, :] for h in heads]
        kds = [(ks[unit(ti, h)][rs] * jnp.exp(gends[h] - gcs[unit(ti, h)][rs])).astype(BF16) for h in heads]
        return rs, gends, kds

    def state_terms(rs, states):
        return [_dot(jnp.concatenate([wbs[h][rs], qds[h][rs]], axis=0), states[h].astype(BF16)) for h in heads]

    def next_state(states, gends, kds, us_c):
        return [states[h] * jnp.exp(gends[h]) + lax.dot_general(
            kds[h], us_c[h], (((0,), (0,)), ((), ())), preferred_element_type=F32) for h in heads]

    us = [[] for _ in units]
    outs = [[] for _ in units]
    if seqs == 1:
        trans = []
        for ti in range(ntile):
            for ic in range(nchunk):
                rs, gends, kds = chunk_terms(ti, ic)
                kt_wu = [lax.dot_general(
                    kds[h], jnp.concatenate([wbs[unit(ti, h)][rs], u0s[unit(ti, h)][rs].astype(BF16)], axis=1),
                    (((0,), (0,)), ((), ())), preferred_element_type=F32) for h in heads]
                trans.append(([jnp.exp(g) for g in gends], [m[:, :GDN_DK].astype(BF16) for m in kt_wu],
                              [m[:, GDN_DK:] for m in kt_wu]))
        states = [s_ref[h] for h in heads]
        chunk_states = []
        for decay, kt_w, kt_u0 in trans:
            sb = [s.astype(BF16) for s in states]
            chunk_states.append(sb)
            states = [states[h] * decay[h] - _dot(kt_w[h], sb[h]) + kt_u0[h] for h in heads]
        for h in heads:
            s_ref[h] = states[h]
        for ti in range(ntile):
            for ic in range(nchunk):
                rs = slice(ic * c, (ic + 1) * c)
                sb = chunk_states[ti * nchunk + ic]
                wss = [_dot(jnp.concatenate([wbs[unit(ti, h)][rs], qds[unit(ti, h)][rs]], axis=0), sb[h])
                       for h in heads]
                for h in heads:
                    us[unit(ti, h)].append((u0s[unit(ti, h)][rs] - wss[h][:c]).astype(BF16))
                    outs[unit(ti, h)].append(wss[h][c:])
        o = [jnp.concatenate(outs[u], axis=0) + _dot(qks[u], jnp.concatenate(us[u], axis=0))
             for u in range(len(units))]

        @pl.when(t == pl.num_programs(1) - 1)
        def _():
            s_out_ref[...] = s_ref[...]
    else:
        for ic in range(nchunk):
            rs, gends, kds = chunk_terms(0, ic)
            states = [s0_ref[ic, h] for h in heads]
            wss = state_terms(rs, states)
            us_c = [(u0s[h][rs] - wss[h][:c]).astype(BF16) for h in heads]
            new = next_state(states, gends, kds, us_c)
            for h in heads:
                us[h].append(us_c[h])
                outs[h].append(wss[h][c:])
                s_out_ref[ic, h] = new[h]
        o = [jnp.concatenate(outs[h], axis=0) + _dot(qks[h], jnp.concatenate(us[h], axis=0)) for h in heads]
    for u, (ti, h) in enumerate(units):
        rows, cols = tile_rows[ti], head_cols(h, GDN_DV)
        o_ref[rows, cols] = (_rms(o[u], on) * _silu(z_ref[rows, cols])).astype(o_ref.dtype)


def _gdn_call(conv, z, ba, hist, s0, p, tile, c):
    b, l, _ = conv.shape
    small = [p['conv_w'], p['a_log'], p['dt_bias'], p['gdn_on'], p['head_mat']]
    state = (GDN_HEADS, GDN_DK, GDN_DV)
    if l >= tile:
        tc = _pick_tile(l, GDN_BLOCK_TILES * tile)
        seqs = 1
        grid = (b, l // tc)
        row = lambda w: pl.BlockSpec((None, tc, w), lambda i, j: (i, j, 0))
        per_b = lambda shape: pl.BlockSpec((None,) + shape, lambda i, j: (i,) + (0,) * len(shape))
        scratch = [pltpu.VMEM((tc + CONV_PAD, CONV_CH), F32), pltpu.VMEM(state, F32)]
    else:
        tc = tile = b * l
        assert l == c and tc % LANES == 0
        seqs = b
        grid = (1, 1)
        conv, z, ba = (t.reshape(1, tc, t.shape[-1]) for t in (conv, z, ba))
        row = lambda w: pl.BlockSpec((None, tc, w), lambda i, j: (0, 0, 0))
        per_b = lambda shape: pl.BlockSpec((b,) + shape, lambda i, j: (0,) * (len(shape) + 1))
        scratch = [pltpu.VMEM((seqs * (c + CONV_PAD), CONV_CH), F32)]
    o, s_new = pl.pallas_call(
        functools.partial(_gdn_kernel, tc=tc, tile=tile, c=c, seqs=seqs),
        grid=grid,
        in_specs=[row(CONV_CH), row(GDN_V), row(2 * LANES), per_b((CONV_W - 1, CONV_CH)), per_b(state)] + [
            pl.BlockSpec(w.shape, lambda i, j: (0, 0)) for w in small],
        out_specs=[row(GDN_V), per_b(state)],
        out_shape=[jax.ShapeDtypeStruct(conv.shape[:2] + (GDN_V,), BF16),
                   jax.ShapeDtypeStruct((b,) + state, F32)],
        scratch_shapes=scratch,
        compiler_params=pltpu.CompilerParams(dimension_semantics=("arbitrary", "arbitrary"),
                                             vmem_limit_bytes=VMEM_LIMIT),
        name="gdn",
    )(conv, z, ba, hist, s0, *small)
    return o.reshape(b, l, GDN_V), s_new


def _stack_maps(q):
    lane = lax.broadcasted_iota(jnp.int32, q.shape, 1)
    zero = jnp.zeros_like(q)
    return jnp.concatenate([jnp.where(lane < DIFF_DK, q, zero), jnp.where(lane >= DIFF_DK, q, zero)],
                           axis=0)


def _diff_finish(acc, l, t, lq_ref, lk_ref, dn_ref, lam_init):
    lam_e = jnp.exp(jnp.sum(lq_ref[...] * lk_ref[...], axis=-1, keepdims=True))
    lam = lam_e[0:1] - lam_e[1:2] + lam_init
    o = acc[:t] / l[:t] - lam * (acc[t:] / l[t:])
    return _rms(o, dn_ref[...]) * (1.0 - lam_init)


def _halves_max(s):
    return jnp.maximum(s[:, :LANES], s[:, LANES:])


def _attn_prompt_kernel(q_ref, kt_ref, v_ref, bias_ref, lq_ref, lk_ref, dn_ref, o_ref,
                        kt_scr, vx_scr, s_scr, m_scr, acc_scr, *, tq, tk, lam_init):
    i = pl.program_id(2)
    assert tk == 2 * LANES and tq % tk == 0
    nsub = tq // tk

    @pl.when(i == 0)
    def _():
        for j in range(kt_scr.shape[0]):
            kt_scr[j] = kt_ref[:, j * tk:(j + 1) * tk].astype(BF16)
        vx_scr[:, :DIFF_DV] = v_ref[...]
        vx_scr[:, DIFF_DV:] = jnp.ones((vx_scr.shape[0], LANES), BF16)

    qq = _stack_maps(q_ref[...])

    def scores(j):
        return _dot(qq, kt_scr[j])

    def weighted_values(ss, blocks, mm):
        p = [jnp.exp2(s[:, half * LANES:(half + 1) * LANES] - mm) for s in ss for half in range(tk // LANES)]
        vx = [vx_scr[pl.ds(pl.multiple_of(j * tk, tk), tk), :] for j in blocks]
        return _dot(jnp.concatenate(p, axis=1).astype(BF16), jnp.concatenate(vx, axis=0))

    def pass1(blocks, biases):
        ss = [scores(j) if b is None else scores(j) + b for j, b in zip(blocks, biases)]
        for j, s in zip(blocks, ss):
            s_scr[j] = s
        m_scr[...] = jnp.maximum(m_scr[...], functools.reduce(jnp.maximum, [_halves_max(s) for s in ss]))

    def pass2(blocks, biases):
        del biases
        acc_scr[...] += weighted_values([s_scr[j] for j in blocks], blocks, m_scr[...])

    def over_visible(body):
        ngroups = i + 1

        def group(g):
            return [g * nsub + u for u in range(nsub)]

        def run(first, count, diagonal_last):
            last = group(first + count - 1)
            if diagonal_last:
                biases = [bias_ref[1 + d] for d in range(nsub)]
            else:
                biases = [bias_ref[jnp.maximum(j - i * nsub + 1, 0)] for j in last]
            body([j for g in range(count - 1) for j in group(first + g)] + last,
                 [None] * (nsub * (count - 1)) + biases)

        def trip(t, carry):
            run(4 * t, 4, False)
            return carry

        lax.fori_loop(0, ngroups // 4, trip, 0)

        @pl.when(ngroups % 4 >= 2)
        def _():
            run((ngroups // 4) * 4, 2, False)

        @pl.when(ngroups % 2 == 1)
        def _():
            run(i, 1, True)

    m_scr[...] = jnp.full(m_scr.shape, -jnp.inf, F32)
    over_visible(pass1)
    m_scr[...] = jnp.broadcast_to(jnp.max(m_scr[...], axis=-1, keepdims=True), m_scr.shape)
    acc_scr[...] = jnp.zeros(acc_scr.shape, F32)
    over_visible(pass2)
    acc = acc_scr[...]
    o_ref[...] = _diff_finish(acc[:, :DIFF_DV], acc[:, DIFF_DV:], tq, lq_ref, lk_ref, dn_ref,
                              lam_init).astype(o_ref.dtype)


def _attn_prompt_call(qb, kt, vb, p, tq, tk, lam_init):
    b, l, _ = qb.shape
    small = [p['lambda_q'], p['lambda_k'], p['diff_on']]
    r = (jnp.arange(2 * tq) % tq)[None, :, None] // CHUNK
    cidx = (jnp.arange(tk)[None, None, :] + tk * jnp.arange(tq // tk)[:, None, None]) // CHUNK
    bias = jnp.where(cidx <= r, 0.0, -jnp.inf).astype(F32)
    bias = jnp.concatenate([jnp.zeros_like(bias[:1]), bias], axis=0)
    return pl.pallas_call(
        functools.partial(_attn_prompt_kernel, tq=tq, tk=tk, lam_init=lam_init),
        grid=(b, DIFF_HEADS, l // tq),
        in_specs=[pl.BlockSpec((None, tq, LANES), lambda bi, h, i: (bi, i, h)),
                  pl.BlockSpec((None, None, LANES, l), lambda bi, h, i: (bi, h, 0, 0)),
                  pl.BlockSpec((None, l, LANES), lambda bi, h, i: (bi, 0, h)),
                  _const_spec(bias.shape)] + [
                      pl.BlockSpec(w.shape, lambda bi, h, i: (0, 0)) for w in small],
        out_specs=pl.BlockSpec((None, tq, LANES), lambda bi, h, i: (bi, i, h)),
        out_shape=jax.ShapeDtypeStruct((b, l, DIFF_V), BF16),
        scratch_shapes=[pltpu.VMEM((l // tk, LANES, tk), BF16),
                        pltpu.VMEM((l, DIFF_DV + LANES), BF16),
                        pltpu.VMEM((l // tk, 2 * tq, tk), F32),
                        pltpu.VMEM((2 * tq, LANES), F32),
                        pltpu.VMEM((2 * tq, DIFF_DV + LANES), F32)],
        compiler_params=pltpu.CompilerParams(
            dimension_semantics=("arbitrary", "arbitrary", "arbitrary"),
            vmem_limit_bytes=VMEM_LIMIT),
        name="attn_prompt",
    )(qb, kt, vb, bias, *small)


def _attn_sample_kernel(q_ref, kn_ref, vn_ref, ckt_ref, cv_ref, lq_ref, lk_ref, dn_ref, o_ref,
                        *, t, lam_init):
    past = ckt_ref.shape[-1]
    for h in range(DIFF_HEADS):
        cols = slice(h * LANES, (h + 1) * LANES)
        qq = _stack_maps(q_ref[:, cols])
        s_c = _dot(qq, ckt_ref[h].astype(BF16))
        s_n = _dot_nt(qq, kn_ref[:, cols])
        m = jnp.maximum(jnp.max(s_c, axis=-1, keepdims=True), jnp.max(s_n, axis=-1, keepdims=True))
        p_c = jnp.exp2(s_c - m)
        p_n = jnp.exp2(s_n - m)
        l = jnp.sum(p_c, axis=-1, keepdims=True) + jnp.sum(p_n, axis=-1, keepdims=True)
        cv = cv_ref[pl.ds(h, past, stride=DIFF_HEADS), :].astype(BF16)
        acc = _dot(p_c.astype(BF16), cv) + _dot(p_n.astype(BF16), vn_ref[:, cols])
        o_ref[:, cols] = _diff_finish(acc, l, t, lq_ref, lk_ref, dn_ref, lam_init).astype(o_ref.dtype)


def _attn_sample_call(qb, kb, vb, ckt, cv, p, lam_init):
    b, t, _ = qb.shape
    past = ckt.shape[-1]
    small = [p['lambda_q'], p['lambda_k'], p['diff_on']]
    new = pl.BlockSpec((None, t, DIFF_V), lambda bi: (bi, 0, 0))
    return pl.pallas_call(
        functools.partial(_attn_sample_kernel, t=t, lam_init=lam_init),
        grid=(b,),
        in_specs=[new, new, new,
                  pl.BlockSpec((None, DIFF_HEADS, LANES, past), lambda bi: (bi, 0, 0, 0)),
                  pl.BlockSpec((None, past * DIFF_HEADS, DIFF_DV), lambda bi: (bi, 0, 0))] + [
                      pl.BlockSpec(w.shape, lambda bi: (0, 0)) for w in small],
        out_specs=new,
        out_shape=jax.ShapeDtypeStruct((b, t, DIFF_V), BF16),
        compiler_params=pltpu.CompilerParams(dimension_semantics=("arbitrary",),
                                             vmem_limit_bytes=VMEM_LIMIT),
        name="attn_sample",
    )(qb, kb, vb, ckt, cv, *small)


def _rope_tables(pos):
    half = ROT_DIM // 2
    inv = jnp.float32(ROPE_THETA) ** (-jnp.arange(half, dtype=F32) * 2.0 / ROT_DIM)
    ang = pos.astype(F32)[:, None] * inv[None, :]
    cos, sin = jnp.cos(ang), jnp.sin(ang)
    n = pos.shape[0]
    rest = DIFF_DK - ROT_DIM
    one = jnp.ones((n, rest), F32)
    zero = jnp.zeros((n, rest), F32)
    zh = jnp.zeros((n, half), F32)
    c64 = jnp.concatenate([cos, cos, one], axis=1)
    a64 = jnp.concatenate([-sin, zh, zero], axis=1)
    b64 = jnp.concatenate([zh, sin, zero], axis=1)
    rep = LANES // DIFF_DK
    return tuple(jnp.tile(t, (1, rep)) for t in (c64, a64, b64))


def _layer_params(w, l):
    d = w['w_in'].shape[1]
    win = jnp.transpose(w['w_in'][l])
    w_b = win[CONV_CH:CONV_CH + GDN_HEADS]
    w_a = win[CONV_CH + GDN_HEADS:CONV_CH + 2 * GDN_HEADS]
    zpad = jnp.zeros((LANES - GDN_HEADS, d), win.dtype)
    hpad = lambda v: jnp.concatenate([v.astype(F32), jnp.zeros((LANES - GDN_HEADS,), F32)])[None, :]
    grp = jnp.arange(DIFF_QK) // DIFF_DK
    bf = lambda t: t.astype(BF16)
    r2 = lambda v: v.astype(F32)[None, :]
    return dict(
        n1=r2(w['ffn1_norm'][l]), wgu1=bf(w['ffn1_w_gu'][l]), wd1=bf(w['ffn1_w_down'][l]),
        nm=r2(w['mix_norm'][l]), w_conv_t=bf(win[:CONV_CH]), w_rest_t=bf(win[CONV_CH + 2 * GDN_HEADS:]),
        w_ba_t=bf(jnp.concatenate([w_b, zpad, w_a, zpad], axis=0)),
        qn=r2(jnp.tile(w['q_norm'][l], DIFF_QK // DIFF_DK)), kn=r2(jnp.tile(w['k_norm'][l], DIFF_QK // DIFF_DK)),
        gmat=(grp[:, None] == grp[None, :]).astype(BF16),
        head_mat=(grp[:, None] // 2 == grp[None, :] // 2).astype(BF16),
        conv_w=w['conv_w'][l].astype(F32), a_log=hpad(w['a_log'][l]), dt_bias=hpad(w['dt_bias'][l]),
        gdn_on=r2(w['gdn_out_norm'][l]),
        lambda_q=w['lambda_q'][l].astype(F32), lambda_k=w['lambda_k'][l].astype(F32),
        diff_on=r2(w['diff_out_norm'][l]),
        wo=bf(w['w_out'][l]), n2=r2(w['ffn2_norm'][l]), wgu2=bf(w['ffn2_w_gu'][l]), wd2=bf(w['ffn2_w_down'][l]),
    )


def _pick_tile(n, pref):
    t = min(n, pref)
    assert n % t == 0
    return t


def _layer(x, pos, k_hist, v_hist, conv_hist, s0, p, lam_init):
    b, l, d = x.shape
    n = b * l
    tm = _pick_tile(n, ROW_TILE)
    tabs = _rope_tables(pos)
    if l >= tm:
        assert l % tm == 0
        n_pos_tiles = l // tm
    else:
        assert tm % l == 0
        tabs = tuple(jnp.tile(t, (tm // l, 1)) for t in tabs)
        n_pos_tiles = 1
    prompt = k_hist is None
    pre = _pre_call(x.reshape(n, d), tabs, p, tm, n_pos_tiles, b, prompt)
    x1, conv, z, ba, qb, vb, kf, vf = pre[:8]
    c = min(CHUNK, l)
    assert l % c == 0
    tc = GDN_TILE
    assert l % tc == 0 or l == c
    if conv_hist is None:
        conv_hist = jnp.zeros((b, CONV_W - 1, CONV_CH), F32)
        s0 = jnp.zeros((b, GDN_HEADS, GDN_DK, GDN_DV), F32)
    conv3 = conv.reshape(b, l, CONV_CH)
    og, s_new = _gdn_call(conv3, z.reshape(b, l, GDN_V), ba.reshape(b, l, 2 * LANES),
                          conv_hist.astype(F32), s0.astype(F32), p, tc, c)
    conv_new = jnp.concatenate([conv_hist.astype(F32), conv3], axis=1)[:, -(CONV_W - 1):]
    q3 = qb.reshape(b, l, DIFF_QK)
    v3 = vb.reshape(b, l, DIFF_V)
    if prompt:
        tk = 2 * LANES
        od = _attn_prompt_call(q3, kf, v3, p, _pick_tile(l, 2 * tk), tk, lam_init)
        k_out = jnp.transpose(kf.reshape(b, DIFF_HEADS, 2, DIFF_DK, l), (0, 4, 1, 2, 3))
    else:
        past = k_hist.shape[1]
        ckt = jnp.transpose(k_hist, (0, 2, 3, 4, 1)).reshape(b, DIFF_HEADS, LANES, past)
        od = _attn_sample_call(q3, pre[8].reshape(b, l, DIFF_QK), v3, ckt,
                               v_hist.reshape(b, past * DIFF_HEADS, DIFF_DV), p, lam_init)
        k_out = kf.reshape(b, l, DIFF_HEADS, 2, DIFF_DK)
    y = _post_call(x1, og.reshape(n, GDN_V), od.reshape(n, DIFF_V), p, tm)
    return (y.reshape(b, l, d), k_out, vf.reshape(b, l, DIFF_HEADS, DIFF_DV), s_new, conv_new)


def kernel(x_prompt, x_sample, cache_k, cache_v, state_gdn, state_conv, ffn1_norm, ffn1_w_gu, ffn1_w_down,
           mix_norm, w_in, conv_w, a_log, dt_bias, gdn_out_norm, q_norm, k_norm, lambda_q, lambda_k,
           diff_out_norm, w_out, ffn2_norm, ffn2_w_gu, ffn2_w_down):
    w = dict(ffn1_norm=ffn1_norm, ffn1_w_gu=ffn1_w_gu, ffn1_w_down=ffn1_w_down, mix_norm=mix_norm, w_in=w_in,
             conv_w=conv_w, a_log=a_log, dt_bias=dt_bias, gdn_out_norm=gdn_out_norm, q_norm=q_norm,
             k_norm=k_norm, lambda_q=lambda_q, lambda_k=lambda_k, diff_out_norm=diff_out_norm, w_out=w_out,
             ffn2_norm=ffn2_norm, ffn2_w_gu=ffn2_w_gu, ffn2_w_down=ffn2_w_down)
    depth = w_in.shape[0]
    pos_p = jnp.arange(x_prompt.shape[1])
    pos_s = cache_k.shape[2] + jnp.arange(x_sample.shape[1])
    hp, hs = x_prompt, x_sample
    outs = [[] for _ in range(8)]
    for l in range(depth):
        lam_init = 0.8 - 0.6 * math.exp(-0.3 * l)
        p = _layer_params(w, l)
        hp, kp, vp, sp, cp = _layer(hp, pos_p, None, None, None, None, p, lam_init)
        hs, ks, vs, ss, cs = _layer(hs, pos_s, cache_k[l], cache_v[l], state_conv[l], state_gdn[l], p, lam_init)
        for acc, val in zip(outs, (kp, vp, sp, cp, ks, vs, ss, cs)):
            acc.append(val)
    return (hp, hs) + tuple(jnp.stack(o) for o in outs)
```

```python
import functools
import math

import jax
import jax.numpy as jnp
from jax import lax
from jax.experimental import pallas as pl
from jax.experimental.pallas import tpu as pltpu

F32 = jnp.float32
BF16 = jnp.bfloat16

EPS = 1e-6
CHUNK = 64
GDN_HEADS = 4
GDN_DK = 128
GDN_DV = 128
CONV_W = 4
DIFF_HEADS = 4
DIFF_DK = 64
DIFF_DV = 128
ROT_DIM = DIFF_DK // 4
ROPE_THETA = 500000.0
GDN_QK = GDN_HEADS * GDN_DK
GDN_V = GDN_HEADS * GDN_DV
CONV_CH = 2 * GDN_QK + GDN_V
DIFF_QK = DIFF_HEADS * 2 * DIFF_DK
DIFF_V = DIFF_HEADS * DIFF_DV

LANES = 128
INV_BLOCK = 16
ROW_TILE = 512
GDN_TILE = 128
GDN_BLOCK_TILES = 4
VMEM_LIMIT = 60 * 1024 * 1024
Q_SCALE = (DIFF_DK ** -0.5) * math.log2(math.e)


def _dot(a, b):
    return jnp.dot(a, b, preferred_element_type=F32)


def _dot_nt(a, b):
    return lax.dot_general(a, b, (((1,), (1,)), ((), ())), preferred_element_type=F32)


def _rms(x, g):
    return x * lax.rsqrt(jnp.mean(x * x, axis=-1, keepdims=True) + EPS) * g


def _silu(x):
    return x * jax.nn.sigmoid(x)


def _rms_parts(x, g):
    return (x * g).astype(BF16), lax.rsqrt(jnp.mean(x * x, axis=-1, keepdims=True) + EPS)


def _swiglu(xg, r, wgu_ref, wd_ref):
    d_ff = wd_ref.shape[0]
    g = _dot(xg, wgu_ref[:, :d_ff]) * r
    u = _dot(xg, wgu_ref[:, d_ff:]) * r
    act = (_silu(g) * u).astype(BF16)
    return _dot(act, wd_ref[...])


def _const_spec(shape):
    nd = len(shape)
    return pl.BlockSpec(shape, lambda *_: (0,) * nd, pipeline_mode=pl.Buffered(1))


_WREST = dict(z=slice(0, GDN_V), q=slice(GDN_V, GDN_V + DIFF_QK), k=slice(GDN_V + DIFF_QK, GDN_V + 2 * DIFF_QK),
              v=slice(GDN_V + 2 * DIFF_QK, GDN_V + 2 * DIFF_QK + DIFF_V))


def _pre_kernel(x_ref, cos_ref, sa_ref, sb_ref, n1_ref, wgu_ref, wd_ref, nm_ref, wconv_ref, wrest_ref, wba_ref, qn_ref, kn_ref,
                gm_ref,
                x1_ref, conv_ref, z_ref, ba_ref, qb_ref, vb_ref, kf_ref, vf_ref, *maybe_kb_ref,
                k_pos_minor):
    x = x_ref[...]
    xg, r = _rms_parts(x, n1_ref[...])
    x1 = x + 0.5 * _swiglu(xg, r, wgu_ref, wd_ref)
    x1_ref[...] = x1
    h, rh = _rms_parts(x1, nm_ref[...])
    tm = x.shape[0]
    cos = cos_ref[...]
    sa = sa_ref[...]
    sb = sb_ref[...]
    gm = gm_ref[...]

    def norm_rope(t, gw):
        ss = _dot((t * t).astype(BF16), gm)
        t = t * lax.rsqrt(ss * (1.0 / DIFF_DK) + EPS) * gw
        outs = []
        for hh in range(DIFF_HEADS):
            th = t[:, hh * LANES:(hh + 1) * LANES]
            up = pltpu.roll(th, LANES - ROT_DIM // 2, 1)
            dn = pltpu.roll(th, ROT_DIM // 2, 1)
            outs.append(th * cos + up * sa + dn * sb)
        return jnp.concatenate(outs, axis=1)

    k = norm_rope(_dot_nt(h, wrest_ref[_WREST['k'], :]) * rh, kn_ref[...])
    if k_pos_minor:
        kf_ref[...] = k.T.reshape(DIFF_HEADS, LANES, tm)
    else:
        kf_ref[...] = k
        maybe_kb_ref[0][...] = k.astype(BF16)
    q = norm_rope(_dot_nt(h, wrest_ref[_WREST['q'], :]) * rh, qn_ref[...])
    qb_ref[...] = (q * Q_SCALE).astype(BF16)
    v = _dot_nt(h, wrest_ref[_WREST['v'], :]) * rh
    vb_ref[...] = v.astype(BF16)
    if k_pos_minor:
        for hh in range(DIFF_HEADS):
            vf_ref[pl.ds(hh, tm, stride=DIFF_HEADS), :] = v[:, hh * DIFF_DV:(hh + 1) * DIFF_DV]
    else:
        vf_ref[...] = v
    z_ref[...] = _dot_nt(h, wrest_ref[_WREST['z'], :]) * rh
    ba_ref[...] = _dot_nt(h, wba_ref[...]) * rh
    conv_ref[...] = _dot_nt(h, wconv_ref[...]) * rh


def _pre_call(x2d, tabs, p, tm, n_pos_tiles, batch, k_pos_minor):
    n, d = x2d.shape
    grid = (n // tm,)
    row = lambda w: pl.BlockSpec((tm, w), lambda i: (i, 0))
    tab = pl.BlockSpec((tm, LANES), lambda i: (i % n_pos_tiles, 0))
    weights = [p['n1'], p['wgu1'], p['wd1'], p['nm'], p['w_conv_t'], p['w_rest_t'], p['w_ba_t'], p['qn'], p['kn'], p['gmat']]
    in_specs = [row(d), tab, tab, tab] + [_const_spec(w.shape) for w in weights]
    outs = [(d, F32), (CONV_CH, F32), (GDN_V, F32), (2 * LANES, F32), (DIFF_QK, BF16), (DIFF_V, BF16)]
    out_specs = [row(w) for w, _ in outs]
    out_shape = [jax.ShapeDtypeStruct((n, w), dt) for w, dt in outs]
    if k_pos_minor:
        l = n // batch
        tiles = l // tm
        out_specs += [pl.BlockSpec((None, DIFF_HEADS, LANES, tm), lambda i: (i // tiles, 0, 0, i % tiles)),
                      pl.BlockSpec((tm * DIFF_HEADS, DIFF_DV), lambda i: (i, 0))]
        out_shape += [jax.ShapeDtypeStruct((batch, DIFF_HEADS, LANES, l), F32),
                      jax.ShapeDtypeStruct((n * DIFF_HEADS, DIFF_DV), F32)]
    else:
        out_specs += [row(DIFF_QK), row(DIFF_V), row(DIFF_QK)]
        out_shape += [jax.ShapeDtypeStruct((n, DIFF_QK), F32), jax.ShapeDtypeStruct((n, DIFF_V), F32),
                      jax.ShapeDtypeStruct((n, DIFF_QK), BF16)]
    return pl.pallas_call(
        functools.partial(_pre_kernel, k_pos_minor=k_pos_minor),
        grid=grid,
        in_specs=in_specs,
        out_specs=out_specs,
        out_shape=out_shape,
        compiler_params=pltpu.CompilerParams(dimension_semantics=("arbitrary",),
                                             vmem_limit_bytes=VMEM_LIMIT),
        name="pre",
    )(x2d, *tabs, *weights)


def _post_kernel(x1_ref, og_ref, od_ref, wo_ref, n2_ref, wgu_ref, wd_ref, y_ref):
    mixed = jnp.concatenate([og_ref[...], od_ref[...]], axis=1)
    x2 = x1_ref[...] + _dot(mixed, wo_ref[...])
    xg, r = _rms_parts(x2, n2_ref[...])
    y_ref[...] = x2 + 0.5 * _swiglu(xg, r, wgu_ref, wd_ref)


def _post_call(x1, og, od, p, tm):
    n, d = x1.shape
    row = lambda w: pl.BlockSpec((tm, w), lambda i: (i, 0))
    weights = [p['wo'], p['n2'], p['wgu2'], p['wd2']]
    return pl.pallas_call(
        _post_kernel,
        grid=(n // tm,),
        in_specs=[row(d), row(GDN_V), row(DIFF_V)] + [_const_spec(w.shape) for w in weights],
        out_specs=row(d),
        out_shape=jax.ShapeDtypeStruct((n, d), F32),
        compiler_params=pltpu.CompilerParams(dimension_semantics=("arbitrary",),
                                             vmem_limit_bytes=VMEM_LIMIT),
        name="post",
    )(x1, og, od, *weights)


def _exact3(m01, x):
    x1 = x.astype(BF16)
    r1 = x - x1.astype(F32)
    x2 = r1.astype(BF16)
    x3 = (r1 - x2.astype(F32)).astype(BF16)
    return _dot(m01, x1) + (_dot(m01, x2) + _dot(m01, x3))


CONV_PAD = 8


def _gdn_kernel(x_ref, z_ref, ba_ref, hist_ref, s0_ref, cw_ref, alog_ref, dtb_ref, on_ref, hm_ref,
                o_ref, s_out_ref, xp_ref, *maybe_s_ref, tc, tile, c, seqs):
    pad = CONV_PAD
    nchunk = tile // c
    ntile = tc // tile
    assert c % INV_BLOCK == 0 and tile % LANES == 0 and tc % tile == 0
    assert seqs == 1 or (seqs == nchunk and ntile == 1)
    cw = cw_ref[...]
    x = x_ref[...]

    def conv(xp, rows):
        y = xp[rows] * cw[CONV_W - 1:CONV_W, :]
        for k in range(1, CONV_W):
            y = y + pltpu.roll(xp, k, 0)[rows] * cw[CONV_W - 1 - k:CONV_W - k, :]
        return y

    if seqs == 1:
        s_ref, = maybe_s_ref
        t = pl.program_id(1)

        @pl.when(t == 0)
        def _():
            xp_ref[0:pad, :] = jnp.zeros((pad, CONV_CH), F32)
            xp_ref[pad - (CONV_W - 1):pad, :] = hist_ref[...]
            s_ref[...] = s0_ref[...]

        xp_ref[pad:pad + tc, :] = x
        y = conv(xp_ref[...], slice(pad, pad + tc))
        xp_ref[pad - (CONV_W - 1):pad, :] = x[tc - (CONV_W - 1):tc, :]
    else:
        stride = pad + c
        xp_ref[...] = jnp.zeros(xp_ref.shape, F32)
        for b in range(seqs):
            xp_ref[b * stride + pad - (CONV_W - 1):b * stride + pad, :] = hist_ref[b]
            xp_ref[b * stride + pad:(b + 1) * stride, :] = x[b * c:(b + 1) * c, :]
        yp = conv(xp_ref[...], slice(None))
        y = jnp.concatenate([yp[b * stride + pad:(b + 1) * stride, :] for b in range(seqs)], axis=0)
    y = _silu(y)
    hm = hm_ref[...]
    yk = y[:, GDN_QK:2 * GDN_QK]
    yk = yk * lax.rsqrt(_dot((yk * yk).astype(BF16), hm) + EPS)

    ba = ba_ref[...]
    beta = jax.nn.sigmoid(ba[:, :LANES])
    g = -jnp.exp(alog_ref[...]) * jax.nn.softplus(ba[:, LANES:] + dtb_ref[...])
    ri = lax.broadcasted_iota(jnp.int32, (tile, tile), 0)
    ci = lax.broadcasted_iota(jnp.int32, (tile, tile), 1)
    incl = ((ri // c) == (ci // c)) & (ci <= ri)
    same_blk = (ri // INV_BLOCK) == (ci // INV_BLOCK)
    diag = ri == ci
    eye = jnp.where(diag, 1.0, 0.0)
    incl01 = jnp.where(incl, 1.0, 0.0).astype(BF16)
    tile_rows = [slice(ti * tile, (ti + 1) * tile) for ti in range(ntile)]
    gcums = [_exact3(incl01, g[rows]) for rows in tile_rows]
    gcum_ts = [gc.T for gc in gcums]
    on = on_ref[...]

    heads = range(GDN_HEADS)
    units = [(ti, h) for ti in range(ntile) for h in heads]

    def per_head(f, *lists):
        return [f(*args) for args in zip(*lists)]

    def head_cols(h, width):
        return slice(h * width, (h + 1) * width)

    ks = [yk[tile_rows[ti], head_cols(h, GDN_DK)] for ti, h in units]
    vs = [y[tile_rows[ti], 2 * GDN_QK + h * GDN_DV:2 * GDN_QK + (h + 1) * GDN_DV] for ti, h in units]
    gcs = [gcums[ti][:, h:h + 1] for ti, h in units]
    bcs = [beta[tile_rows[ti], h:h + 1] for ti, h in units]
    decs = [jnp.exp(jnp.where(incl, gcs[u] - gcum_ts[ti][h:h + 1, :], -jnp.inf))
            for u, (ti, h) in enumerate(units)]
    kbs = per_head(lambda k: k.astype(BF16), ks)
    a_s = per_head(lambda bc, kb, dec: jnp.where(diag, 0.0, bc * _dot_nt(kb, kb) * dec), bcs, kbs, decs)

    assert INV_BLOCK == 16 and c in (INV_BLOCK, 4 * INV_BLOCK)
    bf = lambda xs: per_head(lambda x: x.astype(BF16), xs)
    mm = lambda xs, ys: per_head(_dot, xs, ys)
    ds = per_head(lambda a: jnp.where(same_blk, a, 0.0), a_s) if c > INV_BLOCK else a_s
    sd = bf(ds)
    d2 = mm(sd, sd)
    s2 = bf(d2)
    d3 = mm(sd, s2)
    d4 = mm(s2, s2)
    s4 = bf(d4)
    n1 = per_head(lambda d, x2, x3: eye - d + x2 - x3, ds, d2, d3)
    n2 = per_head(jnp.add, n1, mm(bf(n1), s4))
    d8 = mm(s4, s4)
    tinv = per_head(jnp.add, n2, mm(bf(n2), bf(d8)))
    if c > INV_BLOCK:
        std = bf(tinv)
        ms = mm(std, bf(per_head(jnp.subtract, a_s, ds)))
        sm = bf(ms)
        m2 = mm(sm, sm)
        m3 = mm(sm, bf(m2))
        tinv = mm(bf(per_head(lambda m, x2, x3: eye - m + x2 - x3, ms, m2, m3)), std)

    egs = per_head(jnp.exp, gcs)
    rhs = per_head(lambda v, k, bc, eg: jnp.concatenate([v * bc, k * (bc * eg)], axis=1), vs, ks, bcs, egs)
    sols = mm(bf(tinv), bf(rhs))
    yq = y[:, :GDN_QK]
    yq = yq * (lax.rsqrt(_dot((yq * yq).astype(BF16), hm) + EPS) * (GDN_DK ** -0.5))
    qs = [yq[tile_rows[ti], head_cols(h, GDN_DK)] for ti, h in units]
    qks = per_head(lambda q, kb, dec: (_dot_nt(q.astype(BF16), kb) * dec).astype(BF16), qs, kbs, decs)
    u0s = [sol[:, :GDN_DV] for sol in sols]
    wbs = [sol[:, GDN_DV:].astype(BF16) for sol in sols]
    qds = per_head(lambda q, eg: (q * eg).astype(BF16), qs, egs)

    unit = lambda ti, h: ti * GDN_HEADS + h

    def chunk_terms(ti, ic):
        rs = slice(ic * c, (ic + 1) * c)
        gends = [gcs[unit(ti, h)][(ic + 1) * c - 1:(ic + 1) * c, :] for h in heads]
        kds = [(ks[unit(ti, h)][rs] * jnp.exp(gends[h] - gcs[unit(ti, h)][rs])).astype(BF16) for h in heads]
        return rs, gends, kds

    def state_terms(rs, states):
        return [_dot(jnp.concatenate([wbs[h][rs], qds[h][rs]], axis=0), states[h].astype(BF16)) for h in heads]

    def next_state(states, gends, kds, us_c):
        return [states[h] * jnp.exp(gends[h]) + lax.dot_general(
            kds[h], us_c[h], (((0,), (0,)), ((), ())), preferred_element_type=F32) for h in heads]

    us = [[] for _ in units]
    outs = [[] for _ in units]
    if seqs == 1:
        trans = []
        for ti in range(ntile):
            for ic in range(nchunk):
                rs, gends, kds = chunk_terms(ti, ic)
                kt_wu = [lax.dot_general(
                    kds[h], jnp.concatenate([wbs[unit(ti, h)][rs], u0s[unit(ti, h)][rs].astype(BF16)], axis=1),
                    (((0,), (0,)), ((), ())), preferred_element_type=F32) for h in heads]
                trans.append(([jnp.exp(g) for g in gends], [m[:, :GDN_DK].astype(BF16) for m in kt_wu],
                              [m[:, GDN_DK:] for m in kt_wu]))
        states = [s_ref[h] for h in heads]
        chunk_states = []
        for decay, kt_w, kt_u0 in trans:
            sb = [s.astype(BF16) for s in states]
            chunk_states.append(sb)
            states = [states[h] * decay[h] - _dot(kt_w[h], sb[h]) + kt_u0[h] for h in heads]
        for h in heads:
            s_ref[h] = states[h]
        for ti in range(ntile):
            for ic in range(nchunk):
                rs = slice(ic * c, (ic + 1) * c)
                sb = chunk_states[ti * nchunk + ic]
                wss = [_dot(jnp.concatenate([wbs[unit(ti, h)][rs], qds[unit(ti, h)][rs]], axis=0), sb[h])
                       for h in heads]
                for h in heads:
                    us[unit(ti, h)].append((u0s[unit(ti, h)][rs] - wss[h][:c]).astype(BF16))
                    outs[unit(ti, h)].append(wss[h][c:])
        o = [jnp.concatenate(outs[u], axis=0) + _dot(qks[u], jnp.concatenate(us[u], axis=0))
             for u in range(len(units))]

        @pl.when(t == pl.num_programs(1) - 1)
        def _():
            s_out_ref[...] = s_ref[...]
    else:
        for ic in range(nchunk):
            rs, gends, kds = chunk_terms(0, ic)
            states = [s0_ref[ic, h] for h in heads]
            wss = state_terms(rs, states)
            us_c = [(u0s[h][rs] - wss[h][:c]).astype(BF16) for h in heads]
            new = next_state(states, gends, kds, us_c)
            for h in heads:
                us[h].append(us_c[h])
                outs[h].append(wss[h][c:])
                s_out_ref[ic, h] = new[h]
        o = [jnp.concatenate(outs[h], axis=0) + _dot(qks[h], jnp.concatenate(us[h], axis=0)) for h in heads]
    for u, (ti, h) in enumerate(units):
        rows, cols = tile_rows[ti], head_cols(h, GDN_DV)
        o_ref[rows, cols] = (_rms(o[u], on) * _silu(z_ref[rows, cols])).astype(o_ref.dtype)


def _gdn_call(conv, z, ba, hist, s0, p, tile, c):
    b, l, _ = conv.shape
    small = [p['conv_w'], p['a_log'], p['dt_bias'], p['gdn_on'], p['head_mat']]
    state = (GDN_HEADS, GDN_DK, GDN_DV)
    if l >= tile:
        tc = _pick_tile(l, GDN_BLOCK_TILES * tile)
        seqs = 1
        grid = (b, l // tc)
        row = lambda w: pl.BlockSpec((None, tc, w), lambda i, j: (i, j, 0))
        per_b = lambda shape: pl.BlockSpec((None,) + shape, lambda i, j: (i,) + (0,) * len(shape))
        scratch = [pltpu.VMEM((tc + CONV_PAD, CONV_CH), F32), pltpu.VMEM(state, F32)]
    else:
        tc = tile = b * l
        assert l == c and tc % LANES == 0
        seqs = b
        grid = (1, 1)
        conv, z, ba = (t.reshape(1, tc, t.shape[-1]) for t in (conv, z, ba))
        row = lambda w: pl.BlockSpec((None, tc, w), lambda i, j: (0, 0, 0))
        per_b = lambda shape: pl.BlockSpec((b,) + shape, lambda i, j: (0,) * (len(shape) + 1))
        scratch = [pltpu.VMEM((seqs * (c + CONV_PAD), CONV_CH), F32)]
    o, s_new = pl.pallas_call(
        functools.partial(_gdn_kernel, tc=tc, tile=tile, c=c, seqs=seqs),
        grid=grid,
        in_specs=[row(CONV_CH), row(GDN_V), row(2 * LANES), per_b((CONV_W - 1, CONV_CH)), per_b(state)] + [
            pl.BlockSpec(w.shape, lambda i, j: (0, 0)) for w in small],
        out_specs=[row(GDN_V), per_b(state)],
        out_shape=[jax.ShapeDtypeStruct(conv.shape[:2] + (GDN_V,), BF16),
                   jax.ShapeDtypeStruct((b,) + state, F32)],
        scratch_shapes=scratch,
        compiler_params=pltpu.CompilerParams(dimension_semantics=("arbitrary", "arbitrary"),
                                             vmem_limit_bytes=VMEM_LIMIT),
        name="gdn",
    )(conv, z, ba, hist, s0, *small)
    return o.reshape(b, l, GDN_V), s_new


def _stack_maps(q):
    lane = lax.broadcasted_iota(jnp.int32, q.shape, 1)
    zero = jnp.zeros_like(q)
    return jnp.concatenate([jnp.where(lane < DIFF_DK, q, zero), jnp.where(lane >= DIFF_DK, q, zero)],
                           axis=0)


def _diff_finish(acc, l, t, lq_ref, lk_ref, dn_ref, lam_init):
    lam_e = jnp.exp(jnp.sum(lq_ref[...] * lk_ref[...], axis=-1, keepdims=True))
    lam = lam_e[0:1] - lam_e[1:2] + lam_init
    o = acc[:t] / l[:t] - lam * (acc[t:] / l[t:])
    return _rms(o, dn_ref[...]) * (1.0 - lam_init)


def _halves_max(s):
    return jnp.maximum(s[:, :LANES], s[:, LANES:])


def _attn_prompt_kernel(q_ref, kt_ref, v_ref, bias_ref, lq_ref, lk_ref, dn_ref, o_ref,
                        kt_scr, vx_scr, s_scr, m_scr, acc_scr, *, tq, tk, lam_init):
    i = pl.program_id(2)
    assert tk == 2 * LANES and tq % tk == 0
    nsub = tq // tk

    @pl.when(i == 0)
    def _():
        for j in range(kt_scr.shape[0]):
            kt_scr[j] = kt_ref[:, j * tk:(j + 1) * tk].astype(BF16)
        vx_scr[:, :DIFF_DV] = v_ref[...]
        vx_scr[:, DIFF_DV:] = jnp.ones((vx_scr.shape[0], LANES), BF16)

    qq = _stack_maps(q_ref[...])

    def scores(j):
        return _dot(qq, kt_scr[j])

    def weighted_values(ss, blocks, mm):
        p = [jnp.exp2(s[:, half * LANES:(half + 1) * LANES] - mm) for s in ss for half in range(tk // LANES)]
        vx = [vx_scr[pl.ds(pl.multiple_of(j * tk, tk), tk), :] for j in blocks]
        return _dot(jnp.concatenate(p, axis=1).astype(BF16), jnp.concatenate(vx, axis=0))

    def pass1(blocks):
        ss = [scores(j) for j in blocks]
        for j, s in zip(blocks, ss):
            s_scr[j] = s
        m_scr[...] = jnp.maximum(m_scr[...], functools.reduce(jnp.maximum, [_halves_max(s) for s in ss]))

    def pass2(blocks):
        acc_scr[...] += weighted_values([s_scr[j] for j in blocks], blocks, m_scr[...])

    def group(g):
        return [g * nsub + u for u in range(nsub)]

    def over_off_diagonal(body):
        def run(first, count):
            body([j for g in range(count) for j in group(first + g)])

        def trip(t, carry):
            run(4 * t, 4)
            return carry

        lax.fori_loop(0, i // 4, trip, 0)

        @pl.when(i % 4 >= 2)
        def _():
            run((i // 4) * 4, 2)

        @pl.when(i % 2 == 1)
        def _():
            run(i - 1, 1)

    s_diag = [scores(j) + bias_ref[d] for d, j in enumerate(group(i))]
    m_scr[...] = functools.reduce(jnp.maximum, [_halves_max(s) for s in s_diag])
    over_off_diagonal(pass1)
    m = jnp.broadcast_to(jnp.max(m_scr[...], axis=-1, keepdims=True), m_scr.shape)
    m_scr[...] = m
    acc_scr[...] = weighted_values(s_diag, group(i), m)
    over_off_diagonal(pass2)
    acc = acc_scr[...]
    o_ref[...] = _diff_finish(acc[:, :DIFF_DV], acc[:, DIFF_DV:], tq, lq_ref, lk_ref, dn_ref,
                              lam_init).astype(o_ref.dtype)


def _attn_prompt_call(qb, kt, vb, p, tq, tk, lam_init):
    b, l, _ = qb.shape
    small = [p['lambda_q'], p['lambda_k'], p['diff_on']]
    r = (jnp.arange(2 * tq) % tq)[None, :, None] // CHUNK
    cidx = (jnp.arange(tk)[None, None, :] + tk * jnp.arange(tq // tk)[:, None, None]) // CHUNK
    bias = jnp.where(cidx <= r, 0.0, -jnp.inf).astype(F32)
    return pl.pallas_call(
        functools.partial(_attn_prompt_kernel, tq=tq, tk=tk, lam_init=lam_init),
        grid=(b, DIFF_HEADS, l // tq),
        in_specs=[pl.BlockSpec((None, tq, LANES), lambda bi, h, i: (bi, i, h)),
                  pl.BlockSpec((None, None, LANES, l), lambda bi, h, i: (bi, h, 0, 0)),
                  pl.BlockSpec((None, l, LANES), lambda bi, h, i: (bi, 0, h)),
                  _const_spec(bias.shape)] + [
                      pl.BlockSpec(w.shape, lambda bi, h, i: (0, 0)) for w in small],
        out_specs=pl.BlockSpec((None, tq, LANES), lambda bi, h, i: (bi, i, h)),
        out_shape=jax.ShapeDtypeStruct((b, l, DIFF_V), BF16),
        scratch_shapes=[pltpu.VMEM((l // tk, LANES, tk), BF16),
                        pltpu.VMEM((l, DIFF_DV + LANES), BF16),
                        pltpu.VMEM((max((l - tq) // tk, 1), 2 * tq, tk), F32),
                        pltpu.VMEM((2 * tq, LANES), F32),
                        pltpu.VMEM((2 * tq, DIFF_DV + LANES), F32)],
        compiler_params=pltpu.CompilerParams(
            dimension_semantics=("arbitrary", "arbitrary", "arbitrary"),
            vmem_limit_bytes=VMEM_LIMIT),
        name="attn_prompt",
    )(qb, kt, vb, bias, *small)


def _attn_sample_kernel(q_ref, kn_ref, vn_ref, ckt_ref, cv_ref, lq_ref, lk_ref, dn_ref, o_ref,
                        *, t, lam_init):
    past = ckt_ref.shape[-1]
    for h in range(DIFF_HEADS):
        cols = slice(h * LANES, (h + 1) * LANES)
        qq = _stack_maps(q_ref[:, cols])
        s_c = _dot(qq, ckt_ref[h].astype(BF16))
        s_n = _dot_nt(qq, kn_ref[:, cols])
        m = jnp.maximum(jnp.max(s_c, axis=-1, keepdims=True), jnp.max(s_n, axis=-1, keepdims=True))
        p_c = jnp.exp2(s_c - m)
        p_n = jnp.exp2(s_n - m)
        l = jnp.sum(p_c, axis=-1, keepdims=True) + jnp.sum(p_n, axis=-1, keepdims=True)
        cv = cv_ref[pl.ds(h, past, stride=DIFF_HEADS), :].astype(BF16)
        acc = _dot(p_c.astype(BF16), cv) + _dot(p_n.astype(BF16), vn_ref[:, cols])
        o_ref[:, cols] = _diff_finish(acc, l, t, lq_ref, lk_ref, dn_ref, lam_init).astype(o_ref.dtype)


def _attn_sample_call(qb, kb, vb, ckt, cv, p, lam_init):
    b, t, _ = qb.shape
    past = ckt.shape[-1]
    small = [p['lambda_q'], p['lambda_k'], p['diff_on']]
    new = pl.BlockSpec((None, t, DIFF_V), lambda bi: (bi, 0, 0))
    return pl.pallas_call(
        functools.partial(_attn_sample_kernel, t=t, lam_init=lam_init),
        grid=(b,),
        in_specs=[new, new, new,
                  pl.BlockSpec((None, DIFF_HEADS, LANES, past), lambda bi: (bi, 0, 0, 0)),
                  pl.BlockSpec((None, past * DIFF_HEADS, DIFF_DV), lambda bi: (bi, 0, 0))] + [
                      pl.BlockSpec(w.shape, lambda bi: (0, 0)) for w in small],
        out_specs=new,
        out_shape=jax.ShapeDtypeStruct((b, t, DIFF_V), BF16),
        compiler_params=pltpu.CompilerParams(dimension_semantics=("arbitrary",),
                                             vmem_limit_bytes=VMEM_LIMIT),
        name="attn_sample",
    )(qb, kb, vb, ckt, cv, *small)


def _rope_tables(pos):
    half = ROT_DIM // 2
    inv = jnp.float32(ROPE_THETA) ** (-jnp.arange(half, dtype=F32) * 2.0 / ROT_DIM)
    ang = pos.astype(F32)[:, None] * inv[None, :]
    cos, sin = jnp.cos(ang), jnp.sin(ang)
    n = pos.shape[0]
    rest = DIFF_DK - ROT_DIM
    one = jnp.ones((n, rest), F32)
    zero = jnp.zeros((n, rest), F32)
    zh = jnp.zeros((n, half), F32)
    c64 = jnp.concatenate([cos, cos, one], axis=1)
    a64 = jnp.concatenate([-sin, zh, zero], axis=1)
    b64 = jnp.concatenate([zh, sin, zero], axis=1)
    rep = LANES // DIFF_DK
    return tuple(jnp.tile(t, (1, rep)) for t in (c64, a64, b64))


def _layer_params(w, l):
    d = w['w_in'].shape[1]
    win = jnp.transpose(w['w_in'][l])
    w_b = win[CONV_CH:CONV_CH + GDN_HEADS]
    w_a = win[CONV_CH + GDN_HEADS:CONV_CH + 2 * GDN_HEADS]
    zpad = jnp.zeros((LANES - GDN_HEADS, d), win.dtype)
    hpad = lambda v: jnp.concatenate([v.astype(F32), jnp.zeros((LANES - GDN_HEADS,), F32)])[None, :]
    grp = jnp.arange(DIFF_QK) // DIFF_DK
    bf = lambda t: t.astype(BF16)
    r2 = lambda v: v.astype(F32)[None, :]
    return dict(
        n1=r2(w['ffn1_norm'][l]), wgu1=bf(w['ffn1_w_gu'][l]), wd1=bf(w['ffn1_w_down'][l]),
        nm=r2(w['mix_norm'][l]), w_conv_t=bf(win[:CONV_CH]), w_rest_t=bf(win[CONV_CH + 2 * GDN_HEADS:]),
        w_ba_t=bf(jnp.concatenate([w_b, zpad, w_a, zpad], axis=0)),
        qn=r2(jnp.tile(w['q_norm'][l], DIFF_QK // DIFF_DK)), kn=r2(jnp.tile(w['k_norm'][l], DIFF_QK // DIFF_DK)),
        gmat=(grp[:, None] == grp[None, :]).astype(BF16),
        head_mat=(grp[:, None] // 2 == grp[None, :] // 2).astype(BF16),
        conv_w=w['conv_w'][l].astype(F32), a_log=hpad(w['a_log'][l]), dt_bias=hpad(w['dt_bias'][l]),
        gdn_on=r2(w['gdn_out_norm'][l]),
        lambda_q=w['lambda_q'][l].astype(F32), lambda_k=w['lambda_k'][l].astype(F32),
        diff_on=r2(w['diff_out_norm'][l]),
        wo=bf(w['w_out'][l]), n2=r2(w['ffn2_norm'][l]), wgu2=bf(w['ffn2_w_gu'][l]), wd2=bf(w['ffn2_w_down'][l]),
    )


def _pick_tile(n, pref):
    t = min(n, pref)
    assert n % t == 0
    return t


def _layer(x, pos, k_hist, v_hist, conv_hist, s0, p, lam_init):
    b, l, d = x.shape
    n = b * l
    tm = _pick_tile(n, ROW_TILE)
    tabs = _rope_tables(pos)
    if l >= tm:
        assert l % tm == 0
        n_pos_tiles = l // tm
    else:
        assert tm % l == 0
        tabs = tuple(jnp.tile(t, (tm // l, 1)) for t in tabs)
        n_pos_tiles = 1
    prompt = k_hist is None
    pre = _pre_call(x.reshape(n, d), tabs, p, tm, n_pos_tiles, b, prompt)
    x1, conv, z, ba, qb, vb, kf, vf = pre[:8]
    c = min(CHUNK, l)
    assert l % c == 0
    tc = GDN_TILE
    assert l % tc == 0 or l == c
    if conv_hist is None:
        conv_hist = jnp.zeros((b, CONV_W - 1, CONV_CH), F32)
        s0 = jnp.zeros((b, GDN_HEADS, GDN_DK, GDN_DV), F32)
    conv3 = conv.reshape(b, l, CONV_CH)
    og, s_new = _gdn_call(conv3, z.reshape(b, l, GDN_V), ba.reshape(b, l, 2 * LANES),
                          conv_hist.astype(F32), s0.astype(F32), p, tc, c)
    conv_new = jnp.concatenate([conv_hist.astype(F32), conv3], axis=1)[:, -(CONV_W - 1):]
    q3 = qb.reshape(b, l, DIFF_QK)
    v3 = vb.reshape(b, l, DIFF_V)
    if prompt:
        tk = 2 * LANES
        od = _attn_prompt_call(q3, kf, v3, p, _pick_tile(l, 2 * tk), tk, lam_init)
        k_out = jnp.transpose(kf.reshape(b, DIFF_HEADS, 2, DIFF_DK, l), (0, 4, 1, 2, 3))
    else:
        past = k_hist.shape[1]
        ckt = jnp.transpose(k_hist, (0, 2, 3, 4, 1)).reshape(b, DIFF_HEADS, LANES, past)
        od = _attn_sample_call(q3, pre[8].reshape(b, l, DIFF_QK), v3, ckt,
                               v_hist.reshape(b, past * DIFF_HEADS, DIFF_DV), p, lam_init)
        k_out = kf.reshape(b, l, DIFF_HEADS, 2, DIFF_DK)
    y = _post_call(x1, og.reshape(n, GDN_V), od.reshape(n, DIFF_V), p, tm)
    return (y.reshape(b, l, d), k_out, vf.reshape(b, l, DIFF_HEADS, DIFF_DV), s_new, conv_new)


def kernel(x_prompt, x_sample, cache_k, cache_v, state_gdn, state_conv, ffn1_norm, ffn1_w_gu, ffn1_w_down,
           mix_norm, w_in, conv_w, a_log, dt_bias, gdn_out_norm, q_norm, k_norm, lambda_q, lambda_k,
           diff_out_norm, w_out, ffn2_norm, ffn2_w_gu, ffn2_w_down):
    w = dict(ffn1_norm=ffn1_norm, ffn1_w_gu=ffn1_w_gu, ffn1_w_down=ffn1_w_down, mix_norm=mix_norm, w_in=w_in,
             conv_w=conv_w, a_log=a_log, dt_bias=dt_bias, gdn_out_norm=gdn_out_norm, q_norm=q_norm,
             k_norm=k_norm, lambda_q=lambda_q, lambda_k=lambda_k, diff_out_norm=diff_out_norm, w_out=w_out,
             ffn2_norm=ffn2_norm, ffn2_w_gu=ffn2_w_gu, ffn2_w_down=ffn2_w_down)
    depth = w_in.shape[0]
    pos_p = jnp.arange(x_prompt.shape[1])
    pos_s = cache_k.shape[2] + jnp.arange(x_sample.shape[1])
    hp, hs = x_prompt, x_sample
    outs = [[] for _ in range(8)]
    for l in range(depth):
        lam_init = 0.8 - 0.6 * math.exp(-0.3 * l)
        p = _layer_params(w, l)
        hp, kp, vp, sp, cp = _layer(hp, pos_p, None, None, None, None, p, lam_init)
        hs, ks, vs, ss, cs = _layer(hs, pos_s, cache_k[l], cache_v[l], state_conv[l], state_gdn[l], p, lam_init)
        for acc, val in zip(outs, (kp, vp, sp, cp, ks, vs, ss, cs)):
            acc.append(val)
    return (hp, hs) + tuple(jnp.stack(o) for o in outs)
```

```python
import functools
import math

import jax
import jax.numpy as jnp
from jax import lax
from jax.experimental import pallas as pl
from jax.experimental.pallas import tpu as pltpu

F32 = jnp.float32
BF16 = jnp.bfloat16

EPS = 1e-6
CHUNK = 64
GDN_HEADS = 4
GDN_DK = 128
GDN_DV = 128
CONV_W = 4
DIFF_HEADS = 4
DIFF_DK = 64
DIFF_DV = 128
ROT_DIM = DIFF_DK // 4
ROPE_THETA = 500000.0
GDN_QK = GDN_HEADS * GDN_DK
GDN_V = GDN_HEADS * GDN_DV
CONV_CH = 2 * GDN_QK + GDN_V
DIFF_QK = DIFF_HEADS * 2 * DIFF_DK
DIFF_V = DIFF_HEADS * DIFF_DV

LANES = 128
INV_BLOCK = 16
ROW_TILE = 512
GDN_TILE = 128
GDN_BLOCK_TILES = 4
VMEM_LIMIT = 60 * 1024 * 1024
Q_SCALE = (DIFF_DK ** -0.5) * math.log2(math.e)


def _dot(a, b):
    return jnp.dot(a, b, preferred_element_type=F32)


def _dot_nt(a, b):
    return lax.dot_general(a, b, (((1,), (1,)), ((), ())), preferred_element_type=F32)


def _rms(x, g):
    return x * lax.rsqrt(jnp.mean(x * x, axis=-1, keepdims=True) + EPS) * g


def _silu(x):
    return x * jax.nn.sigmoid(x)


def _rms_parts(x, g):
    return (x * g).astype(BF16), lax.rsqrt(jnp.mean(x * x, axis=-1, keepdims=True) + EPS)


def _swiglu(xg, r, wgu_ref, wd_ref):
    d_ff = wd_ref.shape[0]
    g = _dot(xg, wgu_ref[:, :d_ff]) * r
    u = _dot(xg, wgu_ref[:, d_ff:]) * r
    act = (_silu(g) * u).astype(BF16)
    return _dot(act, wd_ref[...])


def _const_spec(shape):
    nd = len(shape)
    return pl.BlockSpec(shape, lambda *_: (0,) * nd, pipeline_mode=pl.Buffered(1))


_WREST = dict(z=slice(0, GDN_V), q=slice(GDN_V, GDN_V + DIFF_QK), k=slice(GDN_V + DIFF_QK, GDN_V + 2 * DIFF_QK),
              v=slice(GDN_V + 2 * DIFF_QK, GDN_V + 2 * DIFF_QK + DIFF_V))


def _pre_kernel(x_ref, cos_ref, sa_ref, sb_ref, n1_ref, wgu_ref, wd_ref, nm_ref, wconv_ref, wrest_ref, wba_ref, qn_ref, kn_ref,
                gm_ref,
                x1_ref, conv_ref, z_ref, ba_ref, qb_ref, vb_ref, kf_ref, vf_ref, *maybe_kb_ref,
                k_pos_minor):
    x = x_ref[...]
    xg, r = _rms_parts(x, n1_ref[...])
    x1 = x + 0.5 * _swiglu(xg, r, wgu_ref, wd_ref)
    x1_ref[...] = x1
    h, rh = _rms_parts(x1, nm_ref[...])
    tm = x.shape[0]
    cos = cos_ref[...]
    sa = sa_ref[...]
    sb = sb_ref[...]
    gm = gm_ref[...]

    def norm_rope(t, gw):
        ss = _dot((t * t).astype(BF16), gm)
        t = t * lax.rsqrt(ss * (1.0 / DIFF_DK) + EPS) * gw
        outs = []
        for hh in range(DIFF_HEADS):
            th = t[:, hh * LANES:(hh + 1) * LANES]
            up = pltpu.roll(th, LANES - ROT_DIM // 2, 1)
            dn = pltpu.roll(th, ROT_DIM // 2, 1)
            outs.append(th * cos + up * sa + dn * sb)
        return jnp.concatenate(outs, axis=1)

    k = norm_rope(_dot_nt(h, wrest_ref[_WREST['k'], :]) * rh, kn_ref[...])
    if k_pos_minor:
        kf_ref[...] = k.T.reshape(DIFF_HEADS, LANES, tm)
    else:
        kf_ref[...] = k
        maybe_kb_ref[0][...] = k.astype(BF16)
    q = norm_rope(_dot_nt(h, wrest_ref[_WREST['q'], :]) * rh, qn_ref[...])
    qb_ref[...] = (q * Q_SCALE).astype(BF16)
    v = _dot_nt(h, wrest_ref[_WREST['v'], :]) * rh
    vb_ref[...] = v.astype(BF16)
    if k_pos_minor:
        for hh in range(DIFF_HEADS):
            vf_ref[pl.ds(hh, tm, stride=DIFF_HEADS), :] = v[:, hh * DIFF_DV:(hh + 1) * DIFF_DV]
    else:
        vf_ref[...] = v
    z_ref[...] = _dot_nt(h, wrest_ref[_WREST['z'], :]) * rh
    ba_ref[...] = _dot_nt(h, wba_ref[...]) * rh
    conv_ref[...] = _dot_nt(h, wconv_ref[...]) * rh


def _pre_call(x2d, tabs, p, tm, n_pos_tiles, batch, k_pos_minor):
    n, d = x2d.shape
    grid = (n // tm,)
    row = lambda w: pl.BlockSpec((tm, w), lambda i: (i, 0))
    tab = pl.BlockSpec((tm, LANES), lambda i: (i % n_pos_tiles, 0))
    weights = [p['n1'], p['wgu1'], p['wd1'], p['nm'], p['w_conv_t'], p['w_rest_t'], p['w_ba_t'], p['qn'], p['kn'], p['gmat']]
    in_specs = [row(d), tab, tab, tab] + [_const_spec(w.shape) for w in weights]
    outs = [(d, F32), (CONV_CH, F32), (GDN_V, F32), (2 * LANES, F32), (DIFF_QK, BF16), (DIFF_V, BF16)]
    out_specs = [row(w) for w, _ in outs]
    out_shape = [jax.ShapeDtypeStruct((n, w), dt) for w, dt in outs]
    if k_pos_minor:
        l = n // batch
        tiles = l // tm
        out_specs += [pl.BlockSpec((None, DIFF_HEADS, LANES, tm), lambda i: (i // tiles, 0, 0, i % tiles)),
                      pl.BlockSpec((tm * DIFF_HEADS, DIFF_DV), lambda i: (i, 0))]
        out_shape += [jax.ShapeDtypeStruct((batch, DIFF_HEADS, LANES, l), F32),
                      jax.ShapeDtypeStruct((n * DIFF_HEADS, DIFF_DV), F32)]
    else:
        out_specs += [row(DIFF_QK), row(DIFF_V), row(DIFF_QK)]
        out_shape += [jax.ShapeDtypeStruct((n, DIFF_QK), F32), jax.ShapeDtypeStruct((n, DIFF_V), F32),
                      jax.ShapeDtypeStruct((n, DIFF_QK), BF16)]
    return pl.pallas_call(
        functools.partial(_pre_kernel, k_pos_minor=k_pos_minor),
        grid=grid,
        in_specs=in_specs,
        out_specs=out_specs,
        out_shape=out_shape,
        compiler_params=pltpu.CompilerParams(dimension_semantics=("arbitrary",),
                                             vmem_limit_bytes=VMEM_LIMIT),
        name="pre",
    )(x2d, *tabs, *weights)


def _post_kernel(x1_ref, og_ref, od_ref, wo_ref, n2_ref, wgu_ref, wd_ref, y_ref):
    mixed = jnp.concatenate([og_ref[...], od_ref[...]], axis=1)
    x2 = x1_ref[...] + _dot(mixed, wo_ref[...])
    xg, r = _rms_parts(x2, n2_ref[...])
    y_ref[...] = x2 + 0.5 * _swiglu(xg, r, wgu_ref, wd_ref)


def _post_call(x1, og, od, p, tm):
    n, d = x1.shape
    row = lambda w: pl.BlockSpec((tm, w), lambda i: (i, 0))
    weights = [p['wo'], p['n2'], p['wgu2'], p['wd2']]
    return pl.pallas_call(
        _post_kernel,
        grid=(n // tm,),
        in_specs=[row(d), row(GDN_V), row(DIFF_V)] + [_const_spec(w.shape) for w in weights],
        out_specs=row(d),
        out_shape=jax.ShapeDtypeStruct((n, d), F32),
        compiler_params=pltpu.CompilerParams(dimension_semantics=("arbitrary",),
                                             vmem_limit_bytes=VMEM_LIMIT),
        name="post",
    )(x1, og, od, *weights)


def _exact3(m01, x):
    x1 = x.astype(BF16)
    r1 = x - x1.astype(F32)
    x2 = r1.astype(BF16)
    x3 = (r1 - x2.astype(F32)).astype(BF16)
    return _dot(m01, x1) + (_dot(m01, x2) + _dot(m01, x3))


CONV_PAD = 8


def _gdn_kernel(x_ref, z_ref, ba_ref, hist_ref, s0_ref, cw_ref, alog_ref, dtb_ref, on_ref, hm_ref,
                o_ref, s_out_ref, xp_ref, *maybe_s_ref, tc, tile, c, seqs):
    pad = CONV_PAD
    nchunk = tile // c
    ntile = tc // tile
    assert c % INV_BLOCK == 0 and tile % LANES == 0 and tc % tile == 0
    assert seqs == 1 or (seqs == nchunk and ntile == 1)
    cw = cw_ref[...]
    x = x_ref[...]

    def conv(xp, rows):
        y = xp[rows] * cw[CONV_W - 1:CONV_W, :]
        for k in range(1, CONV_W):
            y = y + pltpu.roll(xp, k, 0)[rows] * cw[CONV_W - 1 - k:CONV_W - k, :]
        return y

    if seqs == 1:
        s_ref, = maybe_s_ref
        t = pl.program_id(1)

        @pl.when(t == 0)
        def _():
            xp_ref[0:pad, :] = jnp.zeros((pad, CONV_CH), F32)
            xp_ref[pad - (CONV_W - 1):pad, :] = hist_ref[...]
            s_ref[...] = s0_ref[...]

        xp_ref[pad:pad + tc, :] = x
        y = conv(xp_ref[...], slice(pad, pad + tc))
        xp_ref[pad - (CONV_W - 1):pad, :] = x[tc - (CONV_W - 1):tc, :]
    else:
        stride = pad + c
        xp_ref[...] = jnp.zeros(xp_ref.shape, F32)
        for b in range(seqs):
            xp_ref[b * stride + pad - (CONV_W - 1):b * stride + pad, :] = hist_ref[b]
            xp_ref[b * stride + pad:(b + 1) * stride, :] = x[b * c:(b + 1) * c, :]
        yp = conv(xp_ref[...], slice(None))
        y = jnp.concatenate([yp[b * stride + pad:(b + 1) * stride, :] for b in range(seqs)], axis=0)
    y = _silu(y)
    hm = hm_ref[...]
    yk = y[:, GDN_QK:2 * GDN_QK]
    yk = yk * lax.rsqrt(_dot((yk * yk).astype(BF16), hm) + EPS)

    ba = ba_ref[...]
    beta = jax.nn.sigmoid(ba[:, :LANES])
    g = -jnp.exp(alog_ref[...]) * jax.nn.softplus(ba[:, LANES:] + dtb_ref[...])
    ri = lax.broadcasted_iota(jnp.int32, (tile, tile), 0)
    ci = lax.broadcasted_iota(jnp.int32, (tile, tile), 1)
    incl = ((ri // c) == (ci // c)) & (ci <= ri)
    same_blk = (ri // INV_BLOCK) == (ci // INV_BLOCK)
    diag = ri == ci
    eye = jnp.where(diag, 1.0, 0.0)
    incl01 = jnp.where(incl, 1.0, 0.0).astype(BF16)
    tile_rows = [slice(ti * tile, (ti + 1) * tile) for ti in range(ntile)]
    gcums = [_exact3(incl01, g[rows]) for rows in tile_rows]
    gcum_ts = [gc.T for gc in gcums]
    on = on_ref[...]

    heads = range(GDN_HEADS)
    units = [(ti, h) for ti in range(ntile) for h in heads]

    def per_head(f, *lists):
        return [f(*args) for args in zip(*lists)]

    def head_cols(h, width):
        return slice(h * width, (h + 1) * width)

    ks = [yk[tile_rows[ti], head_cols(h, GDN_DK)] for ti, h in units]
    vs = [y[tile_rows[ti], 2 * GDN_QK + h * GDN_DV:2 * GDN_QK + (h + 1) * GDN_DV] for ti, h in units]
    gcs = [gcums[ti][:, h:h + 1] for ti, h in units]
    bcs = [beta[tile_rows[ti], h:h + 1] for ti, h in units]
    decs = [jnp.exp(jnp.where(incl, gcs[u] - gcum_ts[ti][h:h + 1, :], -jnp.inf))
            for u, (ti, h) in enumerate(units)]
    kbs = per_head(lambda k: k.astype(BF16), ks)
    a_s = per_head(lambda bc, kb, dec: jnp.where(diag, 0.0, bc * _dot_nt(kb, kb) * dec), bcs, kbs, decs)

    assert INV_BLOCK == 16 and c in (INV_BLOCK, 4 * INV_BLOCK)
    bf = lambda xs: per_head(lambda x: x.astype(BF16), xs)
    mm = lambda xs, ys: per_head(_dot, xs, ys)
    ds = per_head(lambda a: jnp.where(same_blk, a, 0.0), a_s) if c > INV_BLOCK else a_s
    sd = bf(ds)
    d2 = mm(sd, sd)
    s2 = bf(d2)
    d3 = mm(sd, s2)
    d4 = mm(s2, s2)
    s4 = bf(d4)
    n1 = per_head(lambda d, x2, x3: eye - d + x2 - x3, ds, d2, d3)
    n2 = per_head(jnp.add, n1, mm(bf(n1), s4))
    d8 = mm(s4, s4)
    tinv = per_head(jnp.add, n2, mm(bf(n2), bf(d8)))
    if c > INV_BLOCK:
        std = bf(tinv)
        ms = mm(std, bf(per_head(jnp.subtract, a_s, ds)))
        sm = bf(ms)
        m2 = mm(sm, sm)
        m3 = mm(sm, bf(m2))
        tinv = mm(bf(per_head(lambda m, x2, x3: eye - m + x2 - x3, ms, m2, m3)), std)

    egs = per_head(jnp.exp, gcs)
    rhs = per_head(lambda v, k, bc, eg: jnp.concatenate([v * bc, k * (bc * eg)], axis=1), vs, ks, bcs, egs)
    sols = mm(bf(tinv), bf(rhs))
    yq = y[:, :GDN_QK]
    yq = yq * (lax.rsqrt(_dot((yq * yq).astype(BF16), hm) + EPS) * (GDN_DK ** -0.5))
    qs = [yq[tile_rows[ti], head_cols(h, GDN_DK)] for ti, h in units]
    qks = per_head(lambda q, kb, dec: (_dot_nt(q.astype(BF16), kb) * dec).astype(BF16), qs, kbs, decs)
    u0s = [sol[:, :GDN_DV] for sol in sols]
    wbs = [sol[:, GDN_DV:].astype(BF16) for sol in sols]
    qds = per_head(lambda q, eg: (q * eg).astype(BF16), qs, egs)

    unit = lambda ti, h: ti * GDN_HEADS + h

    def chunk_terms(ti, ic):
        rs = slice(ic * c, (ic + 1) * c)
        gends = [gcs[unit(ti, h)][(ic + 1) * c - 1:(ic + 1) * c, :] for h in heads]
        kds = [(ks[unit(ti, h)][rs] * jnp.exp(gends[h] - gcs[unit(ti, h)][rs])).astype(BF16) for h in heads]
        return rs, gends, kds

    def state_terms(rs, states):
        return [_dot(jnp.concatenate([wbs[h][rs], qds[h][rs]], axis=0), states[h].astype(BF16)) for h in heads]

    def next_state(states, gends, kds, us_c):
        return [states[h] * jnp.exp(gends[h]) + lax.dot_general(
            kds[h], us_c[h], (((0,), (0,)), ((), ())), preferred_element_type=F32) for h in heads]

    us = [[] for _ in units]
    outs = [[] for _ in units]
    if seqs == 1:
        trans = []
        for ti in range(ntile):
            for ic in range(nchunk):
                rs, gends, kds = chunk_terms(ti, ic)
                kt_wu = [lax.dot_general(
                    kds[h], jnp.concatenate([wbs[unit(ti, h)][rs], u0s[unit(ti, h)][rs].astype(BF16)], axis=1),
                    (((0,), (0,)), ((), ())), preferred_element_type=F32) for h in heads]
                trans.append(([jnp.exp(g) for g in gends], [m[:, :GDN_DK].astype(BF16) for m in kt_wu],
                              [m[:, GDN_DK:] for m in kt_wu]))
        states = [s_ref[h] for h in heads]
        chunk_states = []
        for decay, kt_w, kt_u0 in trans:
            sb = [s.astype(BF16) for s in states]
            chunk_states.append(sb)
            states = [states[h] * decay[h] - _dot(kt_w[h], sb[h]) + kt_u0[h] for h in heads]
        for h in heads:
            s_ref[h] = states[h]
        for ti in range(ntile):
            for ic in range(nchunk):
                rs = slice(ic * c, (ic + 1) * c)
                sb = chunk_states[ti * nchunk + ic]
                wss = [_dot(jnp.concatenate([wbs[unit(ti, h)][rs], qds[unit(ti, h)][rs]], axis=0), sb[h])
                       for h in heads]
                for h in heads:
                    us[unit(ti, h)].append((u0s[unit(ti, h)][rs] - wss[h][:c]).astype(BF16))
                    outs[unit(ti, h)].append(wss[h][c:])
        o = [jnp.concatenate(outs[u], axis=0) + _dot(qks[u], jnp.concatenate(us[u], axis=0))
             for u in range(len(units))]

        @pl.when(t == pl.num_programs(1) - 1)
        def _():
            s_out_ref[...] = s_ref[...]
    else:
        for ic in range(nchunk):
            rs, gends, kds = chunk_terms(0, ic)
            states = [s0_ref[ic, h] for h in heads]
            wss = state_terms(rs, states)
            us_c = [(u0s[h][rs] - wss[h][:c]).astype(BF16) for h in heads]
            new = next_state(states, gends, kds, us_c)
            for h in heads:
                us[h].append(us_c[h])
                outs[h].append(wss[h][c:])
                s_out_ref[ic, h] = new[h]
        o = [jnp.concatenate(outs[h], axis=0) + _dot(qks[h], jnp.concatenate(us[h], axis=0)) for h in heads]
    for u, (ti, h) in enumerate(units):
        rows, cols = tile_rows[ti], head_cols(h, GDN_DV)
        o_ref[rows, cols] = (_rms(o[u], on) * _silu(z_ref[rows, cols])).astype(o_ref.dtype)


def _gdn_call(conv, z, ba, hist, s0, p, tile, c):
    b, l, _ = conv.shape
    small = [p['conv_w'], p['a_log'], p['dt_bias'], p['gdn_on'], p['head_mat']]
    state = (GDN_HEADS, GDN_DK, GDN_DV)
    if l >= tile:
        tc = _pick_tile(l, GDN_BLOCK_TILES * tile)
        seqs = 1
        grid = (b, l // tc)
        row = lambda w: pl.BlockSpec((None, tc, w), lambda i, j: (i, j, 0))
        per_b = lambda shape: pl.BlockSpec((None,) + shape, lambda i, j: (i,) + (0,) * len(shape))
        scratch = [pltpu.VMEM((tc + CONV_PAD, CONV_CH), F32), pltpu.VMEM(state, F32)]
    else:
        tc = tile = b * l
        assert l == c and tc % LANES == 0
        seqs = b
        grid = (1, 1)
        conv, z, ba = (t.reshape(1, tc, t.shape[-1]) for t in (conv, z, ba))
        row = lambda w: pl.BlockSpec((None, tc, w), lambda i, j: (0, 0, 0))
        per_b = lambda shape: pl.BlockSpec((b,) + shape, lambda i, j: (0,) * (len(shape) + 1))
        scratch = [pltpu.VMEM((seqs * (c + CONV_PAD), CONV_CH), F32)]
    o, s_new = pl.pallas_call(
        functools.partial(_gdn_kernel, tc=tc, tile=tile, c=c, seqs=seqs),
        grid=grid,
        in_specs=[row(CONV_CH), row(GDN_V), row(2 * LANES), per_b((CONV_W - 1, CONV_CH)), per_b(state)] + [
            pl.BlockSpec(w.shape, lambda i, j: (0, 0)) for w in small],
        out_specs=[row(GDN_V), per_b(state)],
        out_shape=[jax.ShapeDtypeStruct(conv.shape[:2] + (GDN_V,), BF16),
                   jax.ShapeDtypeStruct((b,) + state, F32)],
        scratch_shapes=scratch,
        compiler_params=pltpu.CompilerParams(dimension_semantics=("arbitrary", "arbitrary"),
                                             vmem_limit_bytes=VMEM_LIMIT),
        name="gdn",
    )(conv, z, ba, hist, s0, *small)
    return o.reshape(b, l, GDN_V), s_new


def _stack_maps(q):
    lane = lax.broadcasted_iota(jnp.int32, q.shape, 1)
    zero = jnp.zeros_like(q)
    return jnp.concatenate([jnp.where(lane < DIFF_DK, q, zero), jnp.where(lane >= DIFF_DK, q, zero)],
                           axis=0)


def _diff_finish(acc, l, t, lq_ref, lk_ref, dn_ref, lam_init):
    lam_e = jnp.exp(jnp.sum(lq_ref[...] * lk_ref[...], axis=-1, keepdims=True))
    lam = lam_e[0:1] - lam_e[1:2] + lam_init
    o = acc[:t] / l[:t] - lam * (acc[t:] / l[t:])
    return _rms(o, dn_ref[...]) * (1.0 - lam_init)


def _halves_max(s):
    return jnp.maximum(s[:, :LANES], s[:, LANES:])


def _attn_prompt_kernel(q_ref, kt_ref, v_ref, bias_ref, lq_ref, lk_ref, dn_ref, o_ref,
                        kt_scr, vx_scr, s_scr, m_scr, acc_scr, *, tq, tk, lam_init):
    i = pl.program_id(2)
    assert tk == 2 * LANES and tq % tk == 0
    nsub = tq // tk

    @pl.when(i == 0)
    def _():
        for j in range(kt_scr.shape[0]):
            kt_scr[j] = kt_ref[:, j * tk:(j + 1) * tk].astype(BF16)
        vx_scr[:, :DIFF_DV] = v_ref[...]
        vx_scr[:, DIFF_DV:] = jnp.ones((vx_scr.shape[0], LANES), BF16)

    qq = _stack_maps(q_ref[...])

    def scores(j):
        return _dot(qq, kt_scr[j])

    def weighted_values(ss, blocks, mm):
        p = [jnp.exp2(s[:, half * LANES:(half + 1) * LANES] - mm) for s in ss for half in range(tk // LANES)]
        vx = [vx_scr[pl.ds(pl.multiple_of(j * tk, tk), tk), :] for j in blocks]
        return _dot(jnp.concatenate(p, axis=1).astype(BF16), jnp.concatenate(vx, axis=0))

    def pass1(blocks, biases):
        ss = [scores(j) if b is None else scores(j) + b for j, b in zip(blocks, biases)]
        for j, s in zip(blocks, ss):
            s_scr[j] = s
        m_scr[...] = jnp.maximum(m_scr[...], functools.reduce(jnp.maximum, [_halves_max(s) for s in ss]))

    def pass2(blocks, biases):
        del biases
        acc_scr[...] += weighted_values([s_scr[j] for j in blocks], blocks, m_scr[...])

    def over_visible(body):
        ngroups = i + 1

        def group(g):
            return [g * nsub + u for u in range(nsub)]

        def run(first, count, diagonal_last):
            last = group(first + count - 1)
            if diagonal_last:
                biases = [bias_ref[1 + d] for d in range(nsub)]
            else:
                biases = [bias_ref[jnp.maximum(j - i * nsub + 1, 0)] for j in last]
            body([j for g in range(count - 1) for j in group(first + g)] + last,
                 [None] * (nsub * (count - 1)) + biases)

        def trip(t, carry):
            run(4 * t, 4, False)
            return carry

        lax.fori_loop(0, ngroups // 4, trip, 0)

        for count in range(1, 4):
            @pl.when(ngroups % 4 == count)
            def _(count=count):
                run(ngroups - count, count, True)

    m_scr[...] = jnp.full(m_scr.shape, -jnp.inf, F32)
    over_visible(pass1)
    m_scr[...] = jnp.broadcast_to(jnp.max(m_scr[...], axis=-1, keepdims=True), m_scr.shape)
    acc_scr[...] = jnp.zeros(acc_scr.shape, F32)
    over_visible(pass2)
    acc = acc_scr[...]
    o_ref[...] = _diff_finish(acc[:, :DIFF_DV], acc[:, DIFF_DV:], tq, lq_ref, lk_ref, dn_ref,
                              lam_init).astype(o_ref.dtype)


def _attn_prompt_call(qb, kt, vb, p, tq, tk, lam_init):
    b, l, _ = qb.shape
    small = [p['lambda_q'], p['lambda_k'], p['diff_on']]
    r = (jnp.arange(2 * tq) % tq)[None, :, None] // CHUNK
    cidx = (jnp.arange(tk)[None, None, :] + tk * jnp.arange(tq // tk)[:, None, None]) // CHUNK
    bias = jnp.where(cidx <= r, 0.0, -jnp.inf).astype(F32)
    bias = jnp.concatenate([jnp.zeros_like(bias[:1]), bias], axis=0)
    return pl.pallas_call(
        functools.partial(_attn_prompt_kernel, tq=tq, tk=tk, lam_init=lam_init),
        grid=(b, DIFF_HEADS, l // tq),
        in_specs=[pl.BlockSpec((None, tq, LANES), lambda bi, h, i: (bi, i, h)),
                  pl.BlockSpec((None, None, LANES, l), lambda bi, h, i: (bi, h, 0, 0)),
                  pl.BlockSpec((None, l, LANES), lambda bi, h, i: (bi, 0, h)),
                  _const_spec(bias.shape)] + [
                      pl.BlockSpec(w.shape, lambda bi, h, i: (0, 0)) for w in small],
        out_specs=pl.BlockSpec((None, tq, LANES), lambda bi, h, i: (bi, i, h)),
        out_shape=jax.ShapeDtypeStruct((b, l, DIFF_V), BF16),
        scratch_shapes=[pltpu.VMEM((l // tk, LANES, tk), BF16),
                        pltpu.VMEM((l, DIFF_DV + LANES), BF16),
                        pltpu.VMEM((l // tk, 2 * tq, tk), F32),
                        pltpu.VMEM((2 * tq, LANES), F32),
                        pltpu.VMEM((2 * tq, DIFF_DV + LANES), F32)],
        compiler_params=pltpu.CompilerParams(
            dimension_semantics=("arbitrary", "arbitrary", "arbitrary"),
            vmem_limit_bytes=VMEM_LIMIT),
        name="attn_prompt",
    )(qb, kt, vb, bias, *small)


def _attn_sample_kernel(q_ref, kn_ref, vn_ref, ckt_ref, cv_ref, lq_ref, lk_ref, dn_ref, o_ref,
                        *, t, lam_init):
    past = ckt_ref.shape[-1]
    for h in range(DIFF_HEADS):
        cols = slice(h * LANES, (h + 1) * LANES)
        qq = _stack_maps(q_ref[:, cols])
        s_c = _dot(qq, ckt_ref[h].astype(BF16))
        s_n = _dot_nt(qq, kn_ref[:, cols])
        m = jnp.maximum(jnp.max(s_c, axis=-1, keepdims=True), jnp.max(s_n, axis=-1, keepdims=True))
        p_c = jnp.exp2(s_c - m)
        p_n = jnp.exp2(s_n - m)
        l = jnp.sum(p_c, axis=-1, keepdims=True) + jnp.sum(p_n, axis=-1, keepdims=True)
        cv = cv_ref[pl.ds(h, past, stride=DIFF_HEADS), :].astype(BF16)
        acc = _dot(p_c.astype(BF16), cv) + _dot(p_n.astype(BF16), vn_ref[:, cols])
        o_ref[:, cols] = _diff_finish(acc, l, t, lq_ref, lk_ref, dn_ref, lam_init).astype(o_ref.dtype)


def _attn_sample_call(qb, kb, vb, ckt, cv, p, lam_init):
    b, t, _ = qb.shape
    past = ckt.shape[-1]
    small = [p['lambda_q'], p['lambda_k'], p['diff_on']]
    new = pl.BlockSpec((None, t, DIFF_V), lambda bi: (bi, 0, 0))
    return pl.pallas_call(
        functools.partial(_attn_sample_kernel, t=t, lam_init=lam_init),
        grid=(b,),
        in_specs=[new, new, new,
                  pl.BlockSpec((None, DIFF_HEADS, LANES, past), lambda bi: (bi, 0, 0, 0)),
                  pl.BlockSpec((None, past * DIFF_HEADS, DIFF_DV), lambda bi: (bi, 0, 0))] + [
                      pl.BlockSpec(w.shape, lambda bi: (0, 0)) for w in small],
        out_specs=new,
        out_shape=jax.ShapeDtypeStruct((b, t, DIFF_V), BF16),
        compiler_params=pltpu.CompilerParams(dimension_semantics=("arbitrary",),
                                             vmem_limit_bytes=VMEM_LIMIT),
        name="attn_sample",
    )(qb, kb, vb, ckt, cv, *small)


def _rope_tables(pos):
    half = ROT_DIM // 2
    inv = jnp.float32(ROPE_THETA) ** (-jnp.arange(half, dtype=F32) * 2.0 / ROT_DIM)
    ang = pos.astype(F32)[:, None] * inv[None, :]
    cos, sin = jnp.cos(ang), jnp.sin(ang)
    n = pos.shape[0]
    rest = DIFF_DK - ROT_DIM
    one = jnp.ones((n, rest), F32)
    zero = jnp.zeros((n, rest), F32)
    zh = jnp.zeros((n, half), F32)
    c64 = jnp.concatenate([cos, cos, one], axis=1)
    a64 = jnp.concatenate([-sin, zh, zero], axis=1)
    b64 = jnp.concatenate([zh, sin, zero], axis=1)
    rep = LANES // DIFF_DK
    return tuple(jnp.tile(t, (1, rep)) for t in (c64, a64, b64))


def _layer_params(w, l):
    d = w['w_in'].shape[1]
    win = jnp.transpose(w['w_in'][l])
    w_b = win[CONV_CH:CONV_CH + GDN_HEADS]
    w_a = win[CONV_CH + GDN_HEADS:CONV_CH + 2 * GDN_HEADS]
    zpad = jnp.zeros((LANES - GDN_HEADS, d), win.dtype)
    hpad = lambda v: jnp.concatenate([v.astype(F32), jnp.zeros((LANES - GDN_HEADS,), F32)])[None, :]
    grp = jnp.arange(DIFF_QK) // DIFF_DK
    bf = lambda t: t.astype(BF16)
    r2 = lambda v: v.astype(F32)[None, :]
    return dict(
        n1=r2(w['ffn1_norm'][l]), wgu1=bf(w['ffn1_w_gu'][l]), wd1=bf(w['ffn1_w_down'][l]),
        nm=r2(w['mix_norm'][l]), w_conv_t=bf(win[:CONV_CH]), w_rest_t=bf(win[CONV_CH + 2 * GDN_HEADS:]),
        w_ba_t=bf(jnp.concatenate([w_b, zpad, w_a, zpad], axis=0)),
        qn=r2(jnp.tile(w['q_norm'][l], DIFF_QK // DIFF_DK)), kn=r2(jnp.tile(w['k_norm'][l], DIFF_QK // DIFF_DK)),
        gmat=(grp[:, None] == grp[None, :]).astype(BF16),
        head_mat=(grp[:, None] // 2 == grp[None, :] // 2).astype(BF16),
        conv_w=w['conv_w'][l].astype(F32), a_log=hpad(w['a_log'][l]), dt_bias=hpad(w['dt_bias'][l]),
        gdn_on=r2(w['gdn_out_norm'][l]),
        lambda_q=w['lambda_q'][l].astype(F32), lambda_k=w['lambda_k'][l].astype(F32),
        diff_on=r2(w['diff_out_norm'][l]),
        wo=bf(w['w_out'][l]), n2=r2(w['ffn2_norm'][l]), wgu2=bf(w['ffn2_w_gu'][l]), wd2=bf(w['ffn2_w_down'][l]),
    )


def _pick_tile(n, pref):
    t = min(n, pref)
    assert n % t == 0
    return t


def _layer(x, pos, k_hist, v_hist, conv_hist, s0, p, lam_init):
    b, l, d = x.shape
    n = b * l
    tm = _pick_tile(n, ROW_TILE)
    tabs = _rope_tables(pos)
    if l >= tm:
        assert l % tm == 0
        n_pos_tiles = l // tm
    else:
        assert tm % l == 0
        tabs = tuple(jnp.tile(t, (tm // l, 1)) for t in tabs)
        n_pos_tiles = 1
    prompt = k_hist is None
    pre = _pre_call(x.reshape(n, d), tabs, p, tm, n_pos_tiles, b, prompt)
    x1, conv, z, ba, qb, vb, kf, vf = pre[:8]
    c = min(CHUNK, l)
    assert l % c == 0
    tc = GDN_TILE
    assert l % tc == 0 or l == c
    if conv_hist is None:
        conv_hist = jnp.zeros((b, CONV_W - 1, CONV_CH), F32)
        s0 = jnp.zeros((b, GDN_HEADS, GDN_DK, GDN_DV), F32)
    conv3 = conv.reshape(b, l, CONV_CH)
    og, s_new = _gdn_call(conv3, z.reshape(b, l, GDN_V), ba.reshape(b, l, 2 * LANES),
                          conv_hist.astype(F32), s0.astype(F32), p, tc, c)
    conv_new = jnp.concatenate([conv_hist.astype(F32), conv3], axis=1)[:, -(CONV_W - 1):]
    q3 = qb.reshape(b, l, DIFF_QK)
    v3 = vb.reshape(b, l, DIFF_V)
    if prompt:
        tk = 2 * LANES
        od = _attn_prompt_call(q3, kf, v3, p, _pick_tile(l, 2 * tk), tk, lam_init)
        k_out = jnp.transpose(kf.reshape(b, DIFF_HEADS, 2, DIFF_DK, l), (0, 4, 1, 2, 3))
    else:
        past = k_hist.shape[1]
        ckt = jnp.transpose(k_hist, (0, 2, 3, 4, 1)).reshape(b, DIFF_HEADS, LANES, past)
        od = _attn_sample_call(q3, pre[8].reshape(b, l, DIFF_QK), v3, ckt,
                               v_hist.reshape(b, past * DIFF_HEADS, DIFF_DV), p, lam_init)
        k_out = kf.reshape(b, l, DIFF_HEADS, 2, DIFF_DK)
    y = _post_call(x1, og.reshape(n, GDN_V), od.reshape(n, DIFF_V), p, tm)
    return (y.reshape(b, l, d), k_out, vf.reshape(b, l, DIFF_HEADS, DIFF_DV), s_new, conv_new)


def kernel(x_prompt, x_sample, cache_k, cache_v, state_gdn, state_conv, ffn1_norm, ffn1_w_gu, ffn1_w_down,
           mix_norm, w_in, conv_w, a_log, dt_bias, gdn_out_norm, q_norm, k_norm, lambda_q, lambda_k,
           diff_out_norm, w_out, ffn2_norm, ffn2_w_gu, ffn2_w_down):
    w = dict(ffn1_norm=ffn1_norm, ffn1_w_gu=ffn1_w_gu, ffn1_w_down=ffn1_w_down, mix_norm=mix_norm, w_in=w_in,
             conv_w=conv_w, a_log=a_log, dt_bias=dt_bias, gdn_out_norm=gdn_out_norm, q_norm=q_norm,
             k_norm=k_norm, lambda_q=lambda_q, lambda_k=lambda_k, diff_out_norm=diff_out_norm, w_out=w_out,
             ffn2_norm=ffn2_norm, ffn2_w_gu=ffn2_w_gu, ffn2_w_down=ffn2_w_down)
    depth = w_in.shape[0]
    pos_p = jnp.arange(x_prompt.shape[1])
    pos_s = cache_k.shape[2] + jnp.arange(x_sample.shape[1])
    hp, hs = x_prompt, x_sample
    outs = [[] for _ in range(8)]
    for l in range(depth):
        lam_init = 0.8 - 0.6 * math.exp(-0.3 * l)
        p = _layer_params(w, l)
        hp, kp, vp, sp, cp = _layer(hp, pos_p, None, None, None, None, p, lam_init)
        hs, ks, vs, ss, cs = _layer(hs, pos_s, cache_k[l], cache_v[l], state_conv[l], state_gdn[l], p, lam_init)
        for acc, val in zip(outs, (kp, vp, sp, cp, ks, vs, ss, cs)):
            acc.append(val)
    return (hp, hs) + tuple(jnp.stack(o) for o in outs)
```
